```python
import math
import jax, jax.numpy as jnp
from jax import lax
import numpy as np

D_MODEL = 2048
BATCH = 1
SEQ = 8192
DEPTH = 2
DEC_BATCH = 128
DEC_SEQ = 1
PAST_LEN = 8192
PAGE_SIZE = 128

MEM_WIDTH = D_MODEL // 4
TOK_WIDTH = D_MODEL - MEM_WIDTH
N_MIXERS = 2
N_RWKV_LAYERS = (DEPTH + 1) // 2
N_SWA_LAYERS = DEPTH // 2

RWKV_HEAD = 64
RWKV_HEADS = TOK_WIDTH // RWKV_HEAD
DECAY_LORA = max(32, int(round(1.8 * D_MODEL ** 0.5 / 32)) * 32)
AAA_LORA = max(32, int(round(1.8 * D_MODEL ** 0.5 / 32)) * 32)
GATE_LORA = max(32, int(round(0.6 * D_MODEL ** 0.8 / 32)) * 32)
GN_EPS = RWKV_HEAD * 1e-5

SWA_HEAD = 64
SWA_Q_HEADS = TOK_WIDTH // SWA_HEAD
SWA_KV_HEADS = 4
SWA_GROUP = SWA_Q_HEADS // SWA_KV_HEADS
SWA_KV_WIDTH = SWA_KV_HEADS * SWA_HEAD
WINDOW = 128
BLOCK = 128
SWA_SCALE = SWA_HEAD ** -0.5
ROPE_THETA = 10000.0

MEM_TOKENS = 256
MEM_HEADS = 4
MEM_HEAD = MEM_WIDTH // MEM_HEADS
MEM_SCALE = MEM_HEAD ** -0.5

FFN_HIDDEN = int(math.ceil(8 * D_MODEL / 3 / 256)) * 256
ALPHA = (2 * DEPTH) ** 0.25
BETA = (8 * DEPTH) ** -0.25
LN_EPS = 1e-5

kernel_name = 'rwkv7_swa_sink_memxattn_deepnorm_step'


def layer_norm(x, g, b):
    xf = x.astype(jnp.float32)
    mu = jnp.mean(xf, -1, keepdims=True)
    var = jnp.mean(jnp.square(xf - mu), -1, keepdims=True)
    return ((xf - mu) * lax.rsqrt(var + LN_EPS) * g + b).astype(x.dtype)


def rotary(x, pos):
    half = x.shape[-1] // 2
    inv = ROPE_THETA ** (-jnp.arange(half, dtype=jnp.float32) / half)
    ang = pos.astype(jnp.float32)[:, None] * inv[None, :]
    cos = jnp.cos(ang)[:, None, :]
    sin = jnp.sin(ang)[:, None, :]
    xf = x.astype(jnp.float32)
    x1, x2 = xf[..., :half], xf[..., half:]
    return jnp.concatenate([x1 * cos - x2 * sin, x2 * cos + x1 * sin], -1).astype(x.dtype)


def sink_softmax(s, sinks):
    sink = sinks.astype(jnp.float32).reshape(SWA_KV_HEADS, SWA_GROUP, 1, 1)
    m = jnp.maximum(jnp.max(s, -1, keepdims=True), sink)
    p = jnp.exp(s - m)
    return p / (jnp.sum(p, -1, keepdims=True) + jnp.exp(sink - m))


def wkv7_scan(S0, r, w, k, v, a, b):
    def step(S, inp):
        r_t, w_t, k_t, v_t, a_t, b_t = inp
        sa = jnp.einsum('bhij,bhj->bhi', S, a_t)
        S = S * w_t[:, :, None, :] + sa[..., None] * b_t[:, :, None, :] + v_t[..., None] * k_t[:, :, None, :]
        return S, jnp.einsum('bhij,bhj->bhi', S, r_t)
    xs = tuple(jnp.moveaxis(t, 1, 0) for t in (r, w, k, v, a, b))
    S, ys = lax.scan(step, S0, xs)
    return jnp.moveaxis(ys, 0, 1), S


def rwkv7_mixer(x, x_last, S0, w_in, mu, w0, w1, w2, a0, a1, a2, g1, g2, k_k, k_a, r_k, gn_g, gn_b):
    B, T, _ = x.shape
    f32 = jnp.float32
    x_prev = jnp.concatenate([x_last[:, None, :].astype(x.dtype), x[:, :-1]], axis=1)
    xx = x_prev - x
    xr, xw, xk, xv, xa, xg = (x + xx * mu[i] for i in range(6))
    r = xr @ w_in[:, :TOK_WIDTH]
    k = xk @ w_in[:, TOK_WIDTH:2 * TOK_WIDTH]
    v = xv @ w_in[:, 2 * TOK_WIDTH:3 * TOK_WIDTH]
    q_mem = x @ w_in[:, 3 * TOK_WIDTH:]
    w_log = -jax.nn.softplus(-(w0 + jnp.tanh(xw @ w1) @ w2)) - 0.5
    decay = jnp.exp(-jnp.exp(w_log.astype(f32)))
    a = jax.nn.sigmoid(a0 + (xa @ a1) @ a2)
    g = jax.nn.sigmoid(xg @ g1) @ g2
    heads = lambda t: t.astype(f32).reshape(B, T, RWKV_HEADS, RWKV_HEAD)
    kk = heads(k * k_k)
    kk = kk / jnp.maximum(jnp.sqrt(jnp.sum(kk * kk, -1, keepdims=True)), 1e-12)
    k = k * (1 + (a - 1) * k_a)
    r_h, k_h, v_h, a_h = heads(r), heads(k), heads(v), heads(a)
    y, S = wkv7_scan(S0.astype(f32), r_h, heads(decay), k_h, v_h, -kk, kk * a_h)
    ym = jnp.mean(y, -1, keepdims=True)
    yv = jnp.mean(jnp.square(y - ym), -1, keepdims=True)
    y = ((y - ym) * lax.rsqrt(yv + GN_EPS)).reshape(B, T, TOK_WIDTH) * gn_g + gn_b
    bonus = jnp.sum(r_h * k_h * r_k.astype(f32), -1, keepdims=True) * v_h
    out = ((y + bonus.reshape(B, T, TOK_WIDTH)) * g).astype(x.dtype)
    return out, q_mem, x[:, -1], S


def swa_banded(q, k, v, sinks):
    B, T = q.shape[:2]
    nb = T // BLOCK
    qb = q.reshape(B, nb, BLOCK, SWA_KV_HEADS, SWA_GROUP, SWA_HEAD)
    kb = k.reshape(B, nb, BLOCK, SWA_KV_HEADS, SWA_HEAD)
    vb = v.reshape(B, nb, BLOCK, SWA_KV_HEADS, SWA_HEAD)

    def with_prev(t):
        prev = jnp.concatenate([jnp.zeros_like(t[:, :1]), t[:, :-1]], axis=1)
        return jnp.concatenate([prev, t], axis=2)

    kw, vw = with_prev(kb), with_prev(vb)
    s = jnp.einsum('bnqkgd,bnskd->bnkgqs', qb, kw).astype(jnp.float32) * SWA_SCALE
    qi = BLOCK + jnp.arange(BLOCK)[:, None]
    si = jnp.arange(2 * BLOCK)[None, :]
    diff = qi - si
    band = (diff >= 0) & (diff < WINDOW)
    has_prev = (jnp.arange(nb) > 0)[:, None, None] | (si >= BLOCK)[None]
    mask = band[None] & has_prev
    s = jnp.where(mask[None, :, None, None], s, -jnp.inf)
    p = sink_softmax(s, sinks).astype(v.dtype)
    o = jnp.einsum('bnkgqs,bnskd->bnqkgd', p, vw)
    return o.reshape(B, T, TOK_WIDTH)


def swa_cached(q, k, v, buf_k, buf_v, pos0, sinks):
    B, T = q.shape[:2]
    kc = jnp.concatenate([buf_k.astype(k.dtype), k], axis=1)
    vc = jnp.concatenate([buf_v.astype(v.dtype), v], axis=1)
    kpos = pos0 - WINDOW + jnp.arange(kc.shape[1])
    qpos = pos0 + jnp.arange(T)
    diff = qpos[:, None] - kpos[None, :]
    mask = (diff >= 0) & (diff < WINDOW)
    qg = q.reshape(B, T, SWA_KV_HEADS, SWA_GROUP, SWA_HEAD)
    s = jnp.einsum('bqkgd,bskd->bkgqs', qg, kc).astype(jnp.float32) * SWA_SCALE
    s = jnp.where(mask, s, -jnp.inf)
    p = sink_softmax(s, sinks).astype(v.dtype)
    o = jnp.einsum('bkgqs,bskd->bqkgd', p, vc).reshape(B, T, TOK_WIDTH)
    return o, kc[:, -WINDOW:], vc[:, -WINDOW:]


def swa_mixer(x, pos0, w_in, sinks, buf_k, buf_v):
    B, T, _ = x.shape
    proj = x @ w_in
    q = proj[..., :TOK_WIDTH].reshape(B, T, SWA_Q_HEADS, SWA_HEAD)
    k = proj[..., TOK_WIDTH:TOK_WIDTH + SWA_KV_WIDTH].reshape(B, T, SWA_KV_HEADS, SWA_HEAD)
    v = proj[..., TOK_WIDTH + SWA_KV_WIDTH:TOK_WIDTH + 2 * SWA_KV_WIDTH].reshape(B, T, SWA_KV_HEADS, SWA_HEAD)
    q_mem = proj[..., TOK_WIDTH + 2 * SWA_KV_WIDTH:]
    pos = pos0 + jnp.arange(T)
    q = rotary(q, pos)
    k = rotary(k, pos)
    if buf_k is None:
        o = swa_banded(q, k, v, sinks)
        new_k, new_v = k[:, -WINDOW:], v[:, -WINDOW:]
    else:
        o, new_k, new_v = swa_cached(q, k, v, buf_k, buf_v, pos0, sinks)
    return o, q_mem, new_k, new_v


def mem_attend(q, mk, mv):
    B, T, _ = q.shape
    qh = q.reshape(B, T, MEM_HEADS, MEM_HEAD)
    s = jnp.einsum('bthd,bmhd->bhtm', qh, mk.astype(q.dtype)).astype(jnp.float32) * MEM_SCALE
    p = jax.nn.softmax(s, axis=-1).astype(q.dtype)
    return jnp.einsum('bhtm,bmhd->bthd', p, mv.astype(q.dtype)).reshape(B, T, MEM_WIDTH)


def swiglu(x, wg, wu, wd):
    return (jax.nn.silu(x @ wg) * (x @ wu)) @ wd


def trunk(x, pos0, mem_k, mem_v, shift, wkv, win_k, win_v, rwkv_p, swa_p, shared):
    w_out, ln1_g, ln1_b, w_gate, w_up, w_down, ln2_g, ln2_b = shared
    new_shift, new_wkv, new_wk, new_wv = [], [], [], []
    for i in range(DEPTH):
        j = i // N_MIXERS
        if i % N_MIXERS == 0:
            tok, q_mem, s_last, S = rwkv7_mixer(x, shift[j], wkv[j], *[p[j] for p in rwkv_p])
            new_shift.append(s_last)
            new_wkv.append(S)
        else:
            bk = None if win_k is None else win_k[j]
            bv = None if win_v is None else win_v[j]
            tok, q_mem, nk, nv = swa_mixer(x, pos0, swa_p[0][j], swa_p[1][j], bk, bv)
            new_wk.append(nk)
            new_wv.append(nv)
        mo = mem_attend(q_mem, mem_k[i], mem_v[i])
        h = jnp.concatenate([tok, mo], axis=-1) @ w_out[i]
        x = layer_norm(ALPHA * x + h, ln1_g[i], ln1_b[i])
        x = layer_norm(ALPHA * x + swiglu(x, w_gate[i], w_up[i], w_down[i]), ln2_g[i], ln2_b[i])
    return x, jnp.stack(new_shift), jnp.stack(new_wkv), jnp.stack(new_wk), jnp.stack(new_wv)


def setup_inputs(seed: int = 0) -> dict:
    key = jax.random.key(seed)
    ks = iter(jax.random.split(key, 48))
    f32 = jnp.float32
    nrm = lambda shape, scale: jax.random.normal(next(ks), shape, f32) * scale
    D = D_MODEL
    NR, NS = N_RWKV_LAYERS, N_SWA_LAYERS
    v_scale_rwkv = jnp.concatenate([jnp.ones((2 * TOK_WIDTH,), f32), jnp.full((TOK_WIDTH,), BETA, f32),
                                    jnp.ones((MEM_WIDTH,), f32)])
    v_scale_swa = jnp.concatenate([jnp.ones((TOK_WIDTH + SWA_KV_WIDTH,), f32), jnp.full((SWA_KV_WIDTH,), BETA, f32),
                                   jnp.ones((MEM_WIDTH,), f32)])
    v_scale_mem = jnp.concatenate([jnp.ones((MEM_WIDTH,), f32), jnp.full((MEM_WIDTH,), BETA, f32)])
    return {
        'x_prompt': nrm((BATCH, SEQ, D), 1.0),
        'x_sample': nrm((DEC_BATCH, DEC_SEQ, D), 1.0),
        'mem_prompt': nrm((BATCH, MEM_TOKENS, D), 1.0),
        'cache_mem_k': nrm((DEPTH, DEC_BATCH, MEM_TOKENS, MEM_HEADS, MEM_HEAD), 1.0),
        'cache_mem_v': nrm((DEPTH, DEC_BATCH, MEM_TOKENS, MEM_HEADS, MEM_HEAD), BETA),
        'state_rwkv_shift': nrm((NR, DEC_BATCH, D), 1.0),
        'state_rwkv_wkv': nrm((NR, DEC_BATCH, RWKV_HEADS, RWKV_HEAD, RWKV_HEAD), 0.1),
        'cache_swa_k': nrm((NS, DEC_BATCH, WINDOW, SWA_KV_HEADS, SWA_HEAD), 1.0),
        'cache_swa_v': nrm((NS, DEC_BATCH, WINDOW, SWA_KV_HEADS, SWA_HEAD), BETA),
        'w_in_rwkv': nrm((NR, D, 3 * TOK_WIDTH + MEM_WIDTH), D ** -0.5) * v_scale_rwkv,
        'rwkv_mu': jax.random.uniform(next(ks), (NR, 6, D), f32),
        'rwkv_w0': nrm((NR, TOK_WIDTH), 0.5) - 1.0,
        'rwkv_w1': nrm((NR, D, DECAY_LORA), D ** -0.5),
        'rwkv_w2': nrm((NR, DECAY_LORA, TOK_WIDTH), 0.1 * DECAY_LORA ** -0.5),
        'rwkv_a0': nrm((NR, TOK_WIDTH), 0.1),
        'rwkv_a1': nrm((NR, D, AAA_LORA), D ** -0.5),
        'rwkv_a2': nrm((NR, AAA_LORA, TOK_WIDTH), 0.5 * AAA_LORA ** -0.5),
        'rwkv_g1': nrm((NR, D, GATE_LORA), D ** -0.5),
        'rwkv_g2': nrm((NR, GATE_LORA, TOK_WIDTH), GATE_LORA ** -0.5),
        'rwkv_k_k': 0.85 + nrm((NR, TOK_WIDTH), 0.05),
        'rwkv_k_a': 1.0 + nrm((NR, TOK_WIDTH), 0.05),
        'rwkv_r_k': nrm((NR, RWKV_HEADS, RWKV_HEAD), 0.1),
        'rwkv_gn_g': 1.0 + nrm((NR, TOK_WIDTH), 0.05),
        'rwkv_gn_b': nrm((NR, TOK_WIDTH), 0.02),
        'w_in_swa': nrm((NS, D, TOK_WIDTH + 2 * SWA_KV_WIDTH + MEM_WIDTH), D ** -0.5) * v_scale_swa,
        'swa_sinks': nrm((NS, SWA_Q_HEADS), 0.5),
        'w_mem_kv': nrm((DEPTH, D, 2 * MEM_WIDTH), D ** -0.5) * v_scale_mem,
        'w_out': nrm((DEPTH, D, D), BETA * D ** -0.5),
        'ln1_g': 1.0 + nrm((DEPTH, D), 0.05),
        'ln1_b': nrm((DEPTH, D), 0.02),
        'w_gate': nrm((DEPTH, D, FFN_HIDDEN), D ** -0.5),
        'w_up': nrm((DEPTH, D, FFN_HIDDEN), D ** -0.5),
        'w_down': nrm((DEPTH, FFN_HIDDEN, D), BETA * FFN_HIDDEN ** -0.5),
        'ln2_g': 1.0 + nrm((DEPTH, D), 0.05),
        'ln2_b': nrm((DEPTH, D), 0.02),
    }


def reference(x_prompt, x_sample, mem_prompt, cache_mem_k, cache_mem_v, state_rwkv_shift, state_rwkv_wkv,
              cache_swa_k, cache_swa_v, w_in_rwkv, rwkv_mu, rwkv_w0, rwkv_w1, rwkv_w2, rwkv_a0, rwkv_a1,
              rwkv_a2, rwkv_g1, rwkv_g2, rwkv_k_k, rwkv_k_a, rwkv_r_k, rwkv_gn_g, rwkv_gn_b, w_in_swa,
              swa_sinks, w_mem_kv, w_out, ln1_g, ln1_b, w_gate, w_up, w_down, ln2_g, ln2_b):
    rwkv_p = (w_in_rwkv, rwkv_mu, rwkv_w0, rwkv_w1, rwkv_w2, rwkv_a0, rwkv_a1, rwkv_a2, rwkv_g1, rwkv_g2,
              rwkv_k_k, rwkv_k_a, rwkv_r_k, rwkv_gn_g, rwkv_gn_b)
    swa_p = (w_in_swa, swa_sinks)
    shared = (w_out, ln1_g, ln1_b, w_gate, w_up, w_down, ln2_g, ln2_b)

    B, M = x_prompt.shape[0], mem_prompt.shape[1]
    kv = jnp.einsum('bmd,lde->lbme', mem_prompt, w_mem_kv)
    prompt_mem_k = kv[..., :MEM_WIDTH].reshape(DEPTH, B, M, MEM_HEADS, MEM_HEAD)
    prompt_mem_v = kv[..., MEM_WIDTH:].reshape(DEPTH, B, M, MEM_HEADS, MEM_HEAD)
    shift0 = jnp.zeros((N_RWKV_LAYERS, B, D_MODEL), x_prompt.dtype)
    wkv0 = jnp.zeros((N_RWKV_LAYERS, B, RWKV_HEADS, RWKV_HEAD, RWKV_HEAD), jnp.float32)
    y_prompt, prompt_rwkv_shift, prompt_rwkv_wkv, prompt_swa_k, prompt_swa_v = trunk(
        x_prompt, 0, prompt_mem_k, prompt_mem_v, shift0, wkv0, None, None, rwkv_p, swa_p, shared)

    y_sample, sample_rwkv_shift, sample_rwkv_wkv, sample_swa_k, sample_swa_v = trunk(
        x_sample, PAST_LEN, cache_mem_k, cache_mem_v, state_rwkv_shift, state_rwkv_wkv,
        cache_swa_k, cache_swa_v, rwkv_p, swa_p, shared)

    return (y_prompt, y_sample, prompt_mem_k, prompt_mem_v, prompt_rwkv_shift, prompt_rwkv_wkv,
            prompt_swa_k, prompt_swa_v, sample_rwkv_shift, sample_rwkv_wkv, sample_swa_k, sample_swa_v)
```

```python
import functools
import math

import jax
import jax.numpy as jnp
from jax import lax
from jax.experimental import pallas as pl
from jax.experimental.pallas import tpu as pltpu

D_MODEL = 2048
DEPTH = 2
MEM_WIDTH = D_MODEL // 4
TOK_WIDTH = D_MODEL - MEM_WIDTH
RWKV_HEAD = 64
RWKV_HEADS = TOK_WIDTH // RWKV_HEAD
GN_EPS = RWKV_HEAD * 1e-5
SWA_HEAD = 64
SWA_Q_HEADS = TOK_WIDTH // SWA_HEAD
SWA_KV_HEADS = 4
SWA_GROUP = SWA_Q_HEADS // SWA_KV_HEADS
SWA_KV_WIDTH = SWA_KV_HEADS * SWA_HEAD
WINDOW = 128
BLOCK = 128
SWA_SCALE = SWA_HEAD ** -0.5
ROPE_THETA = 10000.0
MEM_TOKENS = 256
MEM_HEADS = 4
MEM_HEAD = MEM_WIDTH // MEM_HEADS
MEM_SCALE = MEM_HEAD ** -0.5
FFN_HIDDEN = int(math.ceil(8 * D_MODEL / 3 / 256)) * 256
ALPHA = (2 * DEPTH) ** 0.25
LN_EPS = 1e-5
LORA_PAD = 128
LORA_IN_WIDTH = 512

LANES = 128
SUBLANES = 8
VMEM_LIMIT_BYTES = 56 * 1024 * 1024

BF16 = jnp.bfloat16
F32 = jnp.float32
NT_DIMS = (((1,), (1,)), ((), ()))
TN_DIMS = (((0,), (0,)), ((), ()))


def _dot(a, b):
    return jnp.dot(a.astype(BF16), b.astype(BF16), preferred_element_type=F32)


def _dot_nt(a, b):
    return lax.dot_general(a.astype(BF16), b.astype(BF16), NT_DIMS, preferred_element_type=F32)


def _dot_tn(a, b):
    return lax.dot_general(a.astype(BF16), b.astype(BF16), TN_DIMS, preferred_element_type=F32)


def _split_dot(x, m):
    hi = x.astype(BF16)
    lo = (x - hi.astype(F32)).astype(BF16)
    return (jnp.dot(hi, m, preferred_element_type=F32)
            + jnp.dot(lo, m, preferred_element_type=F32))


def _head_ones():
    p = lax.broadcasted_iota(jnp.int32, (LANES, LANES), 0)
    q = lax.broadcasted_iota(jnp.int32, (LANES, LANES), 1)
    return ((p // RWKV_HEAD) == (q // RWKV_HEAD)).astype(BF16)


def _cparams(*sem):
    return pltpu.CompilerParams(dimension_semantics=sem, vmem_limit_bytes=VMEM_LIMIT_BYTES)


def _proj_kernel(*refs, lerp):
    if lerp:
        x_ref, xp_ref, mu_ref, w_ref, o_ref = refs
        x = x_ref[...]
        lhs = x + (xp_ref[...] - x) * mu_ref[0]
    else:
        x_ref, w_ref, o_ref = refs
        lhs = x_ref[...]
    o_ref[...] = jnp.dot(lhs.astype(BF16), w_ref[...], preferred_element_type=F32).astype(o_ref.dtype)


def _proj(x, w, *, tm, tn, xprev=None, mu=None, groups=None, out_dtype=F32, name):
    M, K = x.shape
    N = w.shape[1]
    assert M % tm == 0 and N % tn == 0
    lerp = xprev is not None
    if lerp:
        gid = jnp.asarray(groups, jnp.int32)
        assert len(groups) == N // tn
        grid_spec = pltpu.PrefetchScalarGridSpec(
            num_scalar_prefetch=1, grid=(M // tm, N // tn),
            in_specs=[pl.BlockSpec((tm, K), lambda i, j, g: (i, 0)),
                      pl.BlockSpec((tm, K), lambda i, j, g: (i, 0)),
                      pl.BlockSpec((1, 1, K), lambda i, j, g: (g[j], 0, 0)),
                      pl.BlockSpec((K, tn), lambda i, j, g: (0, j))],
            out_specs=pl.BlockSpec((tm, tn), lambda i, j, g: (i, j)))
        kern = lambda g_ref, *refs: _proj_kernel(*refs, lerp=True)
        args = (gid, x, xprev, mu, w)
    else:
        grid_spec = pl.GridSpec(
            grid=(M // tm, N // tn),
            in_specs=[pl.BlockSpec((tm, K), lambda i, j: (i, 0)),
                      pl.BlockSpec((K, tn), lambda i, j: (0, j))],
            out_specs=pl.BlockSpec((tm, tn), lambda i, j: (i, j)))
        kern = functools.partial(_proj_kernel, lerp=False)
        args = (x, w)
    return pl.pallas_call(
        kern, grid_spec=grid_spec, out_shape=jax.ShapeDtypeStruct((M, N), out_dtype),
        compiler_params=_cparams("parallel", "arbitrary"), name=name)(*args)


def _softplus(z):
    return jnp.maximum(z, 0.0) + jnp.log1p(jnp.exp(-jnp.abs(z)))


def _rwkv_prep_kernel(hl_ref, k_ref, w2_ref, a2_ref, g2_ref, w0_ref, a0_ref, kk_ref, ka_ref,
                      lw_ref, kp_ref, kn_ref, bv_ref, g_ref):
    hl = hl_ref[...]
    hw = jnp.tanh(hl[:, 0:LORA_PAD])
    ha = hl[:, LORA_PAD:2 * LORA_PAD]
    hg = jax.nn.sigmoid(hl[:, 2 * LORA_PAD:])
    w_log = -_softplus(-(w0_ref[...] + _dot(hw, w2_ref[...]))) - 0.5
    lw_ref[...] = -jnp.exp(w_log)
    agate = jax.nn.sigmoid(a0_ref[...] + _dot(ha, a2_ref[...]))
    g_ref[...] = _dot(hg, g2_ref[...])
    k = k_ref[...]
    kkr = k * kk_ref[...]
    ss = _split_dot(kkr * kkr, _head_ones())
    kn = kkr / jnp.maximum(jnp.sqrt(ss), 1e-12)
    kn_ref[...] = kn
    bv_ref[...] = kn * agate
    kp_ref[...] = k * (1.0 + (agate - 1.0) * ka_ref[...])


def _rwkv_prep(hl, proj, w2p, a2p, g2, w0, a0, k_k, k_a, *, tm):
    M = hl.shape[0]
    nslab = TOK_WIDTH // LANES
    kbase = TOK_WIDTH // LANES
    row = lambda i, s: (i, s)
    col = lambda i, s: (0, s)
    outs = [jax.ShapeDtypeStruct((M, TOK_WIDTH), F32)] * 5
    return pl.pallas_call(
        _rwkv_prep_kernel, grid=(M // tm, nslab),
        in_specs=[pl.BlockSpec((tm, LORA_IN_WIDTH), lambda i, s: (i, 0)),
                  pl.BlockSpec((tm, LANES), lambda i, s: (i, kbase + s)),
                  pl.BlockSpec((LORA_PAD, LANES), col),
                  pl.BlockSpec((LORA_PAD, LANES), col),
                  pl.BlockSpec((2 * LORA_PAD, LANES), col),
                  pl.BlockSpec((1, LANES), col), pl.BlockSpec((1, LANES), col),
                  pl.BlockSpec((1, LANES), col), pl.BlockSpec((1, LANES), col)],
        out_specs=[pl.BlockSpec((tm, LANES), row)] * 5,
        out_shape=outs, compiler_params=_cparams("parallel", "arbitrary"),
        name="rwkv_prep")(hl, proj, w2p, a2p, g2, w0, a0, k_k, k_a)


def _gn_gate(y, r, kp, v, g, rk, gg, gb, ones):
    inv_n = 1.0 / RWKV_HEAD
    mean = _split_dot(y, ones) * inv_n
    d = y - mean
    var = _split_dot(d * d, ones) * inv_n
    yn = d * lax.rsqrt(var + GN_EPS) * gg + gb
    bonus = _split_dot(r * kp * rk, ones) * v
    return (yn + bonus) * g


def _gn_gate_kernel(y_ref, r_ref, kp_ref, v_ref, g_ref, rk_ref, gg_ref, gb_ref, o_ref):
    o_ref[...] = _gn_gate(y_ref[...], r_ref[...], kp_ref[...], v_ref[...], g_ref[...],
                          rk_ref[...], gg_ref[...], gb_ref[...], _head_ones()).astype(o_ref.dtype)


def _gn_gate_rows(y, proj, kp, g, rk, gn_g, gn_b, *, tm):
    M = y.shape[0]
    nslab = TOK_WIDTH // LANES
    row = lambda i, s: (i, s)
    col = lambda i, s: (0, s)
    blk = pl.BlockSpec((tm, LANES), row)
    cblk = pl.BlockSpec((1, LANES), col)
    return pl.pallas_call(
        _gn_gate_kernel, grid=(M // tm, nslab),
        in_specs=[blk, blk, blk, pl.BlockSpec((tm, LANES), lambda i, s: (i, 2 * nslab + s)), blk,
                  cblk, cblk, cblk],
        out_specs=blk, out_shape=jax.ShapeDtypeStruct((M, TOK_WIDTH), BF16),
        compiler_params=_cparams("parallel", "arbitrary"), name="gn_gate")(
            y, proj, kp, proj, g, rk, gn_g, gn_b)


WKV_CHUNK = 64


def _wkv_masks():
    n = 2 * WKV_CHUNK
    p = lax.broadcasted_iota(jnp.int32, (n, n), 0)
    q = lax.broadcasted_iota(jnp.int32, (n, n), 1)
    same = lambda b: (p // b) == (q // b)
    pt, qt = p % WKV_CHUNK, q % WKV_CHUNK
    s8, s16, s32, s64 = same(8), same(16), same(32), same(WKV_CHUNK)
    return dict(strict=s64 & (pt > qt), incl=s64 & (pt >= qt), s8=s8,
                e16=s16 & ~s8, e32=s32 & ~s16, e64=s64 & ~s32,
                eye=(p == q).astype(F32))


def _wkv_pair(r, lw, k, v, kn, bv, S, mk, tri, lane_lo):
    stack = lambda x: jnp.concatenate([jnp.where(lane_lo, x, 0.0), jnp.where(lane_lo, 0.0, x)], axis=0)
    c = _split_dot_left(tri, lw)
    c_last = c[WKV_CHUNK - 1:WKV_CHUNK, :]
    e_in = jnp.exp(c)
    e_out = jnp.exp(-c)
    e_end = jnp.exp(c_last - c)
    ah = stack(-kn * jnp.exp(c - lw))
    rh = stack(r * e_in)
    bh = stack(bv * e_out)
    kh = stack(k * e_out)
    bbar = stack(bv * e_end)
    kbar = stack(k * e_end)
    vs = stack(v)
    n = 2 * WKV_CHUNK
    gm = _dot_nt(jnp.concatenate([ah, rh], axis=0), jnp.concatenate([bh, kh], axis=0))
    a_ab = jnp.where(mk["strict"], gm[:n, :n], 0.0)
    a_ak = jnp.where(mk["strict"], gm[:n, n:], 0.0)
    l_rb = jnp.where(mk["incl"], gm[n:, :n], 0.0)
    l_rk = jnp.where(mk["incl"], gm[n:, n:], 0.0)
    d1 = jnp.where(mk["s8"], a_ab, 0.0)
    x = mk["eye"] + d1
    d2 = _dot(d1, d1)
    x = x + _dot(x, d2)
    d4 = _dot(d2, d2)
    x = x + _dot(x, d4)
    for lvl in ("e16", "e32", "e64"):
        e = jnp.where(mk[lvl], a_ab, 0.0)
        x = x + _dot(x, _dot(e, x))
    av = _dot(a_ak, vs)
    tw = _dot(x, jnp.concatenate([ah, av], axis=1))
    lwm = _dot(l_rb, tw)
    qm = rh + lwm[:, :n]
    y0 = lwm[:, n:] + _dot(l_rk, vs)
    mt = _dot_tn(tw[:, :n], bbar)
    nt = _dot_tn(tw[:, n:], bbar) + _dot_tn(vs, kbar)
    ys = _dot_nt(qm, S) + y0
    y = ys[:WKV_CHUNK, :] + ys[WKV_CHUNK:, :]
    s_new = S * jnp.exp(c_last) + _dot(S, mt) + nt
    return y, s_new


def _split_dot_left(m, x):
    hi = x.astype(BF16)
    lo = (x - hi.astype(F32)).astype(BF16)
    return (jnp.dot(m, hi, preferred_element_type=F32)
            + jnp.dot(m, lo, preferred_element_type=F32))


def _wkv_kernel(r_ref, v_ref, lw_ref, kp_ref, kn_ref, bv_ref, g_ref, rk_ref, gg_ref, gb_ref,
                o_ref, sout_ref, s_scr, *, pairs):
    c_idx = pl.program_id(1)

    @pl.when(c_idx == 0)
    def _():
        s_scr[...] = jnp.zeros_like(s_scr)

    mk = _wkv_masks()
    ti = lax.broadcasted_iota(jnp.int32, (WKV_CHUNK, WKV_CHUNK), 0)
    tj = lax.broadcasted_iota(jnp.int32, (WKV_CHUNK, WKV_CHUNK), 1)
    tri = (ti >= tj).astype(BF16)
    lane_lo = lax.broadcasted_iota(jnp.int32, (WKV_CHUNK, LANES), 1) < RWKV_HEAD
    ones = _head_ones()
    for p in range(pairs):
        sl = slice(p * LANES, (p + 1) * LANES)
        r, kp, v = r_ref[:, sl], kp_ref[:, sl], v_ref[:, sl]
        y, s_new = _wkv_pair(r, lw_ref[:, sl], kp, v, kn_ref[:, sl], bv_ref[:, sl], s_scr[p],
                             mk, tri, lane_lo)
        s_scr[p] = s_new
        o_ref[:, sl] = _gn_gate(y, r, kp, v, g_ref[:, sl], rk_ref[:, sl], gg_ref[:, sl],
                                gb_ref[:, sl], ones).astype(o_ref.dtype)

    @pl.when(c_idx == pl.num_programs(1) - 1)
    def _():
        sout_ref[...] = s_scr[...]


def _wkv_prompt(proj, lw, kp, kn, bv, g, rk, gn_g, gn_b, *, pairs):
    T = proj.shape[0]
    npair = TOK_WIDTH // LANES
    assert npair % pairs == 0 and T % WKV_CHUNK == 0
    wb = pairs * LANES
    ngrp = npair // pairs
    nb = TOK_WIDTH // wb
    tok = lambda off: pl.BlockSpec((WKV_CHUNK, wb), lambda gi, c, off=off: (c, off + gi))
    cst = pl.BlockSpec((1, wb), lambda gi, c: (0, gi))
    return pl.pallas_call(
        functools.partial(_wkv_kernel, pairs=pairs), grid=(ngrp, T // WKV_CHUNK),
        in_specs=[tok(0), tok(2 * nb), tok(0), tok(0), tok(0), tok(0), tok(0), cst, cst, cst],
        out_specs=[tok(0), pl.BlockSpec((pairs, LANES, LANES), lambda gi, c: (gi, 0, 0))],
        out_shape=[jax.ShapeDtypeStruct((T, TOK_WIDTH), BF16),
                   jax.ShapeDtypeStruct((npair, LANES, LANES), F32)],
        scratch_shapes=[pltpu.VMEM((pairs, LANES, LANES), F32)],
        compiler_params=_cparams("parallel", "arbitrary"), name="wkv_chunked")(
            proj, proj, lw, kp, kn, bv, g, rk, gn_g, gn_b)


def _wkv_step_kernel(s_ref, kn_ref, lw_ref, bv_ref, kp_ref, r_ref, vt_ref, so_ref, y_ref, *, bs):
    lane_h = lax.broadcasted_iota(jnp.int32, (RWKV_HEAD, RWKV_HEADS), 1)

    def sample(b, carry):
        vt = vt_ref[b]
        yacc = jnp.zeros((RWKV_HEAD, RWKV_HEADS), F32)
        for h in range(RWKV_HEADS):
            S = s_ref[b, h]
            sa = -jnp.sum(S * kn_ref[b, h], axis=1, keepdims=True)
            s_new = S * jnp.exp(lw_ref[b, h]) + sa * bv_ref[b, h] + vt[:, h:h + 1] * kp_ref[b, h]
            so_ref[b, h] = s_new
            ycol = jnp.sum(s_new * r_ref[b, h], axis=1, keepdims=True)
            yacc = jnp.where(lane_h == h, ycol, yacc)
        y_ref[b] = yacc
        return carry

    lax.fori_loop(0, bs, sample, 0)


def _wkv_step(state, kn, lw, bv, kp, r, vt, *, bs):
    B = state.shape[0]
    sblk = pl.BlockSpec((bs, RWKV_HEADS, RWKV_HEAD, RWKV_HEAD), lambda i: (i, 0, 0, 0))
    vblk = pl.BlockSpec((bs, RWKV_HEADS, 1, RWKV_HEAD), lambda i: (i, 0, 0, 0))
    tblk = pl.BlockSpec((bs, RWKV_HEAD, RWKV_HEADS), lambda i: (i, 0, 0))
    return pl.pallas_call(
        functools.partial(_wkv_step_kernel, bs=bs), grid=(B // bs,),
        in_specs=[sblk, vblk, vblk, vblk, vblk, vblk, tblk],
        out_specs=[sblk, tblk],
        out_shape=[jax.ShapeDtypeStruct(state.shape, F32),
                   jax.ShapeDtypeStruct((B, RWKV_HEAD, RWKV_HEADS), F32)],
        compiler_params=_cparams("parallel"), name="wkv_step")(state, kn, lw, bv, kp, r, vt)


def _softmax_rows(s):
    m = jnp.max(s, axis=-1, keepdims=True)
    e = jnp.exp(s - m)
    return e / jnp.sum(e, axis=-1, keepdims=True)


def _mem_attn_kernel(q_ref, k_ref, v_ref, o_ref):
    for h in range(MEM_HEADS):
        sl = slice(h * MEM_HEAD, (h + 1) * MEM_HEAD)
        p = _softmax_rows(_dot_nt(q_ref[:, sl], k_ref[:, sl]) * MEM_SCALE)
        o_ref[:, sl] = _dot(p, v_ref[:, sl]).astype(o_ref.dtype)


def _mem_attn(proj, qblk, kv, *, tm):
    M = proj.shape[0]
    return pl.pallas_call(
        _mem_attn_kernel, grid=(M // tm,),
        in_specs=[pl.BlockSpec((tm, MEM_WIDTH), lambda i: (i, qblk)),
                  pl.BlockSpec((MEM_TOKENS, MEM_WIDTH), lambda i: (0, 0)),
                  pl.BlockSpec((MEM_TOKENS, MEM_WIDTH), lambda i: (0, 1))],
        out_specs=pl.BlockSpec((tm, MEM_WIDTH), lambda i: (i, 0)),
        out_shape=jax.ShapeDtypeStruct((M, MEM_WIDTH), BF16),
        compiler_params=_cparams("parallel"), name="mem_attn")(proj, kv, kv)


def _mem_attn_step_kernel(q_ref, k_ref, v_ref, o_ref, *, bs):
    def sample(b, carry):
        p = _softmax_rows(_dot_nt(q_ref[b], k_ref[b]) * MEM_SCALE)
        o_ref[b] = _dot(p, v_ref[b])
        return carry

    lax.fori_loop(0, bs, sample, 0)


def _mem_attn_step(q_bd, mk, mv, *, bs):
    B = q_bd.shape[0]
    qblk = pl.BlockSpec((bs, SUBLANES, MEM_WIDTH), lambda i: (i, 0, 0))
    cblk = pl.BlockSpec((bs, MEM_TOKENS, MEM_WIDTH), lambda i: (i, 0, 0))
    return pl.pallas_call(
        functools.partial(_mem_attn_step_kernel, bs=bs), grid=(B // bs,),
        in_specs=[qblk, cblk, cblk], out_specs=qblk,
        out_shape=jax.ShapeDtypeStruct(q_bd.shape, F32),
        compiler_params=_cparams("parallel"), name="mem_attn_step")(q_bd, mk, mv)


def _out_ln_kernel(*refs, ka, two):
    if two:
        a_ref, b_ref, w_ref, res_ref, g_ref, beta_ref, of_ref, ob_ref, acc_ref = refs
    else:
        a_ref, w_ref, res_ref, g_ref, beta_ref, of_ref, ob_ref, acc_ref = refs
    k = pl.program_id(1)

    @pl.when(k == 0)
    def _():
        acc_ref[...] = jnp.zeros_like(acc_ref)

    if two:
        @pl.when(k < ka)
        def _():
            acc_ref[...] += jnp.dot(a_ref[...], w_ref[...], preferred_element_type=F32)

        @pl.when(k >= ka)
        def _():
            acc_ref[...] += jnp.dot(b_ref[...], w_ref[...], preferred_element_type=F32)
    else:
        acc_ref[...] += jnp.dot(a_ref[...], w_ref[...], preferred_element_type=F32)

    @pl.when(k == pl.num_programs(1) - 1)
    def _():
        z = ALPHA * res_ref[...] + acc_ref[...]
        mu = jnp.mean(z, axis=-1, keepdims=True)
        d = z - mu
        var = jnp.mean(d * d, axis=-1, keepdims=True)
        out = d * lax.rsqrt(var + LN_EPS) * g_ref[...] + beta_ref[...]
        of_ref[...] = out
        ob_ref[...] = out.astype(BF16)


def _out_ln(a, b, w, res, g, beta, *, tm, tk):
    M, Ka = a.shape
    Kb = 0 if b is None else b.shape[1]
    N = w.shape[1]
    assert Ka % tk == 0 and Kb % tk == 0 and w.shape[0] == Ka + Kb
    ka, nk = Ka // tk, (Ka + Kb) // tk
    two = b is not None
    in_specs = [pl.BlockSpec((tm, tk), lambda i, k: (i, jnp.minimum(k, ka - 1)))]
    args = [a]
    if two:
        in_specs.append(pl.BlockSpec((tm, tk), lambda i, k: (i, jnp.maximum(k - ka, 0))))
        args.append(b)
    in_specs += [pl.BlockSpec((tk, N), lambda i, k: (k, 0)),
                 pl.BlockSpec((tm, N), lambda i, k: (i, 0)),
                 pl.BlockSpec((1, N), lambda i, k: (0, 0)),
                 pl.BlockSpec((1, N), lambda i, k: (0, 0))]
    args += [w, res, g, beta]
    oblk = pl.BlockSpec((tm, N), lambda i, k: (i, 0))
    return pl.pallas_call(
        functools.partial(_out_ln_kernel, ka=ka, two=two), grid=(M // tm, nk),
        in_specs=in_specs, out_specs=[oblk, oblk],
        out_shape=[jax.ShapeDtypeStruct((M, N), F32), jax.ShapeDtypeStruct((M, N), BF16)],
        scratch_shapes=[pltpu.VMEM((tm, N), F32)],
        compiler_params=_cparams("parallel", "arbitrary"), name="out_ln")(*args)


def _ffn_up_kernel(x_ref, wg_ref, wu_ref, o_ref):
    x = x_ref[...]
    gate = jnp.dot(x, wg_ref[...], preferred_element_type=F32)
    up = jnp.dot(x, wu_ref[...], preferred_element_type=F32)
    o_ref[...] = (gate * jax.nn.sigmoid(gate) * up).astype(o_ref.dtype)


def _ffn_up(x, wg, wu, *, tm, tn):
    M, K = x.shape
    N = wg.shape[1]
    wblk = pl.BlockSpec((K, tn), lambda i, j: (0, j))
    return pl.pallas_call(
        _ffn_up_kernel, grid=(M // tm, N // tn),
        in_specs=[pl.BlockSpec((tm, K), lambda i, j: (i, 0)), wblk, wblk],
        out_specs=pl.BlockSpec((tm, tn), lambda i, j: (i, j)),
        out_shape=jax.ShapeDtypeStruct((M, N), BF16),
        compiler_params=_cparams("parallel", "arbitrary"), name="ffn_up")(x, wg, wu)


def _rope_kernel(x_ref, cos_ref, sin_ref, o_ref):
    x = x_ref[...]
    lane = lax.broadcasted_iota(jnp.int32, x.shape, 1)
    first_half = (lane % SWA_HEAD) < (SWA_HEAD // 2)
    partner = jnp.where(first_half, pltpu.roll(x, LANES - SWA_HEAD // 2, 1),
                        pltpu.roll(x, SWA_HEAD // 2, 1))
    o_ref[...] = x * cos_ref[...] + partner * sin_ref[...]


def _rope(proj, cos, sin_signed, *, tm):
    M = proj.shape[0]
    width = TOK_WIDTH + SWA_KV_WIDTH
    blk = pl.BlockSpec((tm, LANES), lambda i, s: (i, s))
    tab = pl.BlockSpec((tm, LANES), lambda i, s: (i, 0))
    return pl.pallas_call(
        _rope_kernel, grid=(M // tm, width // LANES),
        in_specs=[blk, tab, tab], out_specs=blk,
        out_shape=jax.ShapeDtypeStruct((M, width), F32),
        compiler_params=_cparams("parallel", "arbitrary"), name="rope")(proj, cos, sin_signed)


def _sink_column(sink_ref, base, rows_per_head, nheads):
    rows = rows_per_head * nheads
    hid = lax.broadcasted_iota(jnp.int32, (rows, 1), 0) // rows_per_head
    col = jnp.zeros((rows, 1), F32)
    for j in range(nheads):
        col = jnp.where(hid == j, sink_ref[base + j], col)
    return col


def _sink_softmax(s, sink):
    m = jnp.maximum(jnp.max(s, axis=-1, keepdims=True), sink)
    p = jnp.exp(s - m)
    return p / (jnp.sum(p, axis=-1, keepdims=True) + jnp.exp(sink - m))


def _swa_kernel(sink_ref, q_ref, kp_ref, kc_ref, vp_ref, vc_ref, o_ref):
    g = pl.program_id(0)
    n = pl.program_id(1)
    rows = SWA_GROUP * BLOCK
    q = q_ref[...].reshape(rows, SWA_HEAD)
    kcat = jnp.concatenate([kp_ref[0], kc_ref[0]], axis=0)
    vcat = jnp.concatenate([vp_ref[0], vc_ref[0]], axis=0)
    s = _dot_nt(q, kcat) * SWA_SCALE
    qi = lax.broadcasted_iota(jnp.int32, s.shape, 0) % BLOCK
    si = lax.broadcasted_iota(jnp.int32, s.shape, 1)
    valid = (si > qi) & (si <= qi + WINDOW) & ((n > 0) | (si >= BLOCK))
    s = jnp.where(valid, s, -jnp.inf)
    p = _sink_softmax(s, _sink_column(sink_ref, g * SWA_GROUP, BLOCK, SWA_GROUP))
    o_ref[...] = _dot(p, vcat).reshape(SWA_GROUP, BLOCK, SWA_HEAD)


def _swa_prompt(q, k, v, sinks):
    T = q.shape[1]
    nb = T // BLOCK
    qblk = pl.BlockSpec((SWA_GROUP, BLOCK, SWA_HEAD), lambda g, n: (g, n, 0))
    cur = pl.BlockSpec((1, BLOCK, SWA_HEAD), lambda g, n: (g, n, 0))
    prev = pl.BlockSpec((1, BLOCK, SWA_HEAD), lambda g, n: (g, jnp.maximum(n - 1, 0), 0))
    return pl.pallas_call(
        _swa_kernel, grid=(SWA_KV_HEADS, nb),
        in_specs=[pl.BlockSpec(memory_space=pltpu.SMEM), qblk, prev, cur, prev, cur],
        out_specs=qblk, out_shape=jax.ShapeDtypeStruct(q.shape, F32),
        compiler_params=_cparams("parallel", "arbitrary"), name="swa_banded")(sinks, q, k, k, v, v)


def _swa_step_kernel(sink_ref, q_ref, k_ref, v_ref, o_ref, *, bs):
    sink = _sink_column(sink_ref, 0, 1, SWA_Q_HEADS)

    def sample(b, carry):
        p = _sink_softmax(_dot_nt(q_ref[b], k_ref[b]) * SWA_SCALE, sink)
        o_ref[b] = _dot(p, v_ref[b])
        return carry

    lax.fori_loop(0, bs, sample, 0)


def _swa_step(q_bd, kc, vc, sinks, *, bs):
    B = q_bd.shape[0]
    qblk = pl.BlockSpec((bs, SWA_Q_HEADS, SWA_KV_WIDTH), lambda i: (i, 0, 0))
    cblk = pl.BlockSpec((bs, WINDOW, SWA_KV_WIDTH), lambda i: (i, 0, 0))
    return pl.pallas_call(
        functools.partial(_swa_step_kernel, bs=bs), grid=(B // bs,),
        in_specs=[pl.BlockSpec(memory_space=pltpu.SMEM), qblk, cblk, cblk], out_specs=qblk,
        out_shape=jax.ShapeDtypeStruct(q_bd.shape, F32),
        compiler_params=_cparams("parallel"), name="swa_step")(sinks, q_bd, kc, vc)


ROW_TILE = 512
COL_TILE = 512
STEP_BATCH = 8
WKV_STEP_BATCH = 4
WKV_PAIRS = 6
RWKV_PROJ_GROUPS = [0, 0, 0, 1, 1, 1, 2, 2, 2, 3]
RWKV_LORA_GROUPS = [0, 1, 2, 2]


def _row_tile(m):
    return ROW_TILE if m % ROW_TILE == 0 else m


def _pad_rows(w, rows):
    return jnp.pad(w, ((0, rows - w.shape[0]), (0, 0)))


def _pad_cols(w, cols):
    return jnp.pad(w, ((0, 0), (0, cols - w.shape[1])))


def _rwkv_weights(w_in, mu, w1, w2, a1, a2, g1, g2):
    zeros = jnp.zeros_like(mu[0])
    return dict(
        w_in=w_in.astype(BF16),
        mu_proj=jnp.stack([mu[0], mu[2], mu[3], zeros])[:, None, :],
        mu_lora=jnp.stack([mu[1], mu[4], mu[5]])[:, None, :],
        w_lora=jnp.concatenate([_pad_cols(w1, LORA_PAD), _pad_cols(a1, LORA_PAD), g1], axis=1).astype(BF16),
        w2=_pad_rows(w2, LORA_PAD).astype(BF16), a2=_pad_rows(a2, LORA_PAD).astype(BF16),
        g2=g2.astype(BF16))


def _rwkv_inputs(x, xprev, W, w0, a0, k_k, k_a):
    tm = _row_tile(x.shape[0])
    proj = _proj(x, W["w_in"], tm=tm, tn=COL_TILE, xprev=xprev, mu=W["mu_proj"],
                 groups=RWKV_PROJ_GROUPS, name="rwkv_proj")
    hl = _proj(x, W["w_lora"], tm=tm, tn=LANES, xprev=xprev, mu=W["mu_lora"],
               groups=RWKV_LORA_GROUPS, name="rwkv_lora_in")
    row = lambda t: t[None, :]
    return (proj,) + tuple(_rwkv_prep(hl, proj, W["w2"], W["a2"], W["g2"], row(w0), row(a0),
                                      row(k_k), row(k_a), tm=tm))


def _unblock_state(s_bd):
    n = RWKV_HEAD
    return jnp.stack([s_bd[:, :n, :n], s_bd[:, n:, n:]], axis=1).reshape(RWKV_HEADS, n, n)


def _rope_tables(pos):
    half = SWA_HEAD // 2
    inv = ROPE_THETA ** (-jnp.arange(half, dtype=F32) / half)
    ang = pos.astype(F32)[:, None] * inv[None, :]
    cos, sin = jnp.cos(ang), jnp.sin(ang)
    reps = LANES // SWA_HEAD
    return jnp.tile(cos, (1, 2 * reps)), jnp.tile(jnp.concatenate([-sin, sin], axis=1), (1, reps))


def _post_mixer(tok, mo, x, lw, tm):
    x1f, x1b = _out_ln(tok, mo, lw["w_out"], x, lw["ln1_g"], lw["ln1_b"], tm=tm, tk=COL_TILE)
    hff = _ffn_up(x1b, lw["w_gate"], lw["w_up"], tm=tm, tn=COL_TILE)
    return _out_ln(hff, None, lw["w_down"], x1f, lw["ln2_g"], lw["ln2_b"], tm=tm, tk=COL_TILE)


def kernel(x_prompt, x_sample, mem_prompt, cache_mem_k, cache_mem_v, state_rwkv_shift, state_rwkv_wkv, cache_swa_k, cache_swa_v, w_in_rwkv, rwkv_mu, rwkv_w0, rwkv_w1, rwkv_w2, rwkv_a0, rwkv_a1, rwkv_a2, rwkv_g1, rwkv_g2, rwkv_k_k, rwkv_k_a, rwkv_r_k, rwkv_gn_g, rwkv_gn_b, w_in_swa, swa_sinks, w_mem_kv, w_out, ln1_g, ln1_b, w_gate, w_up, w_down, ln2_g, ln2_b):
    assert DEPTH == 2 and x_prompt.shape[0] == 1 and x_sample.shape[1] == 1
    T = x_prompt.shape[1]
    B = x_sample.shape[0]
    past_len = T
    row = lambda t: t[None, :]
    shared = [dict(w_out=w_out[i].astype(BF16), ln1_g=row(ln1_g[i]), ln1_b=row(ln1_b[i]),
                   w_gate=w_gate[i].astype(BF16), w_up=w_up[i].astype(BF16),
                   w_down=w_down[i].astype(BF16), ln2_g=row(ln2_g[i]), ln2_b=row(ln2_b[i]))
              for i in range(DEPTH)]
    RW = _rwkv_weights(w_in_rwkv[0], rwkv_mu[0], rwkv_w1[0], rwkv_w2[0], rwkv_a1[0], rwkv_a2[0],
                       rwkv_g1[0], rwkv_g2[0])
    rk, gn_g, gn_b = row(rwkv_r_k[0].reshape(-1)), row(rwkv_gn_g[0]), row(rwkv_gn_b[0])
    w_swa = w_in_swa[0].astype(BF16)
    sinks = swa_sinks[0]
    q_blk_rwkv = 3 * TOK_WIDTH // MEM_WIDTH
    q_blk_swa = (TOK_WIDTH + 2 * SWA_KV_WIDTH) // MEM_WIDTH

    xp = x_prompt[0]
    tm = _row_tile(T)
    kv = [_proj(mem_prompt[0], w_mem_kv[i].astype(BF16), tm=MEM_TOKENS, tn=COL_TILE, name="mem_kv")
          for i in range(DEPTH)]
    prompt_mem_k = jnp.stack([t[:, :MEM_WIDTH] for t in kv]).reshape(DEPTH, 1, MEM_TOKENS, MEM_HEADS, MEM_HEAD)
    prompt_mem_v = jnp.stack([t[:, MEM_WIDTH:] for t in kv]).reshape(DEPTH, 1, MEM_TOKENS, MEM_HEADS, MEM_HEAD)

    xprev = jnp.concatenate([jnp.zeros((1, D_MODEL), xp.dtype), xp[:-1]], axis=0)
    proj, lw, kp, kn, bv, g = _rwkv_inputs(xp, xprev, RW, rwkv_w0[0], rwkv_a0[0], rwkv_k_k[0], rwkv_k_a[0])
    tok, s_bd = _wkv_prompt(proj, lw, kp, kn, bv, g, rk, gn_g, gn_b, pairs=WKV_PAIRS)
    mo = _mem_attn(proj, q_blk_rwkv, kv[0], tm=tm)
    xf, xb = _post_mixer(tok, mo, xp, shared[0], tm)
    prompt_shift = xp[-1][None, None, :]
    prompt_wkv = _unblock_state(s_bd)[None, None]

    proj = _proj(xb, w_swa, tm=tm, tn=COL_TILE, name="swa_proj")
    cos, sin = _rope_tables(jnp.arange(T))
    qk = _rope(proj, cos, sin, tm=tm)
    head_major = lambda t, h: t.reshape(T, h, SWA_HEAD).transpose(1, 0, 2)
    v_rows = proj[:, TOK_WIDTH + SWA_KV_WIDTH:TOK_WIDTH + 2 * SWA_KV_WIDTH]
    o = _swa_prompt(head_major(qk[:, :TOK_WIDTH], SWA_Q_HEADS), head_major(qk[:, TOK_WIDTH:], SWA_KV_HEADS),
                    head_major(v_rows, SWA_KV_HEADS), sinks)
    tok = o.transpose(1, 0, 2).reshape(T, TOK_WIDTH).astype(BF16)
    mo = _mem_attn(proj, q_blk_swa, kv[1], tm=tm)
    y_prompt, _ = _post_mixer(tok, mo, xf, shared[1], tm)
    prompt_swa_k = qk[T - WINDOW:, TOK_WIDTH:].reshape(1, 1, WINDOW, SWA_KV_HEADS, SWA_HEAD)
    prompt_swa_v = v_rows[T - WINDOW:].reshape(1, 1, WINDOW, SWA_KV_HEADS, SWA_HEAD)

    xs = x_sample[:, 0]
    tms = _row_tile(B)
    slab_mask = (jnp.arange(MEM_WIDTH)[None, :] // MEM_HEAD == jnp.arange(SUBLANES)[:, None]).astype(F32)

    def mem_step(q, layer):
        mk = cache_mem_k[layer].reshape(B, MEM_TOKENS, MEM_WIDTH)
        mv = cache_mem_v[layer].reshape(B, MEM_TOKENS, MEM_WIDTH)
        out = _mem_attn_step(q[:, None, :] * slab_mask[None], mk, mv, bs=STEP_BATCH)
        return jnp.sum(out * slab_mask[None], axis=1).astype(BF16)

    proj, lw, kp, kn, bv, g = _rwkv_inputs(xs, state_rwkv_shift[0], RW, rwkv_w0[0], rwkv_a0[0],
                                           rwkv_k_k[0], rwkv_k_a[0])
    rows4 = lambda t: t.reshape(B, RWKV_HEADS, 1, RWKV_HEAD)
    vt = proj[:, 2 * TOK_WIDTH:3 * TOK_WIDTH].reshape(B, RWKV_HEADS, RWKV_HEAD).transpose(0, 2, 1)
    s_new, y_t = _wkv_step(state_rwkv_wkv[0], rows4(kn), rows4(lw), rows4(bv), rows4(kp),
                           rows4(proj[:, :TOK_WIDTH]), vt, bs=WKV_STEP_BATCH)
    y = y_t.transpose(0, 2, 1).reshape(B, TOK_WIDTH)
    tok = _gn_gate_rows(y, proj, kp, g, rk, gn_g, gn_b, tm=tms)
    mo = mem_step(proj[:, 3 * TOK_WIDTH:], 0)
    xf, xb = _post_mixer(tok, mo, xs, shared[0], tms)
    sample_shift = xs[None]
    sample_wkv = s_new[None]

    proj = _proj(xb, w_swa, tm=tms, tn=COL_TILE, name="swa_proj")
    cos, sin = _rope_tables(jnp.full((B,), past_len))
    qk = _rope(proj, cos, sin, tm=tms)
    k_new = qk[:, TOK_WIDTH:].reshape(B, 1, SWA_KV_HEADS, SWA_HEAD)
    v_new = proj[:, TOK_WIDTH + SWA_KV_WIDTH:TOK_WIDTH + 2 * SWA_KV_WIDTH].reshape(B, 1, SWA_KV_HEADS, SWA_HEAD)
    kc = jnp.concatenate([cache_swa_k[0][:, 1:], k_new], axis=1)
    vc = jnp.concatenate([cache_swa_v[0][:, 1:], v_new], axis=1)
    kv_of_head = (jnp.arange(SWA_Q_HEADS)[:, None] // SWA_GROUP == jnp.arange(SWA_KV_HEADS)[None, :]).astype(F32)
    q3 = qk[:, :TOK_WIDTH].reshape(B, SWA_Q_HEADS, 1, SWA_HEAD)
    q_bd = (q3 * kv_of_head[None, :, :, None]).reshape(B, SWA_Q_HEADS, SWA_KV_WIDTH)
    o = _swa_step(q_bd, kc.reshape(B, WINDOW, SWA_KV_WIDTH), vc.reshape(B, WINDOW, SWA_KV_WIDTH),
                  sinks, bs=STEP_BATCH)
    o = jnp.sum(o.reshape(B, SWA_Q_HEADS, SWA_KV_HEADS, SWA_HEAD) * kv_of_head[None, :, :, None], axis=2)
    tok = o.reshape(B, TOK_WIDTH).astype(BF16)
    mo = mem_step(proj[:, TOK_WIDTH + 2 * SWA_KV_WIDTH:], 1)
    y_sample, _ = _post_mixer(tok, mo, xf, shared[1], tms)

    return (y_prompt[None], y_sample[:, None, :], prompt_mem_k, prompt_mem_v, prompt_shift, prompt_wkv,
            prompt_swa_k, prompt_swa_v, sample_shift, sample_wkv, kc[None], vc[None])
```

```python
import functools
import math

import jax
import jax.numpy as jnp
from jax import lax
from jax.experimental import pallas as pl
from jax.experimental.pallas import tpu as pltpu

D_MODEL = 2048
DEPTH = 2
MEM_WIDTH = D_MODEL // 4
TOK_WIDTH = D_MODEL - MEM_WIDTH
RWKV_HEAD = 64
RWKV_HEADS = TOK_WIDTH // RWKV_HEAD
GN_EPS = RWKV_HEAD * 1e-5
SWA_HEAD = 64
SWA_Q_HEADS = TOK_WIDTH // SWA_HEAD
SWA_KV_HEADS = 4
SWA_GROUP = SWA_Q_HEADS // SWA_KV_HEADS
SWA_KV_WIDTH = SWA_KV_HEADS * SWA_HEAD
WINDOW = 128
BLOCK = 128
SWA_SCALE = SWA_HEAD ** -0.5
ROPE_THETA = 10000.0
MEM_TOKENS = 256
MEM_HEADS = 4
MEM_HEAD = MEM_WIDTH // MEM_HEADS
MEM_SCALE = MEM_HEAD ** -0.5
FFN_HIDDEN = int(math.ceil(8 * D_MODEL / 3 / 256)) * 256
ALPHA = (2 * DEPTH) ** 0.25
LN_EPS = 1e-5
LORA_PAD = 128
LORA_IN_WIDTH = 512

LANES = 128
SUBLANES = 8
VMEM_LIMIT_BYTES = 56 * 1024 * 1024

BF16 = jnp.bfloat16
F32 = jnp.float32
NT_DIMS = (((1,), (1,)), ((), ()))
TN_DIMS = (((0,), (0,)), ((), ()))


def _dot(a, b):
    return jnp.dot(a.astype(BF16), b.astype(BF16), preferred_element_type=F32)


def _dot_nt(a, b):
    return lax.dot_general(a.astype(BF16), b.astype(BF16), NT_DIMS, preferred_element_type=F32)


def _dot_tn(a, b):
    return lax.dot_general(a.astype(BF16), b.astype(BF16), TN_DIMS, preferred_element_type=F32)


def _split_dot(x, m):
    hi = x.astype(BF16)
    lo = (x - hi.astype(F32)).astype(BF16)
    return (jnp.dot(hi, m, preferred_element_type=F32)
            + jnp.dot(lo, m, preferred_element_type=F32))


def _head_ones():
    p = lax.broadcasted_iota(jnp.int32, (LANES, LANES), 0)
    q = lax.broadcasted_iota(jnp.int32, (LANES, LANES), 1)
    return ((p // RWKV_HEAD) == (q // RWKV_HEAD)).astype(BF16)


def _cparams(*sem):
    return pltpu.CompilerParams(dimension_semantics=sem, vmem_limit_bytes=VMEM_LIMIT_BYTES)


def _proj_kernel(*refs, lerp):
    if lerp:
        x_ref, xp_ref, mu_ref, w_ref, o_ref = refs
        x = x_ref[...]
        lhs = x + (xp_ref[...] - x) * mu_ref[0]
    else:
        x_ref, w_ref, o_ref = refs
        lhs = x_ref[...]
    o_ref[...] = jnp.dot(lhs.astype(BF16), w_ref[...], preferred_element_type=F32).astype(o_ref.dtype)


def _proj(x, w, layer, *, tm, tn, xprev=None, mu=None, groups=None, out_dtype=F32, name):
    M, K = x.shape
    N = w.shape[2]
    assert M % tm == 0 and N % tn == 0
    lerp = xprev is not None
    if lerp:
        gid = jnp.asarray(groups, jnp.int32)
        assert len(groups) == N // tn
        grid_spec = pltpu.PrefetchScalarGridSpec(
            num_scalar_prefetch=1, grid=(M // tm, N // tn),
            in_specs=[pl.BlockSpec((tm, K), lambda i, j, g: (i, 0)),
                      pl.BlockSpec((tm, K), lambda i, j, g: (i, 0)),
                      pl.BlockSpec((1, 1, K), lambda i, j, g: (g[j], 0, 0)),
                      pl.BlockSpec((None, K, tn), lambda i, j, g: (layer, 0, j))],
            out_specs=pl.BlockSpec((tm, tn), lambda i, j, g: (i, j)))
        kern = lambda g_ref, *refs: _proj_kernel(*refs, lerp=True)
        args = (gid, x, xprev, mu, w)
    else:
        grid_spec = pl.GridSpec(
            grid=(M // tm, N // tn),
            in_specs=[pl.BlockSpec((tm, K), lambda i, j: (i, 0)),
                      pl.BlockSpec((None, K, tn), lambda i, j: (layer, 0, j))],
            out_specs=pl.BlockSpec((tm, tn), lambda i, j: (i, j)))
        kern = functools.partial(_proj_kernel, lerp=False)
        args = (x, w)
    return pl.pallas_call(
        kern, grid_spec=grid_spec, out_shape=jax.ShapeDtypeStruct((M, N), out_dtype),
        compiler_params=_cparams("parallel", "arbitrary"), name=name)(*args)


def _softplus(z):
    return jnp.maximum(z, 0.0) + jnp.log1p(jnp.exp(-jnp.abs(z)))


def _rwkv_prep_kernel(hl_ref, k_ref, w2_ref, a2_ref, g2_ref, w0_ref, a0_ref, kk_ref, ka_ref,
                      lw_ref, kp_ref, kn_ref, bv_ref, g_ref):
    hl = hl_ref[...]
    hw = jnp.tanh(hl[:, 0:LORA_PAD])
    ha = hl[:, LORA_PAD:2 * LORA_PAD]
    hg = jax.nn.sigmoid(hl[:, 2 * LORA_PAD:])
    w_log = -_softplus(-(w0_ref[...] + _dot(hw, w2_ref[...]))) - 0.5
    lw_ref[...] = -jnp.exp(w_log)
    agate = jax.nn.sigmoid(a0_ref[...] + _dot(ha, a2_ref[...]))
    g_ref[...] = _dot(hg, g2_ref[...])
    k = k_ref[...]
    kkr = k * kk_ref[...]
    ss = _split_dot(kkr * kkr, _head_ones())
    kn = kkr / jnp.maximum(jnp.sqrt(ss), 1e-12)
    kn_ref[...] = kn
    bv_ref[...] = kn * agate
    kp_ref[...] = k * (1.0 + (agate - 1.0) * ka_ref[...])


def _rwkv_prep(hl, proj, w2p, a2p, g2, w0, a0, k_k, k_a, *, tm):
    M = hl.shape[0]
    nslab = TOK_WIDTH // LANES
    kbase = TOK_WIDTH // LANES
    row = lambda i, s: (i, s)
    col = lambda i, s: (0, s)
    outs = [jax.ShapeDtypeStruct((M, TOK_WIDTH), F32)] * 5
    return pl.pallas_call(
        _rwkv_prep_kernel, grid=(M // tm, nslab),
        in_specs=[pl.BlockSpec((tm, LORA_IN_WIDTH), lambda i, s: (i, 0)),
                  pl.BlockSpec((tm, LANES), lambda i, s: (i, kbase + s)),
                  pl.BlockSpec((LORA_PAD, LANES), col),
                  pl.BlockSpec((LORA_PAD, LANES), col),
                  pl.BlockSpec((2 * LORA_PAD, LANES), col),
                  pl.BlockSpec((1, LANES), col), pl.BlockSpec((1, LANES), col),
                  pl.BlockSpec((1, LANES), col), pl.BlockSpec((1, LANES), col)],
        out_specs=[pl.BlockSpec((tm, LANES), row)] * 5,
        out_shape=outs, compiler_params=_cparams("parallel", "arbitrary"),
        name="rwkv_prep")(hl, proj, w2p, a2p, g2, w0, a0, k_k, k_a)


def _each(f, *lists):
    return [f(*a) for a in zip(*lists)]


def _gn_gate(y, r, kp, v, g, rk, gg, gb, ones):
    inv_n = 1.0 / RWKV_HEAD
    rows = y[0].shape[0]
    sums = _each(lambda y_, r_, k_, rk_: _split_dot(jnp.concatenate([y_, r_ * k_ * rk_], axis=0), ones),
                 y, r, kp, rk)
    d = _each(lambda y_, s_: y_ - s_[:rows] * inv_n, y, sums)
    var = _each(lambda d_: _split_dot(d_ * d_, ones) * inv_n, d)
    return _each(lambda d_, var_, gg_, gb_, s_, v_, g_:
                 (d_ * lax.rsqrt(var_ + GN_EPS) * gg_ + gb_ + s_[rows:] * v_) * g_,
                 d, var, gg, gb, sums, v, g)


def _gn_gate_kernel(y_ref, r_ref, kp_ref, v_ref, g_ref, rk_ref, gg_ref, gb_ref, o_ref):
    refs = (y_ref, r_ref, kp_ref, v_ref, g_ref, rk_ref, gg_ref, gb_ref)
    o_ref[...] = _gn_gate(*[[t[...]] for t in refs], _head_ones())[0].astype(o_ref.dtype)


def _gn_gate_rows(y, proj, kp, g, rk, gn_g, gn_b, *, tm):
    M = y.shape[0]
    nslab = TOK_WIDTH // LANES
    row = lambda i, s: (i, s)
    col = lambda i, s: (0, s)
    blk = pl.BlockSpec((tm, LANES), row)
    cblk = pl.BlockSpec((1, LANES), col)
    return pl.pallas_call(
        _gn_gate_kernel, grid=(M // tm, nslab),
        in_specs=[blk, blk, blk, pl.BlockSpec((tm, LANES), lambda i, s: (i, 2 * nslab + s)), blk,
                  cblk, cblk, cblk],
        out_specs=blk, out_shape=jax.ShapeDtypeStruct((M, TOK_WIDTH), BF16),
        compiler_params=_cparams("parallel", "arbitrary"), name="gn_gate")(
            y, proj, kp, proj, g, rk, gn_g, gn_b)


WKV_CHUNK = 64


def _wkv_masks():
    n = 2 * WKV_CHUNK
    p = lax.broadcasted_iota(jnp.int32, (n, n), 0)
    q = lax.broadcasted_iota(jnp.int32, (n, n), 1)
    same = lambda b: (p // b) == (q // b)
    pt, qt = p % WKV_CHUNK, q % WKV_CHUNK
    s8, s16, s32, s64 = same(8), same(16), same(32), same(WKV_CHUNK)
    return dict(strict=s64 & (pt > qt), incl=s64 & (pt >= qt), s8=s8,
                e16=s16 & ~s8, e32=s32 & ~s16, e64=s64 & ~s32,
                eye=(p == q).astype(F32))


def _wkv_pairs(r, lw, k, v, kn, bv, S, mk, tri, lane_lo):
    stack = lambda x: jnp.concatenate([jnp.where(lane_lo, x, 0.0), jnp.where(lane_lo, 0.0, x)], axis=0)
    n = 2 * WKV_CHUNK
    c = _each(lambda t: _split_dot_left(tri, t), lw)
    c_last = _each(lambda t: t[WKV_CHUNK - 1:WKV_CHUNK, :], c)
    e_out = _each(lambda t: jnp.exp(-t), c)
    e_end = _each(lambda t, tl: jnp.exp(tl - t), c, c_last)
    ah = _each(lambda kn_, c_, lw_: stack(-kn_ * jnp.exp(c_ - lw_)), kn, c, lw)
    rh = _each(lambda r_, c_: stack(r_ * jnp.exp(c_)), r, c)
    bh = _each(lambda b_, e_: stack(b_ * e_), bv, e_out)
    kh = _each(lambda k_, e_: stack(k_ * e_), k, e_out)
    bbar = _each(lambda b_, e_: stack(b_ * e_), bv, e_end)
    kbar = _each(lambda k_, e_: stack(k_ * e_), k, e_end)
    vs = _each(stack, v)
    gm = _each(lambda a_, r_, b_, k_: _dot_nt(jnp.concatenate([a_, r_], axis=0),
                                              jnp.concatenate([b_, k_], axis=0)), ah, rh, bh, kh)
    a_ab = _each(lambda g_: jnp.where(mk["strict"], g_[:n, :n], 0.0), gm)
    a_ak = _each(lambda g_: jnp.where(mk["strict"], g_[:n, n:], 0.0), gm)
    l_rb = _each(lambda g_: jnp.where(mk["incl"], g_[n:, :n], 0.0), gm)
    l_rk = _each(lambda g_: jnp.where(mk["incl"], g_[n:, n:], 0.0), gm)
    d1 = _each(lambda a_: jnp.where(mk["s8"], a_, 0.0), a_ab)
    x = _each(lambda d_: mk["eye"] + d_, d1)
    d2 = _each(lambda d_: _dot(d_, d_), d1)
    x = _each(lambda x_, d_: x_ + _dot(x_, d_), x, d2)
    d4 = _each(lambda d_: _dot(d_, d_), d2)
    x = _each(lambda x_, d_: x_ + _dot(x_, d_), x, d4)
    for lvl in ("e16", "e32", "e64"):
        ex = _each(lambda a_, x_: _dot(jnp.where(mk[lvl], a_, 0.0), x_), a_ab, x)
        x = _each(lambda x_, e_: x_ + _dot(x_, e_), x, ex)
    av = _each(_dot, a_ak, vs)
    tw = _each(lambda x_, a_, v_: _dot(x_, jnp.concatenate([a_, v_], axis=1)), x, ah, av)
    lwm = _each(_dot, l_rb, tw)
    lv = _each(_dot, l_rk, vs)
    qm = _each(lambda r_, l_: r_ + l_[:, :n], rh, lwm)
    y0 = _each(lambda l_, v_: l_[:, n:] + v_, lwm, lv)
    mt = _each(lambda t_, b_: _dot_tn(t_[:, :n], b_), tw, bbar)
    nt = _each(lambda t_, b_, v_, k_: _dot_tn(t_[:, n:], b_) + _dot_tn(v_, k_), tw, bbar, vs, kbar)
    ys = _each(lambda q_, s_, y_: _dot_nt(q_, s_) + y_, qm, S, y0)
    y = _each(lambda t: t[:WKV_CHUNK, :] + t[WKV_CHUNK:, :], ys)
    s_new = _each(lambda s_, cl, m_, n_: s_ * jnp.exp(cl) + _dot(s_, m_) + n_, S, c_last, mt, nt)
    return y, s_new


def _split_dot_left(m, x):
    hi = x.astype(BF16)
    lo = (x - hi.astype(F32)).astype(BF16)
    return (jnp.dot(m, hi, preferred_element_type=F32)
            + jnp.dot(m, lo, preferred_element_type=F32))


def _wkv_kernel(r_ref, v_ref, lw_ref, kp_ref, kn_ref, bv_ref, g_ref, rk_ref, gg_ref, gb_ref,
                o_ref, sout_ref, s_scr, *, pairs):
    c_idx = pl.program_id(1)

    @pl.when(c_idx == 0)
    def _():
        s_scr[...] = jnp.zeros_like(s_scr)

    mk = _wkv_masks()
    ti = lax.broadcasted_iota(jnp.int32, (WKV_CHUNK, WKV_CHUNK), 0)
    tj = lax.broadcasted_iota(jnp.int32, (WKV_CHUNK, WKV_CHUNK), 1)
    tri = (ti >= tj).astype(BF16)
    lane_lo = lax.broadcasted_iota(jnp.int32, (WKV_CHUNK, LANES), 1) < RWKV_HEAD
    slabs = [slice(p * LANES, (p + 1) * LANES) for p in range(pairs)]
    cols = lambda ref: [ref[:, sl] for sl in slabs]
    r, kp, v = cols(r_ref), cols(kp_ref), cols(v_ref)
    y, s_new = _wkv_pairs(r, cols(lw_ref), kp, v, cols(kn_ref), cols(bv_ref),
                          [s_scr[p] for p in range(pairs)], mk, tri, lane_lo)
    for p in range(pairs):
        s_scr[p] = s_new[p]
    tok = _gn_gate(y, r, kp, v, cols(g_ref), cols(rk_ref), cols(gg_ref), cols(gb_ref), _head_ones())
    for sl, t in zip(slabs, tok):
        o_ref[:, sl] = t.astype(o_ref.dtype)

    @pl.when(c_idx == pl.num_programs(1) - 1)
    def _():
        sout_ref[...] = s_scr[...]


def _wkv_prompt(proj, lw, kp, kn, bv, g, rk, gn_g, gn_b, *, pairs):
    T = proj.shape[0]
    npair = TOK_WIDTH // LANES
    assert npair % pairs == 0 and T % WKV_CHUNK == 0
    wb = pairs * LANES
    ngrp = npair // pairs
    nb = TOK_WIDTH // wb
    tok = lambda off: pl.BlockSpec((WKV_CHUNK, wb), lambda gi, c, off=off: (c, off + gi))
    cst = pl.BlockSpec((1, wb), lambda gi, c: (0, gi))
    return pl.pallas_call(
        functools.partial(_wkv_kernel, pairs=pairs), grid=(ngrp, T // WKV_CHUNK),
        in_specs=[tok(0), tok(2 * nb), tok(0), tok(0), tok(0), tok(0), tok(0), cst, cst, cst],
        out_specs=[tok(0), pl.BlockSpec((pairs, LANES, LANES), lambda gi, c: (gi, 0, 0))],
        out_shape=[jax.ShapeDtypeStruct((T, TOK_WIDTH), BF16),
                   jax.ShapeDtypeStruct((npair, LANES, LANES), F32)],
        scratch_shapes=[pltpu.VMEM((pairs, LANES, LANES), F32)],
        compiler_params=_cparams("parallel", "arbitrary"), name="wkv_chunked")(
            proj, proj, lw, kp, kn, bv, g, rk, gn_g, gn_b)


def _wkv_step_kernel(s_ref, kn_ref, lw_ref, bv_ref, kp_ref, r_ref, vt_ref, so_ref, y_ref, *, bs):
    lane_h = lax.broadcasted_iota(jnp.int32, (RWKV_HEAD, RWKV_HEADS), 1)

    heads = list(range(RWKV_HEADS))

    def sample(b, carry):
        vt = vt_ref[b]
        sa = [-jnp.sum(s_ref[b, h] * kn_ref[b, h], axis=1, keepdims=True) for h in heads]
        s_new = [s_ref[b, h] * jnp.exp(lw_ref[b, h]) + sa[h] * bv_ref[b, h] + vt[:, h:h + 1] * kp_ref[b, h]
                 for h in heads]
        for h in heads:
            so_ref[b, h] = s_new[h]
        ycol = [jnp.sum(s_new[h] * r_ref[b, h], axis=1, keepdims=True) for h in heads]
        yacc = jnp.zeros((RWKV_HEAD, RWKV_HEADS), F32)
        for h in heads:
            yacc = jnp.where(lane_h == h, ycol[h], yacc)
        y_ref[b] = yacc
        return carry

    lax.fori_loop(0, bs, sample, 0)


def _wkv_step(state, layer, kn, lw, bv, kp, r, vt, *, bs):
    B = state.shape[1]
    sshape = (bs, RWKV_HEADS, RWKV_HEAD, RWKV_HEAD)
    sblk = pl.BlockSpec(sshape, lambda i: (i, 0, 0, 0))
    vblk = pl.BlockSpec((bs, RWKV_HEADS, 1, RWKV_HEAD), lambda i: (i, 0, 0, 0))
    tblk = pl.BlockSpec((bs, RWKV_HEAD, RWKV_HEADS), lambda i: (i, 0, 0))
    return pl.pallas_call(
        functools.partial(_wkv_step_kernel, bs=bs), grid=(B // bs,),
        in_specs=[pl.BlockSpec((None,) + sshape, lambda i: (layer, i, 0, 0, 0)),
                  vblk, vblk, vblk, vblk, vblk, tblk],
        out_specs=[sblk, tblk],
        out_shape=[jax.ShapeDtypeStruct(state.shape[1:], F32),
                   jax.ShapeDtypeStruct((B, RWKV_HEAD, RWKV_HEADS), F32)],
        compiler_params=_cparams("parallel"), name="wkv_step")(state, kn, lw, bv, kp, r, vt)


def _softmax_rows(s):
    m = jnp.max(s, axis=-1, keepdims=True)
    e = jnp.exp(s - m)
    return e / jnp.sum(e, axis=-1, keepdims=True)


def _mem_attn_kernel(q_ref, k_ref, v_ref, o_ref):
    for h in range(MEM_HEADS):
        sl = slice(h * MEM_HEAD, (h + 1) * MEM_HEAD)
        p = _softmax_rows(_dot_nt(q_ref[:, sl], k_ref[:, sl]) * MEM_SCALE)
        o_ref[:, sl] = _dot(p, v_ref[:, sl]).astype(o_ref.dtype)


def _mem_attn(proj, qblk, kv, *, tm):
    M = proj.shape[0]
    return pl.pallas_call(
        _mem_attn_kernel, grid=(M // tm,),
        in_specs=[pl.BlockSpec((tm, MEM_WIDTH), lambda i: (i, qblk)),
                  pl.BlockSpec((MEM_TOKENS, MEM_WIDTH), lambda i: (0, 0)),
                  pl.BlockSpec((MEM_TOKENS, MEM_WIDTH), lambda i: (0, 1))],
        out_specs=pl.BlockSpec((tm, MEM_WIDTH), lambda i: (i, 0)),
        out_shape=jax.ShapeDtypeStruct((M, MEM_WIDTH), BF16),
        compiler_params=_cparams("parallel"), name="mem_attn")(proj, kv, kv)


def _mem_attn_step_kernel(q_ref, k_ref, v_ref, o_ref, *, bs):
    def sample(b, carry):
        p = _softmax_rows(_dot_nt(q_ref[b], k_ref[b]) * MEM_SCALE)
        o_ref[b] = _dot(p, v_ref[b])
        return carry

    lax.fori_loop(0, bs, sample, 0)


def _mem_attn_step(q_bd, mk, mv, layer, *, bs):
    B = q_bd.shape[0]
    qblk = pl.BlockSpec((bs, SUBLANES, MEM_WIDTH), lambda i: (i, 0, 0))
    cblk = pl.BlockSpec((None, bs, MEM_TOKENS, MEM_WIDTH), lambda i: (layer, i, 0, 0))
    return pl.pallas_call(
        functools.partial(_mem_attn_step_kernel, bs=bs), grid=(B // bs,),
        in_specs=[qblk, cblk, cblk], out_specs=qblk,
        out_shape=jax.ShapeDtypeStruct(q_bd.shape, F32),
        compiler_params=_cparams("parallel"), name="mem_attn_step")(q_bd, mk, mv)


def _out_ln_kernel(*refs, ka, two):
    if two:
        a_ref, b_ref, w_ref, res_ref, g_ref, beta_ref, of_ref, ob_ref, acc_ref = refs
    else:
        a_ref, w_ref, res_ref, g_ref, beta_ref, of_ref, ob_ref, acc_ref = refs
    k = pl.program_id(1)

    @pl.when(k == 0)
    def _():
        acc_ref[...] = jnp.zeros_like(acc_ref)

    if two:
        @pl.when(k < ka)
        def _():
            acc_ref[...] += jnp.dot(a_ref[...], w_ref[...], preferred_element_type=F32)

        @pl.when(k >= ka)
        def _():
            acc_ref[...] += jnp.dot(b_ref[...], w_ref[...], preferred_element_type=F32)
    else:
        acc_ref[...] += jnp.dot(a_ref[...], w_ref[...], preferred_element_type=F32)

    @pl.when(k == pl.num_programs(1) - 1)
    def _():
        z = ALPHA * res_ref[...] + acc_ref[...]
        mu = jnp.mean(z, axis=-1, keepdims=True)
        d = z - mu
        var = jnp.mean(d * d, axis=-1, keepdims=True)
        out = d * lax.rsqrt(var + LN_EPS) * g_ref[...] + beta_ref[...]
        of_ref[...] = out
        ob_ref[...] = out.astype(BF16)


def _out_ln(a, b, w, layer, res, g, beta, *, tm, tk):
    M, Ka = a.shape
    Kb = 0 if b is None else b.shape[1]
    N = w.shape[2]
    assert Ka % tk == 0 and Kb % tk == 0 and w.shape[1] == Ka + Kb
    ka, nk = Ka // tk, (Ka + Kb) // tk
    two = b is not None
    in_specs = [pl.BlockSpec((tm, tk), lambda i, k: (i, jnp.minimum(k, ka - 1)))]
    args = [a]
    if two:
        in_specs.append(pl.BlockSpec((tm, tk), lambda i, k: (i, jnp.maximum(k - ka, 0))))
        args.append(b)
    in_specs += [pl.BlockSpec((None, tk, N), lambda i, k: (layer, k, 0)),
                 pl.BlockSpec((tm, N), lambda i, k: (i, 0)),
                 pl.BlockSpec((1, N), lambda i, k: (0, 0)),
                 pl.BlockSpec((1, N), lambda i, k: (0, 0))]
    args += [w, res, g, beta]
    oblk = pl.BlockSpec((tm, N), lambda i, k: (i, 0))
    return pl.pallas_call(
        functools.partial(_out_ln_kernel, ka=ka, two=two), grid=(M // tm, nk),
        in_specs=in_specs, out_specs=[oblk, oblk],
        out_shape=[jax.ShapeDtypeStruct((M, N), F32), jax.ShapeDtypeStruct((M, N), BF16)],
        scratch_shapes=[pltpu.VMEM((tm, N), F32)],
        compiler_params=_cparams("parallel", "arbitrary"), name="out_ln")(*args)


def _ffn_up_kernel(x_ref, wg_ref, wu_ref, o_ref):
    x = x_ref[...]
    gate = jnp.dot(x, wg_ref[...].astype(BF16), preferred_element_type=F32)
    up = jnp.dot(x, wu_ref[...].astype(BF16), preferred_element_type=F32)
    o_ref[...] = (gate * jax.nn.sigmoid(gate) * up).astype(o_ref.dtype)


def _ffn_up(x, wg, wu, layer, *, tm, tn):
    M, K = x.shape
    N = wg.shape[2]
    wblk = pl.BlockSpec((None, K, tn), lambda i, j: (layer, 0, j))
    return pl.pallas_call(
        _ffn_up_kernel, grid=(M // tm, N // tn),
        in_specs=[pl.BlockSpec((tm, K), lambda i, j: (i, 0)), wblk, wblk],
        out_specs=pl.BlockSpec((tm, tn), lambda i, j: (i, j)),
        out_shape=jax.ShapeDtypeStruct((M, N), BF16),
        compiler_params=_cparams("parallel", "arbitrary"), name="ffn_up")(x, wg, wu)


def _rope_kernel(x_ref, cos_ref, sin_ref, o_ref):
    x = x_ref[...]
    lane = lax.broadcasted_iota(jnp.int32, x.shape, 1)
    first_half = (lane % SWA_HEAD) < (SWA_HEAD // 2)
    partner = jnp.where(first_half, pltpu.roll(x, LANES - SWA_HEAD // 2, 1),
                        pltpu.roll(x, SWA_HEAD // 2, 1))
    o_ref[...] = x * cos_ref[...] + partner * sin_ref[...]


def _rope(proj, cos, sin_signed, *, tm):
    M = proj.shape[0]
    width = TOK_WIDTH + SWA_KV_WIDTH
    blk = pl.BlockSpec((tm, LANES), lambda i, s: (i, s))
    tab = pl.BlockSpec((tm, LANES), lambda i, s: (i, 0))
    return pl.pallas_call(
        _rope_kernel, grid=(M // tm, width // LANES),
        in_specs=[blk, tab, tab], out_specs=blk,
        out_shape=jax.ShapeDtypeStruct((M, width), F32),
        compiler_params=_cparams("parallel", "arbitrary"), name="rope")(proj, cos, sin_signed)


def _sink_column(sink_ref, base, rows_per_head, nheads):
    rows = rows_per_head * nheads
    hid = lax.broadcasted_iota(jnp.int32, (rows, 1), 0) // rows_per_head
    col = jnp.zeros((rows, 1), F32)
    for j in range(nheads):
        col = jnp.where(hid == j, sink_ref[base + j], col)
    return col


def _sink_softmax(s, sink):
    m = jnp.maximum(jnp.max(s, axis=-1, keepdims=True), sink)
    p = jnp.exp(s - m)
    return p / (jnp.sum(p, axis=-1, keepdims=True) + jnp.exp(sink - m))


def _swa_kernel(sink_ref, q_ref, kc_ref, kp_ref, vc_ref, vp_ref, cosc_ref, sinc_ref, cosp_ref, sinp_ref,
                o_ref, krot_ref):
    n = pl.program_id(0)
    nslab_q = TOK_WIDTH // LANES
    slab = lambda ref, s: ref[:, s * LANES:(s + 1) * LANES]
    cos_c, sin_c, cos_p, sin_p = cosc_ref[...], sinc_ref[...], cosp_ref[...], sinp_ref[...]
    lane = lax.broadcasted_iota(jnp.int32, (BLOCK, LANES), 1)
    first_half = (lane % SWA_HEAD) < (SWA_HEAD // 2)
    lo = lane < SWA_HEAD

    def rope(x, cos, sin):
        partner = jnp.where(first_half, pltpu.roll(x, LANES - SWA_HEAD // 2, 1),
                            pltpu.roll(x, SWA_HEAD // 2, 1))
        return x * cos + partner * sin

    kv_slabs = SWA_KV_WIDTH // LANES
    k_cur = [rope(slab(kc_ref, j), cos_c, sin_c) for j in range(kv_slabs)]
    k_prev = [rope(slab(kp_ref, j), cos_p, sin_p) for j in range(kv_slabs)]
    for j in range(kv_slabs):
        krot_ref[:, j * LANES:(j + 1) * LANES] = k_cur[j]
    lane2 = lax.broadcasted_iota(jnp.int32, (2 * BLOCK, LANES), 1)
    kd, vd = [], []
    for g in range(SWA_KV_HEADS):
        j, half = divmod(g, 2)
        keep = (lane2 < SWA_HEAD) if half == 0 else (lane2 >= SWA_HEAD)
        dup = lambda t: jnp.where(keep, t, pltpu.roll(t, SWA_HEAD, 1)).astype(BF16)
        kd.append(dup(jnp.concatenate([k_prev[j], k_cur[j]], axis=0)))
        vd.append(dup(jnp.concatenate([slab(vp_ref, j), slab(vc_ref, j)], axis=0)))
    qi = lax.broadcasted_iota(jnp.int32, (2 * BLOCK, 2 * BLOCK), 0) % BLOCK
    si = lax.broadcasted_iota(jnp.int32, (2 * BLOCK, 2 * BLOCK), 1)
    valid = (si > qi) & (si <= qi + WINDOW) & ((n > 0) | (si >= BLOCK))
    row_lo = lax.broadcasted_iota(jnp.int32, (2 * BLOCK, 1), 0) < BLOCK
    slabs = list(range(nslab_q))
    kv_of = [(2 * s) // SWA_GROUP for s in slabs]
    q = [rope(slab(q_ref, s), cos_c, sin_c) for s in slabs]
    qs = [jnp.concatenate([jnp.where(lo, t, 0.0), jnp.where(lo, 0.0, t)], axis=0).astype(BF16) for t in q]
    sc = [lax.dot_general(t, kd[g], NT_DIMS, preferred_element_type=F32) * SWA_SCALE
          for t, g in zip(qs, kv_of)]
    sc = [jnp.where(valid, t, -jnp.inf) for t in sc]
    p = [_sink_softmax(t, jnp.where(row_lo, sink_ref[2 * s], sink_ref[2 * s + 1])) for t, s in zip(sc, slabs)]
    o = [jnp.dot(t.astype(BF16), vd[g], preferred_element_type=F32) for t, g in zip(p, kv_of)]
    for s, t in zip(slabs, o):
        o_ref[:, s * LANES:(s + 1) * LANES] = jnp.where(lo, t[:BLOCK], t[BLOCK:]).astype(o_ref.dtype)


def _swa_prompt(proj, cos, sin_signed, sinks):
    T = proj.shape[0]
    kblk, vblk = TOK_WIDTH // SWA_KV_WIDTH, TOK_WIDTH // SWA_KV_WIDTH + 1
    prev = lambda n: jnp.maximum(n - 1, 0)
    kv_spec = lambda blk, row: pl.BlockSpec((BLOCK, SWA_KV_WIDTH), lambda n: (row(n), blk))
    tab = lambda row: pl.BlockSpec((BLOCK, LANES), lambda n: (row(n), 0))
    cur = lambda n: n
    return pl.pallas_call(
        _swa_kernel, grid=(T // BLOCK,),
        in_specs=[pl.BlockSpec(memory_space=pltpu.SMEM),
                  pl.BlockSpec((BLOCK, TOK_WIDTH), lambda n: (n, 0)),
                  kv_spec(kblk, cur), kv_spec(kblk, prev), kv_spec(vblk, cur), kv_spec(vblk, prev),
                  tab(cur), tab(cur), tab(prev), tab(prev)],
        out_specs=[pl.BlockSpec((BLOCK, TOK_WIDTH), lambda n: (n, 0)),
                   pl.BlockSpec((BLOCK, SWA_KV_WIDTH), lambda n: (n, 0))],
        out_shape=[jax.ShapeDtypeStruct((T, TOK_WIDTH), BF16),
                   jax.ShapeDtypeStruct((T, SWA_KV_WIDTH), F32)],
        compiler_params=_cparams("arbitrary"), name="swa_banded")(
            sinks, proj, proj, proj, proj, proj, cos, sin_signed, cos, sin_signed)


def _swa_step_kernel(sink_ref, q_ref, k_ref, v_ref, o_ref, *, bs):
    sink = _sink_column(sink_ref, 0, 1, SWA_Q_HEADS)

    def sample(b, carry):
        p = _sink_softmax(_dot_nt(q_ref[b], k_ref[b]) * SWA_SCALE, sink)
        o_ref[b] = _dot(p, v_ref[b])
        return carry

    lax.fori_loop(0, bs, sample, 0)


def _swa_step(q_bd, kc, vc, sinks, *, bs):
    B = q_bd.shape[0]
    qblk = pl.BlockSpec((bs, SWA_Q_HEADS, SWA_KV_WIDTH), lambda i: (i, 0, 0))
    cblk = pl.BlockSpec((bs, WINDOW, SWA_KV_WIDTH), lambda i: (i, 0, 0))
    return pl.pallas_call(
        functools.partial(_swa_step_kernel, bs=bs), grid=(B // bs,),
        in_specs=[pl.BlockSpec(memory_space=pltpu.SMEM), qblk, cblk, cblk], out_specs=qblk,
        out_shape=jax.ShapeDtypeStruct(q_bd.shape, F32),
        compiler_params=_cparams("parallel"), name="swa_step")(sinks, q_bd, kc, vc)


ROW_TILE = 512
FFN_ROW_TILE = 1024
COL_TILE = 512
STEP_BATCH = 8
WKV_STEP_BATCH = 4
WKV_PAIRS = 12
RWKV_PROJ_GROUPS = [0, 0, 0, 1, 1, 1, 2, 2, 2, 3]
RWKV_LORA_GROUPS = [0, 1, 2, 2]


def _row_tile(m):
    return ROW_TILE if m % ROW_TILE == 0 else m


def _ffn_row_tile(m):
    return FFN_ROW_TILE if m % FFN_ROW_TILE == 0 else _row_tile(m)


def _pad_rows(w, rows):
    return jnp.pad(w, ((0, rows - w.shape[0]), (0, 0)))


def _pad_cols(w, cols):
    return jnp.pad(w, ((0, 0), (0, cols - w.shape[1])))


def _rwkv_weights(w_in, mu, w1, w2, a1, a2, g1, g2):
    zeros = jnp.zeros_like(mu[0])
    w_lora = jnp.concatenate([_pad_cols(w1, LORA_PAD), _pad_cols(a1, LORA_PAD), g1], axis=1)
    return dict(
        w_in=w_in.astype(BF16),
        mu_proj=jnp.stack([mu[0], mu[2], mu[3], zeros])[:, None, :],
        mu_lora=jnp.stack([mu[1], mu[4], mu[5]])[:, None, :],
        w_lora=w_lora.astype(BF16)[None],
        w2=_pad_rows(w2, LORA_PAD).astype(BF16), a2=_pad_rows(a2, LORA_PAD).astype(BF16),
        g2=g2.astype(BF16))


def _rwkv_inputs(x, xprev, W, layer, w0, a0, k_k, k_a):
    tm = _row_tile(x.shape[0])
    proj = _proj(x, W["w_in"], layer, tm=tm, tn=COL_TILE, xprev=xprev, mu=W["mu_proj"],
                 groups=RWKV_PROJ_GROUPS, name="rwkv_proj")
    hl = _proj(x, W["w_lora"], 0, tm=tm, tn=LANES, xprev=xprev, mu=W["mu_lora"],
               groups=RWKV_LORA_GROUPS, name="rwkv_lora_in")
    row = lambda t: t[None, :]
    return (proj,) + tuple(_rwkv_prep(hl, proj, W["w2"], W["a2"], W["g2"], row(w0), row(a0),
                                      row(k_k), row(k_a), tm=tm))


def _unblock_state(s_bd):
    n = RWKV_HEAD
    return jnp.stack([s_bd[:, :n, :n], s_bd[:, n:, n:]], axis=1).reshape(RWKV_HEADS, n, n)


def _rope_tables(pos):
    half = SWA_HEAD // 2
    inv = ROPE_THETA ** (-jnp.arange(half, dtype=F32) / half)
    ang = pos.astype(F32)[:, None] * inv[None, :]
    cos, sin = jnp.cos(ang), jnp.sin(ang)
    reps = LANES // SWA_HEAD
    return jnp.tile(cos, (1, 2 * reps)), jnp.tile(jnp.concatenate([-sin, sin], axis=1), (1, reps))


def _post_mixer(tok, mo, x, sw, layer, tm):
    row = lambda t: t[layer][None, :]
    x1f, x1b = _out_ln(tok, mo, sw["w_out"], layer, x, row(sw["ln1_g"]), row(sw["ln1_b"]),
                       tm=tm, tk=COL_TILE)
    hff = _ffn_up(x1b, sw["w_gate"], sw["w_up"], layer, tm=_ffn_row_tile(x.shape[0]), tn=COL_TILE)
    return _out_ln(hff, None, sw["w_down"], layer, x1f, row(sw["ln2_g"]), row(sw["ln2_b"]),
                   tm=tm, tk=COL_TILE)


def kernel(x_prompt, x_sample, mem_prompt, cache_mem_k, cache_mem_v, state_rwkv_shift, state_rwkv_wkv, cache_swa_k, cache_swa_v, w_in_rwkv, rwkv_mu, rwkv_w0, rwkv_w1, rwkv_w2, rwkv_a0, rwkv_a1, rwkv_a2, rwkv_g1, rwkv_g2, rwkv_k_k, rwkv_k_a, rwkv_r_k, rwkv_gn_g, rwkv_gn_b, w_in_swa, swa_sinks, w_mem_kv, w_out, ln1_g, ln1_b, w_gate, w_up, w_down, ln2_g, ln2_b):
    assert DEPTH == 2 and x_prompt.shape[0] == 1 and x_sample.shape[1] == 1
    T = x_prompt.shape[1]
    B = x_sample.shape[0]
    past_len = T
    row = lambda t: t[None, :]
    shared = dict(w_out=w_out.astype(BF16), ln1_g=ln1_g, ln1_b=ln1_b, w_gate=w_gate, w_up=w_up,
                  w_down=w_down.astype(BF16), ln2_g=ln2_g, ln2_b=ln2_b)
    RW = _rwkv_weights(w_in_rwkv, rwkv_mu[0], rwkv_w1[0], rwkv_w2[0], rwkv_a1[0], rwkv_a2[0],
                       rwkv_g1[0], rwkv_g2[0])
    rk, gn_g, gn_b = row(rwkv_r_k[0].reshape(-1)), row(rwkv_gn_g[0]), row(rwkv_gn_b[0])
    w_swa = w_in_swa.astype(BF16)
    w_kv = w_mem_kv.astype(BF16)
    sinks = swa_sinks[0]
    q_blk_rwkv = 3 * TOK_WIDTH // MEM_WIDTH
    q_blk_swa = (TOK_WIDTH + 2 * SWA_KV_WIDTH) // MEM_WIDTH

    xp = x_prompt[0]
    tm = _row_tile(T)
    kv = [_proj(mem_prompt[0], w_kv, i, tm=MEM_TOKENS, tn=COL_TILE, name="mem_kv") for i in range(DEPTH)]
    prompt_mem_k = jnp.stack([t[:, :MEM_WIDTH] for t in kv]).reshape(DEPTH, 1, MEM_TOKENS, MEM_HEADS, MEM_HEAD)
    prompt_mem_v = jnp.stack([t[:, MEM_WIDTH:] for t in kv]).reshape(DEPTH, 1, MEM_TOKENS, MEM_HEADS, MEM_HEAD)

    xprev = jnp.concatenate([jnp.zeros((1, D_MODEL), xp.dtype), xp[:-1]], axis=0)
    proj, lw, kp, kn, bv, g = _rwkv_inputs(xp, xprev, RW, 0, rwkv_w0[0], rwkv_a0[0], rwkv_k_k[0], rwkv_k_a[0])
    tok, s_bd = _wkv_prompt(proj, lw, kp, kn, bv, g, rk, gn_g, gn_b, pairs=WKV_PAIRS)
    mo = _mem_attn(proj, q_blk_rwkv, kv[0], tm=tm)
    xf, xb = _post_mixer(tok, mo, xp, shared, 0, tm)
    prompt_shift = xp[-1][None, None, :]
    prompt_wkv = _unblock_state(s_bd)[None, None]

    proj = _proj(xb, w_swa, 0, tm=tm, tn=COL_TILE, name="swa_proj")
    cos, sin = _rope_tables(jnp.arange(T))
    tok, k_rot = _swa_prompt(proj, cos, sin, sinks)
    mo = _mem_attn(proj, q_blk_swa, kv[1], tm=tm)
    y_prompt, _ = _post_mixer(tok, mo, xf, shared, 1, tm)
    v_last = proj[T - WINDOW:, TOK_WIDTH + SWA_KV_WIDTH:TOK_WIDTH + 2 * SWA_KV_WIDTH]
    prompt_swa_k = k_rot[T - WINDOW:].reshape(1, 1, WINDOW, SWA_KV_HEADS, SWA_HEAD)
    prompt_swa_v = v_last.reshape(1, 1, WINDOW, SWA_KV_HEADS, SWA_HEAD)

    xs = x_sample[:, 0]
    tms = _row_tile(B)
    slab_mask = (jnp.arange(MEM_WIDTH)[None, :] // MEM_HEAD == jnp.arange(SUBLANES)[:, None]).astype(F32)

    mem_k = cache_mem_k.reshape(DEPTH, B, MEM_TOKENS, MEM_WIDTH)
    mem_v = cache_mem_v.reshape(DEPTH, B, MEM_TOKENS, MEM_WIDTH)

    def mem_step(q, layer):
        out = _mem_attn_step(q[:, None, :] * slab_mask[None], mem_k, mem_v, layer, bs=STEP_BATCH)
        return jnp.sum(out * slab_mask[None], axis=1).astype(BF16)

    proj, lw, kp, kn, bv, g = _rwkv_inputs(xs, state_rwkv_shift[0], RW, 0, rwkv_w0[0], rwkv_a0[0],
                                           rwkv_k_k[0], rwkv_k_a[0])
    rows4 = lambda t: t.reshape(B, RWKV_HEADS, 1, RWKV_HEAD)
    vt = proj[:, 2 * TOK_WIDTH:3 * TOK_WIDTH].reshape(B, RWKV_HEADS, RWKV_HEAD).transpose(0, 2, 1)
    s_new, y_t = _wkv_step(state_rwkv_wkv, 0, rows4(kn), rows4(lw), rows4(bv), rows4(kp),
                           rows4(proj[:, :TOK_WIDTH]), vt, bs=WKV_STEP_BATCH)
    y = y_t.transpose(0, 2, 1).reshape(B, TOK_WIDTH)
    tok = _gn_gate_rows(y, proj, kp, g, rk, gn_g, gn_b, tm=tms)
    mo = mem_step(proj[:, 3 * TOK_WIDTH:], 0)
    xf, xb = _post_mixer(tok, mo, xs, shared, 0, tms)
    sample_shift = xs[None]
    sample_wkv = s_new[None]

    proj = _proj(xb, w_swa, 0, tm=tms, tn=COL_TILE, name="swa_proj")
    cos, sin = _rope_tables(jnp.full((B,), past_len))
    qk = _rope(proj, cos, sin, tm=tms)
    k_new = qk[:, TOK_WIDTH:].reshape(B, 1, SWA_KV_HEADS, SWA_HEAD)
    v_new = proj[:, TOK_WIDTH + SWA_KV_WIDTH:TOK_WIDTH + 2 * SWA_KV_WIDTH].reshape(B, 1, SWA_KV_HEADS, SWA_HEAD)
    kc = jnp.concatenate([cache_swa_k[0][:, 1:], k_new], axis=1)
    vc = jnp.concatenate([cache_swa_v[0][:, 1:], v_new], axis=1)
    kv_of_head = (jnp.arange(SWA_Q_HEADS)[:, None] // SWA_GROUP == jnp.arange(SWA_KV_HEADS)[None, :]).astype(F32)
    q3 = qk[:, :TOK_WIDTH].reshape(B, SWA_Q_HEADS, 1, SWA_HEAD)
    q_bd = (q3 * kv_of_head[None, :, :, None]).reshape(B, SWA_Q_HEADS, SWA_KV_WIDTH)
    o = _swa_step(q_bd, kc.reshape(B, WINDOW, SWA_KV_WIDTH), vc.reshape(B, WINDOW, SWA_KV_WIDTH),
                  sinks, bs=STEP_BATCH)
    o = jnp.sum(o.reshape(B, SWA_Q_HEADS, SWA_KV_HEADS, SWA_HEAD) * kv_of_head[None, :, :, None], axis=2)
    tok = o.reshape(B, TOK_WIDTH).astype(BF16)
    mo = mem_step(proj[:, TOK_WIDTH + 2 * SWA_KV_WIDTH:], 1)
    y_sample, _ = _post_mixer(tok, mo, xf, shared, 1, tms)

    return (y_prompt[None], y_sample[:, None, :], prompt_mem_k, prompt_mem_v, prompt_shift, prompt_wkv,
            prompt_swa_k, prompt_swa_v, sample_shift, sample_wkv, kc[None], vc[None])
```

```python
import functools
import math

import jax
import jax.numpy as jnp
from jax import lax
from jax.experimental import pallas as pl
from jax.experimental.pallas import tpu as pltpu

D_MODEL = 2048
DEPTH = 2
MEM_WIDTH = D_MODEL // 4
TOK_WIDTH = D_MODEL - MEM_WIDTH
RWKV_HEAD = 64
RWKV_HEADS = TOK_WIDTH // RWKV_HEAD
GN_EPS = RWKV_HEAD * 1e-5
SWA_HEAD = 64
SWA_Q_HEADS = TOK_WIDTH // SWA_HEAD
SWA_KV_HEADS = 4
SWA_GROUP = SWA_Q_HEADS // SWA_KV_HEADS
SWA_KV_WIDTH = SWA_KV_HEADS * SWA_HEAD
WINDOW = 128
BLOCK = 128
SWA_SCALE = SWA_HEAD ** -0.5
ROPE_THETA = 10000.0
MEM_TOKENS = 256
MEM_HEADS = 4
MEM_HEAD = MEM_WIDTH // MEM_HEADS
MEM_SCALE = MEM_HEAD ** -0.5
FFN_HIDDEN = int(math.ceil(8 * D_MODEL / 3 / 256)) * 256
ALPHA = (2 * DEPTH) ** 0.25
LN_EPS = 1e-5
LORA_PAD = 128
LORA_IN_WIDTH = 512

LANES = 128
SUBLANES = 8
VMEM_LIMIT_BYTES = 56 * 1024 * 1024

BF16 = jnp.bfloat16
F32 = jnp.float32
NT_DIMS = (((1,), (1,)), ((), ()))
TN_DIMS = (((0,), (0,)), ((), ()))


def _dot(a, b):
    return jnp.dot(a.astype(BF16), b.astype(BF16), preferred_element_type=F32)


def _dot_nt(a, b):
    return lax.dot_general(a.astype(BF16), b.astype(BF16), NT_DIMS, preferred_element_type=F32)


def _dot_tn(a, b):
    return lax.dot_general(a.astype(BF16), b.astype(BF16), TN_DIMS, preferred_element_type=F32)


def _split_dot(x, m):
    hi = x.astype(BF16)
    lo = (x - hi.astype(F32)).astype(BF16)
    return (jnp.dot(hi, m, preferred_element_type=F32)
            + jnp.dot(lo, m, preferred_element_type=F32))


def _head_ones():
    p = lax.broadcasted_iota(jnp.int32, (LANES, LANES), 0)
    q = lax.broadcasted_iota(jnp.int32, (LANES, LANES), 1)
    return ((p // RWKV_HEAD) == (q // RWKV_HEAD)).astype(BF16)


def _cparams(*sem):
    return pltpu.CompilerParams(dimension_semantics=sem, vmem_limit_bytes=VMEM_LIMIT_BYTES)


def _proj_kernel(x_ref, w_ref, o_ref):
    o_ref[...] = jnp.dot(x_ref[...].astype(BF16), w_ref[...], preferred_element_type=F32)


def _proj(x, w, layer, *, tm, tn, name):
    M, K = x.shape
    N = w.shape[2]
    assert M % tm == 0 and N % tn == 0
    return pl.pallas_call(
        _proj_kernel, grid=(M // tm, N // tn),
        in_specs=[pl.BlockSpec((tm, K), lambda i, j: (i, 0)),
                  pl.BlockSpec((None, K, tn), lambda i, j: (layer, 0, j))],
        out_specs=pl.BlockSpec((tm, tn), lambda i, j: (i, j)),
        out_shape=jax.ShapeDtypeStruct((M, N), F32),
        compiler_params=_cparams("parallel", "arbitrary"), name=name)(x, w)


MIX_R, MIX_W, MIX_K, MIX_V, MIX_A, MIX_G, MIX_NONE = range(7)
RWKV_IN_WIDTH = 3 * TOK_WIDTH + MEM_WIDTH + LORA_IN_WIDTH
RWKV_IN_TILE = LORA_IN_WIDTH
RWKV_IN_MIX = ([MIX_R] * (TOK_WIDTH // RWKV_IN_TILE) + [MIX_K] * (TOK_WIDTH // RWKV_IN_TILE)
               + [MIX_V] * (TOK_WIDTH // RWKV_IN_TILE) + [MIX_NONE] * (MEM_WIDTH // RWKV_IN_TILE))
LORA_PARTS = ((MIX_W, 0, LORA_PAD), (MIX_A, LORA_PAD, 2 * LORA_PAD), (MIX_G, 2 * LORA_PAD, LORA_IN_WIDTH))


def _rwkv_in_kernel(mix_ref, x_ref, xp_ref, mu_ref, w_ref, o_ref, lhs_ref, *, shift):
    i, j = pl.program_id(0), pl.program_id(1)
    tm = x_ref.shape[0]

    @pl.when(j == 0)
    def _():
        x = x_ref[...]
        if shift:
            first = jnp.where(i > 0, xp_ref[SUBLANES - 1:SUBLANES, :], 0.0)
            rowid = lax.broadcasted_iota(jnp.int32, (tm, 1), 0)
            xprev = jnp.where(rowid == 0, first, pltpu.roll(x, 1, 0))
        else:
            xprev = xp_ref[...]
        d = xprev - x
        for m in range(MIX_NONE):
            lhs_ref[m] = (x + d * mu_ref[m]).astype(BF16)
        lhs_ref[MIX_NONE] = x.astype(BF16)

    nproj = pl.num_programs(1) - 1

    @pl.when(j < nproj)
    def _():
        o_ref[...] = jnp.dot(lhs_ref[mix_ref[j]], w_ref[...], preferred_element_type=F32)

    @pl.when(j == nproj)
    def _():
        for m, lo, hi in LORA_PARTS:
            o_ref[:, lo:hi] = jnp.dot(lhs_ref[m], w_ref[:, lo:hi], preferred_element_type=F32)


def _rwkv_in(x, xprev, mu, w, layer, *, tm):
    M, K = x.shape
    N = w.shape[2]
    tn = RWKV_IN_TILE
    assert M % tm == 0 and N == RWKV_IN_WIDTH and tm % SUBLANES == 0
    shift = xprev is None
    if shift:
        rows8 = tm // SUBLANES
        xp_spec = pl.BlockSpec((SUBLANES, K), lambda i, j, m: (jnp.maximum(i * rows8 - 1, 0), 0))
        xprev = x
    else:
        xp_spec = pl.BlockSpec((tm, K), lambda i, j, m: (i, 0))
    mix = jnp.asarray(RWKV_IN_MIX + [MIX_NONE], jnp.int32)
    grid_spec = pltpu.PrefetchScalarGridSpec(
        num_scalar_prefetch=1, grid=(M // tm, N // tn),
        in_specs=[pl.BlockSpec((tm, K), lambda i, j, m: (i, 0)), xp_spec,
                  pl.BlockSpec((MIX_NONE, 1, K), lambda i, j, m: (0, 0, 0)),
                  pl.BlockSpec((None, K, tn), lambda i, j, m: (layer, 0, j))],
        out_specs=pl.BlockSpec((tm, tn), lambda i, j, m: (i, j)),
        scratch_shapes=[pltpu.VMEM((MIX_NONE + 1, tm, K), BF16)])
    return pl.pallas_call(
        functools.partial(_rwkv_in_kernel, shift=shift), grid_spec=grid_spec,
        out_shape=jax.ShapeDtypeStruct((M, N), F32),
        compiler_params=_cparams("parallel", "arbitrary"), name="rwkv_in")(mix, x, xprev, mu, w)


def _softplus(z):
    return jnp.maximum(z, 0.0) + jnp.log1p(jnp.exp(-jnp.abs(z)))


def _each(f, *lists):
    return [f(*a) for a in zip(*lists)]


def _lora_hidden(hl):
    return jnp.tanh(hl[:, 0:LORA_PAD]), hl[:, LORA_PAD:2 * LORA_PAD], jax.nn.sigmoid(hl[:, 2 * LORA_PAD:])


def _prep_slabs(k, wl, al, w0, a0, k_k, k_a, ones):
    lw = _each(lambda wl_, w0_: -jnp.exp(-_softplus(-(w0_ + wl_)) - 0.5), wl, w0)
    agate = _each(lambda al_, a0_: jax.nn.sigmoid(a0_ + al_), al, a0)
    kkr = _each(lambda k_, c_: k_ * c_, k, k_k)
    ss = _each(lambda t: _split_dot(t * t, ones), kkr)
    kn = _each(lambda t, s_: t / jnp.maximum(jnp.sqrt(s_), 1e-12), kkr, ss)
    bv = _each(lambda n_, a_: n_ * a_, kn, agate)
    kp = _each(lambda k_, a_, c_: k_ * (1.0 + (a_ - 1.0) * c_), k, agate, k_a)
    return lw, kp, kn, bv


def _rwkv_prep_kernel(hl_ref, k_ref, w2_ref, a2_ref, g2_ref, w0_ref, a0_ref, kk_ref, ka_ref,
                      lw_ref, kp_ref, kn_ref, bv_ref, g_ref):
    hw, ha, hg = _lora_hidden(hl_ref[...])
    g_ref[...] = _dot(hg, g2_ref[...])
    outs = _prep_slabs([k_ref[...]], [_dot(hw, w2_ref[...])], [_dot(ha, a2_ref[...])], [w0_ref[...]],
                       [a0_ref[...]], [kk_ref[...]], [ka_ref[...]], _head_ones())
    for ref, val in zip((lw_ref, kp_ref, kn_ref, bv_ref), outs):
        ref[...] = val[0]


def _rwkv_prep(proj, w2p, a2p, g2, w0, a0, k_k, k_a, *, tm):
    M = proj.shape[0]
    nslab = TOK_WIDTH // LANES
    kbase = TOK_WIDTH // LANES
    lora_blk = (RWKV_IN_WIDTH - LORA_IN_WIDTH) // LORA_IN_WIDTH
    row = lambda i, s: (i, s)
    col = lambda i, s: (0, s)
    outs = [jax.ShapeDtypeStruct((M, TOK_WIDTH), F32)] * 5
    return pl.pallas_call(
        _rwkv_prep_kernel, grid=(M // tm, nslab),
        in_specs=[pl.BlockSpec((tm, LORA_IN_WIDTH), lambda i, s: (i, lora_blk)),
                  pl.BlockSpec((tm, LANES), lambda i, s: (i, kbase + s)),
                  pl.BlockSpec((LORA_PAD, LANES), col),
                  pl.BlockSpec((LORA_PAD, LANES), col),
                  pl.BlockSpec((2 * LORA_PAD, LANES), col),
                  pl.BlockSpec((1, LANES), col), pl.BlockSpec((1, LANES), col),
                  pl.BlockSpec((1, LANES), col), pl.BlockSpec((1, LANES), col)],
        out_specs=[pl.BlockSpec((tm, LANES), row)] * 5,
        out_shape=outs, compiler_params=_cparams("parallel", "arbitrary"),
        name="rwkv_prep")(proj, proj, w2p, a2p, g2, w0, a0, k_k, k_a)


def _gn_gate(y, r, kp, v, g, rk, gg, gb, ones):
    inv_n = 1.0 / RWKV_HEAD
    rows = y[0].shape[0]
    sums = _each(lambda y_, r_, k_, rk_: _split_dot(jnp.concatenate([y_, r_ * k_ * rk_], axis=0), ones),
                 y, r, kp, rk)
    d = _each(lambda y_, s_: y_ - s_[:rows] * inv_n, y, sums)
    var = _each(lambda d_: _split_dot(d_ * d_, ones) * inv_n, d)
    return _each(lambda d_, var_, gg_, gb_, s_, v_, g_:
                 (d_ * lax.rsqrt(var_ + GN_EPS) * gg_ + gb_ + s_[rows:] * v_) * g_,
                 d, var, gg, gb, sums, v, g)


def _gn_gate_kernel(y_ref, r_ref, kp_ref, v_ref, g_ref, rk_ref, gg_ref, gb_ref, o_ref):
    refs = (y_ref, r_ref, kp_ref, v_ref, g_ref, rk_ref, gg_ref, gb_ref)
    o_ref[...] = _gn_gate(*[[t[...]] for t in refs], _head_ones())[0].astype(o_ref.dtype)


def _gn_gate_rows(y, proj, kp, g, rk, gn_g, gn_b, *, tm):
    M = y.shape[0]
    nslab = TOK_WIDTH // LANES
    row = lambda i, s: (i, s)
    col = lambda i, s: (0, s)
    blk = pl.BlockSpec((tm, LANES), row)
    cblk = pl.BlockSpec((1, LANES), col)
    return pl.pallas_call(
        _gn_gate_kernel, grid=(M // tm, nslab),
        in_specs=[blk, blk, blk, pl.BlockSpec((tm, LANES), lambda i, s: (i, 2 * nslab + s)), blk,
                  cblk, cblk, cblk],
        out_specs=blk, out_shape=jax.ShapeDtypeStruct((M, TOK_WIDTH), BF16),
        compiler_params=_cparams("parallel", "arbitrary"), name="gn_gate")(
            y, proj, kp, proj, g, rk, gn_g, gn_b)


WKV_CHUNK = 64


def _wkv_masks():
    n = 2 * WKV_CHUNK
    p = lax.broadcasted_iota(jnp.int32, (n, n), 0)
    q = lax.broadcasted_iota(jnp.int32, (n, n), 1)
    same = lambda b: (p // b) == (q // b)
    pt, qt = p % WKV_CHUNK, q % WKV_CHUNK
    s8, s16, s32, s64 = same(8), same(16), same(32), same(WKV_CHUNK)
    return dict(strict=s64 & (pt > qt), incl=s64 & (pt >= qt), s8=s8,
                e16=s16 & ~s8, e32=s32 & ~s16, e64=s64 & ~s32,
                eye=(p == q).astype(F32))


def _wkv_pairs(r, lw, k, v, kn, bv, S, mk, tri, lane_lo):
    stack = lambda x: jnp.concatenate([jnp.where(lane_lo, x, 0.0), jnp.where(lane_lo, 0.0, x)], axis=0)
    n = 2 * WKV_CHUNK
    c = _each(lambda t: _split_dot_left(tri, t), lw)
    c_last = _each(lambda t: t[WKV_CHUNK - 1:WKV_CHUNK, :], c)
    e_out = _each(lambda t: jnp.exp(-t), c)
    e_end = _each(lambda t, tl: jnp.exp(tl - t), c, c_last)
    ah = _each(lambda kn_, c_, lw_: stack(-kn_ * jnp.exp(c_ - lw_)), kn, c, lw)
    rh = _each(lambda r_, c_: stack(r_ * jnp.exp(c_)), r, c)
    bh = _each(lambda b_, e_: stack(b_ * e_), bv, e_out)
    kh = _each(lambda k_, e_: stack(k_ * e_), k, e_out)
    bbar = _each(lambda b_, e_: stack(b_ * e_), bv, e_end)
    kbar = _each(lambda k_, e_: stack(k_ * e_), k, e_end)
    vs = _each(stack, v)
    gm = _each(lambda a_, r_, b_, k_: _dot_nt(jnp.concatenate([a_, r_], axis=0),
                                              jnp.concatenate([b_, k_], axis=0)), ah, rh, bh, kh)
    a_ab = _each(lambda g_: jnp.where(mk["strict"], g_[:n, :n], 0.0), gm)
    a_ak = _each(lambda g_: jnp.where(mk["strict"], g_[:n, n:], 0.0), gm)
    l_rb = _each(lambda g_: jnp.where(mk["incl"], g_[n:, :n], 0.0), gm)
    l_rk = _each(lambda g_: jnp.where(mk["incl"], g_[n:, n:], 0.0), gm)
    d1 = _each(lambda a_: jnp.where(mk["s8"], a_, 0.0), a_ab)
    x = _each(lambda d_: mk["eye"] + d_, d1)
    d2 = _each(lambda d_: _dot(d_, d_), d1)
    x = _each(lambda x_, d_: x_ + _dot(x_, d_), x, d2)
    d4 = _each(lambda d_: _dot(d_, d_), d2)
    x = _each(lambda x_, d_: x_ + _dot(x_, d_), x, d4)
    for lvl in ("e16", "e32", "e64"):
        ex = _each(lambda a_, x_: _dot(jnp.where(mk[lvl], a_, 0.0), x_), a_ab, x)
        x = _each(lambda x_, e_: x_ + _dot(x_, e_), x, ex)
    av = _each(_dot, a_ak, vs)
    tw = _each(lambda x_, a_, v_: _dot(x_, jnp.concatenate([a_, v_], axis=1)), x, ah, av)
    lwm = _each(_dot, l_rb, tw)
    lv = _each(_dot, l_rk, vs)
    qm = _each(lambda r_, l_: r_ + l_[:, :n], rh, lwm)
    y0 = _each(lambda l_, v_: l_[:, n:] + v_, lwm, lv)
    mt = _each(lambda t_, b_: _dot_tn(t_[:, :n], b_), tw, bbar)
    nt = _each(lambda t_, b_, v_, k_: _dot_tn(t_[:, n:], b_) + _dot_tn(v_, k_), tw, bbar, vs, kbar)
    ys = _each(lambda q_, s_, y_: _dot_nt(q_, s_) + y_, qm, S, y0)
    y = _each(lambda t: t[:WKV_CHUNK, :] + t[WKV_CHUNK:, :], ys)
    s_new = _each(lambda s_, cl, m_, n_: s_ * jnp.exp(cl) + _dot(s_, m_) + n_, S, c_last, mt, nt)
    return y, s_new


def _split_dot_left(m, x):
    hi = x.astype(BF16)
    lo = (x - hi.astype(F32)).astype(BF16)
    return (jnp.dot(m, hi, preferred_element_type=F32)
            + jnp.dot(m, lo, preferred_element_type=F32))


WKV_PAIRS = TOK_WIDTH // LANES


def _wkv_kernel(r_ref, k_ref, v_ref, hl_ref, w2_ref, a2_ref, g2_ref, w0_ref, a0_ref, kk_ref, ka_ref,
                rk_ref, gg_ref, gb_ref, o_ref, sout_ref, s_scr):
    c_idx = pl.program_id(0)

    @pl.when(c_idx == 0)
    def _():
        s_scr[...] = jnp.zeros_like(s_scr)

    mk = _wkv_masks()
    ti = lax.broadcasted_iota(jnp.int32, (WKV_CHUNK, WKV_CHUNK), 0)
    tj = lax.broadcasted_iota(jnp.int32, (WKV_CHUNK, WKV_CHUNK), 1)
    tri = (ti >= tj).astype(BF16)
    lane_lo = lax.broadcasted_iota(jnp.int32, (WKV_CHUNK, LANES), 1) < RWKV_HEAD
    ones = _head_ones()
    slabs = [slice(p * LANES, (p + 1) * LANES) for p in range(WKV_PAIRS)]
    cols = lambda t: [t[:, sl] for sl in slabs]
    hw, ha, hg = _lora_hidden(hl_ref[...])
    wl, al, g = _dot(hw, w2_ref[...]), _dot(ha, a2_ref[...]), _dot(hg, g2_ref[...])
    r, v = cols(r_ref), cols(v_ref)
    lw, kp, kn, bv = _prep_slabs(cols(k_ref), cols(wl), cols(al), cols(w0_ref), cols(a0_ref),
                                 cols(kk_ref), cols(ka_ref), ones)
    y, s_new = _wkv_pairs(r, lw, kp, v, kn, bv, [s_scr[p] for p in range(WKV_PAIRS)], mk, tri, lane_lo)
    for p in range(WKV_PAIRS):
        s_scr[p] = s_new[p]
    tok = _gn_gate(y, r, kp, v, cols(g), cols(rk_ref), cols(gg_ref), cols(gb_ref), ones)
    for sl, t in zip(slabs, tok):
        o_ref[:, sl] = t.astype(o_ref.dtype)

    @pl.when(c_idx == pl.num_programs(0) - 1)
    def _():
        sout_ref[...] = s_scr[...]


def _wkv_prompt(proj, w2p, a2p, g2, consts):
    T = proj.shape[0]
    assert T % WKV_CHUNK == 0
    lora_blk = (RWKV_IN_WIDTH - LORA_IN_WIDTH) // LORA_IN_WIDTH
    tok = lambda blk: pl.BlockSpec((WKV_CHUNK, TOK_WIDTH), lambda c, blk=blk: (c, blk))
    full = lambda a: pl.BlockSpec(a.shape, lambda c: (0,) * a.ndim)
    weights = (w2p, a2p, g2) + tuple(consts)
    return pl.pallas_call(
        _wkv_kernel, grid=(T // WKV_CHUNK,),
        in_specs=[tok(0), tok(1), tok(2), pl.BlockSpec((WKV_CHUNK, LORA_IN_WIDTH), lambda c: (c, lora_blk))]
        + [full(a) for a in weights],
        out_specs=[tok(0), pl.BlockSpec((WKV_PAIRS, LANES, LANES), lambda c: (0, 0, 0))],
        out_shape=[jax.ShapeDtypeStruct((T, TOK_WIDTH), BF16),
                   jax.ShapeDtypeStruct((WKV_PAIRS, LANES, LANES), F32)],
        scratch_shapes=[pltpu.VMEM((WKV_PAIRS, LANES, LANES), F32)],
        compiler_params=_cparams("arbitrary"), name="wkv_chunked")(proj, proj, proj, proj, *weights)


def _wkv_step_kernel(s_ref, kn_ref, lw_ref, bv_ref, kp_ref, r_ref, vt_ref, so_ref, y_ref, *, bs):
    lane_h = lax.broadcasted_iota(jnp.int32, (RWKV_HEAD, RWKV_HEADS), 1)

    heads = list(range(RWKV_HEADS))

    def sample(b, carry):
        vt = vt_ref[b]
        sa = [-jnp.sum(s_ref[b, h] * kn_ref[b, h], axis=1, keepdims=True) for h in heads]
        s_new = [s_ref[b, h] * jnp.exp(lw_ref[b, h]) + sa[h] * bv_ref[b, h] + vt[:, h:h + 1] * kp_ref[b, h]
                 for h in heads]
        for h in heads:
            so_ref[b, h] = s_new[h]
        ycol = [jnp.sum(s_new[h] * r_ref[b, h], axis=1, keepdims=True) for h in heads]
        yacc = jnp.zeros((RWKV_HEAD, RWKV_HEADS), F32)
        for h in heads:
            yacc = jnp.where(lane_h == h, ycol[h], yacc)
        y_ref[b] = yacc
        return carry

    lax.fori_loop(0, bs, sample, 0)


def _wkv_step(state, layer, kn, lw, bv, kp, r, vt, *, bs):
    B = state.shape[1]
    sshape = (bs, RWKV_HEADS, RWKV_HEAD, RWKV_HEAD)
    sblk = pl.BlockSpec(sshape, lambda i: (i, 0, 0, 0))
    vblk = pl.BlockSpec((bs, RWKV_HEADS, 1, RWKV_HEAD), lambda i: (i, 0, 0, 0))
    tblk = pl.BlockSpec((bs, RWKV_HEAD, RWKV_HEADS), lambda i: (i, 0, 0))
    return pl.pallas_call(
        functools.partial(_wkv_step_kernel, bs=bs), grid=(B // bs,),
        in_specs=[pl.BlockSpec((None,) + sshape, lambda i: (layer, i, 0, 0, 0)),
                  vblk, vblk, vblk, vblk, vblk, tblk],
        out_specs=[sblk, tblk],
        out_shape=[jax.ShapeDtypeStruct(state.shape[1:], F32),
                   jax.ShapeDtypeStruct((B, RWKV_HEAD, RWKV_HEADS), F32)],
        compiler_params=_cparams("parallel"), name="wkv_step")(state, kn, lw, bv, kp, r, vt)


def _softmax_rows(s):
    m = jnp.max(s, axis=-1, keepdims=True)
    e = jnp.exp(s - m)
    return e / jnp.sum(e, axis=-1, keepdims=True)


def _mem_attn_kernel(q_ref, k_ref, v_ref, o_ref):
    for h in range(MEM_HEADS):
        sl = slice(h * MEM_HEAD, (h + 1) * MEM_HEAD)
        p = _softmax_rows(_dot_nt(q_ref[:, sl], k_ref[:, sl]) * MEM_SCALE)
        o_ref[:, sl] = _dot(p, v_ref[:, sl]).astype(o_ref.dtype)


def _mem_attn(proj, qblk, kv, *, tm):
    M = proj.shape[0]
    return pl.pallas_call(
        _mem_attn_kernel, grid=(M // tm,),
        in_specs=[pl.BlockSpec((tm, MEM_WIDTH), lambda i: (i, qblk)),
                  pl.BlockSpec((MEM_TOKENS, MEM_WIDTH), lambda i: (0, 0)),
                  pl.BlockSpec((MEM_TOKENS, MEM_WIDTH), lambda i: (0, 1))],
        out_specs=pl.BlockSpec((tm, MEM_WIDTH), lambda i: (i, 0)),
        out_shape=jax.ShapeDtypeStruct((M, MEM_WIDTH), BF16),
        compiler_params=_cparams("parallel"), name="mem_attn")(proj, kv, kv)


def _mem_attn_step_kernel(q_ref, k_ref, v_ref, o_ref, *, bs):
    rows = MEM_TOKENS * MEM_HEADS
    col_head = lax.broadcasted_iota(jnp.int32, (SUBLANES, rows), 1) % MEM_HEADS
    row_head = lax.broadcasted_iota(jnp.int32, (SUBLANES, rows), 0) % MEM_HEADS
    own = col_head == row_head

    def sample(b, carry):
        s = _dot_nt(q_ref[b], k_ref[b]) * MEM_SCALE
        o_ref[b] = _dot(_softmax_rows(jnp.where(own, s, -jnp.inf)), v_ref[b])
        return carry

    lax.fori_loop(0, bs, sample, 0)


def _mem_attn_step(q, mk, mv, layer, *, bs):
    B = q.shape[0]
    qblk = pl.BlockSpec((bs, SUBLANES, MEM_HEAD), lambda i: (i, 0, 0))
    cblk = pl.BlockSpec((None, bs, MEM_TOKENS * MEM_HEADS, MEM_HEAD), lambda i: (layer, i, 0, 0))
    return pl.pallas_call(
        functools.partial(_mem_attn_step_kernel, bs=bs), grid=(B // bs,),
        in_specs=[qblk, cblk, cblk], out_specs=qblk,
        out_shape=jax.ShapeDtypeStruct(q.shape, F32),
        compiler_params=_cparams("parallel"), name="mem_attn_step")(q, mk, mv)


def _deepnorm_ln(res, h, g, beta):
    z = ALPHA * res + h
    mu = jnp.mean(z, axis=-1, keepdims=True)
    d = z - mu
    var = jnp.mean(d * d, axis=-1, keepdims=True)
    return d * lax.rsqrt(var + LN_EPS) * g + beta


def _out_ln_kernel(*refs, widths):
    n = len(widths)
    lhs, (w_ref, res_ref, g_ref, beta_ref, of_ref, ob_ref) = refs[:n], refs[n:]
    h, off = None, 0
    for a_ref, kw in zip(lhs, widths):
        part = jnp.dot(a_ref[...], w_ref[off:off + kw, :], preferred_element_type=F32)
        h = part if h is None else h + part
        off += kw
    out = _deepnorm_ln(res_ref[...], h, g_ref[...], beta_ref[...])
    of_ref[...] = out
    ob_ref[...] = out.astype(BF16)


def _out_ln_tiled_kernel(a_ref, w_ref, res_ref, g_ref, beta_ref, of_ref, ob_ref, acc_ref):
    k = pl.program_id(1)

    @pl.when(k == 0)
    def _():
        acc_ref[...] = jnp.zeros_like(acc_ref)

    acc_ref[...] += jnp.dot(a_ref[...], w_ref[...], preferred_element_type=F32)

    @pl.when(k == pl.num_programs(1) - 1)
    def _():
        out = _deepnorm_ln(res_ref[...], acc_ref[...], g_ref[...], beta_ref[...])
        of_ref[...] = out
        ob_ref[...] = out.astype(BF16)


def _out_ln(pieces, w, layer, res, g, beta, *, tm, tk=None):
    M = pieces[0].shape[0]
    K, N = w.shape[1], w.shape[2]
    widths = tuple(p.shape[1] for p in pieces)
    assert sum(widths) == K and M % tm == 0
    out_shape = [jax.ShapeDtypeStruct((M, N), F32), jax.ShapeDtypeStruct((M, N), BF16)]
    if tk is None:
        row = lambda i: (i, 0)
        cst = lambda i: (0, 0)
        in_specs = ([pl.BlockSpec((tm, kw), row) for kw in widths]
                    + [pl.BlockSpec((None, K, N), lambda i: (layer, 0, 0)),
                       pl.BlockSpec((tm, N), row), pl.BlockSpec((1, N), cst), pl.BlockSpec((1, N), cst)])
        oblk = pl.BlockSpec((tm, N), row)
        return pl.pallas_call(
            functools.partial(_out_ln_kernel, widths=widths), grid=(M // tm,),
            in_specs=in_specs, out_specs=[oblk, oblk], out_shape=out_shape,
            compiler_params=_cparams("parallel"), name="out_ln")(*pieces, w, res, g, beta)
    assert len(pieces) == 1 and K % tk == 0
    row = lambda i, k: (i, 0)
    cst = lambda i, k: (0, 0)
    oblk = pl.BlockSpec((tm, N), row)
    return pl.pallas_call(
        _out_ln_tiled_kernel, grid=(M // tm, K // tk),
        in_specs=[pl.BlockSpec((tm, tk), lambda i, k: (i, k)),
                  pl.BlockSpec((None, tk, N), lambda i, k: (layer, k, 0)),
                  pl.BlockSpec((tm, N), row), pl.BlockSpec((1, N), cst), pl.BlockSpec((1, N), cst)],
        out_specs=[oblk, oblk], out_shape=out_shape,
        scratch_shapes=[pltpu.VMEM((tm, N), F32)],
        compiler_params=_cparams("parallel", "arbitrary"), name="ffn_down_ln")(
            pieces[0], w, res, g, beta)


def _ffn_up_kernel(x_ref, wg_ref, wu_ref, o_ref):
    x = x_ref[...]
    gate = jnp.dot(x, wg_ref[...].astype(BF16), preferred_element_type=F32)
    up = jnp.dot(x, wu_ref[...].astype(BF16), preferred_element_type=F32)
    o_ref[...] = (gate * jax.nn.sigmoid(gate) * up).astype(o_ref.dtype)


def _ffn_up(x, wg, wu, layer, *, tm, tn):
    M, K = x.shape
    N = wg.shape[2]
    wblk = pl.BlockSpec((None, K, tn), lambda i, j: (layer, 0, j))
    return pl.pallas_call(
        _ffn_up_kernel, grid=(M // tm, N // tn),
        in_specs=[pl.BlockSpec((tm, K), lambda i, j: (i, 0)), wblk, wblk],
        out_specs=pl.BlockSpec((tm, tn), lambda i, j: (i, j)),
        out_shape=jax.ShapeDtypeStruct((M, N), BF16),
        compiler_params=_cparams("parallel", "arbitrary"), name="ffn_up")(x, wg, wu)


def _rope_kernel(x_ref, cos_ref, sin_ref, o_ref):
    x = x_ref[...]
    lane = lax.broadcasted_iota(jnp.int32, x.shape, 1)
    first_half = (lane % SWA_HEAD) < (SWA_HEAD // 2)
    partner = jnp.where(first_half, pltpu.roll(x, LANES - SWA_HEAD // 2, 1),
                        pltpu.roll(x, SWA_HEAD // 2, 1))
    o_ref[...] = x * cos_ref[...] + partner * sin_ref[...]


def _rope(proj, cos, sin_signed, *, tm):
    M = proj.shape[0]
    width = TOK_WIDTH + SWA_KV_WIDTH
    blk = pl.BlockSpec((tm, LANES), lambda i, s: (i, s))
    tab = pl.BlockSpec((tm, LANES), lambda i, s: (i, 0))
    return pl.pallas_call(
        _rope_kernel, grid=(M // tm, width // LANES),
        in_specs=[blk, tab, tab], out_specs=blk,
        out_shape=jax.ShapeDtypeStruct((M, width), F32),
        compiler_params=_cparams("parallel", "arbitrary"), name="rope")(proj, cos, sin_signed)


def _sink_column(sink_ref, base, rows_per_head, nheads):
    rows = rows_per_head * nheads
    hid = lax.broadcasted_iota(jnp.int32, (rows, 1), 0) // rows_per_head
    col = jnp.zeros((rows, 1), F32)
    for j in range(nheads):
        col = jnp.where(hid == j, sink_ref[base + j], col)
    return col


def _sink_softmax(s, sink):
    m = jnp.maximum(jnp.max(s, axis=-1, keepdims=True), sink)
    p = jnp.exp(s - m)
    return p / (jnp.sum(p, axis=-1, keepdims=True) + jnp.exp(sink - m))


def _swa_kernel(sink_ref, q_ref, kc_ref, kp_ref, vc_ref, vp_ref, cosc_ref, sinc_ref, cosp_ref, sinp_ref,
                o_ref, krot_ref):
    n = pl.program_id(0)
    nslab_q = TOK_WIDTH // LANES
    slab = lambda ref, s: ref[:, s * LANES:(s + 1) * LANES]
    cos_c, sin_c, cos_p, sin_p = cosc_ref[...], sinc_ref[...], cosp_ref[...], sinp_ref[...]
    lane = lax.broadcasted_iota(jnp.int32, (BLOCK, LANES), 1)
    first_half = (lane % SWA_HEAD) < (SWA_HEAD // 2)
    lo = lane < SWA_HEAD

    def rope(x, cos, sin):
        partner = jnp.where(first_half, pltpu.roll(x, LANES - SWA_HEAD // 2, 1),
                            pltpu.roll(x, SWA_HEAD // 2, 1))
        return x * cos + partner * sin

    kv_slabs = SWA_KV_WIDTH // LANES
    k_cur = [rope(slab(kc_ref, j), cos_c, sin_c) for j in range(kv_slabs)]
    k_prev = [rope(slab(kp_ref, j), cos_p, sin_p) for j in range(kv_slabs)]
    for j in range(kv_slabs):
        krot_ref[:, j * LANES:(j + 1) * LANES] = k_cur[j]
    lane2 = lax.broadcasted_iota(jnp.int32, (2 * BLOCK, LANES), 1)
    kd, vd = [], []
    for g in range(SWA_KV_HEADS):
        j, half = divmod(g, 2)
        keep = (lane2 < SWA_HEAD) if half == 0 else (lane2 >= SWA_HEAD)
        dup = lambda t: jnp.where(keep, t, pltpu.roll(t, SWA_HEAD, 1)).astype(BF16)
        kd.append(dup(jnp.concatenate([k_prev[j], k_cur[j]], axis=0)))
        vd.append(dup(jnp.concatenate([slab(vp_ref, j), slab(vc_ref, j)], axis=0)))
    qi = lax.broadcasted_iota(jnp.int32, (2 * BLOCK, 2 * BLOCK), 0) % BLOCK
    si = lax.broadcasted_iota(jnp.int32, (2 * BLOCK, 2 * BLOCK), 1)
    valid = (si > qi) & (si <= qi + WINDOW) & ((n > 0) | (si >= BLOCK))
    row_lo = lax.broadcasted_iota(jnp.int32, (2 * BLOCK, 1), 0) < BLOCK
    slabs = list(range(nslab_q))
    kv_of = [(2 * s) // SWA_GROUP for s in slabs]
    q = [rope(slab(q_ref, s), cos_c, sin_c) for s in slabs]
    qs = [jnp.concatenate([jnp.where(lo, t, 0.0), jnp.where(lo, 0.0, t)], axis=0).astype(BF16) for t in q]
    sc = [lax.dot_general(t, kd[g], NT_DIMS, preferred_element_type=F32) * SWA_SCALE
          for t, g in zip(qs, kv_of)]
    sc = [jnp.where(valid, t, -jnp.inf) for t in sc]
    p = [_sink_softmax(t, jnp.where(row_lo, sink_ref[2 * s], sink_ref[2 * s + 1])) for t, s in zip(sc, slabs)]
    o = [jnp.dot(t.astype(BF16), vd[g], preferred_element_type=F32) for t, g in zip(p, kv_of)]
    for s, t in zip(slabs, o):
        o_ref[:, s * LANES:(s + 1) * LANES] = jnp.where(lo, t[:BLOCK], t[BLOCK:]).astype(o_ref.dtype)


def _swa_prompt(proj, cos, sin_signed, sinks):
    T = proj.shape[0]
    kblk, vblk = TOK_WIDTH // SWA_KV_WIDTH, TOK_WIDTH // SWA_KV_WIDTH + 1
    prev = lambda n: jnp.maximum(n - 1, 0)
    kv_spec = lambda blk, row: pl.BlockSpec((BLOCK, SWA_KV_WIDTH), lambda n: (row(n), blk))
    tab = lambda row: pl.BlockSpec((BLOCK, LANES), lambda n: (row(n), 0))
    cur = lambda n: n
    return pl.pallas_call(
        _swa_kernel, grid=(T // BLOCK,),
        in_specs=[pl.BlockSpec(memory_space=pltpu.SMEM),
                  pl.BlockSpec((BLOCK, TOK_WIDTH), lambda n: (n, 0)),
                  kv_spec(kblk, cur), kv_spec(kblk, prev), kv_spec(vblk, cur), kv_spec(vblk, prev),
                  tab(cur), tab(cur), tab(prev), tab(prev)],
        out_specs=[pl.BlockSpec((BLOCK, TOK_WIDTH), lambda n: (n, 0)),
                   pl.BlockSpec((BLOCK, SWA_KV_WIDTH), lambda n: (n, 0))],
        out_shape=[jax.ShapeDtypeStruct((T, TOK_WIDTH), BF16),
                   jax.ShapeDtypeStruct((T, SWA_KV_WIDTH), F32)],
        compiler_params=_cparams("arbitrary"), name="swa_banded")(
            sinks, proj, proj, proj, proj, proj, cos, sin_signed, cos, sin_signed)


def _swa_step_kernel(sink_ref, q_ref, k_ref, v_ref, o_ref, *, bs):
    sink = _sink_column(sink_ref, 0, 1, SWA_Q_HEADS)

    def sample(b, carry):
        p = _sink_softmax(_dot_nt(q_ref[b], k_ref[b]) * SWA_SCALE, sink)
        o_ref[b] = _dot(p, v_ref[b])
        return carry

    lax.fori_loop(0, bs, sample, 0)


def _swa_step(q_bd, kc, vc, sinks, *, bs):
    B = q_bd.shape[0]
    qblk = pl.BlockSpec((bs, SWA_Q_HEADS, SWA_KV_WIDTH), lambda i: (i, 0, 0))
    cblk = pl.BlockSpec((bs, WINDOW, SWA_KV_WIDTH), lambda i: (i, 0, 0))
    return pl.pallas_call(
        functools.partial(_swa_step_kernel, bs=bs), grid=(B // bs,),
        in_specs=[pl.BlockSpec(memory_space=pltpu.SMEM), qblk, cblk, cblk], out_specs=qblk,
        out_shape=jax.ShapeDtypeStruct(q_bd.shape, F32),
        compiler_params=_cparams("parallel"), name="swa_step")(sinks, q_bd, kc, vc)


ROW_TILE = 512
FFN_ROW_TILE = 1024
COL_TILE = 512
STEP_BATCH = 8
WKV_STEP_BATCH = 4
FFN_DOWN_K_TILE = FFN_HIDDEN // 4


def _row_tile(m):
    return ROW_TILE if m % ROW_TILE == 0 else m


def _ffn_row_tile(m):
    return FFN_ROW_TILE if m % FFN_ROW_TILE == 0 else _row_tile(m)


def _pad_rows(w, rows):
    return jnp.pad(w, ((0, rows - w.shape[0]), (0, 0)))


def _pad_cols(w, cols):
    return jnp.pad(w, ((0, 0), (0, cols - w.shape[1])))


def _rwkv_weights(w_in, mu, w1, w2, a1, a2, g1, g2):
    w_cat = jnp.concatenate([w_in[0], _pad_cols(w1[0], LORA_PAD), _pad_cols(a1[0], LORA_PAD), g1[0]], axis=1)
    return dict(w_in=w_cat.astype(BF16)[None], mu=mu[0][:, None, :],
                w2=_pad_rows(w2[0], LORA_PAD).astype(BF16), a2=_pad_rows(a2[0], LORA_PAD).astype(BF16),
                g2=g2[0].astype(BF16))


def _unblock_state(s_bd):
    n = RWKV_HEAD
    return jnp.stack([s_bd[:, :n, :n], s_bd[:, n:, n:]], axis=1).reshape(RWKV_HEADS, n, n)


def _rope_tables(pos):
    half = SWA_HEAD // 2
    inv = ROPE_THETA ** (-jnp.arange(half, dtype=F32) / half)
    ang = pos.astype(F32)[:, None] * inv[None, :]
    cos, sin = jnp.cos(ang), jnp.sin(ang)
    reps = LANES // SWA_HEAD
    return jnp.tile(cos, (1, 2 * reps)), jnp.tile(jnp.concatenate([-sin, sin], axis=1), (1, reps))


def _post_mixer(tok, mo, x, sw, layer, tm):
    row = lambda t: t[layer][None, :]
    x1f, x1b = _out_ln([tok, mo], sw["w_out"], layer, x, row(sw["ln1_g"]), row(sw["ln1_b"]), tm=tm)
    hff = _ffn_up(x1b, sw["w_gate"], sw["w_up"], layer, tm=_ffn_row_tile(x.shape[0]), tn=COL_TILE)
    return _out_ln([hff], sw["w_down"], layer, x1f, row(sw["ln2_g"]), row(sw["ln2_b"]),
                   tm=tm, tk=FFN_DOWN_K_TILE)


def kernel(x_prompt, x_sample, mem_prompt, cache_mem_k, cache_mem_v, state_rwkv_shift, state_rwkv_wkv, cache_swa_k, cache_swa_v, w_in_rwkv, rwkv_mu, rwkv_w0, rwkv_w1, rwkv_w2, rwkv_a0, rwkv_a1, rwkv_a2, rwkv_g1, rwkv_g2, rwkv_k_k, rwkv_k_a, rwkv_r_k, rwkv_gn_g, rwkv_gn_b, w_in_swa, swa_sinks, w_mem_kv, w_out, ln1_g, ln1_b, w_gate, w_up, w_down, ln2_g, ln2_b):
    assert DEPTH == 2 and x_prompt.shape[0] == 1 and x_sample.shape[1] == 1
    T = x_prompt.shape[1]
    B = x_sample.shape[0]
    past_len = T
    row = lambda t: t[None, :]
    shared = dict(w_out=w_out.astype(BF16), ln1_g=ln1_g, ln1_b=ln1_b, w_gate=w_gate, w_up=w_up,
                  w_down=w_down.astype(BF16), ln2_g=ln2_g, ln2_b=ln2_b)
    RW = _rwkv_weights(w_in_rwkv, rwkv_mu, rwkv_w1, rwkv_w2, rwkv_a1, rwkv_a2, rwkv_g1, rwkv_g2)
    rk, gn_g, gn_b = row(rwkv_r_k[0].reshape(-1)), row(rwkv_gn_g[0]), row(rwkv_gn_b[0])
    prep_consts = (row(rwkv_w0[0]), row(rwkv_a0[0]), row(rwkv_k_k[0]), row(rwkv_k_a[0]))
    w_swa = w_in_swa.astype(BF16)
    w_kv = w_mem_kv.astype(BF16)
    sinks = swa_sinks[0]
    q_blk_rwkv = 3 * TOK_WIDTH // MEM_WIDTH
    q_blk_swa = (TOK_WIDTH + 2 * SWA_KV_WIDTH) // MEM_WIDTH

    xp = x_prompt[0]
    tm = _row_tile(T)
    kv = [_proj(mem_prompt[0], w_kv, i, tm=MEM_TOKENS, tn=COL_TILE, name="mem_kv") for i in range(DEPTH)]
    prompt_mem_k = jnp.stack([t[:, :MEM_WIDTH] for t in kv]).reshape(DEPTH, 1, MEM_TOKENS, MEM_HEADS, MEM_HEAD)
    prompt_mem_v = jnp.stack([t[:, MEM_WIDTH:] for t in kv]).reshape(DEPTH, 1, MEM_TOKENS, MEM_HEADS, MEM_HEAD)

    proj = _rwkv_in(xp, None, RW["mu"], RW["w_in"], 0, tm=tm)
    tok, s_bd = _wkv_prompt(proj, RW["w2"], RW["a2"], RW["g2"], prep_consts + (rk, gn_g, gn_b))
    mo = _mem_attn(proj, q_blk_rwkv, kv[0], tm=tm)
    xf, xb = _post_mixer(tok, mo, xp, shared, 0, tm)
    prompt_shift = xp[-1][None, None, :]
    prompt_wkv = _unblock_state(s_bd)[None, None]

    proj = _proj(xb, w_swa, 0, tm=_ffn_row_tile(T), tn=COL_TILE, name="swa_proj")
    cos, sin = _rope_tables(jnp.arange(T))
    tok, k_rot = _swa_prompt(proj, cos, sin, sinks)
    mo = _mem_attn(proj, q_blk_swa, kv[1], tm=tm)
    y_prompt, _ = _post_mixer(tok, mo, xf, shared, 1, tm)
    v_last = proj[T - WINDOW:, TOK_WIDTH + SWA_KV_WIDTH:TOK_WIDTH + 2 * SWA_KV_WIDTH]
    prompt_swa_k = k_rot[T - WINDOW:].reshape(1, 1, WINDOW, SWA_KV_HEADS, SWA_HEAD)
    prompt_swa_v = v_last.reshape(1, 1, WINDOW, SWA_KV_HEADS, SWA_HEAD)

    xs = x_sample[:, 0]
    tms = _row_tile(B)
    mem_k = cache_mem_k.reshape(DEPTH, B, MEM_TOKENS * MEM_HEADS, MEM_HEAD)
    mem_v = cache_mem_v.reshape(DEPTH, B, MEM_TOKENS * MEM_HEADS, MEM_HEAD)

    def mem_step(q, layer):
        q_rows = jnp.pad(q.reshape(B, MEM_HEADS, MEM_HEAD), ((0, 0), (0, SUBLANES - MEM_HEADS), (0, 0)))
        out = _mem_attn_step(q_rows, mem_k, mem_v, layer, bs=STEP_BATCH)
        return out[:, :MEM_HEADS].reshape(B, MEM_WIDTH).astype(BF16)

    proj = _rwkv_in(xs, state_rwkv_shift[0], RW["mu"], RW["w_in"], 0, tm=tms)
    lw, kp, kn, bv, g = _rwkv_prep(proj, RW["w2"], RW["a2"], RW["g2"], *prep_consts, tm=tms)
    rows4 = lambda t: t.reshape(B, RWKV_HEADS, 1, RWKV_HEAD)
    vt = proj[:, 2 * TOK_WIDTH:3 * TOK_WIDTH].reshape(B, RWKV_HEADS, RWKV_HEAD).transpose(0, 2, 1)
    s_new, y_t = _wkv_step(state_rwkv_wkv, 0, rows4(kn), rows4(lw), rows4(bv), rows4(kp),
                           rows4(proj[:, :TOK_WIDTH]), vt, bs=WKV_STEP_BATCH)
    y = y_t.transpose(0, 2, 1).reshape(B, TOK_WIDTH)
    tok = _gn_gate_rows(y, proj, kp, g, rk, gn_g, gn_b, tm=tms)
    mo = mem_step(proj[:, 3 * TOK_WIDTH:3 * TOK_WIDTH + MEM_WIDTH], 0)
    xf, xb = _post_mixer(tok, mo, xs, shared, 0, tms)
    sample_shift = xs[None]
    sample_wkv = s_new[None]

    proj = _proj(xb, w_swa, 0, tm=tms, tn=COL_TILE, name="swa_proj")
    cos, sin = _rope_tables(jnp.full((B,), past_len))
    qk = _rope(proj, cos, sin, tm=tms)
    k_new = qk[:, TOK_WIDTH:].reshape(B, 1, SWA_KV_HEADS, SWA_HEAD)
    v_new = proj[:, TOK_WIDTH + SWA_KV_WIDTH:TOK_WIDTH + 2 * SWA_KV_WIDTH].reshape(B, 1, SWA_KV_HEADS, SWA_HEAD)
    kc = jnp.concatenate([cache_swa_k[0][:, 1:], k_new], axis=1)
    vc = jnp.concatenate([cache_swa_v[0][:, 1:], v_new], axis=1)
    kv_of_head = (jnp.arange(SWA_Q_HEADS)[:, None] // SWA_GROUP == jnp.arange(SWA_KV_HEADS)[None, :]).astype(F32)
    q3 = qk[:, :TOK_WIDTH].reshape(B, SWA_Q_HEADS, 1, SWA_HEAD)
    q_bd = (q3 * kv_of_head[None, :, :, None]).reshape(B, SWA_Q_HEADS, SWA_KV_WIDTH)
    o = _swa_step(q_bd, kc.reshape(B, WINDOW, SWA_KV_WIDTH), vc.reshape(B, WINDOW, SWA_KV_WIDTH),
                  sinks, bs=STEP_BATCH)
    o = jnp.sum(o.reshape(B, SWA_Q_HEADS, SWA_KV_HEADS, SWA_HEAD) * kv_of_head[None, :, :, None], axis=2)
    tok = o.reshape(B, TOK_WIDTH).astype(BF16)
    mo = mem_step(proj[:, TOK_WIDTH + 2 * SWA_KV_WIDTH:], 1)
    y_sample, _ = _post_mixer(tok, mo, xf, shared, 1, tms)

    return (y_prompt[None], y_sample[:, None, :], prompt_mem_k, prompt_mem_v, prompt_shift, prompt_wkv,
            prompt_swa_k, prompt_swa_v, sample_shift, sample_wkv, kc[None], vc[None])
```

```python
import functools
import math

import jax
import jax.numpy as jnp
from jax import lax
from jax.experimental import pallas as pl
from jax.experimental.pallas import tpu as pltpu

D_MODEL = 2048
DEPTH = 2
MEM_WIDTH = D_MODEL // 4
TOK_WIDTH = D_MODEL - MEM_WIDTH
RWKV_HEAD = 64
RWKV_HEADS = TOK_WIDTH // RWKV_HEAD
GN_EPS = RWKV_HEAD * 1e-5
SWA_HEAD = 64
SWA_Q_HEADS = TOK_WIDTH // SWA_HEAD
SWA_KV_HEADS = 4
SWA_GROUP = SWA_Q_HEADS // SWA_KV_HEADS
SWA_KV_WIDTH = SWA_KV_HEADS * SWA_HEAD
WINDOW = 128
BLOCK = 128
SWA_SCALE = SWA_HEAD ** -0.5
ROPE_THETA = 10000.0
MEM_TOKENS = 256
MEM_HEADS = 4
MEM_HEAD = MEM_WIDTH // MEM_HEADS
MEM_SCALE = MEM_HEAD ** -0.5
FFN_HIDDEN = int(math.ceil(8 * D_MODEL / 3 / 256)) * 256
ALPHA = (2 * DEPTH) ** 0.25
LN_EPS = 1e-5
LORA_PAD = 128
LORA_IN_WIDTH = 512

LANES = 128
SUBLANES = 8
VMEM_LIMIT_BYTES = 56 * 1024 * 1024

BF16 = jnp.bfloat16
F32 = jnp.float32
NT_DIMS = (((1,), (1,)), ((), ()))
TN_DIMS = (((0,), (0,)), ((), ()))


def _dot(a, b):
    return jnp.dot(a.astype(BF16), b.astype(BF16), preferred_element_type=F32)


def _dot_nt(a, b):
    return lax.dot_general(a.astype(BF16), b.astype(BF16), NT_DIMS, preferred_element_type=F32)


def _dot_tn(a, b):
    return lax.dot_general(a.astype(BF16), b.astype(BF16), TN_DIMS, preferred_element_type=F32)


def _split_dot(x, m):
    hi = x.astype(BF16)
    lo = (x - hi.astype(F32)).astype(BF16)
    return (jnp.dot(hi, m, preferred_element_type=F32)
            + jnp.dot(lo, m, preferred_element_type=F32))


def _head_ones():
    p = lax.broadcasted_iota(jnp.int32, (LANES, LANES), 0)
    q = lax.broadcasted_iota(jnp.int32, (LANES, LANES), 1)
    return ((p // RWKV_HEAD) == (q // RWKV_HEAD)).astype(BF16)


def _cparams(*sem):
    return pltpu.CompilerParams(dimension_semantics=sem, vmem_limit_bytes=VMEM_LIMIT_BYTES)


def _proj_kernel(x_ref, w_ref, o_ref):
    o_ref[...] = jnp.dot(x_ref[...].astype(BF16), w_ref[...], preferred_element_type=F32)


def _proj(x, w, layer, *, tm, tn, name):
    M, K = x.shape
    N = w.shape[2]
    assert M % tm == 0 and N % tn == 0
    return pl.pallas_call(
        _proj_kernel, grid=(M // tm, N // tn),
        in_specs=[pl.BlockSpec((tm, K), lambda i, j: (i, 0)),
                  pl.BlockSpec((None, K, tn), lambda i, j: (layer, 0, j))],
        out_specs=pl.BlockSpec((tm, tn), lambda i, j: (i, j)),
        out_shape=jax.ShapeDtypeStruct((M, N), F32),
        compiler_params=_cparams("parallel", "arbitrary"), name=name)(x, w)


MIX_R, MIX_W, MIX_K, MIX_V, MIX_A, MIX_G, MIX_NONE = range(7)
RWKV_IN_WIDTH = 3 * TOK_WIDTH + MEM_WIDTH + LORA_IN_WIDTH
RWKV_IN_TILE = LORA_IN_WIDTH
RWKV_IN_MIX = ([MIX_R] * (TOK_WIDTH // RWKV_IN_TILE) + [MIX_K] * (TOK_WIDTH // RWKV_IN_TILE)
               + [MIX_V] * (TOK_WIDTH // RWKV_IN_TILE) + [MIX_NONE] * (MEM_WIDTH // RWKV_IN_TILE))
LORA_PARTS = ((MIX_W, 0, LORA_PAD), (MIX_A, LORA_PAD, 2 * LORA_PAD), (MIX_G, 2 * LORA_PAD, LORA_IN_WIDTH))


def _rwkv_in_kernel(mix_ref, x_ref, xp_ref, mu_ref, w_ref, o_ref, lhs_ref, *, shift):
    i, j = pl.program_id(0), pl.program_id(1)
    tm = x_ref.shape[0]

    @pl.when(j == 0)
    def _():
        x = x_ref[...]
        if shift:
            first = jnp.where(i > 0, xp_ref[SUBLANES - 1:SUBLANES, :], 0.0)
            rowid = lax.broadcasted_iota(jnp.int32, (tm, 1), 0)
            xprev = jnp.where(rowid == 0, first, pltpu.roll(x, 1, 0))
        else:
            xprev = xp_ref[...]
        d = xprev - x
        for m in range(MIX_NONE):
            lhs_ref[m] = (x + d * mu_ref[m]).astype(BF16)
        lhs_ref[MIX_NONE] = x.astype(BF16)

    nproj = pl.num_programs(1) - 1

    @pl.when(j < nproj)
    def _():
        o_ref[...] = jnp.dot(lhs_ref[mix_ref[j]], w_ref[...], preferred_element_type=F32)

    @pl.when(j == nproj)
    def _():
        for m, lo, hi in LORA_PARTS:
            o_ref[:, lo:hi] = jnp.dot(lhs_ref[m], w_ref[:, lo:hi], preferred_element_type=F32)


def _rwkv_in(x, xprev, mu, w, layer, *, tm):
    M, K = x.shape
    N = w.shape[2]
    tn = RWKV_IN_TILE
    assert M % tm == 0 and N == RWKV_IN_WIDTH and tm % SUBLANES == 0
    shift = xprev is None
    if shift:
        rows8 = tm // SUBLANES
        xp_spec = pl.BlockSpec((SUBLANES, K), lambda i, j, m: (jnp.maximum(i * rows8 - 1, 0), 0))
        xprev = x
    else:
        xp_spec = pl.BlockSpec((tm, K), lambda i, j, m: (i, 0))
    mix = jnp.asarray(RWKV_IN_MIX + [MIX_NONE], jnp.int32)
    grid_spec = pltpu.PrefetchScalarGridSpec(
        num_scalar_prefetch=1, grid=(M // tm, N // tn),
        in_specs=[pl.BlockSpec((tm, K), lambda i, j, m: (i, 0)), xp_spec,
                  pl.BlockSpec((MIX_NONE, 1, K), lambda i, j, m: (0, 0, 0)),
                  pl.BlockSpec((None, K, tn), lambda i, j, m: (layer, 0, j))],
        out_specs=pl.BlockSpec((tm, tn), lambda i, j, m: (i, j)),
        scratch_shapes=[pltpu.VMEM((MIX_NONE + 1, tm, K), BF16)])
    return pl.pallas_call(
        functools.partial(_rwkv_in_kernel, shift=shift), grid_spec=grid_spec,
        out_shape=jax.ShapeDtypeStruct((M, N), F32),
        compiler_params=_cparams("parallel", "arbitrary"), name="rwkv_in")(mix, x, xprev, mu, w)


def _softplus(z):
    return jnp.maximum(z, 0.0) + jnp.log1p(jnp.exp(-jnp.abs(z)))


def _each(f, *lists):
    return [f(*a) for a in zip(*lists)]


def _lora_hidden(hl):
    return jnp.tanh(hl[:, 0:LORA_PAD]), hl[:, LORA_PAD:2 * LORA_PAD], jax.nn.sigmoid(hl[:, 2 * LORA_PAD:])


def _prep_slabs(k, wl, al, w0, a0, k_k, k_a, ones):
    lw = _each(lambda wl_, w0_: -jnp.exp(-_softplus(-(w0_ + wl_)) - 0.5), wl, w0)
    agate = _each(lambda al_, a0_: jax.nn.sigmoid(a0_ + al_), al, a0)
    kkr = _each(lambda k_, c_: k_ * c_, k, k_k)
    ss = _each(lambda t: _split_dot(t * t, ones), kkr)
    kn = _each(lambda t, s_: t / jnp.maximum(jnp.sqrt(s_), 1e-12), kkr, ss)
    bv = _each(lambda n_, a_: n_ * a_, kn, agate)
    kp = _each(lambda k_, a_, c_: k_ * (1.0 + (a_ - 1.0) * c_), k, agate, k_a)
    return lw, kp, kn, bv


def _rwkv_prep_t_kernel(hl_ref, r_ref, k_ref, v_ref, w2_ref, a2_ref, g2_ref, w0_ref, a0_ref, kk_ref, ka_ref,
                        rt_ref, vt_ref, lw_ref, kp_ref, kn_ref, bv_ref, g_ref):
    hw, ha, hg = _lora_hidden(hl_ref[...])
    outs = _prep_slabs([k_ref[...]], [_dot(hw, w2_ref[...])], [_dot(ha, a2_ref[...])], [w0_ref[...]],
                       [a0_ref[...]], [kk_ref[...]], [ka_ref[...]], _head_ones())
    vals = [r_ref[...], v_ref[...]] + [t[0] for t in outs] + [_dot(hg, g2_ref[...])]
    for ref, val in zip((rt_ref, vt_ref, lw_ref, kp_ref, kn_ref, bv_ref, g_ref), vals):
        ref[...] = val.T


def _rwkv_prep_t(proj, w2p, a2p, g2, w0, a0, k_k, k_a):
    B = proj.shape[0]
    assert B == LANES
    nslab = TOK_WIDTH // LANES
    lora_blk = (RWKV_IN_WIDTH - LORA_IN_WIDTH) // LORA_IN_WIDTH
    col = lambda s: (0, s)
    slab = lambda base: pl.BlockSpec((B, LANES), lambda s, base=base: (0, base + s))
    outs = [jax.ShapeDtypeStruct((TOK_WIDTH, B), F32)] * 7
    return pl.pallas_call(
        _rwkv_prep_t_kernel, grid=(nslab,),
        in_specs=[pl.BlockSpec((B, LORA_IN_WIDTH), lambda s: (0, lora_blk)),
                  slab(0), slab(nslab), slab(2 * nslab),
                  pl.BlockSpec((LORA_PAD, LANES), col),
                  pl.BlockSpec((LORA_PAD, LANES), col),
                  pl.BlockSpec((2 * LORA_PAD, LANES), col),
                  pl.BlockSpec((1, LANES), col), pl.BlockSpec((1, LANES), col),
                  pl.BlockSpec((1, LANES), col), pl.BlockSpec((1, LANES), col)],
        out_specs=[pl.BlockSpec((LANES, B), lambda s: (s, 0))] * 7,
        out_shape=outs, compiler_params=_cparams("arbitrary"),
        name="rwkv_prep_t")(proj, proj, proj, proj, w2p, a2p, g2, w0, a0, k_k, k_a)


def _gn_gate(y, r, kp, v, g, rk, gg, gb, ones):
    inv_n = 1.0 / RWKV_HEAD
    rows = y[0].shape[0]
    sums = _each(lambda y_, r_, k_, rk_: _split_dot(jnp.concatenate([y_, r_ * k_ * rk_], axis=0), ones),
                 y, r, kp, rk)
    d = _each(lambda y_, s_: y_ - s_[:rows] * inv_n, y, sums)
    var = _each(lambda d_: _split_dot(d_ * d_, ones) * inv_n, d)
    return _each(lambda d_, var_, gg_, gb_, s_, v_, g_:
                 (d_ * lax.rsqrt(var_ + GN_EPS) * gg_ + gb_ + s_[rows:] * v_) * g_,
                 d, var, gg, gb, sums, v, g)


WKV_CHUNK = 64


def _wkv_masks():
    n = 2 * WKV_CHUNK
    p = lax.broadcasted_iota(jnp.int32, (n, n), 0)
    q = lax.broadcasted_iota(jnp.int32, (n, n), 1)
    same = lambda b: (p // b) == (q // b)
    pt, qt = p % WKV_CHUNK, q % WKV_CHUNK
    s8, s16, s32, s64 = same(8), same(16), same(32), same(WKV_CHUNK)
    return dict(strict=s64 & (pt > qt), incl=s64 & (pt >= qt), s8=s8,
                e16=s16 & ~s8, e32=s32 & ~s16, e64=s64 & ~s32,
                eye=(p == q).astype(F32))


def _wkv_pairs(r, lw, k, v, kn, bv, S, mk, tri, lane_lo):
    stack = lambda x: jnp.concatenate([jnp.where(lane_lo, x, 0.0), jnp.where(lane_lo, 0.0, x)], axis=0)
    n = 2 * WKV_CHUNK
    c = _each(lambda t: _split_dot_left(tri, t), lw)
    c_last = _each(lambda t: t[WKV_CHUNK - 1:WKV_CHUNK, :], c)
    e_out = _each(lambda t: jnp.exp(-t), c)
    e_end = _each(lambda t, tl: jnp.exp(tl - t), c, c_last)
    ah = _each(lambda kn_, c_, lw_: stack(-kn_ * jnp.exp(c_ - lw_)), kn, c, lw)
    rh = _each(lambda r_, c_: stack(r_ * jnp.exp(c_)), r, c)
    bh = _each(lambda b_, e_: stack(b_ * e_), bv, e_out)
    kh = _each(lambda k_, e_: stack(k_ * e_), k, e_out)
    bbar = _each(lambda b_, e_: stack(b_ * e_), bv, e_end)
    kbar = _each(lambda k_, e_: stack(k_ * e_), k, e_end)
    vs = _each(stack, v)
    gm = _each(lambda a_, r_, b_, k_: _dot_nt(jnp.concatenate([a_, r_], axis=0),
                                              jnp.concatenate([b_, k_], axis=0)), ah, rh, bh, kh)
    a_ab = _each(lambda g_: jnp.where(mk["strict"], g_[:n, :n], 0.0), gm)
    a_ak = _each(lambda g_: jnp.where(mk["strict"], g_[:n, n:], 0.0), gm)
    l_rb = _each(lambda g_: jnp.where(mk["incl"], g_[n:, :n], 0.0), gm)
    l_rk = _each(lambda g_: jnp.where(mk["incl"], g_[n:, n:], 0.0), gm)
    d1 = _each(lambda a_: jnp.where(mk["s8"], a_, 0.0), a_ab)
    x = _each(lambda d_: mk["eye"] + d_, d1)
    d2 = _each(lambda d_: _dot(d_, d_), d1)
    x = _each(lambda x_, d_: x_ + _dot(x_, d_), x, d2)
    d4 = _each(lambda d_: _dot(d_, d_), d2)
    x = _each(lambda x_, d_: x_ + _dot(x_, d_), x, d4)
    for lvl in ("e16", "e32", "e64"):
        ex = _each(lambda a_, x_: _dot(jnp.where(mk[lvl], a_, 0.0), x_), a_ab, x)
        x = _each(lambda x_, e_: x_ + _dot(x_, e_), x, ex)
    av = _each(_dot, a_ak, vs)
    tw = _each(lambda x_, a_, v_: _dot(x_, jnp.concatenate([a_, v_], axis=1)), x, ah, av)
    lwm = _each(_dot, l_rb, tw)
    lv = _each(_dot, l_rk, vs)
    qm = _each(lambda r_, l_: r_ + l_[:, :n], rh, lwm)
    y0 = _each(lambda l_, v_: l_[:, n:] + v_, lwm, lv)
    mt = _each(lambda t_, b_: _dot_tn(t_[:, :n], b_), tw, bbar)
    nt = _each(lambda t_, b_, v_, k_: _dot_tn(t_[:, n:], b_) + _dot_tn(v_, k_), tw, bbar, vs, kbar)
    ys = _each(lambda q_, s_, y_: _dot_nt(q_, s_) + y_, qm, S, y0)
    y = _each(lambda t: t[:WKV_CHUNK, :] + t[WKV_CHUNK:, :], ys)
    s_new = _each(lambda s_, cl, m_, n_: s_ * jnp.exp(cl) + _dot(s_, m_) + n_, S, c_last, mt, nt)
    return y, s_new


def _split_dot_left(m, x):
    hi = x.astype(BF16)
    lo = (x - hi.astype(F32)).astype(BF16)
    return (jnp.dot(m, hi, preferred_element_type=F32)
            + jnp.dot(m, lo, preferred_element_type=F32))


WKV_PAIRS = TOK_WIDTH // LANES


def _wkv_kernel(r_ref, k_ref, v_ref, hl_ref, w2_ref, a2_ref, g2_ref, w0_ref, a0_ref, kk_ref, ka_ref,
                rk_ref, gg_ref, gb_ref, o_ref, sout_ref, s_scr):
    c_idx = pl.program_id(0)

    @pl.when(c_idx == 0)
    def _():
        s_scr[...] = jnp.zeros_like(s_scr)

    mk = _wkv_masks()
    ti = lax.broadcasted_iota(jnp.int32, (WKV_CHUNK, WKV_CHUNK), 0)
    tj = lax.broadcasted_iota(jnp.int32, (WKV_CHUNK, WKV_CHUNK), 1)
    tri = (ti >= tj).astype(BF16)
    lane_lo = lax.broadcasted_iota(jnp.int32, (WKV_CHUNK, LANES), 1) < RWKV_HEAD
    ones = _head_ones()
    slabs = [slice(p * LANES, (p + 1) * LANES) for p in range(WKV_PAIRS)]
    cols = lambda t: [t[:, sl] for sl in slabs]
    hw, ha, hg = _lora_hidden(hl_ref[...])
    wl, al, g = _dot(hw, w2_ref[...]), _dot(ha, a2_ref[...]), _dot(hg, g2_ref[...])
    r, v = cols(r_ref), cols(v_ref)
    lw, kp, kn, bv = _prep_slabs(cols(k_ref), cols(wl), cols(al), cols(w0_ref), cols(a0_ref),
                                 cols(kk_ref), cols(ka_ref), ones)
    y, s_new = _wkv_pairs(r, lw, kp, v, kn, bv, [s_scr[p] for p in range(WKV_PAIRS)], mk, tri, lane_lo)
    for p in range(WKV_PAIRS):
        s_scr[p] = s_new[p]
    tok = _gn_gate(y, r, kp, v, cols(g), cols(rk_ref), cols(gg_ref), cols(gb_ref), ones)
    for sl, t in zip(slabs, tok):
        o_ref[:, sl] = t.astype(o_ref.dtype)

    @pl.when(c_idx == pl.num_programs(0) - 1)
    def _():
        sout_ref[...] = s_scr[...]


def _wkv_prompt(proj, w2p, a2p, g2, consts):
    T = proj.shape[0]
    assert T % WKV_CHUNK == 0
    lora_blk = (RWKV_IN_WIDTH - LORA_IN_WIDTH) // LORA_IN_WIDTH
    tok = lambda blk: pl.BlockSpec((WKV_CHUNK, TOK_WIDTH), lambda c, blk=blk: (c, blk))
    full = lambda a: pl.BlockSpec(a.shape, lambda c: (0,) * a.ndim)
    weights = (w2p, a2p, g2) + tuple(consts)
    return pl.pallas_call(
        _wkv_kernel, grid=(T // WKV_CHUNK,),
        in_specs=[tok(0), tok(1), tok(2), pl.BlockSpec((WKV_CHUNK, LORA_IN_WIDTH), lambda c: (c, lora_blk))]
        + [full(a) for a in weights],
        out_specs=[tok(0), pl.BlockSpec((WKV_PAIRS, LANES, LANES), lambda c: (0, 0, 0))],
        out_shape=[jax.ShapeDtypeStruct((T, TOK_WIDTH), BF16),
                   jax.ShapeDtypeStruct((WKV_PAIRS, LANES, LANES), F32)],
        scratch_shapes=[pltpu.VMEM((WKV_PAIRS, LANES, LANES), F32)],
        compiler_params=_cparams("arbitrary"), name="wkv_chunked")(proj, proj, proj, proj, *weights)


WKV_STEP_UNROLL = 4


def _wkv_step_kernel(s_ref, r_ref, v_ref, lw_ref, kp_ref, kn_ref, bv_ref, g_ref, rk_ref, gg_ref, gb_ref,
                     so_ref, tok_ref, y_scr):
    n = RWKV_HEAD
    inv_n = 1.0 / n
    for hh in range(2):
        rows = slice(hh * n, (hh + 1) * n)
        a, w = -kn_ref[rows, :], jnp.exp(lw_ref[rows, :])
        b, k, r = bv_ref[rows, :], kp_ref[rows, :], r_ref[rows, :]

        def value_row(i, carry, hh=hh, a=a, w=w, b=b, k=k, r=r):
            s = s_ref[hh, i]
            sa = jnp.sum(s * a, axis=0, keepdims=True)
            s_new = s * w + sa * b + v_ref[pl.ds(hh * n + i, 1), :] * k
            so_ref[hh, i] = s_new
            y_scr[pl.ds(hh * n + i, 1), :] = jnp.sum(s_new * r, axis=0, keepdims=True)
            return carry

        lax.fori_loop(0, n, value_row, 0, unroll=WKV_STEP_UNROLL)
    for hh in range(2):
        rows = slice(hh * n, (hh + 1) * n)
        y = y_scr[rows, :]
        d = y - jnp.sum(y, axis=0, keepdims=True) * inv_n
        var = jnp.sum(d * d, axis=0, keepdims=True) * inv_n
        bonus = jnp.sum(r_ref[rows, :] * kp_ref[rows, :] * rk_ref[rows, :], axis=0, keepdims=True)
        yn = d * lax.rsqrt(var + GN_EPS) * gg_ref[rows, :] + gb_ref[rows, :]
        tok_ref[rows, :] = (yn + bonus * v_ref[rows, :]) * g_ref[rows, :]


def _wkv_step(state, layer, vecs, consts):
    B = state.shape[-1]
    sshape = (2, RWKV_HEAD, RWKV_HEAD, B)
    vblk = pl.BlockSpec((LANES, B), lambda p: (p, 0))
    return pl.pallas_call(
        _wkv_step_kernel, grid=(WKV_PAIRS,),
        in_specs=[pl.BlockSpec((None,) + sshape, lambda p: (layer, p, 0, 0, 0))] + [vblk] * 10,
        out_specs=[pl.BlockSpec(sshape, lambda p: (p, 0, 0, 0)), vblk],
        out_shape=[jax.ShapeDtypeStruct(state.shape[1:], F32), jax.ShapeDtypeStruct((TOK_WIDTH, B), F32)],
        scratch_shapes=[pltpu.VMEM((LANES, B), F32)],
        compiler_params=_cparams("parallel"), name="wkv_step")(state, *vecs, *consts)


def _softmax_rows(s):
    m = jnp.max(s, axis=-1, keepdims=True)
    e = jnp.exp(s - m)
    return e / jnp.sum(e, axis=-1, keepdims=True)


def _mem_attn_kernel(q_ref, k_ref, v_ref, o_ref):
    for h in range(MEM_HEADS):
        sl = slice(h * MEM_HEAD, (h + 1) * MEM_HEAD)
        p = _softmax_rows(_dot_nt(q_ref[:, sl], k_ref[:, sl]) * MEM_SCALE)
        o_ref[:, sl] = _dot(p, v_ref[:, sl]).astype(o_ref.dtype)


def _mem_attn(proj, qblk, kv, *, tm):
    M = proj.shape[0]
    return pl.pallas_call(
        _mem_attn_kernel, grid=(M // tm,),
        in_specs=[pl.BlockSpec((tm, MEM_WIDTH), lambda i: (i, qblk)),
                  pl.BlockSpec((MEM_TOKENS, MEM_WIDTH), lambda i: (0, 0)),
                  pl.BlockSpec((MEM_TOKENS, MEM_WIDTH), lambda i: (0, 1))],
        out_specs=pl.BlockSpec((tm, MEM_WIDTH), lambda i: (i, 0)),
        out_shape=jax.ShapeDtypeStruct((M, MEM_WIDTH), BF16),
        compiler_params=_cparams("parallel"), name="mem_attn")(proj, kv, kv)


def _mem_attn_step_kernel(q_ref, k_ref, v_ref, o_ref, *, bs):
    rows = MEM_TOKENS * MEM_HEADS
    col_head = lax.broadcasted_iota(jnp.int32, (SUBLANES, rows), 1) % MEM_HEADS
    row_head = lax.broadcasted_iota(jnp.int32, (SUBLANES, rows), 0) % MEM_HEADS
    own = col_head == row_head

    def sample(b, carry):
        s = _dot_nt(q_ref[b], k_ref[b]) * MEM_SCALE
        o_ref[b] = _dot(_softmax_rows(jnp.where(own, s, -jnp.inf)), v_ref[b])
        return carry

    lax.fori_loop(0, bs, sample, 0)


def _mem_attn_step(q, mk, mv, layer, *, bs):
    B = q.shape[0]
    qblk = pl.BlockSpec((bs, SUBLANES, MEM_HEAD), lambda i: (i, 0, 0))
    cblk = pl.BlockSpec((None, bs, MEM_TOKENS * MEM_HEADS, MEM_HEAD), lambda i: (layer, i, 0, 0))
    return pl.pallas_call(
        functools.partial(_mem_attn_step_kernel, bs=bs), grid=(B // bs,),
        in_specs=[qblk, cblk, cblk], out_specs=qblk,
        out_shape=jax.ShapeDtypeStruct(q.shape, F32),
        compiler_params=_cparams("parallel"), name="mem_attn_step")(q, mk, mv)


def _deepnorm_ln(res, h, g, beta):
    z = ALPHA * res + h
    mu = jnp.mean(z, axis=-1, keepdims=True)
    d = z - mu
    var = jnp.mean(d * d, axis=-1, keepdims=True)
    return d * lax.rsqrt(var + LN_EPS) * g + beta


def _out_ln_kernel(*refs, widths):
    n = len(widths)
    lhs, (w_ref, res_ref, g_ref, beta_ref, of_ref, ob_ref) = refs[:n], refs[n:]
    h, off = None, 0
    for a_ref, kw in zip(lhs, widths):
        part = jnp.dot(a_ref[...], w_ref[off:off + kw, :], preferred_element_type=F32)
        h = part if h is None else h + part
        off += kw
    out = _deepnorm_ln(res_ref[...], h, g_ref[...], beta_ref[...])
    of_ref[...] = out
    ob_ref[...] = out.astype(BF16)


def _out_ln_tiled_kernel(a_ref, w_ref, res_ref, g_ref, beta_ref, of_ref, ob_ref, acc_ref):
    k = pl.program_id(1)

    @pl.when(k == 0)
    def _():
        acc_ref[...] = jnp.zeros_like(acc_ref)

    acc_ref[...] += jnp.dot(a_ref[...], w_ref[...], preferred_element_type=F32)

    @pl.when(k == pl.num_programs(1) - 1)
    def _():
        out = _deepnorm_ln(res_ref[...], acc_ref[...], g_ref[...], beta_ref[...])
        of_ref[...] = out
        ob_ref[...] = out.astype(BF16)


def _out_ln(pieces, w, layer, res, g, beta, *, tm, tk=None):
    M = pieces[0].shape[0]
    K, N = w.shape[1], w.shape[2]
    widths = tuple(p.shape[1] for p in pieces)
    assert sum(widths) == K and M % tm == 0
    out_shape = [jax.ShapeDtypeStruct((M, N), F32), jax.ShapeDtypeStruct((M, N), BF16)]
    if tk is None:
        row = lambda i: (i, 0)
        cst = lambda i: (0, 0)
        in_specs = ([pl.BlockSpec((tm, kw), row) for kw in widths]
                    + [pl.BlockSpec((None, K, N), lambda i: (layer, 0, 0)),
                       pl.BlockSpec((tm, N), row), pl.BlockSpec((1, N), cst), pl.BlockSpec((1, N), cst)])
        oblk = pl.BlockSpec((tm, N), row)
        return pl.pallas_call(
            functools.partial(_out_ln_kernel, widths=widths), grid=(M // tm,),
            in_specs=in_specs, out_specs=[oblk, oblk], out_shape=out_shape,
            compiler_params=_cparams("parallel"), name="out_ln")(*pieces, w, res, g, beta)
    assert len(pieces) == 1 and K % tk == 0
    row = lambda i, k: (i, 0)
    cst = lambda i, k: (0, 0)
    oblk = pl.BlockSpec((tm, N), row)
    return pl.pallas_call(
        _out_ln_tiled_kernel, grid=(M // tm, K // tk),
        in_specs=[pl.BlockSpec((tm, tk), lambda i, k: (i, k)),
                  pl.BlockSpec((None, tk, N), lambda i, k: (layer, k, 0)),
                  pl.BlockSpec((tm, N), row), pl.BlockSpec((1, N), cst), pl.BlockSpec((1, N), cst)],
        out_specs=[oblk, oblk], out_shape=out_shape,
        scratch_shapes=[pltpu.VMEM((tm, N), F32)],
        compiler_params=_cparams("parallel", "arbitrary"), name="ffn_down_ln")(
            pieces[0], w, res, g, beta)


def _ffn_up_kernel(x_ref, wg_ref, wu_ref, o_ref):
    x = x_ref[...]
    gate = jnp.dot(x, wg_ref[...].astype(BF16), preferred_element_type=F32)
    up = jnp.dot(x, wu_ref[...].astype(BF16), preferred_element_type=F32)
    o_ref[...] = (gate * jax.nn.sigmoid(gate) * up).astype(o_ref.dtype)


def _ffn_up(x, wg, wu, layer, *, tm, tn):
    M, K = x.shape
    N = wg.shape[2]
    wblk = pl.BlockSpec((None, K, tn), lambda i, j: (layer, 0, j))
    return pl.pallas_call(
        _ffn_up_kernel, grid=(M // tm, N // tn),
        in_specs=[pl.BlockSpec((tm, K), lambda i, j: (i, 0)), wblk, wblk],
        out_specs=pl.BlockSpec((tm, tn), lambda i, j: (i, j)),
        out_shape=jax.ShapeDtypeStruct((M, N), BF16),
        compiler_params=_cparams("parallel", "arbitrary"), name="ffn_up")(x, wg, wu)


def _rope_kernel(x_ref, cos_ref, sin_ref, o_ref):
    x = x_ref[...]
    lane = lax.broadcasted_iota(jnp.int32, x.shape, 1)
    first_half = (lane % SWA_HEAD) < (SWA_HEAD // 2)
    partner = jnp.where(first_half, pltpu.roll(x, LANES - SWA_HEAD // 2, 1),
                        pltpu.roll(x, SWA_HEAD // 2, 1))
    o_ref[...] = x * cos_ref[...] + partner * sin_ref[...]


def _rope(proj, cos, sin_signed, *, tm):
    M = proj.shape[0]
    width = TOK_WIDTH + SWA_KV_WIDTH
    blk = pl.BlockSpec((tm, LANES), lambda i, s: (i, s))
    tab = pl.BlockSpec((tm, LANES), lambda i, s: (i, 0))
    return pl.pallas_call(
        _rope_kernel, grid=(M // tm, width // LANES),
        in_specs=[blk, tab, tab], out_specs=blk,
        out_shape=jax.ShapeDtypeStruct((M, width), F32),
        compiler_params=_cparams("parallel", "arbitrary"), name="rope")(proj, cos, sin_signed)


def _sink_column(sink_ref, base, rows_per_head, nheads):
    rows = rows_per_head * nheads
    hid = lax.broadcasted_iota(jnp.int32, (rows, 1), 0) // rows_per_head
    col = jnp.zeros((rows, 1), F32)
    for j in range(nheads):
        col = jnp.where(hid == j, sink_ref[base + j], col)
    return col


def _sink_softmax(s, sink):
    m = jnp.maximum(jnp.max(s, axis=-1, keepdims=True), sink)
    p = jnp.exp(s - m)
    return p / (jnp.sum(p, axis=-1, keepdims=True) + jnp.exp(sink - m))


def _swa_kernel(sink_ref, q_ref, kc_ref, kp_ref, vc_ref, vp_ref, cosc_ref, sinc_ref, cosp_ref, sinp_ref,
                o_ref, krot_ref):
    n = pl.program_id(0)
    nslab_q = TOK_WIDTH // LANES
    slab = lambda ref, s: ref[:, s * LANES:(s + 1) * LANES]
    cos_c, sin_c, cos_p, sin_p = cosc_ref[...], sinc_ref[...], cosp_ref[...], sinp_ref[...]
    lane = lax.broadcasted_iota(jnp.int32, (BLOCK, LANES), 1)
    first_half = (lane % SWA_HEAD) < (SWA_HEAD // 2)
    lo = lane < SWA_HEAD

    def rope(x, cos, sin):
        partner = jnp.where(first_half, pltpu.roll(x, LANES - SWA_HEAD // 2, 1),
                            pltpu.roll(x, SWA_HEAD // 2, 1))
        return x * cos + partner * sin

    kv_slabs = SWA_KV_WIDTH // LANES
    k_cur = [rope(slab(kc_ref, j), cos_c, sin_c) for j in range(kv_slabs)]
    k_prev = [rope(slab(kp_ref, j), cos_p, sin_p) for j in range(kv_slabs)]
    for j in range(kv_slabs):
        krot_ref[:, j * LANES:(j + 1) * LANES] = k_cur[j]
    lane2 = lax.broadcasted_iota(jnp.int32, (2 * BLOCK, LANES), 1)
    kd, vd = [], []
    for g in range(SWA_KV_HEADS):
        j, half = divmod(g, 2)
        keep = (lane2 < SWA_HEAD) if half == 0 else (lane2 >= SWA_HEAD)
        dup = lambda t: jnp.where(keep, t, pltpu.roll(t, SWA_HEAD, 1)).astype(BF16)
        kd.append(dup(jnp.concatenate([k_prev[j], k_cur[j]], axis=0)))
        vd.append(dup(jnp.concatenate([slab(vp_ref, j), slab(vc_ref, j)], axis=0)))
    qi = lax.broadcasted_iota(jnp.int32, (2 * BLOCK, 2 * BLOCK), 0) % BLOCK
    si = lax.broadcasted_iota(jnp.int32, (2 * BLOCK, 2 * BLOCK), 1)
    valid = (si > qi) & (si <= qi + WINDOW) & ((n > 0) | (si >= BLOCK))
    row_lo = lax.broadcasted_iota(jnp.int32, (2 * BLOCK, 1), 0) < BLOCK
    slabs = list(range(nslab_q))
    kv_of = [(2 * s) // SWA_GROUP for s in slabs]
    q = [rope(slab(q_ref, s), cos_c, sin_c) for s in slabs]
    qs = [jnp.concatenate([jnp.where(lo, t, 0.0), jnp.where(lo, 0.0, t)], axis=0).astype(BF16) for t in q]
    sc = [lax.dot_general(t, kd[g], NT_DIMS, preferred_element_type=F32) * SWA_SCALE
          for t, g in zip(qs, kv_of)]
    sc = [jnp.where(valid, t, -jnp.inf) for t in sc]
    p = [_sink_softmax(t, jnp.where(row_lo, sink_ref[2 * s], sink_ref[2 * s + 1])) for t, s in zip(sc, slabs)]
    o = [jnp.dot(t.astype(BF16), vd[g], preferred_element_type=F32) for t, g in zip(p, kv_of)]
    for s, t in zip(slabs, o):
        o_ref[:, s * LANES:(s + 1) * LANES] = jnp.where(lo, t[:BLOCK], t[BLOCK:]).astype(o_ref.dtype)


def _swa_prompt(proj, cos, sin_signed, sinks):
    T = proj.shape[0]
    kblk, vblk = TOK_WIDTH // SWA_KV_WIDTH, TOK_WIDTH // SWA_KV_WIDTH + 1
    prev = lambda n: jnp.maximum(n - 1, 0)
    kv_spec = lambda blk, row: pl.BlockSpec((BLOCK, SWA_KV_WIDTH), lambda n: (row(n), blk))
    tab = lambda row: pl.BlockSpec((BLOCK, LANES), lambda n: (row(n), 0))
    cur = lambda n: n
    return pl.pallas_call(
        _swa_kernel, grid=(T // BLOCK,),
        in_specs=[pl.BlockSpec(memory_space=pltpu.SMEM),
                  pl.BlockSpec((BLOCK, TOK_WIDTH), lambda n: (n, 0)),
                  kv_spec(kblk, cur), kv_spec(kblk, prev), kv_spec(vblk, cur), kv_spec(vblk, prev),
                  tab(cur), tab(cur), tab(prev), tab(prev)],
        out_specs=[pl.BlockSpec((BLOCK, TOK_WIDTH), lambda n: (n, 0)),
                   pl.BlockSpec((BLOCK, SWA_KV_WIDTH), lambda n: (n, 0))],
        out_shape=[jax.ShapeDtypeStruct((T, TOK_WIDTH), BF16),
                   jax.ShapeDtypeStruct((T, SWA_KV_WIDTH), F32)],
        compiler_params=_cparams("arbitrary"), name="swa_banded")(
            sinks, proj, proj, proj, proj, proj, cos, sin_signed, cos, sin_signed)


def _swa_step_kernel(sink_ref, q_ref, kn_ref, vn_ref, kt_ref, vt_ref, o_ref, kto_ref, vto_ref, *, bs):
    sink = _sink_column(sink_ref, 0, 1, SWA_Q_HEADS)
    newest = lax.broadcasted_iota(jnp.int32, (SWA_HEAD, WINDOW), 1) == WINDOW - 1
    kv_of_head = lax.broadcasted_iota(jnp.int32, (SWA_Q_HEADS, 1), 0) // SWA_GROUP
    kv_heads = list(range(SWA_KV_HEADS))

    def pick(per_kv):
        out = per_kv[0]
        for g in kv_heads[1:]:
            out = jnp.where(kv_of_head == g, per_kv[g], out)
        return out

    def sample(b, carry):
        q, kn, vn = q_ref[b], kn_ref[b], vn_ref[b]
        kt = [jnp.where(newest, kn[:, g:g + 1], pltpu.roll(kt_ref[b, g], WINDOW - 1, 1)) for g in kv_heads]
        vt = [jnp.where(newest, vn[:, g:g + 1], pltpu.roll(vt_ref[b, g], WINDOW - 1, 1)) for g in kv_heads]
        for g in kv_heads:
            kto_ref[b, g] = kt[g]
            vto_ref[b, g] = vt[g]
        p = _sink_softmax(pick([_dot(q, t) for t in kt]) * SWA_SCALE, sink)
        o_ref[b] = pick([_dot_nt(p, t) for t in vt])
        return carry

    lax.fori_loop(0, bs, sample, 0)


def _swa_step(q, k_new, v_new, kt, vt, layer, sinks, *, bs):
    B = q.shape[0]
    qblk = pl.BlockSpec((bs, SWA_Q_HEADS, SWA_HEAD), lambda i: (i, 0, 0))
    nblk = pl.BlockSpec((bs, SWA_HEAD, SWA_KV_HEADS), lambda i: (i, 0, 0))
    cshape = (bs, SWA_KV_HEADS, SWA_HEAD, WINDOW)
    cin = pl.BlockSpec((None,) + cshape, lambda i: (layer, i, 0, 0, 0))
    cout = pl.BlockSpec(cshape, lambda i: (i, 0, 0, 0))
    cache = jax.ShapeDtypeStruct(kt.shape[1:], F32)
    return pl.pallas_call(
        functools.partial(_swa_step_kernel, bs=bs), grid=(B // bs,),
        in_specs=[pl.BlockSpec(memory_space=pltpu.SMEM), qblk, nblk, nblk, cin, cin],
        out_specs=[qblk, cout, cout],
        out_shape=[jax.ShapeDtypeStruct(q.shape, F32), cache, cache],
        compiler_params=_cparams("parallel"), name="swa_step")(sinks, q, k_new, v_new, kt, vt)


ROW_TILE = 512
FFN_ROW_TILE = 1024
COL_TILE = 512
STEP_BATCH = 8
WKV_STEP_BATCH = 4
FFN_DOWN_K_TILE = FFN_HIDDEN // 4


def _row_tile(m):
    return ROW_TILE if m % ROW_TILE == 0 else m


def _ffn_row_tile(m):
    return FFN_ROW_TILE if m % FFN_ROW_TILE == 0 else _row_tile(m)


def _pad_rows(w, rows):
    return jnp.pad(w, ((0, rows - w.shape[0]), (0, 0)))


def _pad_cols(w, cols):
    return jnp.pad(w, ((0, 0), (0, cols - w.shape[1])))


def _rwkv_weights(w_in, mu, w1, w2, a1, a2, g1, g2):
    w_cat = jnp.concatenate([w_in[0], _pad_cols(w1[0], LORA_PAD), _pad_cols(a1[0], LORA_PAD), g1[0]], axis=1)
    return dict(w_in=w_cat.astype(BF16)[None], mu=mu[0][:, None, :],
                w2=_pad_rows(w2[0], LORA_PAD).astype(BF16), a2=_pad_rows(a2[0], LORA_PAD).astype(BF16),
                g2=g2[0].astype(BF16))


def _unblock_state(s_bd):
    n = RWKV_HEAD
    return jnp.stack([s_bd[:, :n, :n], s_bd[:, n:, n:]], axis=1).reshape(RWKV_HEADS, n, n)


def _rope_tables(pos):
    half = SWA_HEAD // 2
    inv = ROPE_THETA ** (-jnp.arange(half, dtype=F32) / half)
    ang = pos.astype(F32)[:, None] * inv[None, :]
    cos, sin = jnp.cos(ang), jnp.sin(ang)
    reps = LANES // SWA_HEAD
    return jnp.tile(cos, (1, 2 * reps)), jnp.tile(jnp.concatenate([-sin, sin], axis=1), (1, reps))


def _post_mixer(tok, mo, x, sw, layer, tm):
    row = lambda t: t[layer][None, :]
    x1f, x1b = _out_ln([tok, mo], sw["w_out"], layer, x, row(sw["ln1_g"]), row(sw["ln1_b"]), tm=tm)
    hff = _ffn_up(x1b, sw["w_gate"], sw["w_up"], layer, tm=_ffn_row_tile(x.shape[0]), tn=COL_TILE)
    return _out_ln([hff], sw["w_down"], layer, x1f, row(sw["ln2_g"]), row(sw["ln2_b"]),
                   tm=tm, tk=FFN_DOWN_K_TILE)


def kernel(x_prompt, x_sample, mem_prompt, cache_mem_k, cache_mem_v, state_rwkv_shift, state_rwkv_wkv, cache_swa_k, cache_swa_v, w_in_rwkv, rwkv_mu, rwkv_w0, rwkv_w1, rwkv_w2, rwkv_a0, rwkv_a1, rwkv_a2, rwkv_g1, rwkv_g2, rwkv_k_k, rwkv_k_a, rwkv_r_k, rwkv_gn_g, rwkv_gn_b, w_in_swa, swa_sinks, w_mem_kv, w_out, ln1_g, ln1_b, w_gate, w_up, w_down, ln2_g, ln2_b):
    assert DEPTH == 2 and x_prompt.shape[0] == 1 and x_sample.shape[1] == 1
    T = x_prompt.shape[1]
    B = x_sample.shape[0]
    past_len = T
    row = lambda t: t[None, :]
    shared = dict(w_out=w_out.astype(BF16), ln1_g=ln1_g, ln1_b=ln1_b, w_gate=w_gate, w_up=w_up,
                  w_down=w_down.astype(BF16), ln2_g=ln2_g, ln2_b=ln2_b)
    RW = _rwkv_weights(w_in_rwkv, rwkv_mu, rwkv_w1, rwkv_w2, rwkv_a1, rwkv_a2, rwkv_g1, rwkv_g2)
    rk, gn_g, gn_b = row(rwkv_r_k[0].reshape(-1)), row(rwkv_gn_g[0]), row(rwkv_gn_b[0])
    prep_consts = (row(rwkv_w0[0]), row(rwkv_a0[0]), row(rwkv_k_k[0]), row(rwkv_k_a[0]))
    w_swa = w_in_swa.astype(BF16)
    w_kv = w_mem_kv.astype(BF16)
    sinks = swa_sinks[0]
    q_blk_rwkv = 3 * TOK_WIDTH // MEM_WIDTH
    q_blk_swa = (TOK_WIDTH + 2 * SWA_KV_WIDTH) // MEM_WIDTH

    xp = x_prompt[0]
    tm = _row_tile(T)
    kv = [_proj(mem_prompt[0], w_kv, i, tm=MEM_TOKENS, tn=COL_TILE, name="mem_kv") for i in range(DEPTH)]
    prompt_mem_k = jnp.stack([t[:, :MEM_WIDTH] for t in kv]).reshape(DEPTH, 1, MEM_TOKENS, MEM_HEADS, MEM_HEAD)
    prompt_mem_v = jnp.stack([t[:, MEM_WIDTH:] for t in kv]).reshape(DEPTH, 1, MEM_TOKENS, MEM_HEADS, MEM_HEAD)

    proj = _rwkv_in(xp, None, RW["mu"], RW["w_in"], 0, tm=tm)
    tok, s_bd = _wkv_prompt(proj, RW["w2"], RW["a2"], RW["g2"], prep_consts + (rk, gn_g, gn_b))
    mo = _mem_attn(proj, q_blk_rwkv, kv[0], tm=tm)
    xf, xb = _post_mixer(tok, mo, xp, shared, 0, tm)
    prompt_shift = xp[-1][None, None, :]
    prompt_wkv = _unblock_state(s_bd)[None, None]

    proj = _proj(xb, w_swa, 0, tm=_ffn_row_tile(T), tn=COL_TILE, name="swa_proj")
    cos, sin = _rope_tables(jnp.arange(T))
    tok, k_rot = _swa_prompt(proj, cos, sin, sinks)
    mo = _mem_attn(proj, q_blk_swa, kv[1], tm=tm)
    y_prompt, _ = _post_mixer(tok, mo, xf, shared, 1, tm)
    v_last = proj[T - WINDOW:, TOK_WIDTH + SWA_KV_WIDTH:TOK_WIDTH + 2 * SWA_KV_WIDTH]
    prompt_swa_k = k_rot[T - WINDOW:].reshape(1, 1, WINDOW, SWA_KV_HEADS, SWA_HEAD)
    prompt_swa_v = v_last.reshape(1, 1, WINDOW, SWA_KV_HEADS, SWA_HEAD)

    xs = x_sample[:, 0]
    tms = _row_tile(B)
    mem_k = cache_mem_k.reshape(DEPTH, B, MEM_TOKENS * MEM_HEADS, MEM_HEAD)
    mem_v = cache_mem_v.reshape(DEPTH, B, MEM_TOKENS * MEM_HEADS, MEM_HEAD)

    def mem_step(q, layer):
        q_rows = jnp.pad(q.reshape(B, MEM_HEADS, MEM_HEAD), ((0, 0), (0, SUBLANES - MEM_HEADS), (0, 0)))
        out = _mem_attn_step(q_rows, mem_k, mem_v, layer, bs=STEP_BATCH)
        return out[:, :MEM_HEADS].reshape(B, MEM_WIDTH).astype(BF16)

    proj = _rwkv_in(xs, state_rwkv_shift[0], RW["mu"], RW["w_in"], 0, tm=tms)
    vecs = _rwkv_prep_t(proj, RW["w2"], RW["a2"], RW["g2"], *prep_consts)
    lanes_b = lambda t: jnp.broadcast_to(t.reshape(TOK_WIDTH, 1), (TOK_WIDTH, B))
    s_new, tok_t = _wkv_step(state_rwkv_wkv.transpose(0, 2, 3, 4, 1), 0, vecs,
                             (lanes_b(rk), lanes_b(gn_g), lanes_b(gn_b)))
    tok = tok_t.T.astype(BF16)
    mo = mem_step(proj[:, 3 * TOK_WIDTH:3 * TOK_WIDTH + MEM_WIDTH], 0)
    xf, xb = _post_mixer(tok, mo, xs, shared, 0, tms)
    sample_shift = xs[None]
    sample_wkv = s_new.transpose(3, 0, 1, 2)[None]

    proj = _proj(xb, w_swa, 0, tm=tms, tn=COL_TILE, name="swa_proj")
    cos, sin = _rope_tables(jnp.full((B,), past_len))
    qk = _rope(proj, cos, sin, tm=tms)
    chan_major = lambda t: t.reshape(B, SWA_KV_HEADS, SWA_HEAD).transpose(0, 2, 1)
    k_new = chan_major(qk[:, TOK_WIDTH:])
    v_new = chan_major(proj[:, TOK_WIDTH + SWA_KV_WIDTH:TOK_WIDTH + 2 * SWA_KV_WIDTH])
    o, kc, vc = _swa_step(qk[:, :TOK_WIDTH].reshape(B, SWA_Q_HEADS, SWA_HEAD), k_new, v_new,
                          cache_swa_k.transpose(0, 1, 3, 4, 2), cache_swa_v.transpose(0, 1, 3, 4, 2),
                          0, sinks, bs=STEP_BATCH)
    tok = o.reshape(B, TOK_WIDTH).astype(BF16)
    mo = mem_step(proj[:, TOK_WIDTH + 2 * SWA_KV_WIDTH:], 1)
    y_sample, _ = _post_mixer(tok, mo, xf, shared, 1, tms)
    sample_swa_k, sample_swa_v = kc.transpose(0, 3, 1, 2)[None], vc.transpose(0, 3, 1, 2)[None]

    return (y_prompt[None], y_sample[:, None, :], prompt_mem_k, prompt_mem_v, prompt_shift, prompt_wkv,
            prompt_swa_k, prompt_swa_v, sample_shift, sample_wkv, sample_swa_k, sample_swa_v)
```

```python
import functools
import math

import jax
import jax.numpy as jnp
from jax import lax
from jax.experimental import pallas as pl
from jax.experimental.pallas import tpu as pltpu

D_MODEL = 2048
DEPTH = 2
MEM_WIDTH = D_MODEL // 4
TOK_WIDTH = D_MODEL - MEM_WIDTH
RWKV_HEAD = 64
RWKV_HEADS = TOK_WIDTH // RWKV_HEAD
GN_EPS = RWKV_HEAD * 1e-5
SWA_HEAD = 64
SWA_Q_HEADS = TOK_WIDTH // SWA_HEAD
SWA_KV_HEADS = 4
SWA_GROUP = SWA_Q_HEADS // SWA_KV_HEADS
SWA_KV_WIDTH = SWA_KV_HEADS * SWA_HEAD
WINDOW = 128
BLOCK = 128
SWA_SCALE = SWA_HEAD ** -0.5
ROPE_THETA = 10000.0
MEM_TOKENS = 256
MEM_HEADS = 4
MEM_HEAD = MEM_WIDTH // MEM_HEADS
MEM_SCALE = MEM_HEAD ** -0.5
FFN_HIDDEN = int(math.ceil(8 * D_MODEL / 3 / 256)) * 256
ALPHA = (2 * DEPTH) ** 0.25
LN_EPS = 1e-5
LORA_PAD = 128
LORA_IN_WIDTH = 512

LANES = 128
SUBLANES = 8
VMEM_LIMIT_BYTES = 56 * 1024 * 1024

BF16 = jnp.bfloat16
F32 = jnp.float32
NT_DIMS = (((1,), (1,)), ((), ()))
TN_DIMS = (((0,), (0,)), ((), ()))


def _dot(a, b):
    return jnp.dot(a.astype(BF16), b.astype(BF16), preferred_element_type=F32)


def _dot_nt(a, b):
    return lax.dot_general(a.astype(BF16), b.astype(BF16), NT_DIMS, preferred_element_type=F32)


def _dot_tn(a, b):
    return lax.dot_general(a.astype(BF16), b.astype(BF16), TN_DIMS, preferred_element_type=F32)


def _split_dot(x, m):
    hi = x.astype(BF16)
    lo = (x - hi.astype(F32)).astype(BF16)
    return (jnp.dot(hi, m, preferred_element_type=F32)
            + jnp.dot(lo, m, preferred_element_type=F32))


def _head_ones():
    p = lax.broadcasted_iota(jnp.int32, (LANES, LANES), 0)
    q = lax.broadcasted_iota(jnp.int32, (LANES, LANES), 1)
    return ((p // RWKV_HEAD) == (q // RWKV_HEAD)).astype(BF16)


def _cparams(*sem):
    return pltpu.CompilerParams(dimension_semantics=sem, vmem_limit_bytes=VMEM_LIMIT_BYTES)


def _proj_kernel(x_ref, w_ref, o_ref):
    o_ref[...] = jnp.dot(x_ref[...].astype(BF16), w_ref[...], preferred_element_type=F32)


def _proj(x, w, layer, *, tm, tn, name):
    M, K = x.shape
    N = w.shape[2]
    assert M % tm == 0 and N % tn == 0
    return pl.pallas_call(
        _proj_kernel, grid=(M // tm, N // tn),
        in_specs=[pl.BlockSpec((tm, K), lambda i, j: (i, 0)),
                  pl.BlockSpec((None, K, tn), lambda i, j: (layer, 0, j))],
        out_specs=pl.BlockSpec((tm, tn), lambda i, j: (i, j)),
        out_shape=jax.ShapeDtypeStruct((M, N), F32),
        compiler_params=_cparams("parallel", "arbitrary"), name=name)(x, w)


MIX_R, MIX_W, MIX_K, MIX_V, MIX_A, MIX_G = range(6)
RWKV_IN_WIDTH = 3 * TOK_WIDTH + MEM_WIDTH + LORA_IN_WIDTH
RWKV_IN_TILE = LORA_IN_WIDTH
RWKV_IN_KEPT = (MIX_R, MIX_K, MIX_V)
RWKV_IN_SLOT = ([0] * (TOK_WIDTH // RWKV_IN_TILE) + [1] * (TOK_WIDTH // RWKV_IN_TILE)
                + [2] * (TOK_WIDTH // RWKV_IN_TILE) + [3] * (MEM_WIDTH // RWKV_IN_TILE))
LORA_PARTS = ((MIX_W, 0, LORA_PAD), (MIX_A, LORA_PAD, 2 * LORA_PAD), (MIX_G, 2 * LORA_PAD, LORA_IN_WIDTH))
RWKV_IN_SUB = 256


def _rwkv_in_kernel(slot_ref, x_ref, xp_ref, mu_ref, w_ref, wl_ref, o_ref, lhs_ref, *, shift):
    i, j = pl.program_id(0), pl.program_id(1)
    tm = x_ref.shape[0]
    sub = min(RWKV_IN_SUB, tm)
    blocks = [slice(s0, s0 + sub) for s0 in range(0, tm, sub)]

    def x_and_delta(rows):
        x = x_ref[rows, :]
        if not shift:
            return x, xp_ref[rows, :] - x
        if rows.start == 0:
            first = jnp.where(i > 0, xp_ref[SUBLANES - 1:SUBLANES, :], 0.0)
        else:
            first = x_ref[rows.start - 1:rows.start, :]
        rowid = lax.broadcasted_iota(jnp.int32, (sub, 1), 0)
        return x, jnp.where(rowid == 0, first, pltpu.roll(x, 1, 0)) - x

    @pl.when(j == 0)
    def _():
        for rows in blocks:
            x, d = x_and_delta(rows)
            for slot, m in enumerate(RWKV_IN_KEPT):
                lhs_ref[slot, rows, :] = (x + d * mu_ref[m]).astype(BF16)
            lhs_ref[len(RWKV_IN_KEPT), rows, :] = x.astype(BF16)

    nproj = pl.num_programs(1) - 1

    @pl.when(j < nproj)
    def _():
        o_ref[...] = jnp.dot(lhs_ref[slot_ref[j]], w_ref[...], preferred_element_type=F32)

    @pl.when(j == nproj)
    def _():
        for rows in blocks:
            x, d = x_and_delta(rows)
            for m, lo, hi in LORA_PARTS:
                o_ref[rows, lo:hi] = jnp.dot((x + d * mu_ref[m]).astype(BF16), wl_ref[:, lo:hi],
                                             preferred_element_type=F32)


def _rwkv_in(x, xprev, mu, w, w_lora, layer, *, tm):
    M, K = x.shape
    tn = RWKV_IN_TILE
    nproj = w.shape[2] // tn
    assert M % tm == 0 and w.shape[2] + w_lora.shape[1] == RWKV_IN_WIDTH and nproj == len(RWKV_IN_SLOT)
    assert tm % min(RWKV_IN_SUB, tm) == 0
    shift = xprev is None
    if shift:
        rows8 = tm // SUBLANES
        xp_spec = pl.BlockSpec((SUBLANES, K), lambda i, j, m: (jnp.maximum(i * rows8 - 1, 0), 0))
        xprev = x
    else:
        xp_spec = pl.BlockSpec((tm, K), lambda i, j, m: (i, 0))
    slot = jnp.asarray(RWKV_IN_SLOT + [0], jnp.int32)
    grid_spec = pltpu.PrefetchScalarGridSpec(
        num_scalar_prefetch=1, grid=(M // tm, nproj + 1),
        in_specs=[pl.BlockSpec((tm, K), lambda i, j, m: (i, 0)), xp_spec,
                  pl.BlockSpec(mu.shape, lambda i, j, m: (0, 0, 0)),
                  pl.BlockSpec((None, K, tn), lambda i, j, m: (layer, 0, jnp.minimum(j, nproj - 1))),
                  pl.BlockSpec(w_lora.shape, lambda i, j, m: (0, 0))],
        out_specs=pl.BlockSpec((tm, tn), lambda i, j, m: (i, j)),
        scratch_shapes=[pltpu.VMEM((len(RWKV_IN_KEPT) + 1, tm, K), BF16)])
    return pl.pallas_call(
        functools.partial(_rwkv_in_kernel, shift=shift), grid_spec=grid_spec,
        out_shape=jax.ShapeDtypeStruct((M, RWKV_IN_WIDTH), F32),
        compiler_params=_cparams("parallel", "arbitrary"), name="rwkv_in")(slot, x, xprev, mu, w, w_lora)


def _softplus(z):
    return jnp.maximum(z, 0.0) + jnp.log1p(jnp.exp(-jnp.abs(z)))


def _each(f, *lists):
    return [f(*a) for a in zip(*lists)]


def _lora_hidden(hl):
    return jnp.tanh(hl[:, 0:LORA_PAD]), hl[:, LORA_PAD:2 * LORA_PAD], jax.nn.sigmoid(hl[:, 2 * LORA_PAD:])


def _prep_slabs(k, wl, al, w0, a0, k_k, k_a, ones):
    lw = _each(lambda wl_, w0_: -jnp.exp(-_softplus(-(w0_ + wl_)) - 0.5), wl, w0)
    agate = _each(lambda al_, a0_: jax.nn.sigmoid(a0_ + al_), al, a0)
    kkr = _each(lambda k_, c_: k_ * c_, k, k_k)
    ss = _each(lambda t: _split_dot(t * t, ones), kkr)
    kn = _each(lambda t, s_: t / jnp.maximum(jnp.sqrt(s_), 1e-12), kkr, ss)
    bv = _each(lambda n_, a_: n_ * a_, kn, agate)
    kp = _each(lambda k_, a_, c_: k_ * (1.0 + (a_ - 1.0) * c_), k, agate, k_a)
    return lw, kp, kn, bv


def _rwkv_prep_t_kernel(hl_ref, r_ref, k_ref, v_ref, w2_ref, a2_ref, g2_ref, w0_ref, a0_ref, kk_ref, ka_ref,
                        rt_ref, vt_ref, lw_ref, kp_ref, kn_ref, bv_ref, g_ref):
    hw, ha, hg = _lora_hidden(hl_ref[...])
    outs = _prep_slabs([k_ref[...]], [_dot(hw, w2_ref[...])], [_dot(ha, a2_ref[...])], [w0_ref[...]],
                       [a0_ref[...]], [kk_ref[...]], [ka_ref[...]], _head_ones())
    vals = [r_ref[...], v_ref[...]] + [t[0] for t in outs] + [_dot(hg, g2_ref[...])]
    for ref, val in zip((rt_ref, vt_ref, lw_ref, kp_ref, kn_ref, bv_ref, g_ref), vals):
        ref[...] = val.T


def _rwkv_prep_t(proj, w2p, a2p, g2, w0, a0, k_k, k_a):
    B = proj.shape[0]
    assert B == LANES
    nslab = TOK_WIDTH // LANES
    lora_blk = (RWKV_IN_WIDTH - LORA_IN_WIDTH) // LORA_IN_WIDTH
    col = lambda s: (0, s)
    slab = lambda base: pl.BlockSpec((B, LANES), lambda s, base=base: (0, base + s))
    outs = [jax.ShapeDtypeStruct((TOK_WIDTH, B), F32)] * 7
    return pl.pallas_call(
        _rwkv_prep_t_kernel, grid=(nslab,),
        in_specs=[pl.BlockSpec((B, LORA_IN_WIDTH), lambda s: (0, lora_blk)),
                  slab(0), slab(nslab), slab(2 * nslab),
                  pl.BlockSpec((LORA_PAD, LANES), col),
                  pl.BlockSpec((LORA_PAD, LANES), col),
                  pl.BlockSpec((2 * LORA_PAD, LANES), col),
                  pl.BlockSpec((1, LANES), col), pl.BlockSpec((1, LANES), col),
                  pl.BlockSpec((1, LANES), col), pl.BlockSpec((1, LANES), col)],
        out_specs=[pl.BlockSpec((LANES, B), lambda s: (s, 0))] * 7,
        out_shape=outs, compiler_params=_cparams("arbitrary"),
        name="rwkv_prep_t")(proj, proj, proj, proj, w2p, a2p, g2, w0, a0, k_k, k_a)


def _gn_gate(y, r, kp, v, g, rk, gg, gb, ones):
    inv_n = 1.0 / RWKV_HEAD
    rows = y[0].shape[0]
    sums = _each(lambda y_, r_, k_, rk_: _split_dot(jnp.concatenate([y_, r_ * k_ * rk_], axis=0), ones),
                 y, r, kp, rk)
    d = _each(lambda y_, s_: y_ - s_[:rows] * inv_n, y, sums)
    var = _each(lambda d_: _split_dot(d_ * d_, ones) * inv_n, d)
    return _each(lambda d_, var_, gg_, gb_, s_, v_, g_:
                 (d_ * lax.rsqrt(var_ + GN_EPS) * gg_ + gb_ + s_[rows:] * v_) * g_,
                 d, var, gg, gb, sums, v, g)


WKV_CHUNK = 64


def _wkv_masks():
    n = 2 * WKV_CHUNK
    p = lax.broadcasted_iota(jnp.int32, (n, n), 0)
    q = lax.broadcasted_iota(jnp.int32, (n, n), 1)
    same = lambda b: (p // b) == (q // b)
    pt, qt = p % WKV_CHUNK, q % WKV_CHUNK
    s8, s16, s32, s64 = same(8), same(16), same(32), same(WKV_CHUNK)
    return dict(strict=s64 & (pt > qt), incl=s64 & (pt >= qt), s8=s8,
                e16=s16 & ~s8, e32=s32 & ~s16, e64=s64 & ~s32,
                eye=(p == q).astype(F32))


WKV_PAIRS = TOK_WIDTH // LANES
WKV_STEP_CHUNKS = 2


def _wkv_lanes(r, lw, k, v, kn, bv, mk, tri, lane_lo):
    stack = lambda x: jnp.concatenate([jnp.where(lane_lo, x, 0.0), jnp.where(lane_lo, 0.0, x)], axis=0)
    n = 2 * WKV_CHUNK
    c = _each(lambda t: _split_dot_left(tri, t), lw)
    c_last = _each(lambda t: t[WKV_CHUNK - 1:WKV_CHUNK, :], c)
    e_out = _each(lambda t: jnp.exp(-t), c)
    e_end = _each(lambda t, tl: jnp.exp(tl - t), c, c_last)
    ah = _each(lambda kn_, c_, lw_: stack(-kn_ * jnp.exp(c_ - lw_)), kn, c, lw)
    rh = _each(lambda r_, c_: stack(r_ * jnp.exp(c_)), r, c)
    bh = _each(lambda b_, e_: stack(b_ * e_), bv, e_out)
    kh = _each(lambda k_, e_: stack(k_ * e_), k, e_out)
    bbar = _each(lambda b_, e_: stack(b_ * e_), bv, e_end)
    kbar = _each(lambda k_, e_: stack(k_ * e_), k, e_end)
    vs = _each(stack, v)
    gm = _each(lambda a_, r_, b_, k_: _dot_nt(jnp.concatenate([a_, r_], axis=0),
                                              jnp.concatenate([b_, k_], axis=0)), ah, rh, bh, kh)
    a_ab = _each(lambda g_: jnp.where(mk["strict"], g_[:n, :n], 0.0), gm)
    a_ak = _each(lambda g_: jnp.where(mk["strict"], g_[:n, n:], 0.0), gm)
    l_rb = _each(lambda g_: jnp.where(mk["incl"], g_[n:, :n], 0.0), gm)
    l_rk = _each(lambda g_: jnp.where(mk["incl"], g_[n:, n:], 0.0), gm)
    d1 = _each(lambda a_: jnp.where(mk["s8"], a_, 0.0), a_ab)
    x = _each(lambda d_: mk["eye"] + d_, d1)
    d2 = _each(lambda d_: _dot(d_, d_), d1)
    x = _each(lambda x_, d_: x_ + _dot(x_, d_), x, d2)
    d4 = _each(lambda d_: _dot(d_, d_), d2)
    x = _each(lambda x_, d_: x_ + _dot(x_, d_), x, d4)
    for lvl in ("e16", "e32", "e64"):
        ex = _each(lambda a_, x_: _dot(jnp.where(mk[lvl], a_, 0.0), x_), a_ab, x)
        x = _each(lambda x_, e_: x_ + _dot(x_, e_), x, ex)
    av = _each(_dot, a_ak, vs)
    tw = _each(lambda x_, a_, v_: _dot(x_, jnp.concatenate([a_, v_], axis=1)), x, ah, av)
    lwm = _each(_dot, l_rb, tw)
    lv = _each(_dot, l_rk, vs)
    qm = _each(lambda r_, l_: r_ + l_[:, :n], rh, lwm)
    y0 = _each(lambda l_, v_: l_[:, n:] + v_, lwm, lv)
    mt = _each(lambda t_, b_: _dot_tn(t_[:, :n], b_), tw, bbar)
    nt = _each(lambda t_, b_, v_, k_: _dot_tn(t_[:, n:], b_) + _dot_tn(v_, k_), tw, bbar, vs, kbar)
    return qm, y0, mt, nt, _each(jnp.exp, c_last)


def _split_dot_left(m, x):
    hi = x.astype(BF16)
    lo = (x - hi.astype(F32)).astype(BF16)
    return (jnp.dot(m, hi, preferred_element_type=F32)
            + jnp.dot(m, lo, preferred_element_type=F32))


def _wkv_kernel(r_ref, k_ref, v_ref, hl_ref, w2_ref, a2_ref, g2_ref, w0_ref, a0_ref, kk_ref, ka_ref,
                rk_ref, gg_ref, gb_ref, o_ref, sout_ref, s_scr):
    c_idx = pl.program_id(0)

    @pl.when(c_idx == 0)
    def _():
        s_scr[...] = jnp.zeros_like(s_scr)

    mk = _wkv_masks()
    ti = lax.broadcasted_iota(jnp.int32, (WKV_CHUNK, WKV_CHUNK), 0)
    tj = lax.broadcasted_iota(jnp.int32, (WKV_CHUNK, WKV_CHUNK), 1)
    tri = (ti >= tj).astype(BF16)
    lane_lo = lax.broadcasted_iota(jnp.int32, (WKV_CHUNK, LANES), 1) < RWKV_HEAD
    ones = _head_ones()
    pairs = range(WKV_PAIRS)
    lanes = [(slice(ch * WKV_CHUNK, (ch + 1) * WKV_CHUNK), slice(p * LANES, (p + 1) * LANES))
             for ch in range(WKV_STEP_CHUNKS) for p in pairs]
    cut = lambda t: [t[rows, sl] for rows, sl in lanes]
    rep = lambda t: [t[:, sl] for _, sl in lanes]
    hw, ha, hg = _lora_hidden(hl_ref[...])
    wl, al, g = _dot(hw, w2_ref[...]), _dot(ha, a2_ref[...]), _dot(hg, g2_ref[...])
    r, v = cut(r_ref), cut(v_ref)
    lw, kp, kn, bv = _prep_slabs(cut(k_ref), cut(wl), cut(al), rep(w0_ref), rep(a0_ref),
                                 rep(kk_ref), rep(ka_ref), ones)
    qm, y0, mt, nt, dec = _wkv_lanes(r, lw, kp, v, kn, bv, mk, tri, lane_lo)
    S = [s_scr[p] for p in pairs]
    y = []
    for ch in range(WKV_STEP_CHUNKS):
        part = slice(ch * WKV_PAIRS, (ch + 1) * WKV_PAIRS)
        ys = _each(lambda q_, s_, y_: _dot_nt(q_, s_) + y_, qm[part], S, y0[part])
        y += _each(lambda t: t[:WKV_CHUNK, :] + t[WKV_CHUNK:, :], ys)
        S = _each(lambda s_, d_, m_, n_: s_ * d_ + _dot(s_, m_) + n_, S, dec[part], mt[part], nt[part])
    for p in pairs:
        s_scr[p] = S[p]
    tok = _gn_gate(y, r, kp, v, cut(g), rep(rk_ref), rep(gg_ref), rep(gb_ref), ones)
    for (rows, sl), t in zip(lanes, tok):
        o_ref[rows, sl] = t.astype(o_ref.dtype)

    @pl.when(c_idx == pl.num_programs(0) - 1)
    def _():
        sout_ref[...] = s_scr[...]


def _wkv_prompt(proj, w2p, a2p, g2, consts):
    T = proj.shape[0]
    rows = WKV_STEP_CHUNKS * WKV_CHUNK
    assert T % rows == 0
    lora_blk = (RWKV_IN_WIDTH - LORA_IN_WIDTH) // LORA_IN_WIDTH
    tok = lambda blk: pl.BlockSpec((rows, TOK_WIDTH), lambda c, blk=blk: (c, blk))
    full = lambda a: pl.BlockSpec(a.shape, lambda c: (0,) * a.ndim)
    weights = (w2p, a2p, g2) + tuple(consts)
    return pl.pallas_call(
        _wkv_kernel, grid=(T // rows,),
        in_specs=[tok(0), tok(1), tok(2), pl.BlockSpec((rows, LORA_IN_WIDTH), lambda c: (c, lora_blk))]
        + [full(a) for a in weights],
        out_specs=[tok(0), pl.BlockSpec((WKV_PAIRS, LANES, LANES), lambda c: (0, 0, 0))],
        out_shape=[jax.ShapeDtypeStruct((T, TOK_WIDTH), BF16),
                   jax.ShapeDtypeStruct((WKV_PAIRS, LANES, LANES), F32)],
        scratch_shapes=[pltpu.VMEM((WKV_PAIRS, LANES, LANES), F32)],
        compiler_params=_cparams("arbitrary"), name="wkv_chunked")(proj, proj, proj, proj, *weights)


WKV_STEP_UNROLL = 4


def _wkv_step_kernel(s_ref, r_ref, v_ref, lw_ref, kp_ref, kn_ref, bv_ref, g_ref, rk_ref, gg_ref, gb_ref,
                     so_ref, tok_ref, y_scr):
    n = RWKV_HEAD
    inv_n = 1.0 / n
    for hh in range(2):
        rows = slice(hh * n, (hh + 1) * n)
        a, w = -kn_ref[rows, :], jnp.exp(lw_ref[rows, :])
        b, k, r = bv_ref[rows, :], kp_ref[rows, :], r_ref[rows, :]

        def value_row(i, carry, hh=hh, a=a, w=w, b=b, k=k, r=r):
            s = s_ref[hh, i]
            sa = jnp.sum(s * a, axis=0, keepdims=True)
            s_new = s * w + sa * b + v_ref[pl.ds(hh * n + i, 1), :] * k
            so_ref[hh, i] = s_new
            y_scr[pl.ds(hh * n + i, 1), :] = jnp.sum(s_new * r, axis=0, keepdims=True)
            return carry

        lax.fori_loop(0, n, value_row, 0, unroll=WKV_STEP_UNROLL)
    for hh in range(2):
        rows = slice(hh * n, (hh + 1) * n)
        y = y_scr[rows, :]
        d = y - jnp.sum(y, axis=0, keepdims=True) * inv_n
        var = jnp.sum(d * d, axis=0, keepdims=True) * inv_n
        bonus = jnp.sum(r_ref[rows, :] * kp_ref[rows, :] * rk_ref[rows, :], axis=0, keepdims=True)
        yn = d * lax.rsqrt(var + GN_EPS) * gg_ref[rows, :] + gb_ref[rows, :]
        tok_ref[rows, :] = (yn + bonus * v_ref[rows, :]) * g_ref[rows, :]


def _wkv_step(state, layer, vecs, consts):
    B = state.shape[-1]
    sshape = (2, RWKV_HEAD, RWKV_HEAD, B)
    vblk = pl.BlockSpec((LANES, B), lambda p: (p, 0))
    return pl.pallas_call(
        _wkv_step_kernel, grid=(WKV_PAIRS,),
        in_specs=[pl.BlockSpec((None,) + sshape, lambda p: (layer, p, 0, 0, 0))] + [vblk] * 10,
        out_specs=[pl.BlockSpec(sshape, lambda p: (p, 0, 0, 0)), vblk],
        out_shape=[jax.ShapeDtypeStruct(state.shape[1:], F32), jax.ShapeDtypeStruct((TOK_WIDTH, B), F32)],
        scratch_shapes=[pltpu.VMEM((LANES, B), F32)],
        compiler_params=_cparams("parallel"), name="wkv_step")(state, *vecs, *consts)


def _softmax_rows(s):
    m = jnp.max(s, axis=-1, keepdims=True)
    e = jnp.exp(s - m)
    return e / jnp.sum(e, axis=-1, keepdims=True)


def _mem_attn_kernel(q_ref, k_ref, v_ref, o_ref):
    for h in range(MEM_HEADS):
        sl = slice(h * MEM_HEAD, (h + 1) * MEM_HEAD)
        p = _softmax_rows(_dot_nt(q_ref[:, sl], k_ref[:, sl]) * MEM_SCALE)
        o_ref[:, sl] = _dot(p, v_ref[:, sl]).astype(o_ref.dtype)


def _mem_attn(proj, qblk, kv, *, tm):
    M = proj.shape[0]
    return pl.pallas_call(
        _mem_attn_kernel, grid=(M // tm,),
        in_specs=[pl.BlockSpec((tm, MEM_WIDTH), lambda i: (i, qblk)),
                  pl.BlockSpec((MEM_TOKENS, MEM_WIDTH), lambda i: (0, 0)),
                  pl.BlockSpec((MEM_TOKENS, MEM_WIDTH), lambda i: (0, 1))],
        out_specs=pl.BlockSpec((tm, MEM_WIDTH), lambda i: (i, 0)),
        out_shape=jax.ShapeDtypeStruct((M, MEM_WIDTH), BF16),
        compiler_params=_cparams("parallel"), name="mem_attn")(proj, kv, kv)


def _mem_attn_step_kernel(q_ref, k_ref, v_ref, o_ref, *, bs):
    rows = MEM_TOKENS * MEM_HEADS
    col_head = lax.broadcasted_iota(jnp.int32, (SUBLANES, rows), 1) % MEM_HEADS
    row_head = lax.broadcasted_iota(jnp.int32, (SUBLANES, rows), 0) % MEM_HEADS
    own = col_head == row_head

    def sample(b, carry):
        s = _dot_nt(q_ref[b], k_ref[b]) * MEM_SCALE
        o_ref[b] = _dot(_softmax_rows(jnp.where(own, s, -jnp.inf)), v_ref[b])
        return carry

    lax.fori_loop(0, bs, sample, 0)


def _mem_attn_step(q, mk, mv, layer, *, bs):
    B = q.shape[0]
    qblk = pl.BlockSpec((bs, SUBLANES, MEM_HEAD), lambda i: (i, 0, 0))
    cblk = pl.BlockSpec((None, bs, MEM_TOKENS * MEM_HEADS, MEM_HEAD), lambda i: (layer, i, 0, 0))
    return pl.pallas_call(
        functools.partial(_mem_attn_step_kernel, bs=bs), grid=(B // bs,),
        in_specs=[qblk, cblk, cblk], out_specs=qblk,
        out_shape=jax.ShapeDtypeStruct(q.shape, F32),
        compiler_params=_cparams("parallel"), name="mem_attn_step")(q, mk, mv)


def _deepnorm_ln(res, h, g, beta):
    z = ALPHA * res + h
    mu = jnp.mean(z, axis=-1, keepdims=True)
    d = z - mu
    var = jnp.mean(d * d, axis=-1, keepdims=True)
    return d * lax.rsqrt(var + LN_EPS) * g + beta


def _out_ln_kernel(*refs, widths):
    n = len(widths)
    lhs, (w_ref, res_ref, g_ref, beta_ref, of_ref, ob_ref) = refs[:n], refs[n:]
    h, off = None, 0
    for a_ref, kw in zip(lhs, widths):
        part = jnp.dot(a_ref[...], w_ref[off:off + kw, :], preferred_element_type=F32)
        h = part if h is None else h + part
        off += kw
    out = _deepnorm_ln(res_ref[...], h, g_ref[...], beta_ref[...])
    of_ref[...] = out
    ob_ref[...] = out.astype(BF16)


def _out_ln_tiled_kernel(a_ref, w_ref, res_ref, g_ref, beta_ref, of_ref, ob_ref, acc_ref):
    k = pl.program_id(1)

    @pl.when(k == 0)
    def _():
        acc_ref[...] = jnp.zeros_like(acc_ref)

    acc_ref[...] += jnp.dot(a_ref[...], w_ref[...], preferred_element_type=F32)

    @pl.when(k == pl.num_programs(1) - 1)
    def _():
        out = _deepnorm_ln(res_ref[...], acc_ref[...], g_ref[...], beta_ref[...])
        of_ref[...] = out
        ob_ref[...] = out.astype(BF16)


def _out_ln(pieces, w, layer, res, g, beta, *, tm, tk=None):
    M = pieces[0].shape[0]
    K, N = w.shape[1], w.shape[2]
    widths = tuple(p.shape[1] for p in pieces)
    assert sum(widths) == K and M % tm == 0
    out_shape = [jax.ShapeDtypeStruct((M, N), F32), jax.ShapeDtypeStruct((M, N), BF16)]
    if tk is None:
        row = lambda i: (i, 0)
        cst = lambda i: (0, 0)
        in_specs = ([pl.BlockSpec((tm, kw), row) for kw in widths]
                    + [pl.BlockSpec((None, K, N), lambda i: (layer, 0, 0)),
                       pl.BlockSpec((tm, N), row), pl.BlockSpec((1, N), cst), pl.BlockSpec((1, N), cst)])
        oblk = pl.BlockSpec((tm, N), row)
        return pl.pallas_call(
            functools.partial(_out_ln_kernel, widths=widths), grid=(M // tm,),
            in_specs=in_specs, out_specs=[oblk, oblk], out_shape=out_shape,
            compiler_params=_cparams("parallel"), name="out_ln")(*pieces, w, res, g, beta)
    assert len(pieces) == 1 and K % tk == 0
    row = lambda i, k: (i, 0)
    cst = lambda i, k: (0, 0)
    oblk = pl.BlockSpec((tm, N), row)
    return pl.pallas_call(
        _out_ln_tiled_kernel, grid=(M // tm, K // tk),
        in_specs=[pl.BlockSpec((tm, tk), lambda i, k: (i, k)),
                  pl.BlockSpec((None, tk, N), lambda i, k: (layer, k, 0)),
                  pl.BlockSpec((tm, N), row), pl.BlockSpec((1, N), cst), pl.BlockSpec((1, N), cst)],
        out_specs=[oblk, oblk], out_shape=out_shape,
        scratch_shapes=[pltpu.VMEM((tm, N), F32)],
        compiler_params=_cparams("parallel", "arbitrary"), name="ffn_down_ln")(
            pieces[0], w, res, g, beta)


def _ffn_up_kernel(x_ref, wg_ref, wu_ref, o_ref):
    x = x_ref[...]
    gate = jnp.dot(x, wg_ref[...].astype(BF16), preferred_element_type=F32)
    up = jnp.dot(x, wu_ref[...].astype(BF16), preferred_element_type=F32)
    o_ref[...] = (gate * jax.nn.sigmoid(gate) * up).astype(o_ref.dtype)


def _ffn_up(x, wg, wu, layer, *, tm, tn):
    M, K = x.shape
    N = wg.shape[2]
    wblk = pl.BlockSpec((None, K, tn), lambda i, j: (layer, 0, j))
    return pl.pallas_call(
        _ffn_up_kernel, grid=(M // tm, N // tn),
        in_specs=[pl.BlockSpec((tm, K), lambda i, j: (i, 0)), wblk, wblk],
        out_specs=pl.BlockSpec((tm, tn), lambda i, j: (i, j)),
        out_shape=jax.ShapeDtypeStruct((M, N), BF16),
        compiler_params=_cparams("parallel", "arbitrary"), name="ffn_up")(x, wg, wu)


def _rope_kernel(x_ref, cos_ref, sin_ref, o_ref):
    x = x_ref[...]
    lane = lax.broadcasted_iota(jnp.int32, x.shape, 1)
    first_half = (lane % SWA_HEAD) < (SWA_HEAD // 2)
    partner = jnp.where(first_half, pltpu.roll(x, LANES - SWA_HEAD // 2, 1),
                        pltpu.roll(x, SWA_HEAD // 2, 1))
    o_ref[...] = x * cos_ref[...] + partner * sin_ref[...]


def _rope(proj, cos, sin_signed, *, tm):
    M = proj.shape[0]
    width = TOK_WIDTH + SWA_KV_WIDTH
    blk = pl.BlockSpec((tm, LANES), lambda i, s: (i, s))
    tab = pl.BlockSpec((tm, LANES), lambda i, s: (i, 0))
    return pl.pallas_call(
        _rope_kernel, grid=(M // tm, width // LANES),
        in_specs=[blk, tab, tab], out_specs=blk,
        out_shape=jax.ShapeDtypeStruct((M, width), F32),
        compiler_params=_cparams("parallel", "arbitrary"), name="rope")(proj, cos, sin_signed)


def _sink_column(sink_ref, base, rows_per_head, nheads):
    rows = rows_per_head * nheads
    hid = lax.broadcasted_iota(jnp.int32, (rows, 1), 0) // rows_per_head
    col = jnp.zeros((rows, 1), F32)
    for j in range(nheads):
        col = jnp.where(hid == j, sink_ref[base + j], col)
    return col


def _sink_softmax(s, sink):
    m = jnp.maximum(jnp.max(s, axis=-1, keepdims=True), sink)
    p = jnp.exp(s - m)
    return p / (jnp.sum(p, axis=-1, keepdims=True) + jnp.exp(sink - m))


def _swa_kernel(sink_ref, q_ref, kc_ref, kp_ref, vc_ref, vp_ref, cosc_ref, sinc_ref, cosp_ref, sinp_ref,
                o_ref, krot_ref):
    n = pl.program_id(0)
    nslab_q = TOK_WIDTH // LANES
    slab = lambda ref, s: ref[:, s * LANES:(s + 1) * LANES]
    cos_c, sin_c, cos_p, sin_p = cosc_ref[...], sinc_ref[...], cosp_ref[...], sinp_ref[...]
    lane = lax.broadcasted_iota(jnp.int32, (BLOCK, LANES), 1)
    first_half = (lane % SWA_HEAD) < (SWA_HEAD // 2)
    lo = lane < SWA_HEAD

    def rope(x, cos, sin):
        partner = jnp.where(first_half, pltpu.roll(x, LANES - SWA_HEAD // 2, 1),
                            pltpu.roll(x, SWA_HEAD // 2, 1))
        return x * cos + partner * sin

    kv_slabs = SWA_KV_WIDTH // LANES
    k_cur = [rope(slab(kc_ref, j), cos_c, sin_c) for j in range(kv_slabs)]
    k_prev = [rope(slab(kp_ref, j), cos_p, sin_p) for j in range(kv_slabs)]
    for j in range(kv_slabs):
        krot_ref[:, j * LANES:(j + 1) * LANES] = k_cur[j]
    lane2 = lax.broadcasted_iota(jnp.int32, (2 * BLOCK, LANES), 1)
    kd, vd = [], []
    for g in range(SWA_KV_HEADS):
        j, half = divmod(g, 2)
        keep = (lane2 < SWA_HEAD) if half == 0 else (lane2 >= SWA_HEAD)
        dup = lambda t: jnp.where(keep, t, pltpu.roll(t, SWA_HEAD, 1)).astype(BF16)
        kd.append(dup(jnp.concatenate([k_prev[j], k_cur[j]], axis=0)))
        vd.append(dup(jnp.concatenate([slab(vp_ref, j), slab(vc_ref, j)], axis=0)))
    qi = lax.broadcasted_iota(jnp.int32, (2 * BLOCK, 2 * BLOCK), 0) % BLOCK
    si = lax.broadcasted_iota(jnp.int32, (2 * BLOCK, 2 * BLOCK), 1)
    valid = (si > qi) & (si <= qi + WINDOW) & ((n > 0) | (si >= BLOCK))
    row_lo = lax.broadcasted_iota(jnp.int32, (2 * BLOCK, 1), 0) < BLOCK
    slabs = list(range(nslab_q))
    kv_of = [(2 * s) // SWA_GROUP for s in slabs]
    q = [rope(slab(q_ref, s), cos_c, sin_c) for s in slabs]
    qs = [jnp.concatenate([jnp.where(lo, t, 0.0), jnp.where(lo, 0.0, t)], axis=0).astype(BF16) for t in q]
    sc = [lax.dot_general(t, kd[g], NT_DIMS, preferred_element_type=F32) * SWA_SCALE
          for t, g in zip(qs, kv_of)]
    sc = [jnp.where(valid, t, -jnp.inf) for t in sc]
    p = [_sink_softmax(t, jnp.where(row_lo, sink_ref[2 * s], sink_ref[2 * s + 1])) for t, s in zip(sc, slabs)]
    o = [jnp.dot(t.astype(BF16), vd[g], preferred_element_type=F32) for t, g in zip(p, kv_of)]
    for s, t in zip(slabs, o):
        o_ref[:, s * LANES:(s + 1) * LANES] = jnp.where(lo, t[:BLOCK], t[BLOCK:]).astype(o_ref.dtype)


def _swa_prompt(proj, cos, sin_signed, sinks):
    T = proj.shape[0]
    kblk, vblk = TOK_WIDTH // SWA_KV_WIDTH, TOK_WIDTH // SWA_KV_WIDTH + 1
    prev = lambda n: jnp.maximum(n - 1, 0)
    kv_spec = lambda blk, row: pl.BlockSpec((BLOCK, SWA_KV_WIDTH), lambda n: (row(n), blk))
    tab = lambda row: pl.BlockSpec((BLOCK, LANES), lambda n: (row(n), 0))
    cur = lambda n: n
    return pl.pallas_call(
        _swa_kernel, grid=(T // BLOCK,),
        in_specs=[pl.BlockSpec(memory_space=pltpu.SMEM),
                  pl.BlockSpec((BLOCK, TOK_WIDTH), lambda n: (n, 0)),
                  kv_spec(kblk, cur), kv_spec(kblk, prev), kv_spec(vblk, cur), kv_spec(vblk, prev),
                  tab(cur), tab(cur), tab(prev), tab(prev)],
        out_specs=[pl.BlockSpec((BLOCK, TOK_WIDTH), lambda n: (n, 0)),
                   pl.BlockSpec((BLOCK, SWA_KV_WIDTH), lambda n: (n, 0))],
        out_shape=[jax.ShapeDtypeStruct((T, TOK_WIDTH), BF16),
                   jax.ShapeDtypeStruct((T, SWA_KV_WIDTH), F32)],
        compiler_params=_cparams("arbitrary"), name="swa_banded")(
            sinks, proj, proj, proj, proj, proj, cos, sin_signed, cos, sin_signed)


def _swa_step_kernel(sink_ref, q_ref, kn_ref, vn_ref, kt_ref, vt_ref, o_ref, kto_ref, vto_ref, *, bs):
    sink = _sink_column(sink_ref, 0, 1, SWA_Q_HEADS)
    newest = lax.broadcasted_iota(jnp.int32, (SWA_HEAD, WINDOW), 1) == WINDOW - 1
    kv_of_head = lax.broadcasted_iota(jnp.int32, (SWA_Q_HEADS, 1), 0) // SWA_GROUP
    kv_heads = list(range(SWA_KV_HEADS))

    def pick(per_kv):
        out = per_kv[0]
        for g in kv_heads[1:]:
            out = jnp.where(kv_of_head == g, per_kv[g], out)
        return out

    def sample(b, carry):
        q, kn, vn = q_ref[b], kn_ref[b], vn_ref[b]
        kt = [jnp.where(newest, kn[:, g:g + 1], pltpu.roll(kt_ref[b, g], WINDOW - 1, 1)) for g in kv_heads]
        vt = [jnp.where(newest, vn[:, g:g + 1], pltpu.roll(vt_ref[b, g], WINDOW - 1, 1)) for g in kv_heads]
        for g in kv_heads:
            kto_ref[b, g] = kt[g]
            vto_ref[b, g] = vt[g]
        p = _sink_softmax(pick([_dot(q, t) for t in kt]) * SWA_SCALE, sink)
        o_ref[b] = pick([_dot_nt(p, t) for t in vt])
        return carry

    lax.fori_loop(0, bs, sample, 0)


def _swa_step(q, k_new, v_new, kt, vt, layer, sinks, *, bs):
    B = q.shape[0]
    qblk = pl.BlockSpec((bs, SWA_Q_HEADS, SWA_HEAD), lambda i: (i, 0, 0))
    nblk = pl.BlockSpec((bs, SWA_HEAD, SWA_KV_HEADS), lambda i: (i, 0, 0))
    cshape = (bs, SWA_KV_HEADS, SWA_HEAD, WINDOW)
    cin = pl.BlockSpec((None,) + cshape, lambda i: (layer, i, 0, 0, 0))
    cout = pl.BlockSpec(cshape, lambda i: (i, 0, 0, 0))
    cache = jax.ShapeDtypeStruct(kt.shape[1:], F32)
    return pl.pallas_call(
        functools.partial(_swa_step_kernel, bs=bs), grid=(B // bs,),
        in_specs=[pl.BlockSpec(memory_space=pltpu.SMEM), qblk, nblk, nblk, cin, cin],
        out_specs=[qblk, cout, cout],
        out_shape=[jax.ShapeDtypeStruct(q.shape, F32), cache, cache],
        compiler_params=_cparams("parallel"), name="swa_step")(sinks, q, k_new, v_new, kt, vt)


ROW_TILE = 512
FFN_ROW_TILE = 1024
COL_TILE = 512
STEP_BATCH = 16
FFN_DOWN_K_TILE = FFN_HIDDEN // 4


def _row_tile(m):
    return ROW_TILE if m % ROW_TILE == 0 else m


def _ffn_row_tile(m):
    return FFN_ROW_TILE if m % FFN_ROW_TILE == 0 else _row_tile(m)


def _pad_rows(w, rows):
    return jnp.pad(w, ((0, rows - w.shape[0]), (0, 0)))


def _pad_cols(w, cols):
    return jnp.pad(w, ((0, 0), (0, cols - w.shape[1])))


def _rwkv_weights(w_in, mu, w1, w2, a1, a2, g1, g2):
    w_lora = jnp.concatenate([_pad_cols(w1[0], LORA_PAD), _pad_cols(a1[0], LORA_PAD), g1[0]], axis=1)
    return dict(w_in=w_in.astype(BF16), w_lora=w_lora.astype(BF16), mu=mu[0][:, None, :],
                w2=_pad_rows(w2[0], LORA_PAD).astype(BF16), a2=_pad_rows(a2[0], LORA_PAD).astype(BF16),
                g2=g2[0].astype(BF16))


def _unblock_state(s_bd):
    n = RWKV_HEAD
    return jnp.stack([s_bd[:, :n, :n], s_bd[:, n:, n:]], axis=1).reshape(RWKV_HEADS, n, n)


def _rope_tables(pos):
    half = SWA_HEAD // 2
    inv = ROPE_THETA ** (-jnp.arange(half, dtype=F32) / half)
    ang = pos.astype(F32)[:, None] * inv[None, :]
    cos, sin = jnp.cos(ang), jnp.sin(ang)
    reps = LANES // SWA_HEAD
    return jnp.tile(cos, (1, 2 * reps)), jnp.tile(jnp.concatenate([-sin, sin], axis=1), (1, reps))


def _post_mixer(tok, mo, x, sw, layer, tm):
    row = lambda t: t[layer][None, :]
    x1f, x1b = _out_ln([tok, mo], sw["w_out"], layer, x, row(sw["ln1_g"]), row(sw["ln1_b"]), tm=tm)
    hff = _ffn_up(x1b, sw["w_gate"], sw["w_up"], layer, tm=_ffn_row_tile(x.shape[0]), tn=COL_TILE)
    return _out_ln([hff], sw["w_down"], layer, x1f, row(sw["ln2_g"]), row(sw["ln2_b"]),
                   tm=tm, tk=FFN_DOWN_K_TILE)


def kernel(x_prompt, x_sample, mem_prompt, cache_mem_k, cache_mem_v, state_rwkv_shift, state_rwkv_wkv, cache_swa_k, cache_swa_v, w_in_rwkv, rwkv_mu, rwkv_w0, rwkv_w1, rwkv_w2, rwkv_a0, rwkv_a1, rwkv_a2, rwkv_g1, rwkv_g2, rwkv_k_k, rwkv_k_a, rwkv_r_k, rwkv_gn_g, rwkv_gn_b, w_in_swa, swa_sinks, w_mem_kv, w_out, ln1_g, ln1_b, w_gate, w_up, w_down, ln2_g, ln2_b):
    assert DEPTH == 2 and x_prompt.shape[0] == 1 and x_sample.shape[1] == 1
    T = x_prompt.shape[1]
    B = x_sample.shape[0]
    past_len = T
    row = lambda t: t[None, :]
    shared = dict(w_out=w_out.astype(BF16), ln1_g=ln1_g, ln1_b=ln1_b, w_gate=w_gate, w_up=w_up,
                  w_down=w_down.astype(BF16), ln2_g=ln2_g, ln2_b=ln2_b)
    RW = _rwkv_weights(w_in_rwkv, rwkv_mu, rwkv_w1, rwkv_w2, rwkv_a1, rwkv_a2, rwkv_g1, rwkv_g2)
    rk, gn_g, gn_b = row(rwkv_r_k[0].reshape(-1)), row(rwkv_gn_g[0]), row(rwkv_gn_b[0])
    prep_consts = (row(rwkv_w0[0]), row(rwkv_a0[0]), row(rwkv_k_k[0]), row(rwkv_k_a[0]))
    w_swa = w_in_swa.astype(BF16)
    w_kv = w_mem_kv.astype(BF16)
    sinks = swa_sinks[0]
    q_blk_rwkv = 3 * TOK_WIDTH // MEM_WIDTH
    q_blk_swa = (TOK_WIDTH + 2 * SWA_KV_WIDTH) // MEM_WIDTH

    xp = x_prompt[0]
    tm = _row_tile(T)
    kv = [_proj(mem_prompt[0], w_kv, i, tm=MEM_TOKENS, tn=COL_TILE, name="mem_kv") for i in range(DEPTH)]
    prompt_mem_k = jnp.stack([t[:, :MEM_WIDTH] for t in kv]).reshape(DEPTH, 1, MEM_TOKENS, MEM_HEADS, MEM_HEAD)
    prompt_mem_v = jnp.stack([t[:, MEM_WIDTH:] for t in kv]).reshape(DEPTH, 1, MEM_TOKENS, MEM_HEADS, MEM_HEAD)

    proj = _rwkv_in(xp, None, RW["mu"], RW["w_in"], RW["w_lora"], 0, tm=_ffn_row_tile(T))
    tok, s_bd = _wkv_prompt(proj, RW["w2"], RW["a2"], RW["g2"], prep_consts + (rk, gn_g, gn_b))
    mo = _mem_attn(proj, q_blk_rwkv, kv[0], tm=tm)
    xf, xb = _post_mixer(tok, mo, xp, shared, 0, tm)
    prompt_shift = xp[-1][None, None, :]
    prompt_wkv = _unblock_state(s_bd)[None, None]

    proj = _proj(xb, w_swa, 0, tm=_ffn_row_tile(T), tn=COL_TILE, name="swa_proj")
    cos, sin = _rope_tables(jnp.arange(T))
    tok, k_rot = _swa_prompt(proj, cos, sin, sinks)
    mo = _mem_attn(proj, q_blk_swa, kv[1], tm=tm)
    y_prompt, _ = _post_mixer(tok, mo, xf, shared, 1, tm)
    v_last = proj[T - WINDOW:, TOK_WIDTH + SWA_KV_WIDTH:TOK_WIDTH + 2 * SWA_KV_WIDTH]
    prompt_swa_k = k_rot[T - WINDOW:].reshape(1, 1, WINDOW, SWA_KV_HEADS, SWA_HEAD)
    prompt_swa_v = v_last.reshape(1, 1, WINDOW, SWA_KV_HEADS, SWA_HEAD)

    xs = x_sample[:, 0]
    tms = _row_tile(B)
    mem_k = cache_mem_k.reshape(DEPTH, B, MEM_TOKENS * MEM_HEADS, MEM_HEAD)
    mem_v = cache_mem_v.reshape(DEPTH, B, MEM_TOKENS * MEM_HEADS, MEM_HEAD)

    def mem_step(q, layer):
        q_rows = jnp.pad(q.reshape(B, MEM_HEADS, MEM_HEAD), ((0, 0), (0, SUBLANES - MEM_HEADS), (0, 0)))
        out = _mem_attn_step(q_rows, mem_k, mem_v, layer, bs=STEP_BATCH)
        return out[:, :MEM_HEADS].reshape(B, MEM_WIDTH).astype(BF16)

    proj = _rwkv_in(xs, state_rwkv_shift[0], RW["mu"], RW["w_in"], RW["w_lora"], 0, tm=tms)
    vecs = _rwkv_prep_t(proj, RW["w2"], RW["a2"], RW["g2"], *prep_consts)
    lanes_b = lambda t: jnp.broadcast_to(t.reshape(TOK_WIDTH, 1), (TOK_WIDTH, B))
    s_new, tok_t = _wkv_step(state_rwkv_wkv.transpose(0, 2, 3, 4, 1), 0, vecs,
                             (lanes_b(rk), lanes_b(gn_g), lanes_b(gn_b)))
    tok = tok_t.T.astype(BF16)
    mo = mem_step(proj[:, 3 * TOK_WIDTH:3 * TOK_WIDTH + MEM_WIDTH], 0)
    xf, xb = _post_mixer(tok, mo, xs, shared, 0, tms)
    sample_shift = xs[None]
    sample_wkv = s_new.transpose(3, 0, 1, 2)[None]

    proj = _proj(xb, w_swa, 0, tm=tms, tn=COL_TILE, name="swa_proj")
    cos, sin = _rope_tables(jnp.full((B,), past_len))
    qk = _rope(proj, cos, sin, tm=tms)
    chan_major = lambda t: t.reshape(B, SWA_KV_HEADS, SWA_HEAD).transpose(0, 2, 1)
    k_new = chan_major(qk[:, TOK_WIDTH:])
    v_new = chan_major(proj[:, TOK_WIDTH + SWA_KV_WIDTH:TOK_WIDTH + 2 * SWA_KV_WIDTH])
    o, kc, vc = _swa_step(qk[:, :TOK_WIDTH].reshape(B, SWA_Q_HEADS, SWA_HEAD), k_new, v_new,
                          cache_swa_k.transpose(0, 1, 3, 4, 2), cache_swa_v.transpose(0, 1, 3, 4, 2),
                          0, sinks, bs=STEP_BATCH)
    tok = o.reshape(B, TOK_WIDTH).astype(BF16)
    mo = mem_step(proj[:, TOK_WIDTH + 2 * SWA_KV_WIDTH:], 1)
    y_sample, _ = _post_mixer(tok, mo, xf, shared, 1, tms)
    sample_swa_k, sample_swa_v = kc.transpose(0, 3, 1, 2)[None], vc.transpose(0, 3, 1, 2)[None]

    return (y_prompt[None], y_sample[:, None, :], prompt_mem_k, prompt_mem_v, prompt_shift, prompt_wkv,
            prompt_swa_k, prompt_swa_v, sample_shift, sample_wkv, sample_swa_k, sample_swa_v)
```

```python
import functools
import math

import jax
import jax.numpy as jnp
from jax import lax
from jax.experimental import pallas as pl
from jax.experimental.pallas import tpu as pltpu

D_MODEL = 2048
DEPTH = 2
MEM_WIDTH = D_MODEL // 4
TOK_WIDTH = D_MODEL - MEM_WIDTH
RWKV_HEAD = 64
RWKV_HEADS = TOK_WIDTH // RWKV_HEAD
GN_EPS = RWKV_HEAD * 1e-5
SWA_HEAD = 64
SWA_Q_HEADS = TOK_WIDTH // SWA_HEAD
SWA_KV_HEADS = 4
SWA_GROUP = SWA_Q_HEADS // SWA_KV_HEADS
SWA_KV_WIDTH = SWA_KV_HEADS * SWA_HEAD
WINDOW = 128
BLOCK = 128
SWA_SCALE = SWA_HEAD ** -0.5
ROPE_THETA = 10000.0
MEM_TOKENS = 256
MEM_HEADS = 4
MEM_HEAD = MEM_WIDTH // MEM_HEADS
MEM_SCALE = MEM_HEAD ** -0.5
FFN_HIDDEN = int(math.ceil(8 * D_MODEL / 3 / 256)) * 256
ALPHA = (2 * DEPTH) ** 0.25
LN_EPS = 1e-5
LORA_PAD = 128
LORA_IN_WIDTH = 512

LANES = 128
SUBLANES = 8
VMEM_LIMIT_BYTES = 56 * 1024 * 1024

BF16 = jnp.bfloat16
F32 = jnp.float32
NT_DIMS = (((1,), (1,)), ((), ()))
TN_DIMS = (((0,), (0,)), ((), ()))


def _dot(a, b):
    return jnp.dot(a.astype(BF16), b.astype(BF16), preferred_element_type=F32)


def _dot_nt(a, b):
    return lax.dot_general(a.astype(BF16), b.astype(BF16), NT_DIMS, preferred_element_type=F32)


def _dot_tn(a, b):
    return lax.dot_general(a.astype(BF16), b.astype(BF16), TN_DIMS, preferred_element_type=F32)


def _split_dot(x, m):
    hi = x.astype(BF16)
    lo = (x - hi.astype(F32)).astype(BF16)
    return (jnp.dot(hi, m, preferred_element_type=F32)
            + jnp.dot(lo, m, preferred_element_type=F32))


def _head_ones():
    p = lax.broadcasted_iota(jnp.int32, (LANES, LANES), 0)
    q = lax.broadcasted_iota(jnp.int32, (LANES, LANES), 1)
    return ((p // RWKV_HEAD) == (q // RWKV_HEAD)).astype(BF16)


def _cparams(*sem):
    return pltpu.CompilerParams(dimension_semantics=sem, vmem_limit_bytes=VMEM_LIMIT_BYTES)


def _proj_kernel(x_ref, w_ref, o_ref):
    o_ref[...] = jnp.dot(x_ref[...].astype(BF16), w_ref[...], preferred_element_type=F32)


def _proj(x, w, layer, *, tm, tn, name):
    M, K = x.shape
    N = w.shape[2]
    assert M % tm == 0 and N % tn == 0
    return pl.pallas_call(
        _proj_kernel, grid=(M // tm, N // tn),
        in_specs=[pl.BlockSpec((tm, K), lambda i, j: (i, 0)),
                  pl.BlockSpec((None, K, tn), lambda i, j: (layer, 0, j))],
        out_specs=pl.BlockSpec((tm, tn), lambda i, j: (i, j)),
        out_shape=jax.ShapeDtypeStruct((M, N), F32),
        compiler_params=_cparams("parallel", "arbitrary"), name=name)(x, w)


MIX_R, MIX_W, MIX_K, MIX_V, MIX_A, MIX_G = range(6)
RWKV_IN_WIDTH = 3 * TOK_WIDTH + MEM_WIDTH + LORA_IN_WIDTH
RWKV_IN_TILE = LORA_IN_WIDTH
RWKV_IN_KEPT = (MIX_R, MIX_K, MIX_V)
RWKV_IN_SLOT = ([0] * (TOK_WIDTH // RWKV_IN_TILE) + [1] * (TOK_WIDTH // RWKV_IN_TILE)
                + [2] * (TOK_WIDTH // RWKV_IN_TILE) + [3] * (MEM_WIDTH // RWKV_IN_TILE))
LORA_PARTS = ((MIX_W, 0, LORA_PAD), (MIX_A, LORA_PAD, 2 * LORA_PAD), (MIX_G, 2 * LORA_PAD, LORA_IN_WIDTH))
RWKV_IN_SUB = 256


def _rwkv_in_kernel(slot_ref, x_ref, xp_ref, mu_ref, w_ref, wl_ref, o_ref, lhs_ref, *, shift):
    i, j = pl.program_id(0), pl.program_id(1)
    tm = x_ref.shape[0]
    sub = min(RWKV_IN_SUB, tm)
    blocks = [slice(s0, s0 + sub) for s0 in range(0, tm, sub)]

    def x_and_delta(rows):
        x = x_ref[rows, :]
        if not shift:
            return x, xp_ref[rows, :] - x
        if rows.start == 0:
            first = jnp.where(i > 0, xp_ref[SUBLANES - 1:SUBLANES, :], 0.0)
        else:
            first = x_ref[rows.start - 1:rows.start, :]
        rowid = lax.broadcasted_iota(jnp.int32, (sub, 1), 0)
        return x, jnp.where(rowid == 0, first, pltpu.roll(x, 1, 0)) - x

    @pl.when(j == 0)
    def _():
        for rows in blocks:
            x, d = x_and_delta(rows)
            for slot, m in enumerate(RWKV_IN_KEPT):
                lhs_ref[slot, rows, :] = (x + d * mu_ref[m]).astype(BF16)
            lhs_ref[len(RWKV_IN_KEPT), rows, :] = x.astype(BF16)

    nproj = pl.num_programs(1) - 1

    @pl.when(j < nproj)
    def _():
        o_ref[...] = jnp.dot(lhs_ref[slot_ref[j]], w_ref[...], preferred_element_type=F32)

    @pl.when(j == nproj)
    def _():
        for rows in blocks:
            x, d = x_and_delta(rows)
            for m, lo, hi in LORA_PARTS:
                o_ref[rows, lo:hi] = jnp.dot((x + d * mu_ref[m]).astype(BF16), wl_ref[:, lo:hi],
                                             preferred_element_type=F32)


def _rwkv_in(x, xprev, mu, w, w_lora, layer, *, tm):
    M, K = x.shape
    tn = RWKV_IN_TILE
    nproj = w.shape[2] // tn
    assert M % tm == 0 and w.shape[2] + w_lora.shape[1] == RWKV_IN_WIDTH and nproj == len(RWKV_IN_SLOT)
    assert tm % min(RWKV_IN_SUB, tm) == 0
    shift = xprev is None
    if shift:
        rows8 = tm // SUBLANES
        xp_spec = pl.BlockSpec((SUBLANES, K), lambda i, j, m: (jnp.maximum(i * rows8 - 1, 0), 0))
        xprev = x
    else:
        xp_spec = pl.BlockSpec((tm, K), lambda i, j, m: (i, 0))
    slot = jnp.asarray(RWKV_IN_SLOT + [0], jnp.int32)
    grid_spec = pltpu.PrefetchScalarGridSpec(
        num_scalar_prefetch=1, grid=(M // tm, nproj + 1),
        in_specs=[pl.BlockSpec((tm, K), lambda i, j, m: (i, 0)), xp_spec,
                  pl.BlockSpec(mu.shape, lambda i, j, m: (0, 0, 0)),
                  pl.BlockSpec((None, K, tn), lambda i, j, m: (layer, 0, jnp.minimum(j, nproj - 1))),
                  pl.BlockSpec(w_lora.shape, lambda i, j, m: (0, 0))],
        out_specs=pl.BlockSpec((tm, tn), lambda i, j, m: (i, j)),
        scratch_shapes=[pltpu.VMEM((len(RWKV_IN_KEPT) + 1, tm, K), BF16)])
    return pl.pallas_call(
        functools.partial(_rwkv_in_kernel, shift=shift), grid_spec=grid_spec,
        out_shape=jax.ShapeDtypeStruct((M, RWKV_IN_WIDTH), F32),
        compiler_params=_cparams("parallel", "arbitrary"), name="rwkv_in")(slot, x, xprev, mu, w, w_lora)


def _softplus(z):
    return jnp.maximum(z, 0.0) + jnp.log1p(jnp.exp(-jnp.abs(z)))


def _each(f, *lists):
    return [f(*a) for a in zip(*lists)]


def _lora_hidden(hl):
    return jnp.tanh(hl[:, 0:LORA_PAD]), hl[:, LORA_PAD:2 * LORA_PAD], jax.nn.sigmoid(hl[:, 2 * LORA_PAD:])


def _prep_slabs(k, wl, al, w0, a0, k_k, k_a, ones):
    lw = _each(lambda wl_, w0_: -jnp.exp(-_softplus(-(w0_ + wl_)) - 0.5), wl, w0)
    agate = _each(lambda al_, a0_: jax.nn.sigmoid(a0_ + al_), al, a0)
    kkr = _each(lambda k_, c_: k_ * c_, k, k_k)
    ss = _each(lambda t: _split_dot(t * t, ones), kkr)
    kn = _each(lambda t, s_: t / jnp.maximum(jnp.sqrt(s_), 1e-12), kkr, ss)
    bv = _each(lambda n_, a_: n_ * a_, kn, agate)
    kp = _each(lambda k_, a_, c_: k_ * (1.0 + (a_ - 1.0) * c_), k, agate, k_a)
    return lw, kp, kn, bv


def _rwkv_prep_t_kernel(hl_ref, r_ref, k_ref, v_ref, w2_ref, a2_ref, g2_ref, w0_ref, a0_ref, kk_ref, ka_ref,
                        rt_ref, vt_ref, lw_ref, kp_ref, kn_ref, bv_ref, g_ref):
    hw, ha, hg = _lora_hidden(hl_ref[...])
    outs = _prep_slabs([k_ref[...]], [_dot(hw, w2_ref[...])], [_dot(ha, a2_ref[...])], [w0_ref[...]],
                       [a0_ref[...]], [kk_ref[...]], [ka_ref[...]], _head_ones())
    vals = [r_ref[...], v_ref[...]] + [t[0] for t in outs] + [_dot(hg, g2_ref[...])]
    for ref, val in zip((rt_ref, vt_ref, lw_ref, kp_ref, kn_ref, bv_ref, g_ref), vals):
        ref[...] = val.T


def _rwkv_prep_t(proj, w2p, a2p, g2, w0, a0, k_k, k_a):
    B = proj.shape[0]
    assert B == LANES
    nslab = TOK_WIDTH // LANES
    lora_blk = (RWKV_IN_WIDTH - LORA_IN_WIDTH) // LORA_IN_WIDTH
    col = lambda s: (0, s)
    slab = lambda base: pl.BlockSpec((B, LANES), lambda s, base=base: (0, base + s))
    outs = [jax.ShapeDtypeStruct((TOK_WIDTH, B), F32)] * 7
    return pl.pallas_call(
        _rwkv_prep_t_kernel, grid=(nslab,),
        in_specs=[pl.BlockSpec((B, LORA_IN_WIDTH), lambda s: (0, lora_blk)),
                  slab(0), slab(nslab), slab(2 * nslab),
                  pl.BlockSpec((LORA_PAD, LANES), col),
                  pl.BlockSpec((LORA_PAD, LANES), col),
                  pl.BlockSpec((2 * LORA_PAD, LANES), col),
                  pl.BlockSpec((1, LANES), col), pl.BlockSpec((1, LANES), col),
                  pl.BlockSpec((1, LANES), col), pl.BlockSpec((1, LANES), col)],
        out_specs=[pl.BlockSpec((LANES, B), lambda s: (s, 0))] * 7,
        out_shape=outs, compiler_params=_cparams("arbitrary"),
        name="rwkv_prep_t")(proj, proj, proj, proj, w2p, a2p, g2, w0, a0, k_k, k_a)


def _gn_gate(y, r, kp, v, g, rk, gg, gb, ones):
    inv_n = 1.0 / RWKV_HEAD
    rows = y[0].shape[0]
    sums = _each(lambda y_, r_, k_, rk_: _split_dot(jnp.concatenate([y_, r_ * k_ * rk_], axis=0), ones),
                 y, r, kp, rk)
    d = _each(lambda y_, s_: y_ - s_[:rows] * inv_n, y, sums)
    var = _each(lambda d_: _split_dot(d_ * d_, ones) * inv_n, d)
    return _each(lambda d_, var_, gg_, gb_, s_, v_, g_:
                 (d_ * lax.rsqrt(var_ + GN_EPS) * gg_ + gb_ + s_[rows:] * v_) * g_,
                 d, var, gg, gb, sums, v, g)


WKV_CHUNK = 64


def _wkv_masks():
    n = 2 * WKV_CHUNK
    p = lax.broadcasted_iota(jnp.int32, (n, n), 0)
    q = lax.broadcasted_iota(jnp.int32, (n, n), 1)
    same = lambda b: (p // b) == (q // b)
    pt, qt = p % WKV_CHUNK, q % WKV_CHUNK
    s8, s16, s32, s64 = same(8), same(16), same(32), same(WKV_CHUNK)
    return dict(strict=s64 & (pt > qt), incl=s64 & (pt >= qt), s8=s8,
                e16=s16 & ~s8, e32=s32 & ~s16, e64=s64 & ~s32,
                eye=(p == q).astype(F32))


WKV_PAIRS = TOK_WIDTH // LANES
WKV_STEP_CHUNKS = 2


def _wkv_lanes(r, lw, k, v, kn, bv, mk, tri, lane_lo):
    stack = lambda x: jnp.concatenate([jnp.where(lane_lo, x, 0.0), jnp.where(lane_lo, 0.0, x)], axis=0)
    n = 2 * WKV_CHUNK
    c = _each(lambda t: _split_dot_left(tri, t), lw)
    c_last = _each(lambda t: t[WKV_CHUNK - 1:WKV_CHUNK, :], c)
    e_out = _each(lambda t: jnp.exp(-t), c)
    e_end = _each(lambda t, tl: jnp.exp(tl - t), c, c_last)
    ah = _each(lambda kn_, c_, lw_: stack(-kn_ * jnp.exp(c_ - lw_)), kn, c, lw)
    rh = _each(lambda r_, c_: stack(r_ * jnp.exp(c_)), r, c)
    bh = _each(lambda b_, e_: stack(b_ * e_), bv, e_out)
    kh = _each(lambda k_, e_: stack(k_ * e_), k, e_out)
    bbar = _each(lambda b_, e_: stack(b_ * e_), bv, e_end)
    kbar = _each(lambda k_, e_: stack(k_ * e_), k, e_end)
    vs = _each(stack, v)
    gm = _each(lambda a_, r_, b_, k_: _dot_nt(jnp.concatenate([a_, r_], axis=0),
                                              jnp.concatenate([b_, k_], axis=0)), ah, rh, bh, kh)
    a_ab = _each(lambda g_: jnp.where(mk["strict"], g_[:n, :n], 0.0), gm)
    a_ak = _each(lambda g_: jnp.where(mk["strict"], g_[:n, n:], 0.0), gm)
    l_rb = _each(lambda g_: jnp.where(mk["incl"], g_[n:, :n], 0.0), gm)
    l_rk = _each(lambda g_: jnp.where(mk["incl"], g_[n:, n:], 0.0), gm)
    d1 = _each(lambda a_: jnp.where(mk["s8"], a_, 0.0), a_ab)
    x = _each(lambda d_: mk["eye"] + d_, d1)
    d2 = _each(lambda d_: _dot(d_, d_), d1)
    x = _each(lambda x_, d_: x_ + _dot(x_, d_), x, d2)
    d4 = _each(lambda d_: _dot(d_, d_), d2)
    x = _each(lambda x_, d_: x_ + _dot(x_, d_), x, d4)
    for lvl in ("e16", "e32", "e64"):
        ex = _each(lambda a_, x_: _dot(jnp.where(mk[lvl], a_, 0.0), x_), a_ab, x)
        x = _each(lambda x_, e_: x_ + _dot(x_, e_), x, ex)
    av = _each(_dot, a_ak, vs)
    tw = _each(lambda x_, a_, v_: _dot(x_, jnp.concatenate([a_, v_], axis=1)), x, ah, av)
    lwm = _each(_dot, l_rb, tw)
    lv = _each(_dot, l_rk, vs)
    qm = _each(lambda r_, l_: r_ + l_[:, :n], rh, lwm)
    y0 = _each(lambda l_, v_: l_[:, n:] + v_, lwm, lv)
    mt = _each(lambda t_, b_: _dot_tn(t_[:, :n], b_), tw, bbar)
    nt = _each(lambda t_, b_, v_, k_: _dot_tn(t_[:, n:], b_) + _dot_tn(v_, k_), tw, bbar, vs, kbar)
    return qm, y0, mt, nt, _each(jnp.exp, c_last)


def _split_dot_left(m, x):
    hi = x.astype(BF16)
    lo = (x - hi.astype(F32)).astype(BF16)
    return (jnp.dot(m, hi, preferred_element_type=F32)
            + jnp.dot(m, lo, preferred_element_type=F32))


def _wkv_kernel(r_ref, k_ref, v_ref, hl_ref, w2_ref, a2_ref, g2_ref, w0_ref, a0_ref, kk_ref, ka_ref,
                rk_ref, gg_ref, gb_ref, o_ref, sout_ref, s_scr):
    c_idx = pl.program_id(0)

    @pl.when(c_idx == 0)
    def _():
        s_scr[...] = jnp.zeros_like(s_scr)

    mk = _wkv_masks()
    ti = lax.broadcasted_iota(jnp.int32, (WKV_CHUNK, WKV_CHUNK), 0)
    tj = lax.broadcasted_iota(jnp.int32, (WKV_CHUNK, WKV_CHUNK), 1)
    tri = (ti >= tj).astype(BF16)
    lane_lo = lax.broadcasted_iota(jnp.int32, (WKV_CHUNK, LANES), 1) < RWKV_HEAD
    ones = _head_ones()
    pairs = range(WKV_PAIRS)
    lanes = [(slice(ch * WKV_CHUNK, (ch + 1) * WKV_CHUNK), slice(p * LANES, (p + 1) * LANES))
             for ch in range(WKV_STEP_CHUNKS) for p in pairs]
    cut = lambda t: [t[rows, sl] for rows, sl in lanes]
    rep = lambda t: [t[:, sl] for _, sl in lanes]
    hw, ha, hg = _lora_hidden(hl_ref[...])
    wl, al, g = _dot(hw, w2_ref[...]), _dot(ha, a2_ref[...]), _dot(hg, g2_ref[...])
    r, v = cut(r_ref), cut(v_ref)
    lw, kp, kn, bv = _prep_slabs(cut(k_ref), cut(wl), cut(al), rep(w0_ref), rep(a0_ref),
                                 rep(kk_ref), rep(ka_ref), ones)
    qm, y0, mt, nt, dec = _wkv_lanes(r, lw, kp, v, kn, bv, mk, tri, lane_lo)
    S = [s_scr[p] for p in pairs]
    y = []
    for ch in range(WKV_STEP_CHUNKS):
        part = slice(ch * WKV_PAIRS, (ch + 1) * WKV_PAIRS)
        ys = _each(lambda q_, s_, y_: _dot_nt(q_, s_) + y_, qm[part], S, y0[part])
        y += _each(lambda t: t[:WKV_CHUNK, :] + t[WKV_CHUNK:, :], ys)
        S = _each(lambda s_, d_, m_, n_: s_ * d_ + _dot(s_, m_) + n_, S, dec[part], mt[part], nt[part])
    for p in pairs:
        s_scr[p] = S[p]
    tok = _gn_gate(y, r, kp, v, cut(g), rep(rk_ref), rep(gg_ref), rep(gb_ref), ones)
    for (rows, sl), t in zip(lanes, tok):
        o_ref[rows, sl] = t.astype(o_ref.dtype)

    @pl.when(c_idx == pl.num_programs(0) - 1)
    def _():
        sout_ref[...] = s_scr[...]


def _wkv_prompt(proj, w2p, a2p, g2, consts):
    T = proj.shape[0]
    rows = WKV_STEP_CHUNKS * WKV_CHUNK
    assert T % rows == 0
    lora_blk = (RWKV_IN_WIDTH - LORA_IN_WIDTH) // LORA_IN_WIDTH
    tok = lambda blk: pl.BlockSpec((rows, TOK_WIDTH), lambda c, blk=blk: (c, blk))
    full = lambda a: pl.BlockSpec(a.shape, lambda c: (0,) * a.ndim)
    weights = (w2p, a2p, g2) + tuple(consts)
    return pl.pallas_call(
        _wkv_kernel, grid=(T // rows,),
        in_specs=[tok(0), tok(1), tok(2), pl.BlockSpec((rows, LORA_IN_WIDTH), lambda c: (c, lora_blk))]
        + [full(a) for a in weights],
        out_specs=[tok(0), pl.BlockSpec((WKV_PAIRS, LANES, LANES), lambda c: (0, 0, 0))],
        out_shape=[jax.ShapeDtypeStruct((T, TOK_WIDTH), BF16),
                   jax.ShapeDtypeStruct((WKV_PAIRS, LANES, LANES), F32)],
        scratch_shapes=[pltpu.VMEM((WKV_PAIRS, LANES, LANES), F32)],
        compiler_params=_cparams("arbitrary"), name="wkv_chunked")(proj, proj, proj, proj, *weights)


WKV_STEP_UNROLL = 4


def _wkv_step_kernel(s_ref, r_ref, v_ref, lw_ref, kp_ref, kn_ref, bv_ref, g_ref, rk_ref, gg_ref, gb_ref,
                     so_ref, tok_ref, y_scr):
    n = RWKV_HEAD
    inv_n = 1.0 / n
    for hh in range(2):
        rows = slice(hh * n, (hh + 1) * n)
        a, w = -kn_ref[rows, :], jnp.exp(lw_ref[rows, :])
        b, k, r = bv_ref[rows, :], kp_ref[rows, :], r_ref[rows, :]

        def value_row(i, carry, hh=hh, a=a, w=w, b=b, k=k, r=r):
            s = s_ref[hh, i]
            sa = jnp.sum(s * a, axis=0, keepdims=True)
            s_new = s * w + sa * b + v_ref[pl.ds(hh * n + i, 1), :] * k
            so_ref[hh, i] = s_new
            y_scr[pl.ds(hh * n + i, 1), :] = jnp.sum(s_new * r, axis=0, keepdims=True)
            return carry

        lax.fori_loop(0, n, value_row, 0, unroll=WKV_STEP_UNROLL)
    for hh in range(2):
        rows = slice(hh * n, (hh + 1) * n)
        y = y_scr[rows, :]
        d = y - jnp.sum(y, axis=0, keepdims=True) * inv_n
        var = jnp.sum(d * d, axis=0, keepdims=True) * inv_n
        bonus = jnp.sum(r_ref[rows, :] * kp_ref[rows, :] * rk_ref[rows, :], axis=0, keepdims=True)
        yn = d * lax.rsqrt(var + GN_EPS) * gg_ref[rows, :] + gb_ref[rows, :]
        tok_ref[rows, :] = (yn + bonus * v_ref[rows, :]) * g_ref[rows, :]


def _wkv_step(state, layer, vecs, consts):
    B = state.shape[-1]
    sshape = (2, RWKV_HEAD, RWKV_HEAD, B)
    vblk = pl.BlockSpec((LANES, B), lambda p: (p, 0))
    return pl.pallas_call(
        _wkv_step_kernel, grid=(WKV_PAIRS,),
        in_specs=[pl.BlockSpec((None,) + sshape, lambda p: (layer, p, 0, 0, 0))] + [vblk] * 10,
        out_specs=[pl.BlockSpec(sshape, lambda p: (p, 0, 0, 0)), vblk],
        out_shape=[jax.ShapeDtypeStruct(state.shape[1:], F32), jax.ShapeDtypeStruct((TOK_WIDTH, B), F32)],
        scratch_shapes=[pltpu.VMEM((LANES, B), F32)],
        compiler_params=_cparams("parallel"), name="wkv_step")(state, *vecs, *consts)


STEP_INTERLEAVE = 4


def _softmax_rows(s):
    m = jnp.max(s, axis=-1, keepdims=True)
    e = jnp.exp(s - m)
    return e * (1.0 / jnp.sum(e, axis=-1, keepdims=True))


def _mem_attn_kernel(q_ref, k_ref, v_ref, o_ref):
    for h in range(MEM_HEADS):
        sl = slice(h * MEM_HEAD, (h + 1) * MEM_HEAD)
        p = _softmax_rows(_dot_nt(q_ref[:, sl], k_ref[:, sl]) * MEM_SCALE)
        o_ref[:, sl] = _dot(p, v_ref[:, sl]).astype(o_ref.dtype)


def _mem_attn(proj, qblk, kv, *, tm):
    M = proj.shape[0]
    return pl.pallas_call(
        _mem_attn_kernel, grid=(M // tm,),
        in_specs=[pl.BlockSpec((tm, MEM_WIDTH), lambda i: (i, qblk)),
                  pl.BlockSpec((MEM_TOKENS, MEM_WIDTH), lambda i: (0, 0)),
                  pl.BlockSpec((MEM_TOKENS, MEM_WIDTH), lambda i: (0, 1))],
        out_specs=pl.BlockSpec((tm, MEM_WIDTH), lambda i: (i, 0)),
        out_shape=jax.ShapeDtypeStruct((M, MEM_WIDTH), BF16),
        compiler_params=_cparams("parallel"), name="mem_attn")(proj, kv, kv)


def _mem_attn_step_kernel(q_ref, k_ref, v_ref, o_ref, *, bs):
    rows = MEM_TOKENS * MEM_HEADS
    col_head = lax.broadcasted_iota(jnp.int32, (SUBLANES, rows), 1) % MEM_HEADS
    row_head = lax.broadcasted_iota(jnp.int32, (SUBLANES, rows), 0) % MEM_HEADS
    own = col_head == row_head

    def group(t, carry):
        bs_ = [t * STEP_INTERLEAVE + u for u in range(STEP_INTERLEAVE)]
        s = [_dot_nt(q_ref[b], k_ref[b]) * MEM_SCALE for b in bs_]
        p = [_softmax_rows(jnp.where(own, t_, -jnp.inf)) for t_ in s]
        for b, p_ in zip(bs_, p):
            o_ref[b] = _dot(p_, v_ref[b])
        return carry

    lax.fori_loop(0, bs // STEP_INTERLEAVE, group, 0)


def _mem_attn_step(q, mk, mv, layer, *, bs):
    B = q.shape[0]
    qblk = pl.BlockSpec((bs, SUBLANES, MEM_HEAD), lambda i: (i, 0, 0))
    cblk = pl.BlockSpec((None, bs, MEM_TOKENS * MEM_HEADS, MEM_HEAD), lambda i: (layer, i, 0, 0))
    return pl.pallas_call(
        functools.partial(_mem_attn_step_kernel, bs=bs), grid=(B // bs,),
        in_specs=[qblk, cblk, cblk], out_specs=qblk,
        out_shape=jax.ShapeDtypeStruct(q.shape, F32),
        compiler_params=_cparams("parallel"), name="mem_attn_step")(q, mk, mv)


def _deepnorm_ln(res, h, g, beta):
    z = ALPHA * res + h
    mu = jnp.mean(z, axis=-1, keepdims=True)
    d = z - mu
    var = jnp.mean(d * d, axis=-1, keepdims=True)
    return d * lax.rsqrt(var + LN_EPS) * g + beta


def _out_ln_kernel(*refs, widths):
    n = len(widths)
    lhs, (w_ref, res_ref, g_ref, beta_ref, of_ref, ob_ref) = refs[:n], refs[n:]
    tm = res_ref.shape[0]
    half = tm // 2 if tm % (2 * SUBLANES) == 0 else tm
    for rows in (slice(r0, r0 + half) for r0 in range(0, tm, half)):
        h, off = None, 0
        for a_ref, kw in zip(lhs, widths):
            part = jnp.dot(a_ref[rows, :], w_ref[off:off + kw, :], preferred_element_type=F32)
            h = part if h is None else h + part
            off += kw
        out = _deepnorm_ln(res_ref[rows, :], h, g_ref[...], beta_ref[...])
        of_ref[rows, :] = out
        ob_ref[rows, :] = out.astype(BF16)


def _out_ln_tiled_kernel(a_ref, w_ref, res_ref, g_ref, beta_ref, of_ref, ob_ref, acc_ref):
    k = pl.program_id(1)

    @pl.when(k == 0)
    def _():
        acc_ref[...] = jnp.zeros_like(acc_ref)

    acc_ref[...] += jnp.dot(a_ref[...], w_ref[...], preferred_element_type=F32)

    @pl.when(k == pl.num_programs(1) - 1)
    def _():
        out = _deepnorm_ln(res_ref[...], acc_ref[...], g_ref[...], beta_ref[...])
        of_ref[...] = out
        ob_ref[...] = out.astype(BF16)


def _out_ln(pieces, w, layer, res, g, beta, *, tm, tk=None):
    M = pieces[0].shape[0]
    K, N = w.shape[1], w.shape[2]
    widths = tuple(p.shape[1] for p in pieces)
    assert sum(widths) == K and M % tm == 0
    out_shape = [jax.ShapeDtypeStruct((M, N), F32), jax.ShapeDtypeStruct((M, N), BF16)]
    if tk is None:
        row = lambda i: (i, 0)
        cst = lambda i: (0, 0)
        in_specs = ([pl.BlockSpec((tm, kw), row) for kw in widths]
                    + [pl.BlockSpec((None, K, N), lambda i: (layer, 0, 0)),
                       pl.BlockSpec((tm, N), row), pl.BlockSpec((1, N), cst), pl.BlockSpec((1, N), cst)])
        oblk = pl.BlockSpec((tm, N), row)
        return pl.pallas_call(
            functools.partial(_out_ln_kernel, widths=widths), grid=(M // tm,),
            in_specs=in_specs, out_specs=[oblk, oblk], out_shape=out_shape,
            compiler_params=_cparams("parallel"), name="out_ln")(*pieces, w, res, g, beta)
    assert len(pieces) == 1 and K % tk == 0
    row = lambda i, k: (i, 0)
    cst = lambda i, k: (0, 0)
    oblk = pl.BlockSpec((tm, N), row)
    return pl.pallas_call(
        _out_ln_tiled_kernel, grid=(M // tm, K // tk),
        in_specs=[pl.BlockSpec((tm, tk), lambda i, k: (i, k)),
                  pl.BlockSpec((None, tk, N), lambda i, k: (layer, k, 0)),
                  pl.BlockSpec((tm, N), row), pl.BlockSpec((1, N), cst), pl.BlockSpec((1, N), cst)],
        out_specs=[oblk, oblk], out_shape=out_shape,
        scratch_shapes=[pltpu.VMEM((tm, N), F32)],
        compiler_params=_cparams("parallel", "arbitrary"), name="ffn_down_ln")(
            pieces[0], w, res, g, beta)


def _ffn_up_kernel(x_ref, wg_ref, wu_ref, o_ref):
    x = x_ref[...]
    gate = jnp.dot(x, wg_ref[...].astype(BF16), preferred_element_type=F32)
    up = jnp.dot(x, wu_ref[...].astype(BF16), preferred_element_type=F32)
    o_ref[...] = (gate * jax.nn.sigmoid(gate) * up).astype(o_ref.dtype)


def _ffn_up(x, wg, wu, layer, *, tm, tn):
    M, K = x.shape
    N = wg.shape[2]
    wblk = pl.BlockSpec((None, K, tn), lambda i, j: (layer, 0, j))
    return pl.pallas_call(
        _ffn_up_kernel, grid=(M // tm, N // tn),
        in_specs=[pl.BlockSpec((tm, K), lambda i, j: (i, 0)), wblk, wblk],
        out_specs=pl.BlockSpec((tm, tn), lambda i, j: (i, j)),
        out_shape=jax.ShapeDtypeStruct((M, N), BF16),
        compiler_params=_cparams("parallel", "arbitrary"), name="ffn_up")(x, wg, wu)


def _rope_kernel(x_ref, cos_ref, sin_ref, o_ref):
    x = x_ref[...]
    lane = lax.broadcasted_iota(jnp.int32, x.shape, 1)
    first_half = (lane % SWA_HEAD) < (SWA_HEAD // 2)
    partner = jnp.where(first_half, pltpu.roll(x, LANES - SWA_HEAD // 2, 1),
                        pltpu.roll(x, SWA_HEAD // 2, 1))
    o_ref[...] = x * cos_ref[...] + partner * sin_ref[...]


def _rope(proj, cos, sin_signed, *, tm):
    M = proj.shape[0]
    width = TOK_WIDTH + SWA_KV_WIDTH
    blk = pl.BlockSpec((tm, LANES), lambda i, s: (i, s))
    tab = pl.BlockSpec((tm, LANES), lambda i, s: (i, 0))
    return pl.pallas_call(
        _rope_kernel, grid=(M // tm, width // LANES),
        in_specs=[blk, tab, tab], out_specs=blk,
        out_shape=jax.ShapeDtypeStruct((M, width), F32),
        compiler_params=_cparams("parallel", "arbitrary"), name="rope")(proj, cos, sin_signed)


def _sink_column(sink_ref, base, rows_per_head, nheads):
    rows = rows_per_head * nheads
    hid = lax.broadcasted_iota(jnp.int32, (rows, 1), 0) // rows_per_head
    col = jnp.zeros((rows, 1), F32)
    for j in range(nheads):
        col = jnp.where(hid == j, sink_ref[base + j], col)
    return col


def _sink_softmax(s, sink):
    m = jnp.maximum(jnp.max(s, axis=-1, keepdims=True), sink)
    p = jnp.exp(s - m)
    return p * (1.0 / (jnp.sum(p, axis=-1, keepdims=True) + jnp.exp(sink - m)))


def _swa_kernel(sink_ref, q_ref, kc_ref, kp_ref, vc_ref, vp_ref, cosc_ref, sinc_ref, cosp_ref, sinp_ref,
                o_ref, krot_ref):
    n = pl.program_id(0)
    nslab_q = TOK_WIDTH // LANES
    slab = lambda ref, s: ref[:, s * LANES:(s + 1) * LANES]
    cos_c, sin_c, cos_p, sin_p = cosc_ref[...], sinc_ref[...], cosp_ref[...], sinp_ref[...]
    lane = lax.broadcasted_iota(jnp.int32, (BLOCK, LANES), 1)
    first_half = (lane % SWA_HEAD) < (SWA_HEAD // 2)
    lo = lane < SWA_HEAD

    def rope(x, cos, sin):
        partner = jnp.where(first_half, pltpu.roll(x, LANES - SWA_HEAD // 2, 1),
                            pltpu.roll(x, SWA_HEAD // 2, 1))
        return x * cos + partner * sin

    kv_slabs = SWA_KV_WIDTH // LANES
    k_cur = [rope(slab(kc_ref, j), cos_c, sin_c) for j in range(kv_slabs)]
    k_prev = [rope(slab(kp_ref, j), cos_p, sin_p) for j in range(kv_slabs)]
    for j in range(kv_slabs):
        krot_ref[:, j * LANES:(j + 1) * LANES] = k_cur[j]
    lane2 = lax.broadcasted_iota(jnp.int32, (2 * BLOCK, LANES), 1)
    kd, vd = [], []
    for g in range(SWA_KV_HEADS):
        j, half = divmod(g, 2)
        keep = (lane2 < SWA_HEAD) if half == 0 else (lane2 >= SWA_HEAD)
        dup = lambda t: jnp.where(keep, t, pltpu.roll(t, SWA_HEAD, 1)).astype(BF16)
        kd.append(dup(jnp.concatenate([k_prev[j], k_cur[j]], axis=0)))
        vd.append(dup(jnp.concatenate([slab(vp_ref, j), slab(vc_ref, j)], axis=0)))
    qi = lax.broadcasted_iota(jnp.int32, (2 * BLOCK, 2 * BLOCK), 0) % BLOCK
    si = lax.broadcasted_iota(jnp.int32, (2 * BLOCK, 2 * BLOCK), 1)
    valid = (si > qi) & (si <= qi + WINDOW) & ((n > 0) | (si >= BLOCK))
    row_lo = lax.broadcasted_iota(jnp.int32, (2 * BLOCK, 1), 0) < BLOCK
    slabs = list(range(nslab_q))
    kv_of = [(2 * s) // SWA_GROUP for s in slabs]
    assert math.frexp(SWA_SCALE)[0] == 0.5
    q = [rope(slab(q_ref, s), cos_c, sin_c) * SWA_SCALE for s in slabs]
    qs = [jnp.concatenate([jnp.where(lo, t, 0.0), jnp.where(lo, 0.0, t)], axis=0).astype(BF16) for t in q]
    sc = [lax.dot_general(t, kd[g], NT_DIMS, preferred_element_type=F32) for t, g in zip(qs, kv_of)]
    sc = [jnp.where(valid, t, -jnp.inf) for t in sc]
    p = [_sink_softmax(t, jnp.where(row_lo, sink_ref[2 * s], sink_ref[2 * s + 1])) for t, s in zip(sc, slabs)]
    o = [jnp.dot(t.astype(BF16), vd[g], preferred_element_type=F32) for t, g in zip(p, kv_of)]
    for s, t in zip(slabs, o):
        o_ref[:, s * LANES:(s + 1) * LANES] = jnp.where(lo, t[:BLOCK], t[BLOCK:]).astype(o_ref.dtype)


def _swa_prompt(proj, cos, sin_signed, sinks):
    T = proj.shape[0]
    kblk, vblk = TOK_WIDTH // SWA_KV_WIDTH, TOK_WIDTH // SWA_KV_WIDTH + 1
    prev = lambda n: jnp.maximum(n - 1, 0)
    kv_spec = lambda blk, row: pl.BlockSpec((BLOCK, SWA_KV_WIDTH), lambda n: (row(n), blk))
    tab = lambda row: pl.BlockSpec((BLOCK, LANES), lambda n: (row(n), 0))
    cur = lambda n: n
    return pl.pallas_call(
        _swa_kernel, grid=(T // BLOCK,),
        in_specs=[pl.BlockSpec(memory_space=pltpu.SMEM),
                  pl.BlockSpec((BLOCK, TOK_WIDTH), lambda n: (n, 0)),
                  kv_spec(kblk, cur), kv_spec(kblk, prev), kv_spec(vblk, cur), kv_spec(vblk, prev),
                  tab(cur), tab(cur), tab(prev), tab(prev)],
        out_specs=[pl.BlockSpec((BLOCK, TOK_WIDTH), lambda n: (n, 0)),
                   pl.BlockSpec((BLOCK, SWA_KV_WIDTH), lambda n: (n, 0))],
        out_shape=[jax.ShapeDtypeStruct((T, TOK_WIDTH), BF16),
                   jax.ShapeDtypeStruct((T, SWA_KV_WIDTH), F32)],
        compiler_params=_cparams("arbitrary"), name="swa_banded")(
            sinks, proj, proj, proj, proj, proj, cos, sin_signed, cos, sin_signed)


def _swa_step_kernel(sink_ref, q_ref, kn_ref, vn_ref, kt_ref, vt_ref, o_ref, kto_ref, vto_ref, *, bs):
    sink = _sink_column(sink_ref, 0, 1, SWA_Q_HEADS)
    newest = lax.broadcasted_iota(jnp.int32, (SWA_HEAD, WINDOW), 1) == WINDOW - 1
    kv_of_head = lax.broadcasted_iota(jnp.int32, (SWA_Q_HEADS, 1), 0) // SWA_GROUP
    kv_heads = list(range(SWA_KV_HEADS))

    def pick(per_kv):
        out = per_kv[0]
        for g in kv_heads[1:]:
            out = jnp.where(kv_of_head == g, per_kv[g], out)
        return out

    def slid(new_ref, cache_ref, b):
        new = new_ref[b]
        return [jnp.where(newest, new[:, g:g + 1], pltpu.roll(cache_ref[b, g], WINDOW - 1, 1)) for g in kv_heads]

    def group(t, carry):
        bs_ = [t * STEP_INTERLEAVE + u for u in range(STEP_INTERLEAVE)]
        kt = [slid(kn_ref, kt_ref, b) for b in bs_]
        vt = [slid(vn_ref, vt_ref, b) for b in bs_]
        for b, kt_, vt_ in zip(bs_, kt, vt):
            for g in kv_heads:
                kto_ref[b, g] = kt_[g]
                vto_ref[b, g] = vt_[g]
        s = [pick([_dot(q_ref[b], t_) for t_ in kt_]) * SWA_SCALE for b, kt_ in zip(bs_, kt)]
        p = [_sink_softmax(s_, sink) for s_ in s]
        for b, p_, vt_ in zip(bs_, p, vt):
            o_ref[b] = pick([_dot_nt(p_, t_) for t_ in vt_])
        return carry

    lax.fori_loop(0, bs // STEP_INTERLEAVE, group, 0)


def _swa_step(q, k_new, v_new, kt, vt, layer, sinks, *, bs):
    B = q.shape[0]
    qblk = pl.BlockSpec((bs, SWA_Q_HEADS, SWA_HEAD), lambda i: (i, 0, 0))
    nblk = pl.BlockSpec((bs, SWA_HEAD, SWA_KV_HEADS), lambda i: (i, 0, 0))
    cshape = (bs, SWA_KV_HEADS, SWA_HEAD, WINDOW)
    cin = pl.BlockSpec((None,) + cshape, lambda i: (layer, i, 0, 0, 0))
    cout = pl.BlockSpec(cshape, lambda i: (i, 0, 0, 0))
    cache = jax.ShapeDtypeStruct(kt.shape[1:], F32)
    return pl.pallas_call(
        functools.partial(_swa_step_kernel, bs=bs), grid=(B // bs,),
        in_specs=[pl.BlockSpec(memory_space=pltpu.SMEM), qblk, nblk, nblk, cin, cin],
        out_specs=[qblk, cout, cout],
        out_shape=[jax.ShapeDtypeStruct(q.shape, F32), cache, cache],
        compiler_params=_cparams("parallel"), name="swa_step")(sinks, q, k_new, v_new, kt, vt)


ROW_TILE = 512
FFN_ROW_TILE = 1024
COL_TILE = 512
STEP_BATCH = 16
FFN_DOWN_K_TILE = FFN_HIDDEN // 4


def _row_tile(m):
    return ROW_TILE if m % ROW_TILE == 0 else m


def _ffn_row_tile(m):
    return FFN_ROW_TILE if m % FFN_ROW_TILE == 0 else _row_tile(m)


def _pad_rows(w, rows):
    return jnp.pad(w, ((0, rows - w.shape[0]), (0, 0)))


def _pad_cols(w, cols):
    return jnp.pad(w, ((0, 0), (0, cols - w.shape[1])))


def _rwkv_weights(w_in, mu, w1, w2, a1, a2, g1, g2):
    w_lora = jnp.concatenate([_pad_cols(w1[0], LORA_PAD), _pad_cols(a1[0], LORA_PAD), g1[0]], axis=1)
    return dict(w_in=w_in.astype(BF16), w_lora=w_lora.astype(BF16), mu=mu[0][:, None, :],
                w2=_pad_rows(w2[0], LORA_PAD).astype(BF16), a2=_pad_rows(a2[0], LORA_PAD).astype(BF16),
                g2=g2[0].astype(BF16))


def _unblock_state(s_bd):
    n = RWKV_HEAD
    return jnp.stack([s_bd[:, :n, :n], s_bd[:, n:, n:]], axis=1).reshape(RWKV_HEADS, n, n)


def _rope_tables(pos):
    half = SWA_HEAD // 2
    inv = ROPE_THETA ** (-jnp.arange(half, dtype=F32) / half)
    ang = pos.astype(F32)[:, None] * inv[None, :]
    cos, sin = jnp.cos(ang), jnp.sin(ang)
    reps = LANES // SWA_HEAD
    return jnp.tile(cos, (1, 2 * reps)), jnp.tile(jnp.concatenate([-sin, sin], axis=1), (1, reps))


def _post_mixer(tok, mo, x, sw, layer, tm):
    row = lambda t: t[layer][None, :]
    x1f, x1b = _out_ln([tok, mo], sw["w_out"], layer, x, row(sw["ln1_g"]), row(sw["ln1_b"]), tm=tm)
    hff = _ffn_up(x1b, sw["w_gate"], sw["w_up"], layer, tm=_ffn_row_tile(x.shape[0]), tn=COL_TILE)
    return _out_ln([hff], sw["w_down"], layer, x1f, row(sw["ln2_g"]), row(sw["ln2_b"]),
                   tm=tm, tk=FFN_DOWN_K_TILE)


def kernel(x_prompt, x_sample, mem_prompt, cache_mem_k, cache_mem_v, state_rwkv_shift, state_rwkv_wkv, cache_swa_k, cache_swa_v, w_in_rwkv, rwkv_mu, rwkv_w0, rwkv_w1, rwkv_w2, rwkv_a0, rwkv_a1, rwkv_a2, rwkv_g1, rwkv_g2, rwkv_k_k, rwkv_k_a, rwkv_r_k, rwkv_gn_g, rwkv_gn_b, w_in_swa, swa_sinks, w_mem_kv, w_out, ln1_g, ln1_b, w_gate, w_up, w_down, ln2_g, ln2_b):
    assert DEPTH == 2 and x_prompt.shape[0] == 1 and x_sample.shape[1] == 1
    T = x_prompt.shape[1]
    B = x_sample.shape[0]
    past_len = T
    row = lambda t: t[None, :]
    shared = dict(w_out=w_out.astype(BF16), ln1_g=ln1_g, ln1_b=ln1_b, w_gate=w_gate, w_up=w_up,
                  w_down=w_down.astype(BF16), ln2_g=ln2_g, ln2_b=ln2_b)
    RW = _rwkv_weights(w_in_rwkv, rwkv_mu, rwkv_w1, rwkv_w2, rwkv_a1, rwkv_a2, rwkv_g1, rwkv_g2)
    rk, gn_g, gn_b = row(rwkv_r_k[0].reshape(-1)), row(rwkv_gn_g[0]), row(rwkv_gn_b[0])
    prep_consts = (row(rwkv_w0[0]), row(rwkv_a0[0]), row(rwkv_k_k[0]), row(rwkv_k_a[0]))
    w_swa = w_in_swa.astype(BF16)
    w_kv = w_mem_kv.astype(BF16)
    sinks = swa_sinks[0]
    q_blk_rwkv = 3 * TOK_WIDTH // MEM_WIDTH
    q_blk_swa = (TOK_WIDTH + 2 * SWA_KV_WIDTH) // MEM_WIDTH

    xp = x_prompt[0]
    tm = _row_tile(T)
    kv = [_proj(mem_prompt[0], w_kv, i, tm=MEM_TOKENS, tn=COL_TILE, name="mem_kv") for i in range(DEPTH)]
    prompt_mem_k = jnp.stack([t[:, :MEM_WIDTH] for t in kv]).reshape(DEPTH, 1, MEM_TOKENS, MEM_HEADS, MEM_HEAD)
    prompt_mem_v = jnp.stack([t[:, MEM_WIDTH:] for t in kv]).reshape(DEPTH, 1, MEM_TOKENS, MEM_HEADS, MEM_HEAD)

    proj = _rwkv_in(xp, None, RW["mu"], RW["w_in"], RW["w_lora"], 0, tm=_ffn_row_tile(T))
    tok, s_bd = _wkv_prompt(proj, RW["w2"], RW["a2"], RW["g2"], prep_consts + (rk, gn_g, gn_b))
    mo = _mem_attn(proj, q_blk_rwkv, kv[0], tm=tm)
    xf, xb = _post_mixer(tok, mo, xp, shared, 0, tm)
    prompt_shift = xp[-1][None, None, :]
    prompt_wkv = _unblock_state(s_bd)[None, None]

    proj = _proj(xb, w_swa, 0, tm=_ffn_row_tile(T), tn=COL_TILE, name="swa_proj")
    cos, sin = _rope_tables(jnp.arange(T))
    tok, k_rot = _swa_prompt(proj, cos, sin, sinks)
    mo = _mem_attn(proj, q_blk_swa, kv[1], tm=tm)
    y_prompt, _ = _post_mixer(tok, mo, xf, shared, 1, tm)
    v_last = proj[T - WINDOW:, TOK_WIDTH + SWA_KV_WIDTH:TOK_WIDTH + 2 * SWA_KV_WIDTH]
    prompt_swa_k = k_rot[T - WINDOW:].reshape(1, 1, WINDOW, SWA_KV_HEADS, SWA_HEAD)
    prompt_swa_v = v_last.reshape(1, 1, WINDOW, SWA_KV_HEADS, SWA_HEAD)

    xs = x_sample[:, 0]
    tms = _row_tile(B)
    mem_k = cache_mem_k.reshape(DEPTH, B, MEM_TOKENS * MEM_HEADS, MEM_HEAD)
    mem_v = cache_mem_v.reshape(DEPTH, B, MEM_TOKENS * MEM_HEADS, MEM_HEAD)

    def mem_step(q, layer):
        q_rows = jnp.pad(q.reshape(B, MEM_HEADS, MEM_HEAD), ((0, 0), (0, SUBLANES - MEM_HEADS), (0, 0)))
        out = _mem_attn_step(q_rows, mem_k, mem_v, layer, bs=STEP_BATCH)
        return out[:, :MEM_HEADS].reshape(B, MEM_WIDTH).astype(BF16)

    proj = _rwkv_in(xs, state_rwkv_shift[0], RW["mu"], RW["w_in"], RW["w_lora"], 0, tm=tms)
    vecs = _rwkv_prep_t(proj, RW["w2"], RW["a2"], RW["g2"], *prep_consts)
    lanes_b = lambda t: jnp.broadcast_to(t.reshape(TOK_WIDTH, 1), (TOK_WIDTH, B))
    s_new, tok_t = _wkv_step(state_rwkv_wkv.transpose(0, 2, 3, 4, 1), 0, vecs,
                             (lanes_b(rk), lanes_b(gn_g), lanes_b(gn_b)))
    tok = tok_t.T.astype(BF16)
    mo = mem_step(proj[:, 3 * TOK_WIDTH:3 * TOK_WIDTH + MEM_WIDTH], 0)
    xf, xb = _post_mixer(tok, mo, xs, shared, 0, tms)
    sample_shift = xs[None]
    sample_wkv = s_new.transpose(3, 0, 1, 2)[None]

    proj = _proj(xb, w_swa, 0, tm=tms, tn=COL_TILE, name="swa_proj")
    cos, sin = _rope_tables(jnp.full((B,), past_len))
    qk = _rope(proj, cos, sin, tm=tms)
    chan_major = lambda t: t.reshape(B, SWA_KV_HEADS, SWA_HEAD).transpose(0, 2, 1)
    k_new = chan_major(qk[:, TOK_WIDTH:])
    v_new = chan_major(proj[:, TOK_WIDTH + SWA_KV_WIDTH:TOK_WIDTH + 2 * SWA_KV_WIDTH])
    o, kc, vc = _swa_step(qk[:, :TOK_WIDTH].reshape(B, SWA_Q_HEADS, SWA_HEAD), k_new, v_new,
                          cache_swa_k.transpose(0, 1, 3, 4, 2), cache_swa_v.transpose(0, 1, 3, 4, 2),
                          0, sinks, bs=STEP_BATCH)
    tok = o.reshape(B, TOK_WIDTH).astype(BF16)
    mo = mem_step(proj[:, TOK_WIDTH + 2 * SWA_KV_WIDTH:], 1)
    y_sample, _ = _post_mixer(tok, mo, xf, shared, 1, tms)
    sample_swa_k, sample_swa_v = kc.transpose(0, 3, 1, 2)[None], vc.transpose(0, 3, 1, 2)[None]

    return (y_prompt[None], y_sample[:, None, :], prompt_mem_k, prompt_mem_v, prompt_shift, prompt_wkv,
            prompt_swa_k, prompt_swa_v, sample_shift, sample_wkv, sample_swa_k, sample_swa_v)
```

```python
import functools
import math

import jax
import jax.numpy as jnp
from jax import lax
from jax.experimental import pallas as pl
from jax.experimental.pallas import tpu as pltpu

D_MODEL = 2048
DEPTH = 2
MEM_WIDTH = D_MODEL // 4
TOK_WIDTH = D_MODEL - MEM_WIDTH
RWKV_HEAD = 64
RWKV_HEADS = TOK_WIDTH // RWKV_HEAD
GN_EPS = RWKV_HEAD * 1e-5
SWA_HEAD = 64
SWA_Q_HEADS = TOK_WIDTH // SWA_HEAD
SWA_KV_HEADS = 4
SWA_GROUP = SWA_Q_HEADS // SWA_KV_HEADS
SWA_KV_WIDTH = SWA_KV_HEADS * SWA_HEAD
WINDOW = 128
BLOCK = 128
SWA_SCALE = SWA_HEAD ** -0.5
ROPE_THETA = 10000.0
MEM_TOKENS = 256
MEM_HEADS = 4
MEM_HEAD = MEM_WIDTH // MEM_HEADS
MEM_SCALE = MEM_HEAD ** -0.5
FFN_HIDDEN = int(math.ceil(8 * D_MODEL / 3 / 256)) * 256
ALPHA = (2 * DEPTH) ** 0.25
LN_EPS = 1e-5
LORA_PAD = 128
LORA_IN_WIDTH = 512

LANES = 128
SUBLANES = 8
VMEM_LIMIT_BYTES = 56 * 1024 * 1024

BF16 = jnp.bfloat16
F32 = jnp.float32
NT_DIMS = (((1,), (1,)), ((), ()))
TN_DIMS = (((0,), (0,)), ((), ()))


def _dot(a, b):
    return jnp.dot(a.astype(BF16), b.astype(BF16), preferred_element_type=F32)


def _dot_nt(a, b):
    return lax.dot_general(a.astype(BF16), b.astype(BF16), NT_DIMS, preferred_element_type=F32)


def _dot_tn(a, b):
    return lax.dot_general(a.astype(BF16), b.astype(BF16), TN_DIMS, preferred_element_type=F32)


def _split_dot(x, m):
    hi = x.astype(BF16)
    lo = (x - hi.astype(F32)).astype(BF16)
    return (jnp.dot(hi, m, preferred_element_type=F32)
            + jnp.dot(lo, m, preferred_element_type=F32))


def _head_ones():
    p = lax.broadcasted_iota(jnp.int32, (LANES, LANES), 0)
    q = lax.broadcasted_iota(jnp.int32, (LANES, LANES), 1)
    return ((p // RWKV_HEAD) == (q // RWKV_HEAD)).astype(BF16)


def _cparams(*sem):
    return pltpu.CompilerParams(dimension_semantics=sem, vmem_limit_bytes=VMEM_LIMIT_BYTES)


def _proj_kernel(x_ref, w_ref, o_ref):
    o_ref[...] = _dot(x_ref[...], w_ref[...])


def _proj(x, w, layer, *, tm, tn, name):
    M, K = x.shape
    N = w.shape[2]
    assert M % tm == 0 and N % tn == 0
    return pl.pallas_call(
        _proj_kernel, grid=(M // tm, N // tn),
        in_specs=[pl.BlockSpec((tm, K), lambda i, j: (i, 0)),
                  pl.BlockSpec((None, K, tn), lambda i, j: (layer, 0, j))],
        out_specs=pl.BlockSpec((tm, tn), lambda i, j: (i, j)),
        out_shape=jax.ShapeDtypeStruct((M, N), F32),
        compiler_params=_cparams("parallel", "arbitrary"), name=name)(x, w)


MIX_R, MIX_W, MIX_K, MIX_V, MIX_A, MIX_G = range(6)
RWKV_IN_WIDTH = 3 * TOK_WIDTH + MEM_WIDTH + LORA_IN_WIDTH
RWKV_IN_TILE = LORA_IN_WIDTH
RWKV_IN_KEPT = (MIX_R, MIX_K, MIX_V)
RWKV_IN_SLOT = ([0] * (TOK_WIDTH // RWKV_IN_TILE) + [1] * (TOK_WIDTH // RWKV_IN_TILE)
                + [2] * (TOK_WIDTH // RWKV_IN_TILE) + [3] * (MEM_WIDTH // RWKV_IN_TILE))
LORA_PARTS = ((MIX_W, 0, LORA_PAD), (MIX_A, LORA_PAD, 2 * LORA_PAD), (MIX_G, 2 * LORA_PAD, LORA_IN_WIDTH))
RWKV_IN_SUB = 256


def _rwkv_in_kernel(slot_ref, x_ref, xp_ref, mu_ref, w_ref, wl_ref, o_ref, lhs_ref, *, shift):
    i, j = pl.program_id(0), pl.program_id(1)
    tm = x_ref.shape[0]
    sub = min(RWKV_IN_SUB, tm)
    blocks = [slice(s0, s0 + sub) for s0 in range(0, tm, sub)]

    def x_and_delta(rows):
        x = x_ref[rows, :]
        if not shift:
            return x, xp_ref[rows, :] - x
        if rows.start == 0:
            first = jnp.where(i > 0, xp_ref[SUBLANES - 1:SUBLANES, :], 0.0)
        else:
            first = x_ref[rows.start - 1:rows.start, :]
        rowid = lax.broadcasted_iota(jnp.int32, (sub, 1), 0)
        return x, jnp.where(rowid == 0, first, pltpu.roll(x, 1, 0)) - x

    @pl.when(j == 0)
    def _():
        for rows in blocks:
            x, d = x_and_delta(rows)
            for slot, m in enumerate(RWKV_IN_KEPT):
                lhs_ref[slot, rows, :] = (x + d * mu_ref[m]).astype(BF16)
            lhs_ref[len(RWKV_IN_KEPT), rows, :] = x.astype(BF16)

    nproj = pl.num_programs(1) - 1

    @pl.when(j < nproj)
    def _():
        o_ref[...] = jnp.dot(lhs_ref[slot_ref[j]], w_ref[...], preferred_element_type=F32)

    @pl.when(j == nproj)
    def _():
        for rows in blocks:
            x, d = x_and_delta(rows)
            for m, lo, hi in LORA_PARTS:
                o_ref[rows, lo:hi] = jnp.dot((x + d * mu_ref[m]).astype(BF16), wl_ref[:, lo:hi],
                                             preferred_element_type=F32)


def _rwkv_in(x, xprev, mu, w, w_lora, layer, *, tm):
    M, K = x.shape
    tn = RWKV_IN_TILE
    nproj = w.shape[2] // tn
    assert M % tm == 0 and w.shape[2] + w_lora.shape[1] == RWKV_IN_WIDTH and nproj == len(RWKV_IN_SLOT)
    assert tm % min(RWKV_IN_SUB, tm) == 0
    shift = xprev is None
    if shift:
        rows8 = tm // SUBLANES
        xp_spec = pl.BlockSpec((SUBLANES, K), lambda i, j, m: (jnp.maximum(i * rows8 - 1, 0), 0))
        xprev = x
    else:
        xp_spec = pl.BlockSpec((tm, K), lambda i, j, m: (i, 0))
    slot = jnp.asarray(RWKV_IN_SLOT + [0], jnp.int32)
    grid_spec = pltpu.PrefetchScalarGridSpec(
        num_scalar_prefetch=1, grid=(M // tm, nproj + 1),
        in_specs=[pl.BlockSpec((tm, K), lambda i, j, m: (i, 0)), xp_spec,
                  pl.BlockSpec(mu.shape, lambda i, j, m: (0, 0, 0)),
                  pl.BlockSpec((None, K, tn), lambda i, j, m: (layer, 0, jnp.minimum(j, nproj - 1))),
                  pl.BlockSpec(w_lora.shape, lambda i, j, m: (0, 0))],
        out_specs=pl.BlockSpec((tm, tn), lambda i, j, m: (i, j)),
        scratch_shapes=[pltpu.VMEM((len(RWKV_IN_KEPT) + 1, tm, K), BF16)])
    return pl.pallas_call(
        functools.partial(_rwkv_in_kernel, shift=shift), grid_spec=grid_spec,
        out_shape=jax.ShapeDtypeStruct((M, RWKV_IN_WIDTH), F32),
        compiler_params=_cparams("parallel", "arbitrary"), name="rwkv_in")(slot, x, xprev, mu, w, w_lora)


def _softplus(z):
    return jnp.maximum(z, 0.0) + jnp.log1p(jnp.exp(-jnp.abs(z)))


def _each(f, *lists):
    return [f(*a) for a in zip(*lists)]


def _lora_hidden(hl):
    return jnp.tanh(hl[:, 0:LORA_PAD]), hl[:, LORA_PAD:2 * LORA_PAD], jax.nn.sigmoid(hl[:, 2 * LORA_PAD:])


def _prep_slabs(k, wl, al, w0, a0, k_k, k_a, ones):
    lw = _each(lambda wl_, w0_: -jnp.exp(-_softplus(-(w0_ + wl_)) - 0.5), wl, w0)
    agate = _each(lambda al_, a0_: jax.nn.sigmoid(a0_ + al_), al, a0)
    kkr = _each(lambda k_, c_: k_ * c_, k, k_k)
    ss = _each(lambda t: _split_dot(t * t, ones), kkr)
    kn = _each(lambda t, s_: t / jnp.maximum(jnp.sqrt(s_), 1e-12), kkr, ss)
    bv = _each(lambda n_, a_: n_ * a_, kn, agate)
    kp = _each(lambda k_, a_, c_: k_ * (1.0 + (a_ - 1.0) * c_), k, agate, k_a)
    return lw, kp, kn, bv


def _rwkv_prep_t_kernel(hl_ref, r_ref, k_ref, v_ref, w2_ref, a2_ref, g2_ref, w0_ref, a0_ref, kk_ref, ka_ref,
                        rt_ref, vt_ref, lw_ref, kp_ref, kn_ref, bv_ref, g_ref):
    hw, ha, hg = _lora_hidden(hl_ref[...])
    outs = _prep_slabs([k_ref[...]], [_dot(hw, w2_ref[...])], [_dot(ha, a2_ref[...])], [w0_ref[...]],
                       [a0_ref[...]], [kk_ref[...]], [ka_ref[...]], _head_ones())
    vals = [r_ref[...], v_ref[...]] + [t[0] for t in outs] + [_dot(hg, g2_ref[...])]
    for ref, val in zip((rt_ref, vt_ref, lw_ref, kp_ref, kn_ref, bv_ref, g_ref), vals):
        ref[...] = val.T


def _rwkv_prep_t(proj, w2p, a2p, g2, w0, a0, k_k, k_a):
    B = proj.shape[0]
    assert B == LANES
    nslab = TOK_WIDTH // LANES
    lora_blk = (RWKV_IN_WIDTH - LORA_IN_WIDTH) // LORA_IN_WIDTH
    col = lambda s: (0, s)
    slab = lambda base: pl.BlockSpec((B, LANES), lambda s, base=base: (0, base + s))
    outs = [jax.ShapeDtypeStruct((TOK_WIDTH, B), F32)] * 7
    return pl.pallas_call(
        _rwkv_prep_t_kernel, grid=(nslab,),
        in_specs=[pl.BlockSpec((B, LORA_IN_WIDTH), lambda s: (0, lora_blk)),
                  slab(0), slab(nslab), slab(2 * nslab),
                  pl.BlockSpec((LORA_PAD, LANES), col),
                  pl.BlockSpec((LORA_PAD, LANES), col),
                  pl.BlockSpec((2 * LORA_PAD, LANES), col),
                  pl.BlockSpec((1, LANES), col), pl.BlockSpec((1, LANES), col),
                  pl.BlockSpec((1, LANES), col), pl.BlockSpec((1, LANES), col)],
        out_specs=[pl.BlockSpec((LANES, B), lambda s: (s, 0))] * 7,
        out_shape=outs, compiler_params=_cparams("arbitrary"),
        name="rwkv_prep_t")(proj, proj, proj, proj, w2p, a2p, g2, w0, a0, k_k, k_a)


def _gn_gate(y, r, kp, v, g, rk, gg, gb, ones):
    inv_n = 1.0 / RWKV_HEAD
    rows = y[0].shape[0]
    sums = _each(lambda y_, r_, k_, rk_: _split_dot(jnp.concatenate([y_, r_ * k_ * rk_], axis=0), ones),
                 y, r, kp, rk)
    d = _each(lambda y_, s_: y_ - s_[:rows] * inv_n, y, sums)
    var = _each(lambda d_: _split_dot(d_ * d_, ones) * inv_n, d)
    return _each(lambda d_, var_, gg_, gb_, s_, v_, g_:
                 (d_ * lax.rsqrt(var_ + GN_EPS) * gg_ + gb_ + s_[rows:] * v_) * g_,
                 d, var, gg, gb, sums, v, g)


WKV_CHUNK = 64


def _wkv_masks():
    n = 2 * WKV_CHUNK
    p = lax.broadcasted_iota(jnp.int32, (n, n), 0)
    q = lax.broadcasted_iota(jnp.int32, (n, n), 1)
    same = lambda b: (p // b) == (q // b)
    pt, qt = p % WKV_CHUNK, q % WKV_CHUNK
    s8, s16, s32, s64 = same(8), same(16), same(32), same(WKV_CHUNK)
    return dict(strict=s64 & (pt > qt), incl=s64 & (pt >= qt), s8=s8,
                e16=s16 & ~s8, e32=s32 & ~s16, e64=s64 & ~s32,
                eye=(p == q).astype(F32))


WKV_PAIRS = TOK_WIDTH // LANES
WKV_STEP_CHUNKS = 2


def _wkv_lanes(r, lw, k, v, kn, bv, mk, tri, lane_lo):
    stack = lambda x: jnp.concatenate([jnp.where(lane_lo, x, 0.0), jnp.where(lane_lo, 0.0, x)], axis=0)
    n = 2 * WKV_CHUNK
    c = _each(lambda t: _split_dot_left(tri, t), lw)
    c_last = _each(lambda t: t[WKV_CHUNK - 1:WKV_CHUNK, :], c)
    e_out = _each(lambda t: jnp.exp(-t), c)
    e_end = _each(lambda t, tl: jnp.exp(tl - t), c, c_last)
    ah = _each(lambda kn_, c_, lw_: stack(-kn_ * jnp.exp(c_ - lw_)), kn, c, lw)
    rh = _each(lambda r_, c_: stack(r_ * jnp.exp(c_)), r, c)
    bh = _each(lambda b_, e_: stack(b_ * e_), bv, e_out)
    kh = _each(lambda k_, e_: stack(k_ * e_), k, e_out)
    bbar = _each(lambda b_, e_: stack(b_ * e_), bv, e_end)
    kbar = _each(lambda k_, e_: stack(k_ * e_), k, e_end)
    vs = _each(stack, v)
    gm = _each(lambda a_, r_, b_, k_: _dot_nt(jnp.concatenate([a_, r_], axis=0),
                                              jnp.concatenate([b_, k_], axis=0)), ah, rh, bh, kh)
    a_ab = _each(lambda g_: jnp.where(mk["strict"], g_[:n, :n], 0.0), gm)
    a_ak = _each(lambda g_: jnp.where(mk["strict"], g_[:n, n:], 0.0), gm)
    l_rb = _each(lambda g_: jnp.where(mk["incl"], g_[n:, :n], 0.0), gm)
    l_rk = _each(lambda g_: jnp.where(mk["incl"], g_[n:, n:], 0.0), gm)
    d1 = _each(lambda a_: jnp.where(mk["s8"], a_, 0.0), a_ab)
    x = _each(lambda d_: mk["eye"] + d_, d1)
    d2 = _each(lambda d_: _dot(d_, d_), d1)
    x = _each(lambda x_, d_: x_ + _dot(x_, d_), x, d2)
    d4 = _each(lambda d_: _dot(d_, d_), d2)
    x = _each(lambda x_, d_: x_ + _dot(x_, d_), x, d4)
    for lvl in ("e16", "e32", "e64"):
        ex = _each(lambda a_, x_: _dot(jnp.where(mk[lvl], a_, 0.0), x_), a_ab, x)
        x = _each(lambda x_, e_: x_ + _dot(x_, e_), x, ex)
    av = _each(_dot, a_ak, vs)
    tw = _each(lambda x_, a_, v_: _dot(x_, jnp.concatenate([a_, v_], axis=1)), x, ah, av)
    lwm = _each(_dot, l_rb, tw)
    lv = _each(_dot, l_rk, vs)
    qm = _each(lambda r_, l_: r_ + l_[:, :n], rh, lwm)
    y0 = _each(lambda l_, v_: l_[:, n:] + v_, lwm, lv)
    mt = _each(lambda t_, b_: _dot_tn(t_[:, :n], b_), tw, bbar)
    nt = _each(lambda t_, b_, v_, k_: _dot_tn(t_[:, n:], b_) + _dot_tn(v_, k_), tw, bbar, vs, kbar)
    return qm, y0, mt, nt, _each(jnp.exp, c_last)


def _split_dot_left(m, x):
    hi = x.astype(BF16)
    lo = (x - hi.astype(F32)).astype(BF16)
    return (jnp.dot(m, hi, preferred_element_type=F32)
            + jnp.dot(m, lo, preferred_element_type=F32))


def _wkv_kernel(r_ref, k_ref, v_ref, hl_ref, w2_ref, a2_ref, g2_ref, w0_ref, a0_ref, kk_ref, ka_ref,
                rk_ref, gg_ref, gb_ref, o_ref, sout_ref, s_scr):
    c_idx = pl.program_id(0)

    @pl.when(c_idx == 0)
    def _():
        s_scr[...] = jnp.zeros_like(s_scr)

    mk = _wkv_masks()
    ti = lax.broadcasted_iota(jnp.int32, (WKV_CHUNK, WKV_CHUNK), 0)
    tj = lax.broadcasted_iota(jnp.int32, (WKV_CHUNK, WKV_CHUNK), 1)
    tri = (ti >= tj).astype(BF16)
    lane_lo = lax.broadcasted_iota(jnp.int32, (WKV_CHUNK, LANES), 1) < RWKV_HEAD
    ones = _head_ones()
    pairs = range(WKV_PAIRS)
    lanes = [(slice(ch * WKV_CHUNK, (ch + 1) * WKV_CHUNK), slice(p * LANES, (p + 1) * LANES))
             for ch in range(WKV_STEP_CHUNKS) for p in pairs]
    cut = lambda t: [t[rows, sl] for rows, sl in lanes]
    rep = lambda t: [t[:, sl] for _, sl in lanes]
    hw, ha, hg = _lora_hidden(hl_ref[...])
    wl, al, g = _dot(hw, w2_ref[...]), _dot(ha, a2_ref[...]), _dot(hg, g2_ref[...])
    r, v = cut(r_ref), cut(v_ref)
    lw, kp, kn, bv = _prep_slabs(cut(k_ref), cut(wl), cut(al), rep(w0_ref), rep(a0_ref),
                                 rep(kk_ref), rep(ka_ref), ones)
    qm, y0, mt, nt, dec = _wkv_lanes(r, lw, kp, v, kn, bv, mk, tri, lane_lo)
    S = [s_scr[p] for p in pairs]
    y = []
    for ch in range(WKV_STEP_CHUNKS):
        part = slice(ch * WKV_PAIRS, (ch + 1) * WKV_PAIRS)
        ys = _each(lambda q_, s_, y_: _dot_nt(q_, s_) + y_, qm[part], S, y0[part])
        y += _each(lambda t: t[:WKV_CHUNK, :] + t[WKV_CHUNK:, :], ys)
        S = _each(lambda s_, d_, m_, n_: s_ * d_ + _dot(s_, m_) + n_, S, dec[part], mt[part], nt[part])
    for p in pairs:
        s_scr[p] = S[p]
    tok = _gn_gate(y, r, kp, v, cut(g), rep(rk_ref), rep(gg_ref), rep(gb_ref), ones)
    for (rows, sl), t in zip(lanes, tok):
        o_ref[rows, sl] = t.astype(o_ref.dtype)

    @pl.when(c_idx == pl.num_programs(0) - 1)
    def _():
        sout_ref[...] = s_scr[...]


def _wkv_prompt(proj, w2p, a2p, g2, consts):
    T = proj.shape[0]
    rows = WKV_STEP_CHUNKS * WKV_CHUNK
    assert T % rows == 0
    lora_blk = (RWKV_IN_WIDTH - LORA_IN_WIDTH) // LORA_IN_WIDTH
    tok = lambda blk: pl.BlockSpec((rows, TOK_WIDTH), lambda c, blk=blk: (c, blk))
    full = lambda a: pl.BlockSpec(a.shape, lambda c: (0,) * a.ndim)
    weights = (w2p, a2p, g2) + tuple(consts)
    return pl.pallas_call(
        _wkv_kernel, grid=(T // rows,),
        in_specs=[tok(0), tok(1), tok(2), pl.BlockSpec((rows, LORA_IN_WIDTH), lambda c: (c, lora_blk))]
        + [full(a) for a in weights],
        out_specs=[tok(0), pl.BlockSpec((WKV_PAIRS, LANES, LANES), lambda c: (0, 0, 0))],
        out_shape=[jax.ShapeDtypeStruct((T, TOK_WIDTH), BF16),
                   jax.ShapeDtypeStruct((WKV_PAIRS, LANES, LANES), F32)],
        scratch_shapes=[pltpu.VMEM((WKV_PAIRS, LANES, LANES), F32)],
        compiler_params=_cparams("arbitrary"), name="wkv_chunked")(proj, proj, proj, proj, *weights)


WKV_STEP_UNROLL = 4


def _wkv_step_kernel(s_ref, r_ref, v_ref, lw_ref, kp_ref, kn_ref, bv_ref, g_ref, rk_ref, gg_ref, gb_ref,
                     so_ref, tok_ref, y_scr):
    n = RWKV_HEAD
    inv_n = 1.0 / n
    for hh in range(2):
        rows = slice(hh * n, (hh + 1) * n)
        a, w = -kn_ref[rows, :], jnp.exp(lw_ref[rows, :])
        b, k, r = bv_ref[rows, :], kp_ref[rows, :], r_ref[rows, :]

        def value_row(i, carry, hh=hh, a=a, w=w, b=b, k=k, r=r):
            s = s_ref[hh, i]
            sa = jnp.sum(s * a, axis=0, keepdims=True)
            s_new = s * w + sa * b + v_ref[pl.ds(hh * n + i, 1), :] * k
            so_ref[hh, i] = s_new
            y_scr[pl.ds(hh * n + i, 1), :] = jnp.sum(s_new * r, axis=0, keepdims=True)
            return carry

        lax.fori_loop(0, n, value_row, 0, unroll=WKV_STEP_UNROLL)
    for hh in range(2):
        rows = slice(hh * n, (hh + 1) * n)
        y = y_scr[rows, :]
        d = y - jnp.sum(y, axis=0, keepdims=True) * inv_n
        var = jnp.sum(d * d, axis=0, keepdims=True) * inv_n
        bonus = jnp.sum(r_ref[rows, :] * kp_ref[rows, :] * rk_ref[rows, :], axis=0, keepdims=True)
        yn = d * lax.rsqrt(var + GN_EPS) * gg_ref[rows, :] + gb_ref[rows, :]
        tok_ref[rows, :] = (yn + bonus * v_ref[rows, :]) * g_ref[rows, :]


def _wkv_step(state, layer, vecs, consts):
    B = state.shape[-1]
    sshape = (2, RWKV_HEAD, RWKV_HEAD, B)
    vblk = pl.BlockSpec((LANES, B), lambda p: (p, 0))
    return pl.pallas_call(
        _wkv_step_kernel, grid=(WKV_PAIRS,),
        in_specs=[pl.BlockSpec((None,) + sshape, lambda p: (layer, p, 0, 0, 0))] + [vblk] * 10,
        out_specs=[pl.BlockSpec(sshape, lambda p: (p, 0, 0, 0)), vblk],
        out_shape=[jax.ShapeDtypeStruct(state.shape[1:], F32), jax.ShapeDtypeStruct((TOK_WIDTH, B), F32)],
        scratch_shapes=[pltpu.VMEM((LANES, B), F32)],
        compiler_params=_cparams("parallel"), name="wkv_step")(state, *vecs, *consts)


STEP_INTERLEAVE = 4


def _softmax_rows(s):
    m = jnp.max(s, axis=-1, keepdims=True)
    e = jnp.exp(s - m)
    return e * (1.0 / jnp.sum(e, axis=-1, keepdims=True))


def _mem_attn_kernel(q_ref, k_ref, v_ref, o_ref):
    for h in range(MEM_HEADS):
        sl = slice(h * MEM_HEAD, (h + 1) * MEM_HEAD)
        p = _softmax_rows(_dot_nt(q_ref[:, sl], k_ref[:, sl]) * MEM_SCALE)
        o_ref[:, sl] = _dot(p, v_ref[:, sl]).astype(o_ref.dtype)


def _mem_attn(proj, qblk, kv, *, tm):
    M = proj.shape[0]
    return pl.pallas_call(
        _mem_attn_kernel, grid=(M // tm,),
        in_specs=[pl.BlockSpec((tm, MEM_WIDTH), lambda i: (i, qblk)),
                  pl.BlockSpec((MEM_TOKENS, MEM_WIDTH), lambda i: (0, 0)),
                  pl.BlockSpec((MEM_TOKENS, MEM_WIDTH), lambda i: (0, 1))],
        out_specs=pl.BlockSpec((tm, MEM_WIDTH), lambda i: (i, 0)),
        out_shape=jax.ShapeDtypeStruct((M, MEM_WIDTH), BF16),
        compiler_params=_cparams("parallel"), name="mem_attn")(proj, kv, kv)


def _mem_attn_step_kernel(q_ref, k_ref, v_ref, o_ref, *, bs):
    rows = MEM_TOKENS * MEM_HEADS
    col_head = lax.broadcasted_iota(jnp.int32, (SUBLANES, rows), 1) % MEM_HEADS
    row_head = lax.broadcasted_iota(jnp.int32, (SUBLANES, rows), 0) % MEM_HEADS
    own = col_head == row_head

    def group(t, carry):
        bs_ = [t * STEP_INTERLEAVE + u for u in range(STEP_INTERLEAVE)]
        s = [_dot_nt(q_ref[b], k_ref[b]) * MEM_SCALE for b in bs_]
        p = [_softmax_rows(jnp.where(own, t_, -jnp.inf)) for t_ in s]
        for b, p_ in zip(bs_, p):
            o_ref[b] = _dot(p_, v_ref[b])
        return carry

    lax.fori_loop(0, bs // STEP_INTERLEAVE, group, 0)


def _mem_attn_step(q, mk, mv, layer, *, bs):
    B = q.shape[0]
    qblk = pl.BlockSpec((bs, SUBLANES, MEM_HEAD), lambda i: (i, 0, 0))
    cblk = pl.BlockSpec((None, bs, MEM_TOKENS * MEM_HEADS, MEM_HEAD), lambda i: (layer, i, 0, 0))
    return pl.pallas_call(
        functools.partial(_mem_attn_step_kernel, bs=bs), grid=(B // bs,),
        in_specs=[qblk, cblk, cblk], out_specs=qblk,
        out_shape=jax.ShapeDtypeStruct(q.shape, F32),
        compiler_params=_cparams("parallel"), name="mem_attn_step")(q, mk, mv)


def _deepnorm_ln(res, h, g, beta):
    z = ALPHA * res + h
    mu = jnp.mean(z, axis=-1, keepdims=True)
    d = z - mu
    var = jnp.mean(d * d, axis=-1, keepdims=True)
    return d * lax.rsqrt(var + LN_EPS) * g + beta


def _out_ln_kernel(*refs, widths):
    n = len(widths)
    lhs, (w_ref, res_ref, g_ref, beta_ref, of_ref, ob_ref) = refs[:n], refs[n:]
    tm = res_ref.shape[0]
    half = tm // 2 if tm % (2 * SUBLANES) == 0 else tm
    for rows in (slice(r0, r0 + half) for r0 in range(0, tm, half)):
        h, off = None, 0
        for a_ref, kw in zip(lhs, widths):
            part = jnp.dot(a_ref[rows, :], w_ref[off:off + kw, :], preferred_element_type=F32)
            h = part if h is None else h + part
            off += kw
        out = _deepnorm_ln(res_ref[rows, :], h, g_ref[...], beta_ref[...])
        of_ref[rows, :] = out
        ob_ref[rows, :] = out.astype(BF16)


def _out_ln(pieces, w, layer, res, g, beta, *, tm):
    M = pieces[0].shape[0]
    K, N = w.shape[1], w.shape[2]
    widths = tuple(p.shape[1] for p in pieces)
    assert sum(widths) == K and M % tm == 0
    row = lambda i: (i, 0)
    cst = lambda i: (0, 0)
    in_specs = ([pl.BlockSpec((tm, kw), row) for kw in widths]
                + [pl.BlockSpec((None, K, N), lambda i: (layer, 0, 0)),
                   pl.BlockSpec((tm, N), row), pl.BlockSpec((1, N), cst), pl.BlockSpec((1, N), cst)])
    oblk = pl.BlockSpec((tm, N), row)
    return pl.pallas_call(
        functools.partial(_out_ln_kernel, widths=widths), grid=(M // tm,),
        in_specs=in_specs, out_specs=[oblk, oblk],
        out_shape=[jax.ShapeDtypeStruct((M, N), F32), jax.ShapeDtypeStruct((M, N), BF16)],
        compiler_params=_cparams("parallel"), name="out_ln")(*pieces, w, res, g, beta)


def _ffn_down_ln_kernel(a_ref, w_ref, res_ref, g_ref, beta_ref, o_ref):
    k = pl.program_id(1)
    last = pl.num_programs(1) - 1
    part = jnp.dot(a_ref[...], w_ref[...], preferred_element_type=F32)

    @pl.when(k == 0)
    def _():
        o_ref[...] = part

    @pl.when((k > 0) & (k < last))
    def _():
        o_ref[...] += part

    @pl.when(k == last)
    def _():
        o_ref[...] = _deepnorm_ln(res_ref[...], o_ref[...] + part, g_ref[...], beta_ref[...])


def _ffn_down_ln(a, w, layer, res, g, beta, *, tm, tk):
    M, K = a.shape
    N = w.shape[2]
    assert w.shape[1] == K and K % tk == 0 and K // tk >= 2 and M % tm == 0
    row = lambda i, k: (i, 0)
    cst = lambda i, k: (0, 0)
    return pl.pallas_call(
        _ffn_down_ln_kernel, grid=(M // tm, K // tk),
        in_specs=[pl.BlockSpec((tm, tk), lambda i, k: (i, k)),
                  pl.BlockSpec((None, tk, N), lambda i, k: (layer, k, 0)),
                  pl.BlockSpec((tm, N), row), pl.BlockSpec((1, N), cst), pl.BlockSpec((1, N), cst)],
        out_specs=pl.BlockSpec((tm, N), row), out_shape=jax.ShapeDtypeStruct((M, N), F32),
        compiler_params=_cparams("parallel", "arbitrary"), name="ffn_down_ln")(a, w, res, g, beta)


def _ffn_up_kernel(x_ref, xs_ref, wg_ref, wu_ref, o_ref, os_ref):
    wg, wu = wg_ref[...].astype(BF16), wu_ref[...].astype(BF16)

    def swiglu(x):
        gate = jnp.dot(x, wg, preferred_element_type=F32)
        up = jnp.dot(x, wu, preferred_element_type=F32)
        return (gate * jax.nn.sigmoid(gate) * up).astype(BF16)

    o_ref[...] = swiglu(x_ref[...])

    @pl.when(pl.program_id(0) == 0)
    def _():
        os_ref[...] = swiglu(xs_ref[...])


def _ffn_up(x, xs, wg, wu, layer, *, tm, tn):
    M, K = x.shape
    N = wg.shape[2]
    nj = N // tn
    wblk = pl.BlockSpec((None, K, tn), lambda i, j: (layer, 0, j))
    return pl.pallas_call(
        _ffn_up_kernel, grid=(M // tm, nj),
        in_specs=[pl.BlockSpec((tm, K), lambda i, j: (i, 0)), pl.BlockSpec(xs.shape, lambda i, j: (0, 0)),
                  wblk, wblk],
        out_specs=[pl.BlockSpec((tm, tn), lambda i, j: (i, j)),
                   pl.BlockSpec((xs.shape[0], tn), lambda i, j: (0, jnp.where(i == 0, j, nj - 1)))],
        out_shape=[jax.ShapeDtypeStruct((M, N), BF16), jax.ShapeDtypeStruct((xs.shape[0], N), BF16)],
        compiler_params=_cparams("arbitrary", "arbitrary"), name="ffn_up")(x, xs, wg, wu)


def _rope_kernel(x_ref, cos_ref, sin_ref, o_ref):
    x = x_ref[...]
    lane = lax.broadcasted_iota(jnp.int32, x.shape, 1)
    first_half = (lane % SWA_HEAD) < (SWA_HEAD // 2)
    partner = jnp.where(first_half, pltpu.roll(x, LANES - SWA_HEAD // 2, 1),
                        pltpu.roll(x, SWA_HEAD // 2, 1))
    o_ref[...] = x * cos_ref[...] + partner * sin_ref[...]


def _rope(proj, cos, sin_signed, *, tm):
    M = proj.shape[0]
    width = TOK_WIDTH + SWA_KV_WIDTH
    blk = pl.BlockSpec((tm, LANES), lambda i, s: (i, s))
    tab = pl.BlockSpec((tm, LANES), lambda i, s: (i, 0))
    return pl.pallas_call(
        _rope_kernel, grid=(M // tm, width // LANES),
        in_specs=[blk, tab, tab], out_specs=blk,
        out_shape=jax.ShapeDtypeStruct((M, width), F32),
        compiler_params=_cparams("parallel", "arbitrary"), name="rope")(proj, cos, sin_signed)


def _sink_column(sink_ref, base, rows_per_head, nheads):
    rows = rows_per_head * nheads
    hid = lax.broadcasted_iota(jnp.int32, (rows, 1), 0) // rows_per_head
    col = jnp.zeros((rows, 1), F32)
    for j in range(nheads):
        col = jnp.where(hid == j, sink_ref[base + j], col)
    return col


def _sink_softmax(s, sink):
    m = jnp.maximum(jnp.max(s, axis=-1, keepdims=True), sink)
    p = jnp.exp(s - m)
    return p * (1.0 / (jnp.sum(p, axis=-1, keepdims=True) + jnp.exp(sink - m)))


def _swa_kernel(sink_ref, q_ref, kc_ref, kp_ref, vc_ref, vp_ref, cosc_ref, sinc_ref, cosp_ref, sinp_ref,
                o_ref, krot_ref):
    n = pl.program_id(0)
    nslab_q = TOK_WIDTH // LANES
    slab = lambda ref, s: ref[:, s * LANES:(s + 1) * LANES]
    cos_c, sin_c, cos_p, sin_p = cosc_ref[...], sinc_ref[...], cosp_ref[...], sinp_ref[...]
    lane = lax.broadcasted_iota(jnp.int32, (BLOCK, LANES), 1)
    first_half = (lane % SWA_HEAD) < (SWA_HEAD // 2)
    lo = lane < SWA_HEAD

    def rope(x, cos, sin):
        partner = jnp.where(first_half, pltpu.roll(x, LANES - SWA_HEAD // 2, 1),
                            pltpu.roll(x, SWA_HEAD // 2, 1))
        return x * cos + partner * sin

    kv_slabs = SWA_KV_WIDTH // LANES
    k_cur = [rope(slab(kc_ref, j), cos_c, sin_c) for j in range(kv_slabs)]
    k_prev = [rope(slab(kp_ref, j), cos_p, sin_p) for j in range(kv_slabs)]
    for j in range(kv_slabs):
        krot_ref[:, j * LANES:(j + 1) * LANES] = k_cur[j]
    lane2 = lax.broadcasted_iota(jnp.int32, (2 * BLOCK, LANES), 1)
    kd, vd = [], []
    for g in range(SWA_KV_HEADS):
        j, half = divmod(g, 2)
        keep = (lane2 < SWA_HEAD) if half == 0 else (lane2 >= SWA_HEAD)
        dup = lambda t: jnp.where(keep, t, pltpu.roll(t, SWA_HEAD, 1)).astype(BF16)
        kd.append(dup(jnp.concatenate([k_prev[j], k_cur[j]], axis=0)))
        vd.append(dup(jnp.concatenate([slab(vp_ref, j), slab(vc_ref, j)], axis=0)))
    qi = lax.broadcasted_iota(jnp.int32, (2 * BLOCK, 2 * BLOCK), 0) % BLOCK
    si = lax.broadcasted_iota(jnp.int32, (2 * BLOCK, 2 * BLOCK), 1)
    valid = (si > qi) & (si <= qi + WINDOW) & ((n > 0) | (si >= BLOCK))
    row_lo = lax.broadcasted_iota(jnp.int32, (2 * BLOCK, 1), 0) < BLOCK
    slabs = list(range(nslab_q))
    kv_of = [(2 * s) // SWA_GROUP for s in slabs]
    assert math.frexp(SWA_SCALE)[0] == 0.5
    q = [rope(slab(q_ref, s), cos_c, sin_c) * SWA_SCALE for s in slabs]
    qs = [jnp.concatenate([jnp.where(lo, t, 0.0), jnp.where(lo, 0.0, t)], axis=0).astype(BF16) for t in q]
    sc = [lax.dot_general(t, kd[g], NT_DIMS, preferred_element_type=F32) for t, g in zip(qs, kv_of)]
    sc = [jnp.where(valid, t, -jnp.inf) for t in sc]
    p = [_sink_softmax(t, jnp.where(row_lo, sink_ref[2 * s], sink_ref[2 * s + 1])) for t, s in zip(sc, slabs)]
    o = [jnp.dot(t.astype(BF16), vd[g], preferred_element_type=F32) for t, g in zip(p, kv_of)]
    for s, t in zip(slabs, o):
        o_ref[:, s * LANES:(s + 1) * LANES] = jnp.where(lo, t[:BLOCK], t[BLOCK:]).astype(o_ref.dtype)


def _swa_prompt(proj, cos, sin_signed, sinks):
    T = proj.shape[0]
    kblk, vblk = TOK_WIDTH // SWA_KV_WIDTH, TOK_WIDTH // SWA_KV_WIDTH + 1
    prev = lambda n: jnp.maximum(n - 1, 0)
    kv_spec = lambda blk, row: pl.BlockSpec((BLOCK, SWA_KV_WIDTH), lambda n: (row(n), blk))
    tab = lambda row: pl.BlockSpec((BLOCK, LANES), lambda n: (row(n), 0))
    cur = lambda n: n
    return pl.pallas_call(
        _swa_kernel, grid=(T // BLOCK,),
        in_specs=[pl.BlockSpec(memory_space=pltpu.SMEM),
                  pl.BlockSpec((BLOCK, TOK_WIDTH), lambda n: (n, 0)),
                  kv_spec(kblk, cur), kv_spec(kblk, prev), kv_spec(vblk, cur), kv_spec(vblk, prev),
                  tab(cur), tab(cur), tab(prev), tab(prev)],
        out_specs=[pl.BlockSpec((BLOCK, TOK_WIDTH), lambda n: (n, 0)),
                   pl.BlockSpec((BLOCK, SWA_KV_WIDTH), lambda n: (n, 0))],
        out_shape=[jax.ShapeDtypeStruct((T, TOK_WIDTH), BF16),
                   jax.ShapeDtypeStruct((T, SWA_KV_WIDTH), F32)],
        compiler_params=_cparams("arbitrary"), name="swa_banded")(
            sinks, proj, proj, proj, proj, proj, cos, sin_signed, cos, sin_signed)


def _swa_step_kernel(sink_ref, q_ref, kn_ref, vn_ref, kt_ref, vt_ref, o_ref, kto_ref, vto_ref, *, bs):
    sink = _sink_column(sink_ref, 0, 1, SWA_Q_HEADS)
    newest = lax.broadcasted_iota(jnp.int32, (SWA_HEAD, WINDOW), 1) == WINDOW - 1
    kv_of_head = lax.broadcasted_iota(jnp.int32, (SWA_Q_HEADS, 1), 0) // SWA_GROUP
    kv_heads = list(range(SWA_KV_HEADS))

    def pick(per_kv):
        out = per_kv[0]
        for g in kv_heads[1:]:
            out = jnp.where(kv_of_head == g, per_kv[g], out)
        return out

    def slid(new_ref, cache_ref, b):
        new = new_ref[b]
        return [jnp.where(newest, new[:, g:g + 1], pltpu.roll(cache_ref[b, g], WINDOW - 1, 1)) for g in kv_heads]

    def group(t, carry):
        bs_ = [t * STEP_INTERLEAVE + u for u in range(STEP_INTERLEAVE)]
        kt = [slid(kn_ref, kt_ref, b) for b in bs_]
        vt = [slid(vn_ref, vt_ref, b) for b in bs_]
        for b, kt_, vt_ in zip(bs_, kt, vt):
            for g in kv_heads:
                kto_ref[b, g] = kt_[g]
                vto_ref[b, g] = vt_[g]
        s = [pick([_dot(q_ref[b], t_) for t_ in kt_]) * SWA_SCALE for b, kt_ in zip(bs_, kt)]
        p = [_sink_softmax(s_, sink) for s_ in s]
        for b, p_, vt_ in zip(bs_, p, vt):
            o_ref[b] = pick([_dot_nt(p_, t_) for t_ in vt_])
        return carry

    lax.fori_loop(0, bs // STEP_INTERLEAVE, group, 0)


def _swa_step(q, k_new, v_new, kt, vt, layer, sinks, *, bs):
    B = q.shape[0]
    qblk = pl.BlockSpec((bs, SWA_Q_HEADS, SWA_HEAD), lambda i: (i, 0, 0))
    nblk = pl.BlockSpec((bs, SWA_HEAD, SWA_KV_HEADS), lambda i: (i, 0, 0))
    cshape = (bs, SWA_KV_HEADS, SWA_HEAD, WINDOW)
    cin = pl.BlockSpec((None,) + cshape, lambda i: (layer, i, 0, 0, 0))
    cout = pl.BlockSpec(cshape, lambda i: (i, 0, 0, 0))
    cache = jax.ShapeDtypeStruct(kt.shape[1:], F32)
    return pl.pallas_call(
        functools.partial(_swa_step_kernel, bs=bs), grid=(B // bs,),
        in_specs=[pl.BlockSpec(memory_space=pltpu.SMEM), qblk, nblk, nblk, cin, cin],
        out_specs=[qblk, cout, cout],
        out_shape=[jax.ShapeDtypeStruct(q.shape, F32), cache, cache],
        compiler_params=_cparams("parallel"), name="swa_step")(sinks, q, k_new, v_new, kt, vt)


ROW_TILE = 512
FFN_ROW_TILE = 1024
COL_TILE = 512
STEP_BATCH = 16
FFN_DOWN_K_TILE = FFN_HIDDEN // 2


def _row_tile(m):
    return ROW_TILE if m % ROW_TILE == 0 else m


def _ffn_row_tile(m):
    return FFN_ROW_TILE if m % FFN_ROW_TILE == 0 else _row_tile(m)


def _pad_rows(w, rows):
    return jnp.pad(w, ((0, rows - w.shape[0]), (0, 0)))


def _pad_cols(w, cols):
    return jnp.pad(w, ((0, 0), (0, cols - w.shape[1])))


def _rwkv_weights(w_in, mu, w1, w2, a1, a2, g1, g2):
    w_lora = jnp.concatenate([_pad_cols(w1[0], LORA_PAD), _pad_cols(a1[0], LORA_PAD), g1[0]], axis=1)
    return dict(w_in=w_in.astype(BF16), w_lora=w_lora.astype(BF16), mu=mu[0][:, None, :],
                w2=_pad_rows(w2[0], LORA_PAD).astype(BF16), a2=_pad_rows(a2[0], LORA_PAD).astype(BF16),
                g2=g2[0].astype(BF16))


def _unblock_state(s_bd):
    n = RWKV_HEAD
    return jnp.stack([s_bd[:, :n, :n], s_bd[:, n:, n:]], axis=1).reshape(RWKV_HEADS, n, n)


def _rope_tables(pos):
    half = SWA_HEAD // 2
    inv = ROPE_THETA ** (-jnp.arange(half, dtype=F32) / half)
    ang = pos.astype(F32)[:, None] * inv[None, :]
    cos, sin = jnp.cos(ang), jnp.sin(ang)
    reps = LANES // SWA_HEAD
    return jnp.tile(cos, (1, 2 * reps)), jnp.tile(jnp.concatenate([-sin, sin], axis=1), (1, reps))


def _post_mixer(groups, sw, layer):
    row = lambda t: t[layer][None, :]
    x1 = [_out_ln([tok, mo], sw["w_out"], layer, x, row(sw["ln1_g"]), row(sw["ln1_b"]),
                  tm=_row_tile(x.shape[0])) for tok, mo, x in groups]
    (x1f_p, x1b_p), (x1f_s, x1b_s) = x1
    hff = _ffn_up(x1b_p, x1b_s, sw["w_gate"], sw["w_up"], layer, tm=_ffn_row_tile(x1b_p.shape[0]), tn=COL_TILE)
    return [_ffn_down_ln(h, sw["w_down"], layer, xf, row(sw["ln2_g"]), row(sw["ln2_b"]),
                         tm=_row_tile(xf.shape[0]), tk=FFN_DOWN_K_TILE) for h, xf in zip(hff, (x1f_p, x1f_s))]


def kernel(x_prompt, x_sample, mem_prompt, cache_mem_k, cache_mem_v, state_rwkv_shift, state_rwkv_wkv, cache_swa_k, cache_swa_v, w_in_rwkv, rwkv_mu, rwkv_w0, rwkv_w1, rwkv_w2, rwkv_a0, rwkv_a1, rwkv_a2, rwkv_g1, rwkv_g2, rwkv_k_k, rwkv_k_a, rwkv_r_k, rwkv_gn_g, rwkv_gn_b, w_in_swa, swa_sinks, w_mem_kv, w_out, ln1_g, ln1_b, w_gate, w_up, w_down, ln2_g, ln2_b):
    assert DEPTH == 2 and x_prompt.shape[0] == 1 and x_sample.shape[1] == 1
    T = x_prompt.shape[1]
    B = x_sample.shape[0]
    past_len = T
    row = lambda t: t[None, :]
    shared = dict(w_out=w_out.astype(BF16), ln1_g=ln1_g, ln1_b=ln1_b, w_gate=w_gate, w_up=w_up,
                  w_down=w_down.astype(BF16), ln2_g=ln2_g, ln2_b=ln2_b)
    RW = _rwkv_weights(w_in_rwkv, rwkv_mu, rwkv_w1, rwkv_w2, rwkv_a1, rwkv_a2, rwkv_g1, rwkv_g2)
    rk, gn_g, gn_b = row(rwkv_r_k[0].reshape(-1)), row(rwkv_gn_g[0]), row(rwkv_gn_b[0])
    prep_consts = (row(rwkv_w0[0]), row(rwkv_a0[0]), row(rwkv_k_k[0]), row(rwkv_k_a[0]))
    w_swa = w_in_swa.astype(BF16)
    sinks = swa_sinks[0]
    q_blk_rwkv = 3 * TOK_WIDTH // MEM_WIDTH
    q_blk_swa = (TOK_WIDTH + 2 * SWA_KV_WIDTH) // MEM_WIDTH

    xp, xs = x_prompt[0], x_sample[:, 0]
    tm, tms = _row_tile(T), _row_tile(B)
    kv = [_proj(mem_prompt[0], w_mem_kv, i, tm=MEM_TOKENS, tn=COL_TILE, name="mem_kv") for i in range(DEPTH)]
    prompt_mem_k = jnp.stack([t[:, :MEM_WIDTH] for t in kv]).reshape(DEPTH, 1, MEM_TOKENS, MEM_HEADS, MEM_HEAD)
    prompt_mem_v = jnp.stack([t[:, MEM_WIDTH:] for t in kv]).reshape(DEPTH, 1, MEM_TOKENS, MEM_HEADS, MEM_HEAD)
    mem_k = cache_mem_k.reshape(DEPTH, B, MEM_TOKENS * MEM_HEADS, MEM_HEAD)
    mem_v = cache_mem_v.reshape(DEPTH, B, MEM_TOKENS * MEM_HEADS, MEM_HEAD)

    def mem_step(q, layer):
        q_rows = jnp.pad(q.reshape(B, MEM_HEADS, MEM_HEAD), ((0, 0), (0, SUBLANES - MEM_HEADS), (0, 0)))
        out = _mem_attn_step(q_rows, mem_k, mem_v, layer, bs=STEP_BATCH)
        return out[:, :MEM_HEADS].reshape(B, MEM_WIDTH).astype(BF16)

    proj = _rwkv_in(xp, None, RW["mu"], RW["w_in"], RW["w_lora"], 0, tm=_ffn_row_tile(T))
    tok_p, s_bd = _wkv_prompt(proj, RW["w2"], RW["a2"], RW["g2"], prep_consts + (rk, gn_g, gn_b))
    mo_p = _mem_attn(proj, q_blk_rwkv, kv[0], tm=tm)
    prompt_shift = xp[-1][None, None, :]
    prompt_wkv = _unblock_state(s_bd)[None, None]
    proj = _rwkv_in(xs, state_rwkv_shift[0], RW["mu"], RW["w_in"], RW["w_lora"], 0, tm=tms)
    vecs = _rwkv_prep_t(proj, RW["w2"], RW["a2"], RW["g2"], *prep_consts)
    lanes_b = lambda t: jnp.broadcast_to(t.reshape(TOK_WIDTH, 1), (TOK_WIDTH, B))
    s_new, tok_t = _wkv_step(state_rwkv_wkv.transpose(0, 2, 3, 4, 1), 0, vecs,
                             (lanes_b(rk), lanes_b(gn_g), lanes_b(gn_b)))
    mo_s = mem_step(proj[:, 3 * TOK_WIDTH:3 * TOK_WIDTH + MEM_WIDTH], 0)
    sample_shift = xs[None]
    sample_wkv = s_new.transpose(3, 0, 1, 2)[None]
    xf_p, xf_s = _post_mixer([(tok_p, mo_p, xp), (tok_t.T.astype(BF16), mo_s, xs)], shared, 0)

    proj = _proj(xf_p, w_swa, 0, tm=_ffn_row_tile(T), tn=COL_TILE, name="swa_proj")
    cos, sin = _rope_tables(jnp.arange(T))
    tok_p, k_rot = _swa_prompt(proj, cos, sin, sinks)
    mo_p = _mem_attn(proj, q_blk_swa, kv[1], tm=tm)
    v_last = proj[T - WINDOW:, TOK_WIDTH + SWA_KV_WIDTH:TOK_WIDTH + 2 * SWA_KV_WIDTH]
    prompt_swa_k = k_rot[T - WINDOW:].reshape(1, 1, WINDOW, SWA_KV_HEADS, SWA_HEAD)
    prompt_swa_v = v_last.reshape(1, 1, WINDOW, SWA_KV_HEADS, SWA_HEAD)
    proj = _proj(xf_s, w_swa, 0, tm=tms, tn=COL_TILE, name="swa_proj")
    cos, sin = _rope_tables(jnp.full((B,), past_len))
    qk = _rope(proj, cos, sin, tm=tms)
    chan_major = lambda t: t.reshape(B, SWA_KV_HEADS, SWA_HEAD).transpose(0, 2, 1)
    k_new = chan_major(qk[:, TOK_WIDTH:])
    v_new = chan_major(proj[:, TOK_WIDTH + SWA_KV_WIDTH:TOK_WIDTH + 2 * SWA_KV_WIDTH])
    o, kc, vc = _swa_step(qk[:, :TOK_WIDTH].reshape(B, SWA_Q_HEADS, SWA_HEAD), k_new, v_new,
                          cache_swa_k.transpose(0, 1, 3, 4, 2), cache_swa_v.transpose(0, 1, 3, 4, 2),
                          0, sinks, bs=STEP_BATCH)
    mo_s = mem_step(proj[:, TOK_WIDTH + 2 * SWA_KV_WIDTH:], 1)
    sample_swa_k, sample_swa_v = kc.transpose(0, 3, 1, 2)[None], vc.transpose(0, 3, 1, 2)[None]
    y_prompt, y_sample = _post_mixer([(tok_p, mo_p, xf_p), (o.reshape(B, TOK_WIDTH).astype(BF16), mo_s, xf_s)],
                                     shared, 1)

    return (y_prompt[None], y_sample[:, None, :], prompt_mem_k, prompt_mem_v, prompt_shift, prompt_wkv,
            prompt_swa_k, prompt_swa_v, sample_shift, sample_wkv, sample_swa_k, sample_swa_v)
```

```python
import functools
import math

import jax
import jax.numpy as jnp
from jax import lax
from jax.experimental import pallas as pl
from jax.experimental.pallas import tpu as pltpu

D_MODEL = 2048
DEPTH = 2
MEM_WIDTH = D_MODEL // 4
TOK_WIDTH = D_MODEL - MEM_WIDTH
RWKV_HEAD = 64
RWKV_HEADS = TOK_WIDTH // RWKV_HEAD
GN_EPS = RWKV_HEAD * 1e-5
SWA_HEAD = 64
SWA_Q_HEADS = TOK_WIDTH // SWA_HEAD
SWA_KV_HEADS = 4
SWA_GROUP = SWA_Q_HEADS // SWA_KV_HEADS
SWA_KV_WIDTH = SWA_KV_HEADS * SWA_HEAD
WINDOW = 128
BLOCK = 128
SWA_SCALE = SWA_HEAD ** -0.5
ROPE_THETA = 10000.0
MEM_TOKENS = 256
MEM_HEADS = 4
MEM_HEAD = MEM_WIDTH // MEM_HEADS
MEM_SCALE = MEM_HEAD ** -0.5
FFN_HIDDEN = int(math.ceil(8 * D_MODEL / 3 / 256)) * 256
ALPHA = (2 * DEPTH) ** 0.25
LN_EPS = 1e-5
LORA_PAD = 128
LORA_IN_WIDTH = 512

LANES = 128
SUBLANES = 8
VMEM_LIMIT_BYTES = 56 * 1024 * 1024

BF16 = jnp.bfloat16
F32 = jnp.float32
NT_DIMS = (((1,), (1,)), ((), ()))
TN_DIMS = (((0,), (0,)), ((), ()))


def _dot(a, b):
    return jnp.dot(a.astype(BF16), b.astype(BF16), preferred_element_type=F32)


def _dot_nt(a, b):
    return lax.dot_general(a.astype(BF16), b.astype(BF16), NT_DIMS, preferred_element_type=F32)


def _dot_tn(a, b):
    return lax.dot_general(a.astype(BF16), b.astype(BF16), TN_DIMS, preferred_element_type=F32)


def _split_dot(x, m):
    hi = x.astype(BF16)
    lo = (x - hi.astype(F32)).astype(BF16)
    return (jnp.dot(hi, m, preferred_element_type=F32)
            + jnp.dot(lo, m, preferred_element_type=F32))


def _head_ones():
    p = lax.broadcasted_iota(jnp.int32, (LANES, LANES), 0)
    q = lax.broadcasted_iota(jnp.int32, (LANES, LANES), 1)
    return ((p // RWKV_HEAD) == (q // RWKV_HEAD)).astype(BF16)


def _cparams(*sem):
    return pltpu.CompilerParams(dimension_semantics=sem, vmem_limit_bytes=VMEM_LIMIT_BYTES)


def _rider_block(i, j, nj):
    return jnp.where(i == 0, j, nj - 1)


def _proj_kernel(*refs, rider):
    if rider:
        x_ref, xs_ref, w_ref, o_ref, os_ref, xb_ref = refs
    else:
        x_ref, w_ref, o_ref, xb_ref = refs

    @pl.when(pl.program_id(1) == 0)
    def _():
        xb_ref[...] = x_ref[...].astype(BF16)

    w = w_ref[...].astype(BF16)
    o_ref[...] = jnp.dot(xb_ref[...], w, preferred_element_type=F32)
    if rider:
        @pl.when(pl.program_id(0) == 0)
        def _():
            os_ref[...] = jnp.dot(xs_ref[...].astype(BF16), w, preferred_element_type=F32)


def _proj(x, w, layer, *, tm, tn, name, xs=None):
    M, K = x.shape
    N = w.shape[2]
    assert M % tm == 0 and N % tn == 0
    nj = N // tn
    rider = xs is not None
    in_specs = [pl.BlockSpec((tm, K), lambda i, j: (i, 0))]
    out_specs = [pl.BlockSpec((tm, tn), lambda i, j: (i, j))]
    out_shape = [jax.ShapeDtypeStruct((M, N), F32)]
    if rider:
        in_specs.append(pl.BlockSpec(xs.shape, lambda i, j: (0, 0)))
        out_specs.append(pl.BlockSpec((xs.shape[0], tn), lambda i, j: (0, _rider_block(i, j, nj))))
        out_shape.append(jax.ShapeDtypeStruct((xs.shape[0], N), F32))
    in_specs.append(pl.BlockSpec((None, K, tn), lambda i, j: (layer, 0, j)))
    out = pl.pallas_call(
        functools.partial(_proj_kernel, rider=rider), grid=(M // tm, nj),
        in_specs=in_specs, out_specs=out_specs, out_shape=out_shape,
        scratch_shapes=[pltpu.VMEM((tm, K), BF16)],
        compiler_params=_cparams("arbitrary", "arbitrary"), name=name)(*([x, xs, w] if rider else [x, w]))
    return out if rider else out[0]


MIX_R, MIX_W, MIX_K, MIX_V, MIX_A, MIX_G = range(6)
RWKV_IN_WIDTH = 3 * TOK_WIDTH + MEM_WIDTH + LORA_IN_WIDTH
RWKV_IN_TILE = LORA_IN_WIDTH
RWKV_IN_KEPT = (MIX_R, MIX_K, MIX_V)
RWKV_IN_SLOT = ([0] * (TOK_WIDTH // RWKV_IN_TILE) + [1] * (TOK_WIDTH // RWKV_IN_TILE)
                + [2] * (TOK_WIDTH // RWKV_IN_TILE) + [3] * (MEM_WIDTH // RWKV_IN_TILE))
LORA_PARTS = ((MIX_W, 0, LORA_PAD), (MIX_A, LORA_PAD, 2 * LORA_PAD), (MIX_G, 2 * LORA_PAD, LORA_IN_WIDTH))
RWKV_IN_SUB = 256


def _rwkv_in_kernel(slot_ref, x_ref, xp_ref, mu_ref, w_ref, wl_ref, o_ref, lhs_ref, *, shift):
    i, j = pl.program_id(0), pl.program_id(1)
    tm = x_ref.shape[0]
    sub = min(RWKV_IN_SUB, tm)
    blocks = [slice(s0, s0 + sub) for s0 in range(0, tm, sub)]

    def x_and_delta(rows):
        x = x_ref[rows, :]
        if not shift:
            return x, xp_ref[rows, :] - x
        if rows.start == 0:
            first = jnp.where(i > 0, xp_ref[SUBLANES - 1:SUBLANES, :], 0.0)
        else:
            first = x_ref[rows.start - 1:rows.start, :]
        rowid = lax.broadcasted_iota(jnp.int32, (sub, 1), 0)
        return x, jnp.where(rowid == 0, first, pltpu.roll(x, 1, 0)) - x

    @pl.when(j == 0)
    def _():
        for rows in blocks:
            x, d = x_and_delta(rows)
            for slot, m in enumerate(RWKV_IN_KEPT):
                lhs_ref[slot, rows, :] = (x + d * mu_ref[m]).astype(BF16)
            lhs_ref[len(RWKV_IN_KEPT), rows, :] = x.astype(BF16)

    nproj = pl.num_programs(1) - 1

    @pl.when(j < nproj)
    def _():
        o_ref[...] = jnp.dot(lhs_ref[slot_ref[j]], w_ref[...], preferred_element_type=F32)

    @pl.when(j == nproj)
    def _():
        for rows in blocks:
            x, d = x_and_delta(rows)
            for m, lo, hi in LORA_PARTS:
                o_ref[rows, lo:hi] = jnp.dot((x + d * mu_ref[m]).astype(BF16), wl_ref[:, lo:hi],
                                             preferred_element_type=F32)


def _rwkv_in(x, xprev, mu, w, w_lora, layer, *, tm):
    M, K = x.shape
    tn = RWKV_IN_TILE
    nproj = w.shape[2] // tn
    assert M % tm == 0 and w.shape[2] + w_lora.shape[1] == RWKV_IN_WIDTH and nproj == len(RWKV_IN_SLOT)
    assert tm % min(RWKV_IN_SUB, tm) == 0
    shift = xprev is None
    if shift:
        rows8 = tm // SUBLANES
        xp_spec = pl.BlockSpec((SUBLANES, K), lambda i, j, m: (jnp.maximum(i * rows8 - 1, 0), 0))
        xprev = x
    else:
        xp_spec = pl.BlockSpec((tm, K), lambda i, j, m: (i, 0))
    slot = jnp.asarray(RWKV_IN_SLOT + [0], jnp.int32)
    grid_spec = pltpu.PrefetchScalarGridSpec(
        num_scalar_prefetch=1, grid=(M // tm, nproj + 1),
        in_specs=[pl.BlockSpec((tm, K), lambda i, j, m: (i, 0)), xp_spec,
                  pl.BlockSpec(mu.shape, lambda i, j, m: (0, 0, 0)),
                  pl.BlockSpec((None, K, tn), lambda i, j, m: (layer, 0, jnp.minimum(j, nproj - 1))),
                  pl.BlockSpec(w_lora.shape, lambda i, j, m: (0, 0))],
        out_specs=pl.BlockSpec((tm, tn), lambda i, j, m: (i, j)),
        scratch_shapes=[pltpu.VMEM((len(RWKV_IN_KEPT) + 1, tm, K), BF16)])
    return pl.pallas_call(
        functools.partial(_rwkv_in_kernel, shift=shift), grid_spec=grid_spec,
        out_shape=jax.ShapeDtypeStruct((M, RWKV_IN_WIDTH), F32),
        compiler_params=_cparams("parallel", "arbitrary"), name="rwkv_in")(slot, x, xprev, mu, w, w_lora)


def _softplus(z):
    return jnp.maximum(z, 0.0) + jnp.log1p(jnp.exp(-jnp.abs(z)))


def _each(f, *lists):
    return [f(*a) for a in zip(*lists)]


def _lora_hidden(hl):
    return jnp.tanh(hl[:, 0:LORA_PAD]), hl[:, LORA_PAD:2 * LORA_PAD], jax.nn.sigmoid(hl[:, 2 * LORA_PAD:])


def _prep_slabs(k, wl, al, w0, a0, k_k, k_a, ones):
    lw = _each(lambda wl_, w0_: -jnp.exp(-_softplus(-(w0_ + wl_)) - 0.5), wl, w0)
    agate = _each(lambda al_, a0_: jax.nn.sigmoid(a0_ + al_), al, a0)
    kkr = _each(lambda k_, c_: k_ * c_, k, k_k)
    ss = _each(lambda t: _split_dot(t * t, ones), kkr)
    kn = _each(lambda t, s_: t / jnp.maximum(jnp.sqrt(s_), 1e-12), kkr, ss)
    bv = _each(lambda n_, a_: n_ * a_, kn, agate)
    kp = _each(lambda k_, a_, c_: k_ * (1.0 + (a_ - 1.0) * c_), k, agate, k_a)
    return lw, kp, kn, bv


def _rwkv_prep_t_kernel(hl_ref, r_ref, k_ref, v_ref, w2_ref, a2_ref, g2_ref, w0_ref, a0_ref, kk_ref, ka_ref,
                        rt_ref, vt_ref, lw_ref, kp_ref, kn_ref, bv_ref, g_ref):
    hw, ha, hg = _lora_hidden(hl_ref[...])
    outs = _prep_slabs([k_ref[...]], [_dot(hw, w2_ref[...])], [_dot(ha, a2_ref[...])], [w0_ref[...]],
                       [a0_ref[...]], [kk_ref[...]], [ka_ref[...]], _head_ones())
    vals = [r_ref[...], v_ref[...]] + [t[0] for t in outs] + [_dot(hg, g2_ref[...])]
    for ref, val in zip((rt_ref, vt_ref, lw_ref, kp_ref, kn_ref, bv_ref, g_ref), vals):
        ref[...] = val.T


def _rwkv_prep_t(proj, w2p, a2p, g2, w0, a0, k_k, k_a):
    B = proj.shape[0]
    assert B == LANES
    nslab = TOK_WIDTH // LANES
    lora_blk = (RWKV_IN_WIDTH - LORA_IN_WIDTH) // LORA_IN_WIDTH
    col = lambda s: (0, s)
    slab = lambda base: pl.BlockSpec((B, LANES), lambda s, base=base: (0, base + s))
    outs = [jax.ShapeDtypeStruct((TOK_WIDTH, B), F32)] * 7
    return pl.pallas_call(
        _rwkv_prep_t_kernel, grid=(nslab,),
        in_specs=[pl.BlockSpec((B, LORA_IN_WIDTH), lambda s: (0, lora_blk)),
                  slab(0), slab(nslab), slab(2 * nslab),
                  pl.BlockSpec((LORA_PAD, LANES), col),
                  pl.BlockSpec((LORA_PAD, LANES), col),
                  pl.BlockSpec((2 * LORA_PAD, LANES), col),
                  pl.BlockSpec((1, LANES), col), pl.BlockSpec((1, LANES), col),
                  pl.BlockSpec((1, LANES), col), pl.BlockSpec((1, LANES), col)],
        out_specs=[pl.BlockSpec((LANES, B), lambda s: (s, 0))] * 7,
        out_shape=outs, compiler_params=_cparams("arbitrary"),
        name="rwkv_prep_t")(proj, proj, proj, proj, w2p, a2p, g2, w0, a0, k_k, k_a)


def _gn_gate(y, r, kp, v, g, rk, gg, gb, ones):
    inv_n = 1.0 / RWKV_HEAD
    rows = y[0].shape[0]
    sums = _each(lambda y_, r_, k_, rk_: _split_dot(jnp.concatenate([y_, r_ * k_ * rk_], axis=0), ones),
                 y, r, kp, rk)
    d = _each(lambda y_, s_: y_ - s_[:rows] * inv_n, y, sums)
    var = _each(lambda d_: _split_dot(d_ * d_, ones) * inv_n, d)
    return _each(lambda d_, var_, gg_, gb_, s_, v_, g_:
                 (d_ * lax.rsqrt(var_ + GN_EPS) * gg_ + gb_ + s_[rows:] * v_) * g_,
                 d, var, gg, gb, sums, v, g)


WKV_CHUNK = 64


def _wkv_masks():
    n = 2 * WKV_CHUNK
    p = lax.broadcasted_iota(jnp.int32, (n, n), 0)
    q = lax.broadcasted_iota(jnp.int32, (n, n), 1)
    same = lambda b: (p // b) == (q // b)
    pt, qt = p % WKV_CHUNK, q % WKV_CHUNK
    s8, s16, s32, s64 = same(8), same(16), same(32), same(WKV_CHUNK)
    return dict(strict=s64 & (pt > qt), incl=s64 & (pt >= qt), s8=s8,
                e16=s16 & ~s8, e32=s32 & ~s16, e64=s64 & ~s32,
                eye=(p == q).astype(F32))


WKV_PAIRS = TOK_WIDTH // LANES
WKV_STEP_CHUNKS = 2


def _wkv_lanes(r, lw, k, v, kn, bv, mk, tri, lane_lo):
    stack = lambda x: jnp.concatenate([jnp.where(lane_lo, x, 0.0), jnp.where(lane_lo, 0.0, x)], axis=0)
    n = 2 * WKV_CHUNK
    c = _each(lambda t: _split_dot_left(tri, t), lw)
    c_last = _each(lambda t: t[WKV_CHUNK - 1:WKV_CHUNK, :], c)
    e_out = _each(lambda t: jnp.exp(-t), c)
    e_end = _each(lambda t, tl: jnp.exp(tl - t), c, c_last)
    ah = _each(lambda kn_, c_, lw_: stack(-kn_ * jnp.exp(c_ - lw_)), kn, c, lw)
    rh = _each(lambda r_, c_: stack(r_ * jnp.exp(c_)), r, c)
    bh = _each(lambda b_, e_: stack(b_ * e_), bv, e_out)
    kh = _each(lambda k_, e_: stack(k_ * e_), k, e_out)
    bbar = _each(lambda b_, e_: stack(b_ * e_), bv, e_end)
    kbar = _each(lambda k_, e_: stack(k_ * e_), k, e_end)
    vs = _each(stack, v)
    gm = _each(lambda a_, r_, b_, k_: _dot_nt(jnp.concatenate([a_, r_], axis=0),
                                              jnp.concatenate([b_, k_], axis=0)), ah, rh, bh, kh)
    a_ab = _each(lambda g_: jnp.where(mk["strict"], g_[:n, :n], 0.0), gm)
    a_ak = _each(lambda g_: jnp.where(mk["strict"], g_[:n, n:], 0.0), gm)
    l_rb = _each(lambda g_: jnp.where(mk["incl"], g_[n:, :n], 0.0), gm)
    l_rk = _each(lambda g_: jnp.where(mk["incl"], g_[n:, n:], 0.0), gm)
    d1 = _each(lambda a_: jnp.where(mk["s8"], a_, 0.0), a_ab)
    x = _each(lambda d_: mk["eye"] + d_, d1)
    d2 = _each(lambda d_: _dot(d_, d_), d1)
    x = _each(lambda x_, d_: x_ + _dot(x_, d_), x, d2)
    d4 = _each(lambda d_: _dot(d_, d_), d2)
    x = _each(lambda x_, d_: x_ + _dot(x_, d_), x, d4)
    for lvl in ("e16", "e32", "e64"):
        ex = _each(lambda a_, x_: _dot(jnp.where(mk[lvl], a_, 0.0), x_), a_ab, x)
        x = _each(lambda x_, e_: x_ + _dot(x_, e_), x, ex)
    av = _each(_dot, a_ak, vs)
    tw = _each(lambda x_, a_, v_: _dot(x_, jnp.concatenate([a_, v_], axis=1)), x, ah, av)
    lwm = _each(_dot, l_rb, tw)
    lv = _each(_dot, l_rk, vs)
    qm = _each(lambda r_, l_: r_ + l_[:, :n], rh, lwm)
    y0 = _each(lambda l_, v_: l_[:, n:] + v_, lwm, lv)
    mt = _each(lambda t_, b_: _dot_tn(t_[:, :n], b_), tw, bbar)
    nt = _each(lambda t_, b_, v_, k_: _dot_tn(t_[:, n:], b_) + _dot_tn(v_, k_), tw, bbar, vs, kbar)
    return qm, y0, mt, nt, _each(jnp.exp, c_last)


def _split_dot_left(m, x):
    hi = x.astype(BF16)
    lo = (x - hi.astype(F32)).astype(BF16)
    return (jnp.dot(m, hi, preferred_element_type=F32)
            + jnp.dot(m, lo, preferred_element_type=F32))


def _wkv_kernel(r_ref, k_ref, v_ref, hl_ref, w2_ref, a2_ref, g2_ref, w0_ref, a0_ref, kk_ref, ka_ref,
                rk_ref, gg_ref, gb_ref, o_ref, sout_ref, s_scr):
    c_idx = pl.program_id(0)

    @pl.when(c_idx == 0)
    def _():
        s_scr[...] = jnp.zeros_like(s_scr)

    mk = _wkv_masks()
    ti = lax.broadcasted_iota(jnp.int32, (WKV_CHUNK, WKV_CHUNK), 0)
    tj = lax.broadcasted_iota(jnp.int32, (WKV_CHUNK, WKV_CHUNK), 1)
    tri = (ti >= tj).astype(BF16)
    lane_lo = lax.broadcasted_iota(jnp.int32, (WKV_CHUNK, LANES), 1) < RWKV_HEAD
    ones = _head_ones()
    pairs = range(WKV_PAIRS)
    lanes = [(slice(ch * WKV_CHUNK, (ch + 1) * WKV_CHUNK), slice(p * LANES, (p + 1) * LANES))
             for ch in range(WKV_STEP_CHUNKS) for p in pairs]
    cut = lambda t: [t[rows, sl] for rows, sl in lanes]
    rep = lambda t: [t[:, sl] for _, sl in lanes]
    hw, ha, hg = _lora_hidden(hl_ref[...])
    wl, al, g = _dot(hw, w2_ref[...]), _dot(ha, a2_ref[...]), _dot(hg, g2_ref[...])
    r, v = cut(r_ref), cut(v_ref)
    lw, kp, kn, bv = _prep_slabs(cut(k_ref), cut(wl), cut(al), rep(w0_ref), rep(a0_ref),
                                 rep(kk_ref), rep(ka_ref), ones)
    qm, y0, mt, nt, dec = _wkv_lanes(r, lw, kp, v, kn, bv, mk, tri, lane_lo)
    S = [s_scr[p] for p in pairs]
    y = []
    for ch in range(WKV_STEP_CHUNKS):
        part = slice(ch * WKV_PAIRS, (ch + 1) * WKV_PAIRS)
        ys = _each(lambda q_, s_, y_: _dot_nt(q_, s_) + y_, qm[part], S, y0[part])
        y += _each(lambda t: t[:WKV_CHUNK, :] + t[WKV_CHUNK:, :], ys)
        S = _each(lambda s_, d_, m_, n_: s_ * d_ + _dot(s_, m_) + n_, S, dec[part], mt[part], nt[part])
    for p in pairs:
        s_scr[p] = S[p]
    tok = _gn_gate(y, r, kp, v, cut(g), rep(rk_ref), rep(gg_ref), rep(gb_ref), ones)
    for (rows, sl), t in zip(lanes, tok):
        o_ref[rows, sl] = t.astype(o_ref.dtype)

    @pl.when(c_idx == pl.num_programs(0) - 1)
    def _():
        sout_ref[...] = s_scr[...]


def _wkv_prompt(proj, w2p, a2p, g2, consts):
    T = proj.shape[0]
    rows = WKV_STEP_CHUNKS * WKV_CHUNK
    assert T % rows == 0
    lora_blk = (RWKV_IN_WIDTH - LORA_IN_WIDTH) // LORA_IN_WIDTH
    tok = lambda blk: pl.BlockSpec((rows, TOK_WIDTH), lambda c, blk=blk: (c, blk))
    full = lambda a: pl.BlockSpec(a.shape, lambda c: (0,) * a.ndim)
    weights = (w2p, a2p, g2) + tuple(consts)
    return pl.pallas_call(
        _wkv_kernel, grid=(T // rows,),
        in_specs=[tok(0), tok(1), tok(2), pl.BlockSpec((rows, LORA_IN_WIDTH), lambda c: (c, lora_blk))]
        + [full(a) for a in weights],
        out_specs=[tok(0), pl.BlockSpec((WKV_PAIRS, LANES, LANES), lambda c: (0, 0, 0))],
        out_shape=[jax.ShapeDtypeStruct((T, TOK_WIDTH), BF16),
                   jax.ShapeDtypeStruct((WKV_PAIRS, LANES, LANES), F32)],
        scratch_shapes=[pltpu.VMEM((WKV_PAIRS, LANES, LANES), F32)],
        compiler_params=_cparams("arbitrary"), name="wkv_chunked")(proj, proj, proj, proj, *weights)


WKV_STEP_UNROLL = 4


def _wkv_step_kernel(s_ref, r_ref, v_ref, lw_ref, kp_ref, kn_ref, bv_ref, g_ref, rk_ref, gg_ref, gb_ref,
                     so_ref, tok_ref, y_scr):
    n = RWKV_HEAD
    inv_n = 1.0 / n
    for hh in range(2):
        rows = slice(hh * n, (hh + 1) * n)
        a, w = -kn_ref[rows, :], jnp.exp(lw_ref[rows, :])
        b, k, r = bv_ref[rows, :], kp_ref[rows, :], r_ref[rows, :]

        def value_row(i, carry, hh=hh, a=a, w=w, b=b, k=k, r=r):
            s = s_ref[hh, i]
            sa = jnp.sum(s * a, axis=0, keepdims=True)
            s_new = s * w + sa * b + v_ref[pl.ds(hh * n + i, 1), :] * k
            so_ref[hh, i] = s_new
            y_scr[pl.ds(hh * n + i, 1), :] = jnp.sum(s_new * r, axis=0, keepdims=True)
            return carry

        lax.fori_loop(0, n, value_row, 0, unroll=WKV_STEP_UNROLL)
    for hh in range(2):
        rows = slice(hh * n, (hh + 1) * n)
        y = y_scr[rows, :]
        d = y - jnp.sum(y, axis=0, keepdims=True) * inv_n
        var = jnp.sum(d * d, axis=0, keepdims=True) * inv_n
        bonus = jnp.sum(r_ref[rows, :] * kp_ref[rows, :] * rk_ref[rows, :], axis=0, keepdims=True)
        yn = d * lax.rsqrt(var + GN_EPS) * gg_ref[rows, :] + gb_ref[rows, :]
        tok_ref[rows, :] = (yn + bonus * v_ref[rows, :]) * g_ref[rows, :]


def _wkv_step(state, layer, vecs, consts):
    B = state.shape[-1]
    sshape = (2, RWKV_HEAD, RWKV_HEAD, B)
    vblk = pl.BlockSpec((LANES, B), lambda p: (p, 0))
    return pl.pallas_call(
        _wkv_step_kernel, grid=(WKV_PAIRS,),
        in_specs=[pl.BlockSpec((None,) + sshape, lambda p: (layer, p, 0, 0, 0))] + [vblk] * 10,
        out_specs=[pl.BlockSpec(sshape, lambda p: (p, 0, 0, 0)), vblk],
        out_shape=[jax.ShapeDtypeStruct(state.shape[1:], F32), jax.ShapeDtypeStruct((TOK_WIDTH, B), F32)],
        scratch_shapes=[pltpu.VMEM((LANES, B), F32)],
        compiler_params=_cparams("parallel"), name="wkv_step")(state, *vecs, *consts)


STEP_INTERLEAVE = 4


def _softmax_rows(s):
    m = jnp.max(s, axis=-1, keepdims=True)
    e = jnp.exp(s - m)
    return e * (1.0 / jnp.sum(e, axis=-1, keepdims=True))


def _mem_attn_kernel(q_ref, k_ref, v_ref, o_ref):
    for h in range(MEM_HEADS):
        sl = slice(h * MEM_HEAD, (h + 1) * MEM_HEAD)
        p = _softmax_rows(_dot_nt(q_ref[:, sl], k_ref[:, sl]) * MEM_SCALE)
        o_ref[:, sl] = _dot(p, v_ref[:, sl]).astype(o_ref.dtype)


def _mem_attn(proj, qblk, kv, *, tm):
    M = proj.shape[0]
    return pl.pallas_call(
        _mem_attn_kernel, grid=(M // tm,),
        in_specs=[pl.BlockSpec((tm, MEM_WIDTH), lambda i: (i, qblk)),
                  pl.BlockSpec((MEM_TOKENS, MEM_WIDTH), lambda i: (0, 0)),
                  pl.BlockSpec((MEM_TOKENS, MEM_WIDTH), lambda i: (0, 1))],
        out_specs=pl.BlockSpec((tm, MEM_WIDTH), lambda i: (i, 0)),
        out_shape=jax.ShapeDtypeStruct((M, MEM_WIDTH), BF16),
        compiler_params=_cparams("parallel"), name="mem_attn")(proj, kv, kv)


def _mem_attn_step_kernel(q_ref, k_ref, v_ref, o_ref, *, bs):
    rows = MEM_TOKENS * MEM_HEADS
    col_head = lax.broadcasted_iota(jnp.int32, (SUBLANES, rows), 1) % MEM_HEADS
    row_head = lax.broadcasted_iota(jnp.int32, (SUBLANES, rows), 0) % MEM_HEADS
    own = col_head == row_head

    def group(t, carry):
        bs_ = [t * STEP_INTERLEAVE + u for u in range(STEP_INTERLEAVE)]
        s = [_dot_nt(q_ref[b], k_ref[b]) * MEM_SCALE for b in bs_]
        p = [_softmax_rows(jnp.where(own, t_, -jnp.inf)) for t_ in s]
        for b, p_ in zip(bs_, p):
            o_ref[b] = _dot(p_, v_ref[b])
        return carry

    lax.fori_loop(0, bs // STEP_INTERLEAVE, group, 0)


def _mem_attn_step(q, mk, mv, layer, *, bs):
    B = q.shape[0]
    qblk = pl.BlockSpec((bs, SUBLANES, MEM_HEAD), lambda i: (i, 0, 0))
    cblk = pl.BlockSpec((None, bs, MEM_TOKENS * MEM_HEADS, MEM_HEAD), lambda i: (layer, i, 0, 0))
    return pl.pallas_call(
        functools.partial(_mem_attn_step_kernel, bs=bs), grid=(B // bs,),
        in_specs=[qblk, cblk, cblk], out_specs=qblk,
        out_shape=jax.ShapeDtypeStruct(q.shape, F32),
        compiler_params=_cparams("parallel"), name="mem_attn_step")(q, mk, mv)


def _deepnorm_ln(res, h, g, beta):
    z = ALPHA * res + h
    mu = jnp.mean(z, axis=-1, keepdims=True)
    d = z - mu
    var = jnp.mean(d * d, axis=-1, keepdims=True)
    return d * lax.rsqrt(var + LN_EPS) * g + beta


def _out_ln_kernel(*refs, widths):
    n = len(widths)
    lhs, res_ref, lhs_s, res_s_ref = refs[:n], refs[n], refs[n + 1:2 * n + 1], refs[2 * n + 1]
    w_ref, g_ref, beta_ref, of_ref, ob_ref, ofs_ref, obs_ref = refs[2 * n + 2:]

    def rows_out(pieces, res, rows):
        h, off = None, 0
        for a_ref, kw in zip(pieces, widths):
            part = jnp.dot(a_ref[rows, :], w_ref[off:off + kw, :], preferred_element_type=F32)
            h = part if h is None else h + part
            off += kw
        return _deepnorm_ln(res[rows, :], h, g_ref[...], beta_ref[...])

    tm = res_ref.shape[0]
    half = tm // 2 if tm % (2 * SUBLANES) == 0 else tm
    for rows in (slice(r0, r0 + half) for r0 in range(0, tm, half)):
        out = rows_out(lhs, res_ref, rows)
        of_ref[rows, :] = out
        ob_ref[rows, :] = out.astype(BF16)

    @pl.when(pl.program_id(0) == 0)
    def _():
        out = rows_out(lhs_s, res_s_ref, slice(None))
        ofs_ref[...] = out
        obs_ref[...] = out.astype(BF16)


def _out_ln(pieces, res, pieces_s, res_s, w, layer, g, beta, *, tm):
    M = pieces[0].shape[0]
    Ms = res_s.shape[0]
    K, N = w.shape[1], w.shape[2]
    widths = tuple(p.shape[1] for p in pieces)
    assert sum(widths) == K and M % tm == 0 and widths == tuple(p.shape[1] for p in pieces_s)
    row = lambda i: (i, 0)
    cst = lambda i: (0, 0)
    in_specs = ([pl.BlockSpec((tm, kw), row) for kw in widths] + [pl.BlockSpec((tm, N), row)]
                + [pl.BlockSpec((Ms, kw), cst) for kw in widths] + [pl.BlockSpec((Ms, N), cst)]
                + [pl.BlockSpec((None, K, N), lambda i: (layer, 0, 0)),
                   pl.BlockSpec((1, N), cst), pl.BlockSpec((1, N), cst)])
    oblk, sblk = pl.BlockSpec((tm, N), row), pl.BlockSpec((Ms, N), cst)
    of, ob, ofs, obs = pl.pallas_call(
        functools.partial(_out_ln_kernel, widths=widths), grid=(M // tm,),
        in_specs=in_specs, out_specs=[oblk, oblk, sblk, sblk],
        out_shape=[jax.ShapeDtypeStruct((M, N), F32), jax.ShapeDtypeStruct((M, N), BF16),
                   jax.ShapeDtypeStruct((Ms, N), F32), jax.ShapeDtypeStruct((Ms, N), BF16)],
        compiler_params=_cparams("arbitrary"), name="out_ln")(*pieces, res, *pieces_s, res_s, w, g, beta)
    return (of, ob), (ofs, obs)


def _ffn_down_ln_kernel(a_ref, w_ref, res_ref, g_ref, beta_ref, o_ref):
    k = pl.program_id(1)
    last = pl.num_programs(1) - 1
    part = jnp.dot(a_ref[...], w_ref[...], preferred_element_type=F32)

    @pl.when(k == 0)
    def _():
        o_ref[...] = part

    @pl.when((k > 0) & (k < last))
    def _():
        o_ref[...] += part

    @pl.when(k == last)
    def _():
        o_ref[...] = _deepnorm_ln(res_ref[...], o_ref[...] + part, g_ref[...], beta_ref[...])


def _ffn_down_ln(a, w, layer, res, g, beta, *, tm, tk):
    M, K = a.shape
    N = w.shape[2]
    assert w.shape[1] == K and K % tk == 0 and K // tk >= 2 and M % tm == 0
    row = lambda i, k: (i, 0)
    cst = lambda i, k: (0, 0)
    return pl.pallas_call(
        _ffn_down_ln_kernel, grid=(M // tm, K // tk),
        in_specs=[pl.BlockSpec((tm, tk), lambda i, k: (i, k)),
                  pl.BlockSpec((None, tk, N), lambda i, k: (layer, k, 0)),
                  pl.BlockSpec((tm, N), row), pl.BlockSpec((1, N), cst), pl.BlockSpec((1, N), cst)],
        out_specs=pl.BlockSpec((tm, N), row), out_shape=jax.ShapeDtypeStruct((M, N), F32),
        compiler_params=_cparams("parallel", "arbitrary"), name="ffn_down_ln")(a, w, res, g, beta)


def _ffn_up_kernel(x_ref, xs_ref, wg_ref, wu_ref, o_ref, os_ref):
    wg, wu = wg_ref[...].astype(BF16), wu_ref[...].astype(BF16)

    def swiglu(x):
        gate = jnp.dot(x, wg, preferred_element_type=F32)
        up = jnp.dot(x, wu, preferred_element_type=F32)
        return (gate * jax.nn.sigmoid(gate) * up).astype(BF16)

    o_ref[...] = swiglu(x_ref[...])

    @pl.when(pl.program_id(0) == 0)
    def _():
        os_ref[...] = swiglu(xs_ref[...])


def _ffn_up(x, xs, wg, wu, layer, *, tm, tn):
    M, K = x.shape
    N = wg.shape[2]
    nj = N // tn
    wblk = pl.BlockSpec((None, K, tn), lambda i, j: (layer, 0, j))
    return pl.pallas_call(
        _ffn_up_kernel, grid=(M // tm, nj),
        in_specs=[pl.BlockSpec((tm, K), lambda i, j: (i, 0)), pl.BlockSpec(xs.shape, lambda i, j: (0, 0)),
                  wblk, wblk],
        out_specs=[pl.BlockSpec((tm, tn), lambda i, j: (i, j)),
                   pl.BlockSpec((xs.shape[0], tn), lambda i, j: (0, _rider_block(i, j, nj)))],
        out_shape=[jax.ShapeDtypeStruct((M, N), BF16), jax.ShapeDtypeStruct((xs.shape[0], N), BF16)],
        compiler_params=_cparams("arbitrary", "arbitrary"), name="ffn_up")(x, xs, wg, wu)


def _rope_kernel(x_ref, cos_ref, sin_ref, o_ref):
    x = x_ref[...]
    lane = lax.broadcasted_iota(jnp.int32, x.shape, 1)
    first_half = (lane % SWA_HEAD) < (SWA_HEAD // 2)
    partner = jnp.where(first_half, pltpu.roll(x, LANES - SWA_HEAD // 2, 1),
                        pltpu.roll(x, SWA_HEAD // 2, 1))
    o_ref[...] = x * cos_ref[...] + partner * sin_ref[...]


def _rope(proj, cos, sin_signed, *, tm):
    M = proj.shape[0]
    width = TOK_WIDTH + SWA_KV_WIDTH
    blk = pl.BlockSpec((tm, LANES), lambda i, s: (i, s))
    tab = pl.BlockSpec((tm, LANES), lambda i, s: (i, 0))
    return pl.pallas_call(
        _rope_kernel, grid=(M // tm, width // LANES),
        in_specs=[blk, tab, tab], out_specs=blk,
        out_shape=jax.ShapeDtypeStruct((M, width), F32),
        compiler_params=_cparams("parallel", "arbitrary"), name="rope")(proj, cos, sin_signed)


def _sink_column(sink_ref, base, rows_per_head, nheads):
    rows = rows_per_head * nheads
    hid = lax.broadcasted_iota(jnp.int32, (rows, 1), 0) // rows_per_head
    col = jnp.zeros((rows, 1), F32)
    for j in range(nheads):
        col = jnp.where(hid == j, sink_ref[base + j], col)
    return col


def _sink_softmax(s, sink):
    m = jnp.maximum(jnp.max(s, axis=-1, keepdims=True), sink)
    p = jnp.exp(s - m)
    return p * (1.0 / (jnp.sum(p, axis=-1, keepdims=True) + jnp.exp(sink - m)))


def _swa_kernel(sink_ref, q_ref, kc_ref, kp_ref, vc_ref, vp_ref, cosc_ref, sinc_ref, cosp_ref, sinp_ref,
                o_ref, krot_ref):
    n = pl.program_id(0)
    nslab_q = TOK_WIDTH // LANES
    slab = lambda ref, s: ref[:, s * LANES:(s + 1) * LANES]
    cos_c, sin_c, cos_p, sin_p = cosc_ref[...], sinc_ref[...], cosp_ref[...], sinp_ref[...]
    lane = lax.broadcasted_iota(jnp.int32, (BLOCK, LANES), 1)
    first_half = (lane % SWA_HEAD) < (SWA_HEAD // 2)
    lo = lane < SWA_HEAD

    def rope(x, cos, sin):
        partner = jnp.where(first_half, pltpu.roll(x, LANES - SWA_HEAD // 2, 1),
                            pltpu.roll(x, SWA_HEAD // 2, 1))
        return x * cos + partner * sin

    kv_slabs = SWA_KV_WIDTH // LANES
    k_cur = [rope(slab(kc_ref, j), cos_c, sin_c) for j in range(kv_slabs)]
    k_prev = [rope(slab(kp_ref, j), cos_p, sin_p) for j in range(kv_slabs)]
    for j in range(kv_slabs):
        krot_ref[:, j * LANES:(j + 1) * LANES] = k_cur[j]
    lane2 = lax.broadcasted_iota(jnp.int32, (2 * BLOCK, LANES), 1)
    kd, vd = [], []
    for g in range(SWA_KV_HEADS):
        j, half = divmod(g, 2)
        keep = (lane2 < SWA_HEAD) if half == 0 else (lane2 >= SWA_HEAD)
        dup = lambda t: jnp.where(keep, t, pltpu.roll(t, SWA_HEAD, 1)).astype(BF16)
        kd.append(dup(jnp.concatenate([k_prev[j], k_cur[j]], axis=0)))
        vd.append(dup(jnp.concatenate([slab(vp_ref, j), slab(vc_ref, j)], axis=0)))
    qi = lax.broadcasted_iota(jnp.int32, (2 * BLOCK, 2 * BLOCK), 0) % BLOCK
    si = lax.broadcasted_iota(jnp.int32, (2 * BLOCK, 2 * BLOCK), 1)
    valid = (si > qi) & (si <= qi + WINDOW) & ((n > 0) | (si >= BLOCK))
    row_lo = lax.broadcasted_iota(jnp.int32, (2 * BLOCK, 1), 0) < BLOCK
    slabs = list(range(nslab_q))
    kv_of = [(2 * s) // SWA_GROUP for s in slabs]
    assert math.frexp(SWA_SCALE)[0] == 0.5
    q = [rope(slab(q_ref, s), cos_c, sin_c) * SWA_SCALE for s in slabs]
    qs = [jnp.concatenate([jnp.where(lo, t, 0.0), jnp.where(lo, 0.0, t)], axis=0).astype(BF16) for t in q]
    sc = [lax.dot_general(t, kd[g], NT_DIMS, preferred_element_type=F32) for t, g in zip(qs, kv_of)]
    sc = [jnp.where(valid, t, -jnp.inf) for t in sc]
    p = [_sink_softmax(t, jnp.where(row_lo, sink_ref[2 * s], sink_ref[2 * s + 1])) for t, s in zip(sc, slabs)]
    o = [jnp.dot(t.astype(BF16), vd[g], preferred_element_type=F32) for t, g in zip(p, kv_of)]
    for s, t in zip(slabs, o):
        o_ref[:, s * LANES:(s + 1) * LANES] = jnp.where(lo, t[:BLOCK], t[BLOCK:]).astype(o_ref.dtype)


def _swa_prompt(proj, cos, sin_signed, sinks):
    T = proj.shape[0]
    kblk, vblk = TOK_WIDTH // SWA_KV_WIDTH, TOK_WIDTH // SWA_KV_WIDTH + 1
    prev = lambda n: jnp.maximum(n - 1, 0)
    kv_spec = lambda blk, row: pl.BlockSpec((BLOCK, SWA_KV_WIDTH), lambda n: (row(n), blk))
    tab = lambda row: pl.BlockSpec((BLOCK, LANES), lambda n: (row(n), 0))
    cur = lambda n: n
    return pl.pallas_call(
        _swa_kernel, grid=(T // BLOCK,),
        in_specs=[pl.BlockSpec(memory_space=pltpu.SMEM),
                  pl.BlockSpec((BLOCK, TOK_WIDTH), lambda n: (n, 0)),
                  kv_spec(kblk, cur), kv_spec(kblk, prev), kv_spec(vblk, cur), kv_spec(vblk, prev),
                  tab(cur), tab(cur), tab(prev), tab(prev)],
        out_specs=[pl.BlockSpec((BLOCK, TOK_WIDTH), lambda n: (n, 0)),
                   pl.BlockSpec((BLOCK, SWA_KV_WIDTH), lambda n: (n, 0))],
        out_shape=[jax.ShapeDtypeStruct((T, TOK_WIDTH), BF16),
                   jax.ShapeDtypeStruct((T, SWA_KV_WIDTH), F32)],
        compiler_params=_cparams("arbitrary"), name="swa_banded")(
            sinks, proj, proj, proj, proj, proj, cos, sin_signed, cos, sin_signed)


def _swa_step_kernel(sink_ref, q_ref, kn_ref, vn_ref, kt_ref, vt_ref, o_ref, kto_ref, vto_ref, *, bs):
    sink = _sink_column(sink_ref, 0, 1, SWA_Q_HEADS)
    newest = lax.broadcasted_iota(jnp.int32, (SWA_HEAD, WINDOW), 1) == WINDOW - 1
    kv_of_head = lax.broadcasted_iota(jnp.int32, (SWA_Q_HEADS, 1), 0) // SWA_GROUP
    kv_heads = list(range(SWA_KV_HEADS))

    def pick(per_kv):
        out = per_kv[0]
        for g in kv_heads[1:]:
            out = jnp.where(kv_of_head == g, per_kv[g], out)
        return out

    def slid(new_ref, cache_ref, b):
        new = new_ref[b]
        return [jnp.where(newest, new[:, g:g + 1], pltpu.roll(cache_ref[b, g], WINDOW - 1, 1)) for g in kv_heads]

    def group(t, carry):
        bs_ = [t * STEP_INTERLEAVE + u for u in range(STEP_INTERLEAVE)]
        kt = [slid(kn_ref, kt_ref, b) for b in bs_]
        vt = [slid(vn_ref, vt_ref, b) for b in bs_]
        for b, kt_, vt_ in zip(bs_, kt, vt):
            for g in kv_heads:
                kto_ref[b, g] = kt_[g]
                vto_ref[b, g] = vt_[g]
        s = [pick([_dot(q_ref[b], t_) for t_ in kt_]) * SWA_SCALE for b, kt_ in zip(bs_, kt)]
        p = [_sink_softmax(s_, sink) for s_ in s]
        for b, p_, vt_ in zip(bs_, p, vt):
            o_ref[b] = pick([_dot_nt(p_, t_) for t_ in vt_])
        return carry

    lax.fori_loop(0, bs // STEP_INTERLEAVE, group, 0)


def _swa_step(q, k_new, v_new, kt, vt, layer, sinks, *, bs):
    B = q.shape[0]
    qblk = pl.BlockSpec((bs, SWA_Q_HEADS, SWA_HEAD), lambda i: (i, 0, 0))
    nblk = pl.BlockSpec((bs, SWA_HEAD, SWA_KV_HEADS), lambda i: (i, 0, 0))
    cshape = (bs, SWA_KV_HEADS, SWA_HEAD, WINDOW)
    cin = pl.BlockSpec((None,) + cshape, lambda i: (layer, i, 0, 0, 0))
    cout = pl.BlockSpec(cshape, lambda i: (i, 0, 0, 0))
    cache = jax.ShapeDtypeStruct(kt.shape[1:], F32)
    return pl.pallas_call(
        functools.partial(_swa_step_kernel, bs=bs), grid=(B // bs,),
        in_specs=[pl.BlockSpec(memory_space=pltpu.SMEM), qblk, nblk, nblk, cin, cin],
        out_specs=[qblk, cout, cout],
        out_shape=[jax.ShapeDtypeStruct(q.shape, F32), cache, cache],
        compiler_params=_cparams("parallel"), name="swa_step")(sinks, q, k_new, v_new, kt, vt)


ROW_TILE = 512
FFN_ROW_TILE = 1024
COL_TILE = 512
STEP_BATCH = 16
FFN_DOWN_K_TILE = FFN_HIDDEN // 2


def _row_tile(m):
    return ROW_TILE if m % ROW_TILE == 0 else m


def _ffn_row_tile(m):
    return FFN_ROW_TILE if m % FFN_ROW_TILE == 0 else _row_tile(m)


def _pad_rows(w, rows):
    return jnp.pad(w, ((0, rows - w.shape[0]), (0, 0)))


def _pad_cols(w, cols):
    return jnp.pad(w, ((0, 0), (0, cols - w.shape[1])))


def _rwkv_weights(w_in, mu, w1, w2, a1, a2, g1, g2):
    w_lora = jnp.concatenate([_pad_cols(w1[0], LORA_PAD), _pad_cols(a1[0], LORA_PAD), g1[0]], axis=1)
    return dict(w_in=w_in.astype(BF16), w_lora=w_lora.astype(BF16), mu=mu[0][:, None, :],
                w2=_pad_rows(w2[0], LORA_PAD).astype(BF16), a2=_pad_rows(a2[0], LORA_PAD).astype(BF16),
                g2=g2[0].astype(BF16))


def _unblock_state(s_bd):
    n = RWKV_HEAD
    return jnp.stack([s_bd[:, :n, :n], s_bd[:, n:, n:]], axis=1).reshape(RWKV_HEADS, n, n)


def _rope_tables(pos):
    half = SWA_HEAD // 2
    inv = ROPE_THETA ** (-jnp.arange(half, dtype=F32) / half)
    ang = pos.astype(F32)[:, None] * inv[None, :]
    cos, sin = jnp.cos(ang), jnp.sin(ang)
    reps = LANES // SWA_HEAD
    return jnp.tile(cos, (1, 2 * reps)), jnp.tile(jnp.concatenate([-sin, sin], axis=1), (1, reps))


def _post_mixer(groups, sw, layer):
    row = lambda t: t[layer][None, :]
    (tok_p, mo_p, x_p), (tok_s, mo_s, x_s) = groups
    (x1f_p, x1b_p), (x1f_s, x1b_s) = _out_ln([tok_p, mo_p], x_p, [tok_s, mo_s], x_s, sw["w_out"], layer,
                                             row(sw["ln1_g"]), row(sw["ln1_b"]), tm=_row_tile(x_p.shape[0]))
    hff = _ffn_up(x1b_p, x1b_s, sw["w_gate"], sw["w_up"], layer, tm=_ffn_row_tile(x1b_p.shape[0]), tn=COL_TILE)
    return [_ffn_down_ln(h, sw["w_down"], layer, xf, row(sw["ln2_g"]), row(sw["ln2_b"]),
                         tm=_row_tile(xf.shape[0]), tk=FFN_DOWN_K_TILE) for h, xf in zip(hff, (x1f_p, x1f_s))]


def kernel(x_prompt, x_sample, mem_prompt, cache_mem_k, cache_mem_v, state_rwkv_shift, state_rwkv_wkv, cache_swa_k, cache_swa_v, w_in_rwkv, rwkv_mu, rwkv_w0, rwkv_w1, rwkv_w2, rwkv_a0, rwkv_a1, rwkv_a2, rwkv_g1, rwkv_g2, rwkv_k_k, rwkv_k_a, rwkv_r_k, rwkv_gn_g, rwkv_gn_b, w_in_swa, swa_sinks, w_mem_kv, w_out, ln1_g, ln1_b, w_gate, w_up, w_down, ln2_g, ln2_b):
    assert DEPTH == 2 and x_prompt.shape[0] == 1 and x_sample.shape[1] == 1
    T = x_prompt.shape[1]
    B = x_sample.shape[0]
    past_len = T
    row = lambda t: t[None, :]
    shared = dict(w_out=w_out.astype(BF16), ln1_g=ln1_g, ln1_b=ln1_b, w_gate=w_gate, w_up=w_up,
                  w_down=w_down.astype(BF16), ln2_g=ln2_g, ln2_b=ln2_b)
    RW = _rwkv_weights(w_in_rwkv, rwkv_mu, rwkv_w1, rwkv_w2, rwkv_a1, rwkv_a2, rwkv_g1, rwkv_g2)
    rk, gn_g, gn_b = row(rwkv_r_k[0].reshape(-1)), row(rwkv_gn_g[0]), row(rwkv_gn_b[0])
    prep_consts = (row(rwkv_w0[0]), row(rwkv_a0[0]), row(rwkv_k_k[0]), row(rwkv_k_a[0]))
    w_swa = w_in_swa.astype(BF16)
    sinks = swa_sinks[0]
    q_blk_rwkv = 3 * TOK_WIDTH // MEM_WIDTH
    q_blk_swa = (TOK_WIDTH + 2 * SWA_KV_WIDTH) // MEM_WIDTH

    xp, xs = x_prompt[0], x_sample[:, 0]
    tm, tms = _row_tile(T), _row_tile(B)
    kv = [_proj(mem_prompt[0], w_mem_kv, i, tm=MEM_TOKENS, tn=COL_TILE, name="mem_kv") for i in range(DEPTH)]
    prompt_mem_k = jnp.stack([t[:, :MEM_WIDTH] for t in kv]).reshape(DEPTH, 1, MEM_TOKENS, MEM_HEADS, MEM_HEAD)
    prompt_mem_v = jnp.stack([t[:, MEM_WIDTH:] for t in kv]).reshape(DEPTH, 1, MEM_TOKENS, MEM_HEADS, MEM_HEAD)
    mem_k = cache_mem_k.reshape(DEPTH, B, MEM_TOKENS * MEM_HEADS, MEM_HEAD)
    mem_v = cache_mem_v.reshape(DEPTH, B, MEM_TOKENS * MEM_HEADS, MEM_HEAD)

    def mem_step(q, layer):
        q_rows = jnp.pad(q.reshape(B, MEM_HEADS, MEM_HEAD), ((0, 0), (0, SUBLANES - MEM_HEADS), (0, 0)))
        out = _mem_attn_step(q_rows, mem_k, mem_v, layer, bs=STEP_BATCH)
        return out[:, :MEM_HEADS].reshape(B, MEM_WIDTH).astype(BF16)

    proj = _rwkv_in(xp, None, RW["mu"], RW["w_in"], RW["w_lora"], 0, tm=_ffn_row_tile(T))
    tok_p, s_bd = _wkv_prompt(proj, RW["w2"], RW["a2"], RW["g2"], prep_consts + (rk, gn_g, gn_b))
    mo_p = _mem_attn(proj, q_blk_rwkv, kv[0], tm=tm)
    prompt_shift = xp[-1][None, None, :]
    prompt_wkv = _unblock_state(s_bd)[None, None]
    proj = _rwkv_in(xs, state_rwkv_shift[0], RW["mu"], RW["w_in"], RW["w_lora"], 0, tm=tms)
    vecs = _rwkv_prep_t(proj, RW["w2"], RW["a2"], RW["g2"], *prep_consts)
    lanes_b = lambda t: jnp.broadcast_to(t.reshape(TOK_WIDTH, 1), (TOK_WIDTH, B))
    s_new, tok_t = _wkv_step(state_rwkv_wkv.transpose(0, 2, 3, 4, 1), 0, vecs,
                             (lanes_b(rk), lanes_b(gn_g), lanes_b(gn_b)))
    mo_s = mem_step(proj[:, 3 * TOK_WIDTH:3 * TOK_WIDTH + MEM_WIDTH], 0)
    sample_shift = xs[None]
    sample_wkv = s_new.transpose(3, 0, 1, 2)[None]
    xf_p, xf_s = _post_mixer([(tok_p, mo_p, xp), (tok_t.T.astype(BF16), mo_s, xs)], shared, 0)

    proj, proj_s = _proj(xf_p, w_swa, 0, tm=_ffn_row_tile(T), tn=COL_TILE, name="swa_proj", xs=xf_s)
    cos, sin = _rope_tables(jnp.arange(T))
    tok_p, k_rot = _swa_prompt(proj, cos, sin, sinks)
    mo_p = _mem_attn(proj, q_blk_swa, kv[1], tm=tm)
    v_last = proj[T - WINDOW:, TOK_WIDTH + SWA_KV_WIDTH:TOK_WIDTH + 2 * SWA_KV_WIDTH]
    prompt_swa_k = k_rot[T - WINDOW:].reshape(1, 1, WINDOW, SWA_KV_HEADS, SWA_HEAD)
    prompt_swa_v = v_last.reshape(1, 1, WINDOW, SWA_KV_HEADS, SWA_HEAD)
    proj = proj_s
    cos, sin = _rope_tables(jnp.full((B,), past_len))
    qk = _rope(proj, cos, sin, tm=tms)
    chan_major = lambda t: t.reshape(B, SWA_KV_HEADS, SWA_HEAD).transpose(0, 2, 1)
    k_new = chan_major(qk[:, TOK_WIDTH:])
    v_new = chan_major(proj[:, TOK_WIDTH + SWA_KV_WIDTH:TOK_WIDTH + 2 * SWA_KV_WIDTH])
    o, kc, vc = _swa_step(qk[:, :TOK_WIDTH].reshape(B, SWA_Q_HEADS, SWA_HEAD), k_new, v_new,
                          cache_swa_k.transpose(0, 1, 3, 4, 2), cache_swa_v.transpose(0, 1, 3, 4, 2),
                          0, sinks, bs=STEP_BATCH)
    mo_s = mem_step(proj[:, TOK_WIDTH + 2 * SWA_KV_WIDTH:], 1)
    sample_swa_k, sample_swa_v = kc.transpose(0, 3, 1, 2)[None], vc.transpose(0, 3, 1, 2)[None]
    y_prompt, y_sample = _post_mixer([(tok_p, mo_p, xf_p), (o.reshape(B, TOK_WIDTH).astype(BF16), mo_s, xf_s)],
                                     shared, 1)

    return (y_prompt[None], y_sample[:, None, :], prompt_mem_k, prompt_mem_v, prompt_shift, prompt_wkv,
            prompt_swa_k, prompt_swa_v, sample_shift, sample_wkv, sample_swa_k, sample_swa_v)
```

```python
import functools
import math

import jax
import jax.numpy as jnp
from jax import lax
from jax.experimental import pallas as pl
from jax.experimental.pallas import tpu as pltpu

D_MODEL = 2048
DEPTH = 2
MEM_WIDTH = D_MODEL // 4
TOK_WIDTH = D_MODEL - MEM_WIDTH
RWKV_HEAD = 64
RWKV_HEADS = TOK_WIDTH // RWKV_HEAD
GN_EPS = RWKV_HEAD * 1e-5
SWA_HEAD = 64
SWA_Q_HEADS = TOK_WIDTH // SWA_HEAD
SWA_KV_HEADS = 4
SWA_GROUP = SWA_Q_HEADS // SWA_KV_HEADS
SWA_KV_WIDTH = SWA_KV_HEADS * SWA_HEAD
WINDOW = 128
BLOCK = 128
SWA_SCALE = SWA_HEAD ** -0.5
ROPE_THETA = 10000.0
MEM_TOKENS = 256
MEM_HEADS = 4
MEM_HEAD = MEM_WIDTH // MEM_HEADS
MEM_SCALE = MEM_HEAD ** -0.5
FFN_HIDDEN = int(math.ceil(8 * D_MODEL / 3 / 256)) * 256
ALPHA = (2 * DEPTH) ** 0.25
LN_EPS = 1e-5
LORA_PAD = 128
LORA_IN_WIDTH = 512

LANES = 128
SUBLANES = 8
VMEM_LIMIT_BYTES = 56 * 1024 * 1024

BF16 = jnp.bfloat16
F32 = jnp.float32
NT_DIMS = (((1,), (1,)), ((), ()))
TN_DIMS = (((0,), (0,)), ((), ()))


def _dot(a, b):
    return jnp.dot(a.astype(BF16), b.astype(BF16), preferred_element_type=F32)


def _dot_nt(a, b):
    return lax.dot_general(a.astype(BF16), b.astype(BF16), NT_DIMS, preferred_element_type=F32)


def _dot_tn(a, b):
    return lax.dot_general(a.astype(BF16), b.astype(BF16), TN_DIMS, preferred_element_type=F32)


def _split_dot(x, m):
    hi = x.astype(BF16)
    lo = (x - hi.astype(F32)).astype(BF16)
    return (jnp.dot(hi, m, preferred_element_type=F32)
            + jnp.dot(lo, m, preferred_element_type=F32))


def _head_ones():
    p = lax.broadcasted_iota(jnp.int32, (LANES, LANES), 0)
    q = lax.broadcasted_iota(jnp.int32, (LANES, LANES), 1)
    return ((p // RWKV_HEAD) == (q // RWKV_HEAD)).astype(BF16)


def _cparams(*sem):
    return pltpu.CompilerParams(dimension_semantics=sem, vmem_limit_bytes=VMEM_LIMIT_BYTES)


def _rider_block(i, j, nj):
    return jnp.where(i == 0, j, nj - 1)


def _proj_kernel(*refs, rider):
    if rider:
        x_ref, xs_ref, w_ref, o_ref, os_ref, xb_ref = refs
    else:
        x_ref, w_ref, o_ref, xb_ref = refs

    @pl.when(pl.program_id(1) == 0)
    def _():
        xb_ref[...] = x_ref[...].astype(BF16)

    w = w_ref[...].astype(BF16)
    o_ref[...] = jnp.dot(xb_ref[...], w, preferred_element_type=F32)
    if rider:
        @pl.when(pl.program_id(0) == 0)
        def _():
            os_ref[...] = jnp.dot(xs_ref[...].astype(BF16), w, preferred_element_type=F32)


def _proj(x, w, layer, *, tm, tn, name, xs=None):
    M, K = x.shape
    N = w.shape[2]
    assert M % tm == 0 and N % tn == 0
    nj = N // tn
    rider = xs is not None
    in_specs = [pl.BlockSpec((tm, K), lambda i, j: (i, 0))]
    out_specs = [pl.BlockSpec((tm, tn), lambda i, j: (i, j))]
    out_shape = [jax.ShapeDtypeStruct((M, N), F32)]
    if rider:
        in_specs.append(pl.BlockSpec(xs.shape, lambda i, j: (0, 0)))
        out_specs.append(pl.BlockSpec((xs.shape[0], tn), lambda i, j: (0, _rider_block(i, j, nj))))
        out_shape.append(jax.ShapeDtypeStruct((xs.shape[0], N), F32))
    in_specs.append(pl.BlockSpec((None, K, tn), lambda i, j: (layer, 0, j)))
    out = pl.pallas_call(
        functools.partial(_proj_kernel, rider=rider), grid=(M // tm, nj),
        in_specs=in_specs, out_specs=out_specs, out_shape=out_shape,
        scratch_shapes=[pltpu.VMEM((tm, K), BF16)],
        compiler_params=_cparams("arbitrary", "arbitrary"), name=name)(*([x, xs, w] if rider else [x, w]))
    return out if rider else out[0]


MIX_R, MIX_W, MIX_K, MIX_V, MIX_A, MIX_G = range(6)
RWKV_IN_WIDTH = 3 * TOK_WIDTH + MEM_WIDTH + LORA_IN_WIDTH
RWKV_IN_TILE = LORA_IN_WIDTH
RWKV_IN_KEPT = (MIX_R, MIX_K, MIX_V)
RWKV_IN_SLOT = ([0] * (TOK_WIDTH // RWKV_IN_TILE) + [1] * (TOK_WIDTH // RWKV_IN_TILE)
                + [2] * (TOK_WIDTH // RWKV_IN_TILE) + [3] * (MEM_WIDTH // RWKV_IN_TILE))
LORA_PARTS = ((MIX_W, 0, LORA_PAD), (MIX_A, LORA_PAD, 2 * LORA_PAD), (MIX_G, 2 * LORA_PAD, LORA_IN_WIDTH))
RWKV_IN_SUB = 256


def _rwkv_in_kernel(slot_ref, x_ref, xp_ref, mu_ref, w_ref, wl_ref, o_ref, lhs_ref, *, shift):
    i, j = pl.program_id(0), pl.program_id(1)
    tm = x_ref.shape[0]
    sub = min(RWKV_IN_SUB, tm)
    blocks = [slice(s0, s0 + sub) for s0 in range(0, tm, sub)]

    def x_and_delta(rows):
        x = x_ref[rows, :]
        if not shift:
            return x, xp_ref[rows, :] - x
        if rows.start == 0:
            first = jnp.where(i > 0, xp_ref[SUBLANES - 1:SUBLANES, :], 0.0)
        else:
            first = x_ref[rows.start - 1:rows.start, :]
        rowid = lax.broadcasted_iota(jnp.int32, (sub, 1), 0)
        return x, jnp.where(rowid == 0, first, pltpu.roll(x, 1, 0)) - x

    @pl.when(j == 0)
    def _():
        for rows in blocks:
            x, d = x_and_delta(rows)
            for slot, m in enumerate(RWKV_IN_KEPT):
                lhs_ref[slot, rows, :] = (x + d * mu_ref[m]).astype(BF16)
            lhs_ref[len(RWKV_IN_KEPT), rows, :] = x.astype(BF16)

    nproj = pl.num_programs(1) - 1

    @pl.when(j < nproj)
    def _():
        o_ref[...] = jnp.dot(lhs_ref[slot_ref[j]], w_ref[...], preferred_element_type=F32)

    @pl.when(j == nproj)
    def _():
        for rows in blocks:
            x, d = x_and_delta(rows)
            for m, lo, hi in LORA_PARTS:
                o_ref[rows, lo:hi] = jnp.dot((x + d * mu_ref[m]).astype(BF16), wl_ref[:, lo:hi],
                                             preferred_element_type=F32)


def _rwkv_in(x, xprev, mu, w, w_lora, layer, *, tm):
    M, K = x.shape
    tn = RWKV_IN_TILE
    nproj = w.shape[2] // tn
    assert M % tm == 0 and w.shape[2] + w_lora.shape[1] == RWKV_IN_WIDTH and nproj == len(RWKV_IN_SLOT)
    assert tm % min(RWKV_IN_SUB, tm) == 0
    shift = xprev is None
    if shift:
        rows8 = tm // SUBLANES
        xp_spec = pl.BlockSpec((SUBLANES, K), lambda i, j, m: (jnp.maximum(i * rows8 - 1, 0), 0))
        xprev = x
    else:
        xp_spec = pl.BlockSpec((tm, K), lambda i, j, m: (i, 0))
    slot = jnp.asarray(RWKV_IN_SLOT + [0], jnp.int32)
    grid_spec = pltpu.PrefetchScalarGridSpec(
        num_scalar_prefetch=1, grid=(M // tm, nproj + 1),
        in_specs=[pl.BlockSpec((tm, K), lambda i, j, m: (i, 0)), xp_spec,
                  pl.BlockSpec(mu.shape, lambda i, j, m: (0, 0, 0)),
                  pl.BlockSpec((None, K, tn), lambda i, j, m: (layer, 0, jnp.minimum(j, nproj - 1))),
                  pl.BlockSpec(w_lora.shape, lambda i, j, m: (0, 0))],
        out_specs=pl.BlockSpec((tm, tn), lambda i, j, m: (i, j)),
        scratch_shapes=[pltpu.VMEM((len(RWKV_IN_KEPT) + 1, tm, K), BF16)])
    return pl.pallas_call(
        functools.partial(_rwkv_in_kernel, shift=shift), grid_spec=grid_spec,
        out_shape=jax.ShapeDtypeStruct((M, RWKV_IN_WIDTH), F32),
        compiler_params=_cparams("parallel", "arbitrary"), name="rwkv_in")(slot, x, xprev, mu, w, w_lora)


def _softplus(z):
    return jnp.maximum(z, 0.0) + jnp.log1p(jnp.exp(-jnp.abs(z)))


def _each(f, *lists):
    return [f(*a) for a in zip(*lists)]


def _lora_hidden(hl):
    return jnp.tanh(hl[:, 0:LORA_PAD]), hl[:, LORA_PAD:2 * LORA_PAD], jax.nn.sigmoid(hl[:, 2 * LORA_PAD:])


def _prep_slabs(k, wl, al, w0, a0, k_k, k_a, ones):
    lw = _each(lambda wl_, w0_: -jnp.exp(-_softplus(-(w0_ + wl_)) - 0.5), wl, w0)
    agate = _each(lambda al_, a0_: jax.nn.sigmoid(a0_ + al_), al, a0)
    kkr = _each(lambda k_, c_: k_ * c_, k, k_k)
    ss = _each(lambda t: _split_dot(t * t, ones), kkr)
    kn = _each(lambda t, s_: t / jnp.maximum(jnp.sqrt(s_), 1e-12), kkr, ss)
    bv = _each(lambda n_, a_: n_ * a_, kn, agate)
    kp = _each(lambda k_, a_, c_: k_ * (1.0 + (a_ - 1.0) * c_), k, agate, k_a)
    return lw, kp, kn, bv


def _rwkv_prep_t_kernel(hl_ref, r_ref, k_ref, v_ref, w2_ref, a2_ref, g2_ref, w0_ref, a0_ref, kk_ref, ka_ref,
                        rt_ref, vt_ref, lw_ref, kp_ref, kn_ref, bv_ref, g_ref):
    hw, ha, hg = _lora_hidden(hl_ref[...])
    outs = _prep_slabs([k_ref[...]], [_dot(hw, w2_ref[...])], [_dot(ha, a2_ref[...])], [w0_ref[...]],
                       [a0_ref[...]], [kk_ref[...]], [ka_ref[...]], _head_ones())
    vals = [r_ref[...], v_ref[...]] + [t[0] for t in outs] + [_dot(hg, g2_ref[...])]
    for ref, val in zip((rt_ref, vt_ref, lw_ref, kp_ref, kn_ref, bv_ref, g_ref), vals):
        ref[...] = val.T


def _rwkv_prep_t(proj, w2p, a2p, g2, w0, a0, k_k, k_a):
    B = proj.shape[0]
    assert B == LANES
    nslab = TOK_WIDTH // LANES
    lora_blk = (RWKV_IN_WIDTH - LORA_IN_WIDTH) // LORA_IN_WIDTH
    col = lambda s: (0, s)
    slab = lambda base: pl.BlockSpec((B, LANES), lambda s, base=base: (0, base + s))
    outs = [jax.ShapeDtypeStruct((TOK_WIDTH, B), F32)] * 7
    return pl.pallas_call(
        _rwkv_prep_t_kernel, grid=(nslab,),
        in_specs=[pl.BlockSpec((B, LORA_IN_WIDTH), lambda s: (0, lora_blk)),
                  slab(0), slab(nslab), slab(2 * nslab),
                  pl.BlockSpec((LORA_PAD, LANES), col),
                  pl.BlockSpec((LORA_PAD, LANES), col),
                  pl.BlockSpec((2 * LORA_PAD, LANES), col),
                  pl.BlockSpec((1, LANES), col), pl.BlockSpec((1, LANES), col),
                  pl.BlockSpec((1, LANES), col), pl.BlockSpec((1, LANES), col)],
        out_specs=[pl.BlockSpec((LANES, B), lambda s: (s, 0))] * 7,
        out_shape=outs, compiler_params=_cparams("arbitrary"),
        name="rwkv_prep_t")(proj, proj, proj, proj, w2p, a2p, g2, w0, a0, k_k, k_a)


def _gn_gate(y, r, kp, v, g, rk, gg, gb, ones):
    inv_n = 1.0 / RWKV_HEAD
    rows = y[0].shape[0]
    sums = _each(lambda y_, r_, k_, rk_: _split_dot(jnp.concatenate([y_, r_ * k_ * rk_], axis=0), ones),
                 y, r, kp, rk)
    d = _each(lambda y_, s_: y_ - s_[:rows] * inv_n, y, sums)
    var = _each(lambda d_: _split_dot(d_ * d_, ones) * inv_n, d)
    return _each(lambda d_, var_, gg_, gb_, s_, v_, g_:
                 (d_ * lax.rsqrt(var_ + GN_EPS) * gg_ + gb_ + s_[rows:] * v_) * g_,
                 d, var, gg, gb, sums, v, g)


WKV_CHUNK = 64


def _wkv_masks():
    n = 2 * WKV_CHUNK
    p = lax.broadcasted_iota(jnp.int32, (n, n), 0)
    q = lax.broadcasted_iota(jnp.int32, (n, n), 1)
    same = lambda b: (p // b) == (q // b)
    pt, qt = p % WKV_CHUNK, q % WKV_CHUNK
    s8, s16, s32, s64 = same(8), same(16), same(32), same(WKV_CHUNK)
    return dict(strict=s64 & (pt > qt), incl=s64 & (pt >= qt), s8=s8,
                e16=s16 & ~s8, e32=s32 & ~s16, e64=s64 & ~s32,
                eye=(p == q).astype(F32))


WKV_PAIRS = TOK_WIDTH // LANES
WKV_STEP_CHUNKS = 2


def _wkv_lanes(r, lw, k, v, kn, bv, mk, tri, lane_lo):
    stack = lambda x: jnp.concatenate([jnp.where(lane_lo, x, 0.0), jnp.where(lane_lo, 0.0, x)], axis=0)
    n = 2 * WKV_CHUNK
    c = _each(lambda t: _split_dot_left(tri, t), lw)
    c_last = _each(lambda t: t[WKV_CHUNK - 1:WKV_CHUNK, :], c)
    e_out = _each(lambda t: jnp.exp(-t), c)
    e_end = _each(lambda t, tl: jnp.exp(tl - t), c, c_last)
    ah = _each(lambda kn_, c_, lw_: stack(-kn_ * jnp.exp(c_ - lw_)), kn, c, lw)
    rh = _each(lambda r_, c_: stack(r_ * jnp.exp(c_)), r, c)
    bh = _each(lambda b_, e_: stack(b_ * e_), bv, e_out)
    kh = _each(lambda k_, e_: stack(k_ * e_), k, e_out)
    bbar = _each(lambda b_, e_: stack(b_ * e_), bv, e_end)
    kbar = _each(lambda k_, e_: stack(k_ * e_), k, e_end)
    vs = _each(stack, v)
    gm = _each(lambda a_, r_, b_, k_: _dot_nt(jnp.concatenate([a_, r_], axis=0),
                                              jnp.concatenate([b_, k_], axis=0)), ah, rh, bh, kh)
    a_ab = _each(lambda g_: jnp.where(mk["strict"], g_[:n, :n], 0.0), gm)
    a_ak = _each(lambda g_: jnp.where(mk["strict"], g_[:n, n:], 0.0), gm)
    l_rb = _each(lambda g_: jnp.where(mk["incl"], g_[n:, :n], 0.0), gm)
    l_rk = _each(lambda g_: jnp.where(mk["incl"], g_[n:, n:], 0.0), gm)
    d1 = _each(lambda a_: jnp.where(mk["s8"], a_, 0.0), a_ab)
    x = _each(lambda d_: mk["eye"] + d_, d1)
    d2 = _each(lambda d_: _dot(d_, d_), d1)
    x = _each(lambda x_, d_: x_ + _dot(x_, d_), x, d2)
    d4 = _each(lambda d_: _dot(d_, d_), d2)
    x = _each(lambda x_, d_: x_ + _dot(x_, d_), x, d4)
    for lvl in ("e16", "e32", "e64"):
        ex = _each(lambda a_, x_: _dot(jnp.where(mk[lvl], a_, 0.0), x_), a_ab, x)
        x = _each(lambda x_, e_: x_ + _dot(x_, e_), x, ex)
    av = _each(_dot, a_ak, vs)
    tw = _each(lambda x_, a_, v_: _dot(x_, jnp.concatenate([a_, v_], axis=1)), x, ah, av)
    lwm = _each(_dot, l_rb, tw)
    lv = _each(_dot, l_rk, vs)
    qm = _each(lambda r_, l_: r_ + l_[:, :n], rh, lwm)
    y0 = _each(lambda l_, v_: l_[:, n:] + v_, lwm, lv)
    mt = _each(lambda t_, b_: _dot_tn(t_[:, :n], b_), tw, bbar)
    nt = _each(lambda t_, b_, v_, k_: _dot_tn(t_[:, n:], b_) + _dot_tn(v_, k_), tw, bbar, vs, kbar)
    return qm, y0, mt, nt, _each(jnp.exp, c_last)


def _split_dot_left(m, x):
    hi = x.astype(BF16)
    lo = (x - hi.astype(F32)).astype(BF16)
    return (jnp.dot(m, hi, preferred_element_type=F32)
            + jnp.dot(m, lo, preferred_element_type=F32))


def _wkv_kernel(r_ref, k_ref, v_ref, hl_ref, w2_ref, a2_ref, g2_ref, w0_ref, a0_ref, kk_ref, ka_ref,
                rk_ref, gg_ref, gb_ref, o_ref, sout_ref, s_scr):
    c_idx = pl.program_id(0)

    @pl.when(c_idx == 0)
    def _():
        s_scr[...] = jnp.zeros_like(s_scr)

    mk = _wkv_masks()
    ti = lax.broadcasted_iota(jnp.int32, (WKV_CHUNK, WKV_CHUNK), 0)
    tj = lax.broadcasted_iota(jnp.int32, (WKV_CHUNK, WKV_CHUNK), 1)
    tri = (ti >= tj).astype(BF16)
    lane_lo = lax.broadcasted_iota(jnp.int32, (WKV_CHUNK, LANES), 1) < RWKV_HEAD
    ones = _head_ones()
    pairs = range(WKV_PAIRS)
    lanes = [(slice(ch * WKV_CHUNK, (ch + 1) * WKV_CHUNK), slice(p * LANES, (p + 1) * LANES))
             for ch in range(WKV_STEP_CHUNKS) for p in pairs]
    cut = lambda t: [t[rows, sl] for rows, sl in lanes]
    rep = lambda t: [t[:, sl] for _, sl in lanes]
    hw, ha, hg = _lora_hidden(hl_ref[...])
    wl, al, g = _dot(hw, w2_ref[...]), _dot(ha, a2_ref[...]), _dot(hg, g2_ref[...])
    r, v = cut(r_ref), cut(v_ref)
    lw, kp, kn, bv = _prep_slabs(cut(k_ref), cut(wl), cut(al), rep(w0_ref), rep(a0_ref),
                                 rep(kk_ref), rep(ka_ref), ones)
    qm, y0, mt, nt, dec = _wkv_lanes(r, lw, kp, v, kn, bv, mk, tri, lane_lo)
    S = [s_scr[p] for p in pairs]
    y = []
    for ch in range(WKV_STEP_CHUNKS):
        part = slice(ch * WKV_PAIRS, (ch + 1) * WKV_PAIRS)
        ys = _each(lambda q_, s_, y_: _dot_nt(q_, s_) + y_, qm[part], S, y0[part])
        y += _each(lambda t: t[:WKV_CHUNK, :] + t[WKV_CHUNK:, :], ys)
        S = _each(lambda s_, d_, m_, n_: s_ * d_ + _dot(s_, m_) + n_, S, dec[part], mt[part], nt[part])
    for p in pairs:
        s_scr[p] = S[p]
    tok = _gn_gate(y, r, kp, v, cut(g), rep(rk_ref), rep(gg_ref), rep(gb_ref), ones)
    for (rows, sl), t in zip(lanes, tok):
        o_ref[rows, sl] = t.astype(o_ref.dtype)

    @pl.when(c_idx == pl.num_programs(0) - 1)
    def _():
        sout_ref[...] = s_scr[...]


def _wkv_prompt(proj, w2p, a2p, g2, consts):
    T = proj.shape[0]
    rows = WKV_STEP_CHUNKS * WKV_CHUNK
    assert T % rows == 0
    lora_blk = (RWKV_IN_WIDTH - LORA_IN_WIDTH) // LORA_IN_WIDTH
    tok = lambda blk: pl.BlockSpec((rows, TOK_WIDTH), lambda c, blk=blk: (c, blk))
    full = lambda a: pl.BlockSpec(a.shape, lambda c: (0,) * a.ndim)
    weights = (w2p, a2p, g2) + tuple(consts)
    return pl.pallas_call(
        _wkv_kernel, grid=(T // rows,),
        in_specs=[tok(0), tok(1), tok(2), pl.BlockSpec((rows, LORA_IN_WIDTH), lambda c: (c, lora_blk))]
        + [full(a) for a in weights],
        out_specs=[tok(0), pl.BlockSpec((WKV_PAIRS, LANES, LANES), lambda c: (0, 0, 0))],
        out_shape=[jax.ShapeDtypeStruct((T, TOK_WIDTH), BF16),
                   jax.ShapeDtypeStruct((WKV_PAIRS, LANES, LANES), F32)],
        scratch_shapes=[pltpu.VMEM((WKV_PAIRS, LANES, LANES), F32)],
        compiler_params=_cparams("arbitrary"), name="wkv_chunked")(proj, proj, proj, proj, *weights)


WKV_STEP_UNROLL = 4


def _wkv_step_kernel(s_ref, r_ref, v_ref, lw_ref, kp_ref, kn_ref, bv_ref, g_ref, rk_ref, gg_ref, gb_ref,
                     so_ref, tok_ref, y_scr):
    n = RWKV_HEAD
    inv_n = 1.0 / n
    for hh in range(2):
        rows = slice(hh * n, (hh + 1) * n)
        a, w = -kn_ref[rows, :], jnp.exp(lw_ref[rows, :])
        b, k, r = bv_ref[rows, :], kp_ref[rows, :], r_ref[rows, :]

        def value_row(i, carry, hh=hh, a=a, w=w, b=b, k=k, r=r):
            s = s_ref[hh, i]
            sa = jnp.sum(s * a, axis=0, keepdims=True)
            s_new = s * w + sa * b + v_ref[pl.ds(hh * n + i, 1), :] * k
            so_ref[hh, i] = s_new
            y_scr[pl.ds(hh * n + i, 1), :] = jnp.sum(s_new * r, axis=0, keepdims=True)
            return carry

        lax.fori_loop(0, n, value_row, 0, unroll=WKV_STEP_UNROLL)
    for hh in range(2):
        rows = slice(hh * n, (hh + 1) * n)
        y = y_scr[rows, :]
        d = y - jnp.sum(y, axis=0, keepdims=True) * inv_n
        var = jnp.sum(d * d, axis=0, keepdims=True) * inv_n
        bonus = jnp.sum(r_ref[rows, :] * kp_ref[rows, :] * rk_ref[rows, :], axis=0, keepdims=True)
        yn = d * lax.rsqrt(var + GN_EPS) * gg_ref[rows, :] + gb_ref[rows, :]
        tok_ref[rows, :] = (yn + bonus * v_ref[rows, :]) * g_ref[rows, :]


def _wkv_step(state, layer, vecs, consts):
    B = state.shape[-1]
    sshape = (2, RWKV_HEAD, RWKV_HEAD, B)
    vblk = pl.BlockSpec((LANES, B), lambda p: (p, 0))
    return pl.pallas_call(
        _wkv_step_kernel, grid=(WKV_PAIRS,),
        in_specs=[pl.BlockSpec((None,) + sshape, lambda p: (layer, p, 0, 0, 0))] + [vblk] * 10,
        out_specs=[pl.BlockSpec(sshape, lambda p: (p, 0, 0, 0)), vblk],
        out_shape=[jax.ShapeDtypeStruct(state.shape[1:], F32), jax.ShapeDtypeStruct((TOK_WIDTH, B), F32)],
        scratch_shapes=[pltpu.VMEM((LANES, B), F32)],
        compiler_params=_cparams("parallel"), name="wkv_step")(state, *vecs, *consts)


STEP_INTERLEAVE = 4


def _softmax_rows(s):
    m = jnp.max(s, axis=-1, keepdims=True)
    e = jnp.exp(s - m)
    return e * (1.0 / jnp.sum(e, axis=-1, keepdims=True))


def _mem_attn_kernel(q_ref, k_ref, v_ref, o_ref):
    heads = [slice(h * MEM_HEAD, (h + 1) * MEM_HEAD) for h in range(MEM_HEADS)]
    s = [_dot_nt(q_ref[:, sl], k_ref[:, sl]) * MEM_SCALE for sl in heads]
    p = [_softmax_rows(t) for t in s]
    for sl, t in zip(heads, p):
        o_ref[:, sl] = _dot(t, v_ref[:, sl]).astype(o_ref.dtype)


def _mem_attn(proj, qblk, kv, *, tm):
    M = proj.shape[0]
    return pl.pallas_call(
        _mem_attn_kernel, grid=(M // tm,),
        in_specs=[pl.BlockSpec((tm, MEM_WIDTH), lambda i: (i, qblk)),
                  pl.BlockSpec((MEM_TOKENS, MEM_WIDTH), lambda i: (0, 0)),
                  pl.BlockSpec((MEM_TOKENS, MEM_WIDTH), lambda i: (0, 1))],
        out_specs=pl.BlockSpec((tm, MEM_WIDTH), lambda i: (i, 0)),
        out_shape=jax.ShapeDtypeStruct((M, MEM_WIDTH), BF16),
        compiler_params=_cparams("parallel"), name="mem_attn")(proj, kv, kv)


def _mem_attn_step_kernel(q_ref, k_ref, v_ref, o_ref, *, bs):
    rows = MEM_TOKENS * MEM_HEADS
    col_head = lax.broadcasted_iota(jnp.int32, (SUBLANES, rows), 1) % MEM_HEADS
    row_head = lax.broadcasted_iota(jnp.int32, (SUBLANES, rows), 0) % MEM_HEADS
    own = col_head == row_head

    def group(t, carry):
        bs_ = [t * STEP_INTERLEAVE + u for u in range(STEP_INTERLEAVE)]
        s = [_dot_nt(q_ref[b], k_ref[b]) * MEM_SCALE for b in bs_]
        p = [_softmax_rows(jnp.where(own, t_, -jnp.inf)) for t_ in s]
        for b, p_ in zip(bs_, p):
            o_ref[b] = _dot(p_, v_ref[b])
        return carry

    lax.fori_loop(0, bs // STEP_INTERLEAVE, group, 0)


def _mem_attn_step(q, mk, mv, layer, *, bs):
    B = q.shape[0]
    qblk = pl.BlockSpec((bs, SUBLANES, MEM_HEAD), lambda i: (i, 0, 0))
    cblk = pl.BlockSpec((None, bs, MEM_TOKENS * MEM_HEADS, MEM_HEAD), lambda i: (layer, i, 0, 0))
    return pl.pallas_call(
        functools.partial(_mem_attn_step_kernel, bs=bs), grid=(B // bs,),
        in_specs=[qblk, cblk, cblk], out_specs=qblk,
        out_shape=jax.ShapeDtypeStruct(q.shape, F32),
        compiler_params=_cparams("parallel"), name="mem_attn_step")(q, mk, mv)


def _deepnorm_ln(res, h, g, beta):
    z = ALPHA * res + h
    mu = jnp.mean(z, axis=-1, keepdims=True)
    d = z - mu
    var = jnp.mean(d * d, axis=-1, keepdims=True)
    return d * lax.rsqrt(var + LN_EPS) * g + beta


def _out_ln_kernel(*refs, widths):
    n = len(widths)
    lhs, res_ref, lhs_s, res_s_ref = refs[:n], refs[n], refs[n + 1:2 * n + 1], refs[2 * n + 1]
    w_ref, g_ref, beta_ref, of_ref, ob_ref, ofs_ref, obs_ref = refs[2 * n + 2:]

    def rows_out(pieces, res, rows):
        h, off = None, 0
        for a_ref, kw in zip(pieces, widths):
            part = jnp.dot(a_ref[rows, :], w_ref[off:off + kw, :], preferred_element_type=F32)
            h = part if h is None else h + part
            off += kw
        return _deepnorm_ln(res[rows, :], h, g_ref[...], beta_ref[...])

    tm = res_ref.shape[0]
    half = tm // 2 if tm % (2 * SUBLANES) == 0 else tm
    for rows in (slice(r0, r0 + half) for r0 in range(0, tm, half)):
        out = rows_out(lhs, res_ref, rows)
        of_ref[rows, :] = out
        ob_ref[rows, :] = out.astype(BF16)

    @pl.when(pl.program_id(0) == 0)
    def _():
        out = rows_out(lhs_s, res_s_ref, slice(None))
        ofs_ref[...] = out
        obs_ref[...] = out.astype(BF16)


def _out_ln(pieces, res, pieces_s, res_s, w, layer, g, beta, *, tm):
    M = pieces[0].shape[0]
    Ms = res_s.shape[0]
    K, N = w.shape[1], w.shape[2]
    widths = tuple(p.shape[1] for p in pieces)
    assert sum(widths) == K and M % tm == 0 and widths == tuple(p.shape[1] for p in pieces_s)
    row = lambda i: (i, 0)
    cst = lambda i: (0, 0)
    in_specs = ([pl.BlockSpec((tm, kw), row) for kw in widths] + [pl.BlockSpec((tm, N), row)]
                + [pl.BlockSpec((Ms, kw), cst) for kw in widths] + [pl.BlockSpec((Ms, N), cst)]
                + [pl.BlockSpec((None, K, N), lambda i: (layer, 0, 0)),
                   pl.BlockSpec((1, N), cst), pl.BlockSpec((1, N), cst)])
    oblk, sblk = pl.BlockSpec((tm, N), row), pl.BlockSpec((Ms, N), cst)
    of, ob, ofs, obs = pl.pallas_call(
        functools.partial(_out_ln_kernel, widths=widths), grid=(M // tm,),
        in_specs=in_specs, out_specs=[oblk, oblk, sblk, sblk],
        out_shape=[jax.ShapeDtypeStruct((M, N), F32), jax.ShapeDtypeStruct((M, N), BF16),
                   jax.ShapeDtypeStruct((Ms, N), F32), jax.ShapeDtypeStruct((Ms, N), BF16)],
        compiler_params=_cparams("arbitrary"), name="out_ln")(*pieces, res, *pieces_s, res_s, w, g, beta)
    return (of, ob), (ofs, obs)


def _ffn_down_ln_kernel(a_ref, w_ref, res_ref, g_ref, beta_ref, o_ref):
    k = pl.program_id(1)
    last = pl.num_programs(1) - 1
    part = jnp.dot(a_ref[...], w_ref[...], preferred_element_type=F32)

    @pl.when(k == 0)
    def _():
        o_ref[...] = part

    @pl.when((k > 0) & (k < last))
    def _():
        o_ref[...] += part

    @pl.when(k == last)
    def _():
        o_ref[...] = _deepnorm_ln(res_ref[...], o_ref[...] + part, g_ref[...], beta_ref[...])


def _ffn_down_ln(a, w, layer, res, g, beta, *, tm, tk):
    M, K = a.shape
    N = w.shape[2]
    assert w.shape[1] == K and K % tk == 0 and K // tk >= 2 and M % tm == 0
    row = lambda i, k: (i, 0)
    cst = lambda i, k: (0, 0)
    return pl.pallas_call(
        _ffn_down_ln_kernel, grid=(M // tm, K // tk),
        in_specs=[pl.BlockSpec((tm, tk), lambda i, k: (i, k)),
                  pl.BlockSpec((None, tk, N), lambda i, k: (layer, k, 0)),
                  pl.BlockSpec((tm, N), row), pl.BlockSpec((1, N), cst), pl.BlockSpec((1, N), cst)],
        out_specs=pl.BlockSpec((tm, N), row), out_shape=jax.ShapeDtypeStruct((M, N), F32),
        compiler_params=_cparams("parallel", "arbitrary"), name="ffn_down_ln")(a, w, res, g, beta)


def _ffn_up_kernel(x_ref, xs_ref, wg_ref, wu_ref, wd_ref, o_ref, os_ref, wdb_ref):
    wg, wu = wg_ref[...].astype(BF16), wu_ref[...].astype(BF16)

    def swiglu(x):
        gate = jnp.dot(x, wg, preferred_element_type=F32)
        up = jnp.dot(x, wu, preferred_element_type=F32)
        return (gate * jax.nn.sigmoid(gate) * up).astype(BF16)

    o_ref[...] = swiglu(x_ref[...])

    @pl.when(pl.program_id(0) == 0)
    def _():
        os_ref[...] = swiglu(xs_ref[...])
        wdb_ref[...] = wd_ref[...].astype(BF16)


def _ffn_up(x, xs, wg, wu, wd, layer, *, tm, tn):
    M, K = x.shape
    N = wg.shape[2]
    nj = N // tn
    assert wd.shape[1] == N
    wblk = pl.BlockSpec((None, K, tn), lambda i, j: (layer, 0, j))
    rider = lambda i, j: _rider_block(i, j, nj)
    return pl.pallas_call(
        _ffn_up_kernel, grid=(M // tm, nj),
        in_specs=[pl.BlockSpec((tm, K), lambda i, j: (i, 0)), pl.BlockSpec(xs.shape, lambda i, j: (0, 0)),
                  wblk, wblk, pl.BlockSpec((None, tn, wd.shape[2]), lambda i, j: (layer, rider(i, j), 0))],
        out_specs=[pl.BlockSpec((tm, tn), lambda i, j: (i, j)),
                   pl.BlockSpec((xs.shape[0], tn), lambda i, j: (0, rider(i, j))),
                   pl.BlockSpec((tn, wd.shape[2]), lambda i, j: (rider(i, j), 0))],
        out_shape=[jax.ShapeDtypeStruct((M, N), BF16), jax.ShapeDtypeStruct((xs.shape[0], N), BF16),
                   jax.ShapeDtypeStruct(wd.shape[1:], BF16)],
        compiler_params=_cparams("arbitrary", "arbitrary"), name="ffn_up")(x, xs, wg, wu, wd)


def _rope_kernel(x_ref, cos_ref, sin_ref, o_ref):
    x = x_ref[...]
    lane = lax.broadcasted_iota(jnp.int32, x.shape, 1)
    first_half = (lane % SWA_HEAD) < (SWA_HEAD // 2)
    partner = jnp.where(first_half, pltpu.roll(x, LANES - SWA_HEAD // 2, 1),
                        pltpu.roll(x, SWA_HEAD // 2, 1))
    o_ref[...] = x * cos_ref[...] + partner * sin_ref[...]


def _rope(proj, cos, sin_signed, *, tm):
    M = proj.shape[0]
    width = TOK_WIDTH + SWA_KV_WIDTH
    blk = pl.BlockSpec((tm, LANES), lambda i, s: (i, s))
    tab = pl.BlockSpec((tm, LANES), lambda i, s: (i, 0))
    return pl.pallas_call(
        _rope_kernel, grid=(M // tm, width // LANES),
        in_specs=[blk, tab, tab], out_specs=blk,
        out_shape=jax.ShapeDtypeStruct((M, width), F32),
        compiler_params=_cparams("parallel", "arbitrary"), name="rope")(proj, cos, sin_signed)


def _sink_column(sink_ref, base, rows_per_head, nheads):
    rows = rows_per_head * nheads
    hid = lax.broadcasted_iota(jnp.int32, (rows, 1), 0) // rows_per_head
    col = jnp.zeros((rows, 1), F32)
    for j in range(nheads):
        col = jnp.where(hid == j, sink_ref[base + j], col)
    return col


def _sink_softmax(s, sink):
    m = jnp.maximum(jnp.max(s, axis=-1, keepdims=True), sink)
    p = jnp.exp(s - m)
    return p * (1.0 / (jnp.sum(p, axis=-1, keepdims=True) + jnp.exp(sink - m)))


def _swa_kernel(sink_ref, q_ref, kc_ref, kp_ref, vc_ref, vp_ref, cosc_ref, sinc_ref, cosp_ref, sinp_ref,
                o_ref, krot_ref):
    n = pl.program_id(0)
    nslab_q = TOK_WIDTH // LANES
    slab = lambda ref, s: ref[:, s * LANES:(s + 1) * LANES]
    cos_c, sin_c, cos_p, sin_p = cosc_ref[...], sinc_ref[...], cosp_ref[...], sinp_ref[...]
    lane = lax.broadcasted_iota(jnp.int32, (BLOCK, LANES), 1)
    first_half = (lane % SWA_HEAD) < (SWA_HEAD // 2)
    lo = lane < SWA_HEAD

    def rope(x, cos, sin):
        partner = jnp.where(first_half, pltpu.roll(x, LANES - SWA_HEAD // 2, 1),
                            pltpu.roll(x, SWA_HEAD // 2, 1))
        return x * cos + partner * sin

    kv_slabs = SWA_KV_WIDTH // LANES
    k_cur = [rope(slab(kc_ref, j), cos_c, sin_c) for j in range(kv_slabs)]
    k_prev = [rope(slab(kp_ref, j), cos_p, sin_p) for j in range(kv_slabs)]
    for j in range(kv_slabs):
        krot_ref[:, j * LANES:(j + 1) * LANES] = k_cur[j]
    lane2 = lax.broadcasted_iota(jnp.int32, (2 * BLOCK, LANES), 1)
    kd, vd = [], []
    for g in range(SWA_KV_HEADS):
        j, half = divmod(g, 2)
        keep = (lane2 < SWA_HEAD) if half == 0 else (lane2 >= SWA_HEAD)
        dup = lambda t: jnp.where(keep, t, pltpu.roll(t, SWA_HEAD, 1)).astype(BF16)
        kd.append(dup(jnp.concatenate([k_prev[j], k_cur[j]], axis=0)))
        vd.append(dup(jnp.concatenate([slab(vp_ref, j), slab(vc_ref, j)], axis=0)))
    qi = lax.broadcasted_iota(jnp.int32, (2 * BLOCK, 2 * BLOCK), 0) % BLOCK
    si = lax.broadcasted_iota(jnp.int32, (2 * BLOCK, 2 * BLOCK), 1)
    valid = (si > qi) & (si <= qi + WINDOW) & ((n > 0) | (si >= BLOCK))
    row_lo = lax.broadcasted_iota(jnp.int32, (2 * BLOCK, 1), 0) < BLOCK
    slabs = list(range(nslab_q))
    kv_of = [(2 * s) // SWA_GROUP for s in slabs]
    assert math.frexp(SWA_SCALE)[0] == 0.5
    q = [rope(slab(q_ref, s), cos_c, sin_c) * SWA_SCALE for s in slabs]
    qs = [jnp.concatenate([jnp.where(lo, t, 0.0), jnp.where(lo, 0.0, t)], axis=0).astype(BF16) for t in q]
    sc = [lax.dot_general(t, kd[g], NT_DIMS, preferred_element_type=F32) for t, g in zip(qs, kv_of)]
    sc = [jnp.where(valid, t, -jnp.inf) for t in sc]
    p = [_sink_softmax(t, jnp.where(row_lo, sink_ref[2 * s], sink_ref[2 * s + 1])) for t, s in zip(sc, slabs)]
    o = [jnp.dot(t.astype(BF16), vd[g], preferred_element_type=F32) for t, g in zip(p, kv_of)]
    for s, t in zip(slabs, o):
        o_ref[:, s * LANES:(s + 1) * LANES] = jnp.where(lo, t[:BLOCK], t[BLOCK:]).astype(o_ref.dtype)


def _swa_prompt(proj, cos, sin_signed, sinks):
    T = proj.shape[0]
    kblk, vblk = TOK_WIDTH // SWA_KV_WIDTH, TOK_WIDTH // SWA_KV_WIDTH + 1
    prev = lambda n: jnp.maximum(n - 1, 0)
    kv_spec = lambda blk, row: pl.BlockSpec((BLOCK, SWA_KV_WIDTH), lambda n: (row(n), blk))
    tab = lambda row: pl.BlockSpec((BLOCK, LANES), lambda n: (row(n), 0))
    cur = lambda n: n
    return pl.pallas_call(
        _swa_kernel, grid=(T // BLOCK,),
        in_specs=[pl.BlockSpec(memory_space=pltpu.SMEM),
                  pl.BlockSpec((BLOCK, TOK_WIDTH), lambda n: (n, 0)),
                  kv_spec(kblk, cur), kv_spec(kblk, prev), kv_spec(vblk, cur), kv_spec(vblk, prev),
                  tab(cur), tab(cur), tab(prev), tab(prev)],
        out_specs=[pl.BlockSpec((BLOCK, TOK_WIDTH), lambda n: (n, 0)),
                   pl.BlockSpec((BLOCK, SWA_KV_WIDTH), lambda n: (n, 0))],
        out_shape=[jax.ShapeDtypeStruct((T, TOK_WIDTH), BF16),
                   jax.ShapeDtypeStruct((T, SWA_KV_WIDTH), F32)],
        compiler_params=_cparams("arbitrary"), name="swa_banded")(
            sinks, proj, proj, proj, proj, proj, cos, sin_signed, cos, sin_signed)


def _swa_step_kernel(sink_ref, q_ref, kn_ref, vn_ref, kt_ref, vt_ref, o_ref, kto_ref, vto_ref, *, bs):
    sink = _sink_column(sink_ref, 0, 1, SWA_Q_HEADS)
    newest = lax.broadcasted_iota(jnp.int32, (SWA_HEAD, WINDOW), 1) == WINDOW - 1
    kv_of_head = lax.broadcasted_iota(jnp.int32, (SWA_Q_HEADS, 1), 0) // SWA_GROUP
    kv_heads = list(range(SWA_KV_HEADS))

    def pick(per_kv):
        out = per_kv[0]
        for g in kv_heads[1:]:
            out = jnp.where(kv_of_head == g, per_kv[g], out)
        return out

    def slid(new_ref, cache_ref, b):
        new = new_ref[b]
        return [jnp.where(newest, new[:, g:g + 1], pltpu.roll(cache_ref[b, g], WINDOW - 1, 1)) for g in kv_heads]

    def group(t, carry):
        bs_ = [t * STEP_INTERLEAVE + u for u in range(STEP_INTERLEAVE)]
        kt = [slid(kn_ref, kt_ref, b) for b in bs_]
        vt = [slid(vn_ref, vt_ref, b) for b in bs_]
        for b, kt_, vt_ in zip(bs_, kt, vt):
            for g in kv_heads:
                kto_ref[b, g] = kt_[g]
                vto_ref[b, g] = vt_[g]
        s = [pick([_dot(q_ref[b], t_) for t_ in kt_]) * SWA_SCALE for b, kt_ in zip(bs_, kt)]
        p = [_sink_softmax(s_, sink) for s_ in s]
        for b, p_, vt_ in zip(bs_, p, vt):
            o_ref[b] = pick([_dot_nt(p_, t_) for t_ in vt_])
        return carry

    lax.fori_loop(0, bs // STEP_INTERLEAVE, group, 0)


def _swa_step(q, k_new, v_new, kt, vt, layer, sinks, *, bs):
    B = q.shape[0]
    qblk = pl.BlockSpec((bs, SWA_Q_HEADS, SWA_HEAD), lambda i: (i, 0, 0))
    nblk = pl.BlockSpec((bs, SWA_HEAD, SWA_KV_HEADS), lambda i: (i, 0, 0))
    cshape = (bs, SWA_KV_HEADS, SWA_HEAD, WINDOW)
    cin = pl.BlockSpec((None,) + cshape, lambda i: (layer, i, 0, 0, 0))
    cout = pl.BlockSpec(cshape, lambda i: (i, 0, 0, 0))
    cache = jax.ShapeDtypeStruct(kt.shape[1:], F32)
    return pl.pallas_call(
        functools.partial(_swa_step_kernel, bs=bs), grid=(B // bs,),
        in_specs=[pl.BlockSpec(memory_space=pltpu.SMEM), qblk, nblk, nblk, cin, cin],
        out_specs=[qblk, cout, cout],
        out_shape=[jax.ShapeDtypeStruct(q.shape, F32), cache, cache],
        compiler_params=_cparams("parallel"), name="swa_step")(sinks, q, k_new, v_new, kt, vt)


ROW_TILE = 512
FFN_ROW_TILE = 1024
COL_TILE = 512
STEP_BATCH = 16
FFN_DOWN_K_TILE = FFN_HIDDEN // 2


def _row_tile(m):
    return ROW_TILE if m % ROW_TILE == 0 else m


def _ffn_row_tile(m):
    return FFN_ROW_TILE if m % FFN_ROW_TILE == 0 else _row_tile(m)


def _pad_rows(w, rows):
    return jnp.pad(w, ((0, rows - w.shape[0]), (0, 0)))


def _pad_cols(w, cols):
    return jnp.pad(w, ((0, 0), (0, cols - w.shape[1])))


def _rwkv_weights(w_in, mu, w1, w2, a1, a2, g1, g2):
    w_lora = jnp.concatenate([_pad_cols(w1[0], LORA_PAD), _pad_cols(a1[0], LORA_PAD), g1[0]], axis=1)
    return dict(w_in=w_in.astype(BF16), w_lora=w_lora.astype(BF16), mu=mu[0][:, None, :],
                w2=_pad_rows(w2[0], LORA_PAD).astype(BF16), a2=_pad_rows(a2[0], LORA_PAD).astype(BF16),
                g2=g2[0].astype(BF16))


def _unblock_state(s_bd):
    n = RWKV_HEAD
    return jnp.stack([s_bd[:, :n, :n], s_bd[:, n:, n:]], axis=1).reshape(RWKV_HEADS, n, n)


def _rope_tables(pos):
    half = SWA_HEAD // 2
    inv = ROPE_THETA ** (-jnp.arange(half, dtype=F32) / half)
    ang = pos.astype(F32)[:, None] * inv[None, :]
    cos, sin = jnp.cos(ang), jnp.sin(ang)
    reps = LANES // SWA_HEAD
    return jnp.tile(cos, (1, 2 * reps)), jnp.tile(jnp.concatenate([-sin, sin], axis=1), (1, reps))


def _post_mixer(groups, sw, layer):
    row = lambda t: t[layer][None, :]
    (tok_p, mo_p, x_p), (tok_s, mo_s, x_s) = groups
    (x1f_p, x1b_p), (x1f_s, x1b_s) = _out_ln([tok_p, mo_p], x_p, [tok_s, mo_s], x_s, sw["w_out"], layer,
                                             row(sw["ln1_g"]), row(sw["ln1_b"]), tm=_row_tile(x_p.shape[0]))
    hff_p, hff_s, wd = _ffn_up(x1b_p, x1b_s, sw["w_gate"], sw["w_up"], sw["w_down"], layer,
                               tm=_ffn_row_tile(x1b_p.shape[0]), tn=COL_TILE)
    return [_ffn_down_ln(h, wd[None], 0, xf, row(sw["ln2_g"]), row(sw["ln2_b"]),
                         tm=_row_tile(xf.shape[0]), tk=FFN_DOWN_K_TILE)
            for h, xf in ((hff_p, x1f_p), (hff_s, x1f_s))]


def kernel(x_prompt, x_sample, mem_prompt, cache_mem_k, cache_mem_v, state_rwkv_shift, state_rwkv_wkv, cache_swa_k, cache_swa_v, w_in_rwkv, rwkv_mu, rwkv_w0, rwkv_w1, rwkv_w2, rwkv_a0, rwkv_a1, rwkv_a2, rwkv_g1, rwkv_g2, rwkv_k_k, rwkv_k_a, rwkv_r_k, rwkv_gn_g, rwkv_gn_b, w_in_swa, swa_sinks, w_mem_kv, w_out, ln1_g, ln1_b, w_gate, w_up, w_down, ln2_g, ln2_b):
    assert DEPTH == 2 and x_prompt.shape[0] == 1 and x_sample.shape[1] == 1
    T = x_prompt.shape[1]
    B = x_sample.shape[0]
    past_len = T
    row = lambda t: t[None, :]
    shared = dict(w_out=w_out.astype(BF16), ln1_g=ln1_g, ln1_b=ln1_b, w_gate=w_gate, w_up=w_up,
                  w_down=w_down, ln2_g=ln2_g, ln2_b=ln2_b)
    RW = _rwkv_weights(w_in_rwkv, rwkv_mu, rwkv_w1, rwkv_w2, rwkv_a1, rwkv_a2, rwkv_g1, rwkv_g2)
    rk, gn_g, gn_b = row(rwkv_r_k[0].reshape(-1)), row(rwkv_gn_g[0]), row(rwkv_gn_b[0])
    prep_consts = (row(rwkv_w0[0]), row(rwkv_a0[0]), row(rwkv_k_k[0]), row(rwkv_k_a[0]))
    w_swa = w_in_swa.astype(BF16)
    sinks = swa_sinks[0]
    q_blk_rwkv = 3 * TOK_WIDTH // MEM_WIDTH
    q_blk_swa = (TOK_WIDTH + 2 * SWA_KV_WIDTH) // MEM_WIDTH

    xp, xs = x_prompt[0], x_sample[:, 0]
    tm, tms = _row_tile(T), _row_tile(B)
    kv = [_proj(mem_prompt[0], w_mem_kv, i, tm=MEM_TOKENS, tn=COL_TILE, name="mem_kv") for i in range(DEPTH)]
    prompt_mem_k = jnp.stack([t[:, :MEM_WIDTH] for t in kv]).reshape(DEPTH, 1, MEM_TOKENS, MEM_HEADS, MEM_HEAD)
    prompt_mem_v = jnp.stack([t[:, MEM_WIDTH:] for t in kv]).reshape(DEPTH, 1, MEM_TOKENS, MEM_HEADS, MEM_HEAD)
    mem_k = cache_mem_k.reshape(DEPTH, B, MEM_TOKENS * MEM_HEADS, MEM_HEAD)
    mem_v = cache_mem_v.reshape(DEPTH, B, MEM_TOKENS * MEM_HEADS, MEM_HEAD)

    def mem_step(q, layer):
        q_rows = jnp.pad(q.reshape(B, MEM_HEADS, MEM_HEAD), ((0, 0), (0, SUBLANES - MEM_HEADS), (0, 0)))
        out = _mem_attn_step(q_rows, mem_k, mem_v, layer, bs=STEP_BATCH)
        return out[:, :MEM_HEADS].reshape(B, MEM_WIDTH).astype(BF16)

    proj = _rwkv_in(xp, None, RW["mu"], RW["w_in"], RW["w_lora"], 0, tm=_ffn_row_tile(T))
    tok_p, s_bd = _wkv_prompt(proj, RW["w2"], RW["a2"], RW["g2"], prep_consts + (rk, gn_g, gn_b))
    mo_p = _mem_attn(proj, q_blk_rwkv, kv[0], tm=tm)
    prompt_shift = xp[-1][None, None, :]
    prompt_wkv = _unblock_state(s_bd)[None, None]
    proj = _rwkv_in(xs, state_rwkv_shift[0], RW["mu"], RW["w_in"], RW["w_lora"], 0, tm=tms)
    vecs = _rwkv_prep_t(proj, RW["w2"], RW["a2"], RW["g2"], *prep_consts)
    lanes_b = lambda t: jnp.broadcast_to(t.reshape(TOK_WIDTH, 1), (TOK_WIDTH, B))
    s_new, tok_t = _wkv_step(state_rwkv_wkv.transpose(0, 2, 3, 4, 1), 0, vecs,
                             (lanes_b(rk), lanes_b(gn_g), lanes_b(gn_b)))
    mo_s = mem_step(proj[:, 3 * TOK_WIDTH:3 * TOK_WIDTH + MEM_WIDTH], 0)
    sample_shift = xs[None]
    sample_wkv = s_new.transpose(3, 0, 1, 2)[None]
    xf_p, xf_s = _post_mixer([(tok_p, mo_p, xp), (tok_t.T.astype(BF16), mo_s, xs)], shared, 0)

    proj, proj_s = _proj(xf_p, w_swa, 0, tm=_ffn_row_tile(T), tn=COL_TILE, name="swa_proj", xs=xf_s)
    cos, sin = _rope_tables(jnp.arange(T))
    tok_p, k_rot = _swa_prompt(proj, cos, sin, sinks)
    mo_p = _mem_attn(proj, q_blk_swa, kv[1], tm=tm)
    v_last = proj[T - WINDOW:, TOK_WIDTH + SWA_KV_WIDTH:TOK_WIDTH + 2 * SWA_KV_WIDTH]
    prompt_swa_k = k_rot[T - WINDOW:].reshape(1, 1, WINDOW, SWA_KV_HEADS, SWA_HEAD)
    prompt_swa_v = v_last.reshape(1, 1, WINDOW, SWA_KV_HEADS, SWA_HEAD)
    proj = proj_s
    cos, sin = _rope_tables(jnp.full((B,), past_len))
    qk = _rope(proj, cos, sin, tm=tms)
    chan_major = lambda t: t.reshape(B, SWA_KV_HEADS, SWA_HEAD).transpose(0, 2, 1)
    k_new = chan_major(qk[:, TOK_WIDTH:])
    v_new = chan_major(proj[:, TOK_WIDTH + SWA_KV_WIDTH:TOK_WIDTH + 2 * SWA_KV_WIDTH])
    o, kc, vc = _swa_step(qk[:, :TOK_WIDTH].reshape(B, SWA_Q_HEADS, SWA_HEAD), k_new, v_new,
                          cache_swa_k.transpose(0, 1, 3, 4, 2), cache_swa_v.transpose(0, 1, 3, 4, 2),
                          0, sinks, bs=STEP_BATCH)
    mo_s = mem_step(proj[:, TOK_WIDTH + 2 * SWA_KV_WIDTH:], 1)
    sample_swa_k, sample_swa_v = kc.transpose(0, 3, 1, 2)[None], vc.transpose(0, 3, 1, 2)[None]
    y_prompt, y_sample = _post_mixer([(tok_p, mo_p, xf_p), (o.reshape(B, TOK_WIDTH).astype(BF16), mo_s, xf_s)],
                                     shared, 1)

    return (y_prompt[None], y_sample[:, None, :], prompt_mem_k, prompt_mem_v, prompt_shift, prompt_wkv,
            prompt_swa_k, prompt_swa_v, sample_shift, sample_wkv, sample_swa_k, sample_swa_v)
```

```python
import functools
import math

import jax
import jax.numpy as jnp
from jax import lax
from jax.experimental import pallas as pl
from jax.experimental.pallas import tpu as pltpu

D_MODEL = 2048
DEPTH = 2
MEM_WIDTH = D_MODEL // 4
TOK_WIDTH = D_MODEL - MEM_WIDTH
RWKV_HEAD = 64
RWKV_HEADS = TOK_WIDTH // RWKV_HEAD
GN_EPS = RWKV_HEAD * 1e-5
SWA_HEAD = 64
SWA_Q_HEADS = TOK_WIDTH // SWA_HEAD
SWA_KV_HEADS = 4
SWA_GROUP = SWA_Q_HEADS // SWA_KV_HEADS
SWA_KV_WIDTH = SWA_KV_HEADS * SWA_HEAD
WINDOW = 128
BLOCK = 128
PAST_LEN = 8192
SWA_SCALE = SWA_HEAD ** -0.5
ROPE_THETA = 10000.0
MEM_TOKENS = 256
MEM_HEADS = 4
MEM_HEAD = MEM_WIDTH // MEM_HEADS
MEM_SCALE = MEM_HEAD ** -0.5
FFN_HIDDEN = int(math.ceil(8 * D_MODEL / 3 / 256)) * 256
ALPHA = (2 * DEPTH) ** 0.25
LN_EPS = 1e-5
LORA_PAD = 128
LORA_IN_WIDTH = 512

LANES = 128
SUBLANES = 8
VMEM_LIMIT_BYTES = 56 * 1024 * 1024

BF16 = jnp.bfloat16
F32 = jnp.float32
NT_DIMS = (((1,), (1,)), ((), ()))
TN_DIMS = (((0,), (0,)), ((), ()))


def _dot(a, b):
    return jnp.dot(a.astype(BF16), b.astype(BF16), preferred_element_type=F32)


def _dot_nt(a, b):
    return lax.dot_general(a.astype(BF16), b.astype(BF16), NT_DIMS, preferred_element_type=F32)


def _dot_tn(a, b):
    return lax.dot_general(a.astype(BF16), b.astype(BF16), TN_DIMS, preferred_element_type=F32)


def _split_dot(x, m):
    hi = x.astype(BF16)
    lo = (x - hi.astype(F32)).astype(BF16)
    return (jnp.dot(hi, m, preferred_element_type=F32)
            + jnp.dot(lo, m, preferred_element_type=F32))


def _head_ones():
    p = lax.broadcasted_iota(jnp.int32, (LANES, LANES), 0)
    q = lax.broadcasted_iota(jnp.int32, (LANES, LANES), 1)
    return ((p // RWKV_HEAD) == (q // RWKV_HEAD)).astype(BF16)


def _cparams(*sem):
    return pltpu.CompilerParams(dimension_semantics=sem, vmem_limit_bytes=VMEM_LIMIT_BYTES)


def _rider_block(i, j, nj):
    return jnp.where(i == 0, j, nj - 1)


def _proj_kernel(*refs, rider):
    if rider:
        x_ref, xs_ref, w_ref, o_ref, os_ref, xb_ref = refs
    else:
        x_ref, w_ref, o_ref, xb_ref = refs

    @pl.when(pl.program_id(1) == 0)
    def _():
        xb_ref[...] = x_ref[...].astype(BF16)

    w = w_ref[...].astype(BF16)
    o_ref[...] = jnp.dot(xb_ref[...], w, preferred_element_type=F32)
    if rider:
        @pl.when(pl.program_id(0) == 0)
        def _():
            os_ref[...] = jnp.dot(xs_ref[...].astype(BF16), w, preferred_element_type=F32)


def _proj(x, w, layer, *, tm, tn, name, xs=None):
    M, K = x.shape
    N = w.shape[2]
    assert M % tm == 0 and N % tn == 0
    nj = N // tn
    rider = xs is not None
    in_specs = [pl.BlockSpec((tm, K), lambda i, j: (i, 0))]
    out_specs = [pl.BlockSpec((tm, tn), lambda i, j: (i, j))]
    out_shape = [jax.ShapeDtypeStruct((M, N), F32)]
    if rider:
        in_specs.append(pl.BlockSpec(xs.shape, lambda i, j: (0, 0)))
        out_specs.append(pl.BlockSpec((xs.shape[0], tn), lambda i, j: (0, _rider_block(i, j, nj))))
        out_shape.append(jax.ShapeDtypeStruct((xs.shape[0], N), F32))
    in_specs.append(pl.BlockSpec((None, K, tn), lambda i, j: (layer, 0, j)))
    out = pl.pallas_call(
        functools.partial(_proj_kernel, rider=rider), grid=(M // tm, nj),
        in_specs=in_specs, out_specs=out_specs, out_shape=out_shape,
        scratch_shapes=[pltpu.VMEM((tm, K), BF16)],
        compiler_params=_cparams("arbitrary", "arbitrary"), name=name)(*([x, xs, w] if rider else [x, w]))
    return out if rider else out[0]


MIX_R, MIX_W, MIX_K, MIX_V, MIX_A, MIX_G = range(6)
RWKV_IN_WIDTH = 3 * TOK_WIDTH + MEM_WIDTH + LORA_IN_WIDTH
RWKV_IN_TILE = LORA_IN_WIDTH
RWKV_IN_KEPT = (MIX_R, MIX_K, MIX_V)
RWKV_IN_SLOT = ([0] * (TOK_WIDTH // RWKV_IN_TILE) + [1] * (TOK_WIDTH // RWKV_IN_TILE)
                + [2] * (TOK_WIDTH // RWKV_IN_TILE) + [3] * (MEM_WIDTH // RWKV_IN_TILE))
LORA_PARTS = ((MIX_W, 0, LORA_PAD), (MIX_A, LORA_PAD, 2 * LORA_PAD), (MIX_G, 2 * LORA_PAD, LORA_IN_WIDTH))
RWKV_IN_SUB = 256


def _rwkv_in_kernel(slot_ref, x_ref, xp_ref, mu_ref, w_ref, wl_ref, o_ref, lhs_ref, *, shift):
    i, j = pl.program_id(0), pl.program_id(1)
    tm = x_ref.shape[0]
    sub = min(RWKV_IN_SUB, tm)
    blocks = [slice(s0, s0 + sub) for s0 in range(0, tm, sub)]

    def x_and_delta(rows):
        x = x_ref[rows, :]
        if not shift:
            return x, xp_ref[rows, :] - x
        if rows.start == 0:
            first = jnp.where(i > 0, xp_ref[SUBLANES - 1:SUBLANES, :], 0.0)
        else:
            first = x_ref[rows.start - 1:rows.start, :]
        rowid = lax.broadcasted_iota(jnp.int32, (sub, 1), 0)
        return x, jnp.where(rowid == 0, first, pltpu.roll(x, 1, 0)) - x

    @pl.when(j == 0)
    def _():
        for rows in blocks:
            x, d = x_and_delta(rows)
            for slot, m in enumerate(RWKV_IN_KEPT):
                lhs_ref[slot, rows, :] = (x + d * mu_ref[m]).astype(BF16)
            lhs_ref[len(RWKV_IN_KEPT), rows, :] = x.astype(BF16)

    nproj = pl.num_programs(1) - 1

    @pl.when(j < nproj)
    def _():
        o_ref[...] = jnp.dot(lhs_ref[slot_ref[j]], w_ref[...], preferred_element_type=F32)

    @pl.when(j == nproj)
    def _():
        for rows in blocks:
            x, d = x_and_delta(rows)
            for m, lo, hi in LORA_PARTS:
                o_ref[rows, lo:hi] = jnp.dot((x + d * mu_ref[m]).astype(BF16), wl_ref[:, lo:hi],
                                             preferred_element_type=F32)


def _rwkv_in(x, xprev, mu, w, w_lora, layer, *, tm):
    M, K = x.shape
    tn = RWKV_IN_TILE
    nproj = w.shape[2] // tn
    assert M % tm == 0 and w.shape[2] + w_lora.shape[1] == RWKV_IN_WIDTH and nproj == len(RWKV_IN_SLOT)
    assert tm % min(RWKV_IN_SUB, tm) == 0
    shift = xprev is None
    if shift:
        rows8 = tm // SUBLANES
        xp_spec = pl.BlockSpec((SUBLANES, K), lambda i, j, m: (jnp.maximum(i * rows8 - 1, 0), 0))
        xprev = x
    else:
        xp_spec = pl.BlockSpec((tm, K), lambda i, j, m: (i, 0))
    slot = jnp.asarray(RWKV_IN_SLOT + [0], jnp.int32)
    grid_spec = pltpu.PrefetchScalarGridSpec(
        num_scalar_prefetch=1, grid=(M // tm, nproj + 1),
        in_specs=[pl.BlockSpec((tm, K), lambda i, j, m: (i, 0)), xp_spec,
                  pl.BlockSpec(mu.shape, lambda i, j, m: (0, 0, 0)),
                  pl.BlockSpec((None, K, tn), lambda i, j, m: (layer, 0, jnp.minimum(j, nproj - 1))),
                  pl.BlockSpec(w_lora.shape, lambda i, j, m: (0, 0))],
        out_specs=pl.BlockSpec((tm, tn), lambda i, j, m: (i, j)),
        scratch_shapes=[pltpu.VMEM((len(RWKV_IN_KEPT) + 1, tm, K), BF16)])
    return pl.pallas_call(
        functools.partial(_rwkv_in_kernel, shift=shift), grid_spec=grid_spec,
        out_shape=jax.ShapeDtypeStruct((M, RWKV_IN_WIDTH), F32),
        compiler_params=_cparams("parallel", "arbitrary"), name="rwkv_in")(slot, x, xprev, mu, w, w_lora)


def _softplus(z):
    return jnp.maximum(z, 0.0) + jnp.log1p(jnp.exp(-jnp.abs(z)))


def _each(f, *lists):
    return [f(*a) for a in zip(*lists)]


def _lora_hidden(hl):
    return jnp.tanh(hl[:, 0:LORA_PAD]), hl[:, LORA_PAD:2 * LORA_PAD], jax.nn.sigmoid(hl[:, 2 * LORA_PAD:])


def _prep_slabs(k, wl, al, w0, a0, k_k, k_a, ones):
    lw = _each(lambda wl_, w0_: -jnp.exp(-_softplus(-(w0_ + wl_)) - 0.5), wl, w0)
    agate = _each(lambda al_, a0_: jax.nn.sigmoid(a0_ + al_), al, a0)
    kkr = _each(lambda k_, c_: k_ * c_, k, k_k)
    ss = _each(lambda t: _split_dot(t * t, ones), kkr)
    kn = _each(lambda t, s_: t / jnp.maximum(jnp.sqrt(s_), 1e-12), kkr, ss)
    bv = _each(lambda n_, a_: n_ * a_, kn, agate)
    kp = _each(lambda k_, a_, c_: k_ * (1.0 + (a_ - 1.0) * c_), k, agate, k_a)
    return lw, kp, kn, bv


def _rwkv_prep_t_kernel(hl_ref, r_ref, k_ref, v_ref, w2_ref, a2_ref, g2_ref, w0_ref, a0_ref, kk_ref, ka_ref,
                        rt_ref, vt_ref, lw_ref, kp_ref, kn_ref, bv_ref, g_ref):
    hw, ha, hg = _lora_hidden(hl_ref[...])
    outs = _prep_slabs([k_ref[...]], [_dot(hw, w2_ref[...])], [_dot(ha, a2_ref[...])], [w0_ref[...]],
                       [a0_ref[...]], [kk_ref[...]], [ka_ref[...]], _head_ones())
    vals = [r_ref[...], v_ref[...]] + [t[0] for t in outs] + [_dot(hg, g2_ref[...])]
    for ref, val in zip((rt_ref, vt_ref, lw_ref, kp_ref, kn_ref, bv_ref, g_ref), vals):
        ref[...] = val.T


def _rwkv_prep_t(proj, w2p, a2p, g2, w0, a0, k_k, k_a):
    B = proj.shape[0]
    assert B == LANES
    nslab = TOK_WIDTH // LANES
    lora_blk = (RWKV_IN_WIDTH - LORA_IN_WIDTH) // LORA_IN_WIDTH
    col = lambda s: (0, s)
    slab = lambda base: pl.BlockSpec((B, LANES), lambda s, base=base: (0, base + s))
    outs = [jax.ShapeDtypeStruct((TOK_WIDTH, B), F32)] * 7
    return pl.pallas_call(
        _rwkv_prep_t_kernel, grid=(nslab,),
        in_specs=[pl.BlockSpec((B, LORA_IN_WIDTH), lambda s: (0, lora_blk)),
                  slab(0), slab(nslab), slab(2 * nslab),
                  pl.BlockSpec((LORA_PAD, LANES), col),
                  pl.BlockSpec((LORA_PAD, LANES), col),
                  pl.BlockSpec((2 * LORA_PAD, LANES), col),
                  pl.BlockSpec((1, LANES), col), pl.BlockSpec((1, LANES), col),
                  pl.BlockSpec((1, LANES), col), pl.BlockSpec((1, LANES), col)],
        out_specs=[pl.BlockSpec((LANES, B), lambda s: (s, 0))] * 7,
        out_shape=outs, compiler_params=_cparams("arbitrary"),
        name="rwkv_prep_t")(proj, proj, proj, proj, w2p, a2p, g2, w0, a0, k_k, k_a)


def _gn_gate(y, r, kp, v, g, rk, gg, gb, ones):
    inv_n = 1.0 / RWKV_HEAD
    rows = y[0].shape[0]
    sums = _each(lambda y_, r_, k_, rk_: _split_dot(jnp.concatenate([y_, r_ * k_ * rk_], axis=0), ones),
                 y, r, kp, rk)
    d = _each(lambda y_, s_: y_ - s_[:rows] * inv_n, y, sums)
    var = _each(lambda d_: _split_dot(d_ * d_, ones) * inv_n, d)
    return _each(lambda d_, var_, gg_, gb_, s_, v_, g_:
                 (d_ * lax.rsqrt(var_ + GN_EPS) * gg_ + gb_ + s_[rows:] * v_) * g_,
                 d, var, gg, gb, sums, v, g)


WKV_CHUNK = 64


def _wkv_masks():
    n = 2 * WKV_CHUNK
    p = lax.broadcasted_iota(jnp.int32, (n, n), 0)
    q = lax.broadcasted_iota(jnp.int32, (n, n), 1)
    same = lambda b: (p // b) == (q // b)
    pt, qt = p % WKV_CHUNK, q % WKV_CHUNK
    s8, s16, s32, s64 = same(8), same(16), same(32), same(WKV_CHUNK)
    return dict(strict=s64 & (pt > qt), incl=s64 & (pt >= qt), s8=s8,
                e16=s16 & ~s8, e32=s32 & ~s16, e64=s64 & ~s32,
                eye=(p == q).astype(F32))


WKV_PAIRS = TOK_WIDTH // LANES
WKV_STEP_CHUNKS = 2


def _wkv_lanes(r, lw, k, v, kn, bv, mk, tri, lane_lo):
    stack = lambda x: jnp.concatenate([jnp.where(lane_lo, x, 0.0), jnp.where(lane_lo, 0.0, x)], axis=0)
    n = 2 * WKV_CHUNK
    c = _each(lambda t: _split_dot_left(tri, t), lw)
    c_last = _each(lambda t: t[WKV_CHUNK - 1:WKV_CHUNK, :], c)
    e_out = _each(lambda t: jnp.exp(-t), c)
    e_end = _each(lambda t, tl: jnp.exp(tl - t), c, c_last)
    ah = _each(lambda kn_, c_, lw_: stack(-kn_ * jnp.exp(c_ - lw_)), kn, c, lw)
    rh = _each(lambda r_, c_: stack(r_ * jnp.exp(c_)), r, c)
    bh = _each(lambda b_, e_: stack(b_ * e_), bv, e_out)
    kh = _each(lambda k_, e_: stack(k_ * e_), k, e_out)
    bbar = _each(lambda b_, e_: stack(b_ * e_), bv, e_end)
    kbar = _each(lambda k_, e_: stack(k_ * e_), k, e_end)
    vs = _each(stack, v)
    gm = _each(lambda a_, r_, b_, k_: _dot_nt(jnp.concatenate([a_, r_], axis=0),
                                              jnp.concatenate([b_, k_], axis=0)), ah, rh, bh, kh)
    a_ab = _each(lambda g_: jnp.where(mk["strict"], g_[:n, :n], 0.0), gm)
    a_ak = _each(lambda g_: jnp.where(mk["strict"], g_[:n, n:], 0.0), gm)
    l_rb = _each(lambda g_: jnp.where(mk["incl"], g_[n:, :n], 0.0), gm)
    l_rk = _each(lambda g_: jnp.where(mk["incl"], g_[n:, n:], 0.0), gm)
    d1 = _each(lambda a_: jnp.where(mk["s8"], a_, 0.0), a_ab)
    x = _each(lambda d_: mk["eye"] + d_, d1)
    d2 = _each(lambda d_: _dot(d_, d_), d1)
    x = _each(lambda x_, d_: x_ + _dot(x_, d_), x, d2)
    d4 = _each(lambda d_: _dot(d_, d_), d2)
    x = _each(lambda x_, d_: x_ + _dot(x_, d_), x, d4)
    for lvl in ("e16", "e32", "e64"):
        ex = _each(lambda a_, x_: _dot(jnp.where(mk[lvl], a_, 0.0), x_), a_ab, x)
        x = _each(lambda x_, e_: x_ + _dot(x_, e_), x, ex)
    av = _each(_dot, a_ak, vs)
    tw = _each(lambda x_, a_, v_: _dot(x_, jnp.concatenate([a_, v_], axis=1)), x, ah, av)
    lwm = _each(_dot, l_rb, tw)
    lv = _each(_dot, l_rk, vs)
    qm = _each(lambda r_, l_: r_ + l_[:, :n], rh, lwm)
    y0 = _each(lambda l_, v_: l_[:, n:] + v_, lwm, lv)
    mt = _each(lambda t_, b_: _dot_tn(t_[:, :n], b_), tw, bbar)
    nt = _each(lambda t_, b_, v_, k_: _dot_tn(t_[:, n:], b_) + _dot_tn(v_, k_), tw, bbar, vs, kbar)
    return qm, y0, mt, nt, _each(jnp.exp, c_last)


def _split_dot_left(m, x):
    hi = x.astype(BF16)
    lo = (x - hi.astype(F32)).astype(BF16)
    return (jnp.dot(m, hi, preferred_element_type=F32)
            + jnp.dot(m, lo, preferred_element_type=F32))


def _wkv_kernel(r_ref, k_ref, v_ref, hl_ref, w2_ref, a2_ref, g2_ref, w0_ref, a0_ref, kk_ref, ka_ref,
                rk_ref, gg_ref, gb_ref, o_ref, sout_ref, s_scr):
    c_idx = pl.program_id(0)

    @pl.when(c_idx == 0)
    def _():
        s_scr[...] = jnp.zeros_like(s_scr)

    mk = _wkv_masks()
    ti = lax.broadcasted_iota(jnp.int32, (WKV_CHUNK, WKV_CHUNK), 0)
    tj = lax.broadcasted_iota(jnp.int32, (WKV_CHUNK, WKV_CHUNK), 1)
    tri = (ti >= tj).astype(BF16)
    lane_lo = lax.broadcasted_iota(jnp.int32, (WKV_CHUNK, LANES), 1) < RWKV_HEAD
    ones = _head_ones()
    pairs = range(WKV_PAIRS)
    lanes = [(slice(ch * WKV_CHUNK, (ch + 1) * WKV_CHUNK), slice(p * LANES, (p + 1) * LANES))
             for ch in range(WKV_STEP_CHUNKS) for p in pairs]
    cut = lambda t: [t[rows, sl] for rows, sl in lanes]
    rep = lambda t: [t[:, sl] for _, sl in lanes]
    hw, ha, hg = _lora_hidden(hl_ref[...])
    wl, al, g = _dot(hw, w2_ref[...]), _dot(ha, a2_ref[...]), _dot(hg, g2_ref[...])
    r, v = cut(r_ref), cut(v_ref)
    lw, kp, kn, bv = _prep_slabs(cut(k_ref), cut(wl), cut(al), rep(w0_ref), rep(a0_ref),
                                 rep(kk_ref), rep(ka_ref), ones)
    qm, y0, mt, nt, dec = _wkv_lanes(r, lw, kp, v, kn, bv, mk, tri, lane_lo)
    S = [s_scr[p] for p in pairs]
    y = []
    for ch in range(WKV_STEP_CHUNKS):
        part = slice(ch * WKV_PAIRS, (ch + 1) * WKV_PAIRS)
        ys = _each(lambda q_, s_, y_: _dot_nt(q_, s_) + y_, qm[part], S, y0[part])
        y += _each(lambda t: t[:WKV_CHUNK, :] + t[WKV_CHUNK:, :], ys)
        S = _each(lambda s_, d_, m_, n_: s_ * d_ + _dot(s_, m_) + n_, S, dec[part], mt[part], nt[part])
    for p in pairs:
        s_scr[p] = S[p]
    tok = _gn_gate(y, r, kp, v, cut(g), rep(rk_ref), rep(gg_ref), rep(gb_ref), ones)
    for (rows, sl), t in zip(lanes, tok):
        o_ref[rows, sl] = t.astype(o_ref.dtype)

    @pl.when(c_idx == pl.num_programs(0) - 1)
    def _():
        sout_ref[...] = s_scr[...]


def _wkv_prompt(proj, w2p, a2p, g2, consts):
    T = proj.shape[0]
    rows = WKV_STEP_CHUNKS * WKV_CHUNK
    assert T % rows == 0
    lora_blk = (RWKV_IN_WIDTH - LORA_IN_WIDTH) // LORA_IN_WIDTH
    tok = lambda blk: pl.BlockSpec((rows, TOK_WIDTH), lambda c, blk=blk: (c, blk))
    full = lambda a: pl.BlockSpec(a.shape, lambda c: (0,) * a.ndim)
    weights = (w2p, a2p, g2) + tuple(consts)
    return pl.pallas_call(
        _wkv_kernel, grid=(T // rows,),
        in_specs=[tok(0), tok(1), tok(2), pl.BlockSpec((rows, LORA_IN_WIDTH), lambda c: (c, lora_blk))]
        + [full(a) for a in weights],
        out_specs=[tok(0), pl.BlockSpec((WKV_PAIRS, LANES, LANES), lambda c: (0, 0, 0))],
        out_shape=[jax.ShapeDtypeStruct((T, TOK_WIDTH), BF16),
                   jax.ShapeDtypeStruct((WKV_PAIRS, LANES, LANES), F32)],
        scratch_shapes=[pltpu.VMEM((WKV_PAIRS, LANES, LANES), F32)],
        compiler_params=_cparams("arbitrary"), name="wkv_chunked")(proj, proj, proj, proj, *weights)


WKV_STEP_UNROLL = 4


def _wkv_step_kernel(s_ref, r_ref, v_ref, lw_ref, kp_ref, kn_ref, bv_ref, g_ref, rk_ref, gg_ref, gb_ref,
                     so_ref, tok_ref, y_scr):
    n = RWKV_HEAD
    inv_n = 1.0 / n
    for hh in range(2):
        rows = slice(hh * n, (hh + 1) * n)
        a, w = -kn_ref[rows, :], jnp.exp(lw_ref[rows, :])
        b, k, r = bv_ref[rows, :], kp_ref[rows, :], r_ref[rows, :]

        def value_row(i, carry, hh=hh, a=a, w=w, b=b, k=k, r=r):
            s = s_ref[hh, i]
            sa = jnp.sum(s * a, axis=0, keepdims=True)
            s_new = s * w + sa * b + v_ref[pl.ds(hh * n + i, 1), :] * k
            so_ref[hh, i] = s_new
            y_scr[pl.ds(hh * n + i, 1), :] = jnp.sum(s_new * r, axis=0, keepdims=True)
            return carry

        lax.fori_loop(0, n, value_row, 0, unroll=WKV_STEP_UNROLL)
    for hh in range(2):
        rows = slice(hh * n, (hh + 1) * n)
        y = y_scr[rows, :]
        d = y - jnp.sum(y, axis=0, keepdims=True) * inv_n
        var = jnp.sum(d * d, axis=0, keepdims=True) * inv_n
        bonus = jnp.sum(r_ref[rows, :] * kp_ref[rows, :] * rk_ref[rows, :], axis=0, keepdims=True)
        yn = d * lax.rsqrt(var + GN_EPS) * gg_ref[rows, :] + gb_ref[rows, :]
        tok_ref[rows, :] = (yn + bonus * v_ref[rows, :]) * g_ref[rows, :]


def _wkv_step(state, layer, vecs, consts):
    B = state.shape[-1]
    sshape = (2, RWKV_HEAD, RWKV_HEAD, B)
    vblk = pl.BlockSpec((LANES, B), lambda p: (p, 0))
    return pl.pallas_call(
        _wkv_step_kernel, grid=(WKV_PAIRS,),
        in_specs=[pl.BlockSpec((None,) + sshape, lambda p: (layer, p, 0, 0, 0))] + [vblk] * 10,
        out_specs=[pl.BlockSpec(sshape, lambda p: (p, 0, 0, 0)), vblk],
        out_shape=[jax.ShapeDtypeStruct(state.shape[1:], F32), jax.ShapeDtypeStruct((TOK_WIDTH, B), F32)],
        scratch_shapes=[pltpu.VMEM((LANES, B), F32)],
        compiler_params=_cparams("parallel"), name="wkv_step")(state, *vecs, *consts)


STEP_INTERLEAVE = 4


def _softmax_rows(s):
    m = jnp.max(s, axis=-1, keepdims=True)
    e = jnp.exp(s - m)
    return e * (1.0 / jnp.sum(e, axis=-1, keepdims=True))


def _mem_attn_rows(q, k, v):
    heads = [slice(h * MEM_HEAD, (h + 1) * MEM_HEAD) for h in range(MEM_HEADS)]
    s = [_dot_nt(q[:, sl], k[:, sl]) * MEM_SCALE for sl in heads]
    p = [_softmax_rows(t) for t in s]
    return jnp.concatenate([_dot(t, v[:, sl]) for sl, t in zip(heads, p)], axis=1)


def _mem_attn_step_kernel(q_ref, k_ref, v_ref, o_ref, *, bs):
    rows = MEM_TOKENS * MEM_HEADS
    col_head = lax.broadcasted_iota(jnp.int32, (SUBLANES, rows), 1) % MEM_HEADS
    row_head = lax.broadcasted_iota(jnp.int32, (SUBLANES, rows), 0) % MEM_HEADS
    own = col_head == row_head

    def group(t, carry):
        bs_ = [t * STEP_INTERLEAVE + u for u in range(STEP_INTERLEAVE)]
        s = [_dot_nt(q_ref[b], k_ref[b]) * MEM_SCALE for b in bs_]
        p = [_softmax_rows(jnp.where(own, t_, -jnp.inf)) for t_ in s]
        for b, p_ in zip(bs_, p):
            o_ref[b] = _dot(p_, v_ref[b])
        return carry

    lax.fori_loop(0, bs // STEP_INTERLEAVE, group, 0)


def _mem_attn_step(q, mk, mv, layer, *, bs):
    B = q.shape[0]
    qblk = pl.BlockSpec((bs, SUBLANES, MEM_HEAD), lambda i: (i, 0, 0))
    cblk = pl.BlockSpec((None, bs, MEM_TOKENS * MEM_HEADS, MEM_HEAD), lambda i: (layer, i, 0, 0))
    return pl.pallas_call(
        functools.partial(_mem_attn_step_kernel, bs=bs), grid=(B // bs,),
        in_specs=[qblk, cblk, cblk], out_specs=qblk,
        out_shape=jax.ShapeDtypeStruct(q.shape, F32),
        compiler_params=_cparams("parallel"), name="mem_attn_step")(q, mk, mv)


def _deepnorm_ln(res, h, g, beta):
    z = ALPHA * res + h
    mu = jnp.mean(z, axis=-1, keepdims=True)
    d = z - mu
    var = jnp.mean(d * d, axis=-1, keepdims=True)
    return d * lax.rsqrt(var + LN_EPS) * g + beta


def _out_ln_kernel(tok_ref, q_ref, k_ref, v_ref, res_ref, toks_ref, mos_ref, ress_ref, w_ref, g_ref, beta_ref,
                   of_ref, ob_ref, ofs_ref, obs_ref):
    kt = tok_ref.shape[1]

    def project_ln(tok, mo, res):
        h = (jnp.dot(tok, w_ref[0:kt, :], preferred_element_type=F32)
             + jnp.dot(mo.astype(BF16), w_ref[kt:, :], preferred_element_type=F32))
        return _deepnorm_ln(res, h, g_ref[...], beta_ref[...])

    tm = res_ref.shape[0]
    half = tm // 2 if tm % (2 * SUBLANES) == 0 else tm
    k, v = k_ref[...], v_ref[...]
    for rows in (slice(r0, r0 + half) for r0 in range(0, tm, half)):
        out = project_ln(tok_ref[rows, :], _mem_attn_rows(q_ref[rows, :], k, v), res_ref[rows, :])
        of_ref[rows, :] = out
        ob_ref[rows, :] = out.astype(BF16)

    @pl.when(pl.program_id(0) == 0)
    def _():
        out = project_ln(toks_ref[...], mos_ref[...], ress_ref[...])
        ofs_ref[...] = out
        obs_ref[...] = out.astype(BF16)


def _out_ln(tok, proj, qblk, kv, res, tok_s, mo_s, res_s, w, layer, g, beta, *, tm):
    M, Kt = tok.shape
    Ms = res_s.shape[0]
    K, N = w.shape[1], w.shape[2]
    assert Kt + MEM_WIDTH == K and M % tm == 0 and tok_s.shape[1] == Kt and mo_s.shape[1] == MEM_WIDTH
    row = lambda i: (i, 0)
    cst = lambda i: (0, 0)
    in_specs = [pl.BlockSpec((tm, Kt), row), pl.BlockSpec((tm, MEM_WIDTH), lambda i: (i, qblk)),
                pl.BlockSpec((MEM_TOKENS, MEM_WIDTH), lambda i: (0, 0)),
                pl.BlockSpec((MEM_TOKENS, MEM_WIDTH), lambda i: (0, 1)),
                pl.BlockSpec((tm, N), row),
                pl.BlockSpec((Ms, Kt), cst), pl.BlockSpec((Ms, MEM_WIDTH), cst), pl.BlockSpec((Ms, N), cst),
                pl.BlockSpec((None, K, N), lambda i: (layer, 0, 0)),
                pl.BlockSpec((1, N), cst), pl.BlockSpec((1, N), cst)]
    oblk, sblk = pl.BlockSpec((tm, N), row), pl.BlockSpec((Ms, N), cst)
    of, ob, ofs, obs = pl.pallas_call(
        _out_ln_kernel, grid=(M // tm,),
        in_specs=in_specs, out_specs=[oblk, oblk, sblk, sblk],
        out_shape=[jax.ShapeDtypeStruct((M, N), F32), jax.ShapeDtypeStruct((M, N), BF16),
                   jax.ShapeDtypeStruct((Ms, N), F32), jax.ShapeDtypeStruct((Ms, N), BF16)],
        compiler_params=_cparams("arbitrary"), name="out_ln")(
            tok, proj, kv, kv, res, tok_s, mo_s, res_s, w, g, beta)
    return (of, ob), (ofs, obs)


def _ffn_down_ln_kernel(a_ref, w_ref, res_ref, g_ref, beta_ref, o_ref):
    k = pl.program_id(1)
    last = pl.num_programs(1) - 1
    part = jnp.dot(a_ref[...], w_ref[...], preferred_element_type=F32)

    @pl.when(k == 0)
    def _():
        o_ref[...] = part

    @pl.when((k > 0) & (k < last))
    def _():
        o_ref[...] += part

    @pl.when(k == last)
    def _():
        o_ref[...] = _deepnorm_ln(res_ref[...], o_ref[...] + part, g_ref[...], beta_ref[...])


def _ffn_down_ln(a, w, layer, res, g, beta, *, tm, tk):
    M, K = a.shape
    N = w.shape[2]
    assert w.shape[1] == K and K % tk == 0 and K // tk >= 2 and M % tm == 0
    row = lambda i, k: (i, 0)
    cst = lambda i, k: (0, 0)
    return pl.pallas_call(
        _ffn_down_ln_kernel, grid=(M // tm, K // tk),
        in_specs=[pl.BlockSpec((tm, tk), lambda i, k: (i, k)),
                  pl.BlockSpec((None, tk, N), lambda i, k: (layer, k, 0)),
                  pl.BlockSpec((tm, N), row), pl.BlockSpec((1, N), cst), pl.BlockSpec((1, N), cst)],
        out_specs=pl.BlockSpec((tm, N), row), out_shape=jax.ShapeDtypeStruct((M, N), F32),
        compiler_params=_cparams("parallel", "arbitrary"), name="ffn_down_ln")(a, w, res, g, beta)


def _ffn_up_kernel(x_ref, xs_ref, wg_ref, wu_ref, wd_ref, o_ref, os_ref, wdb_ref):
    wg, wu = wg_ref[...].astype(BF16), wu_ref[...].astype(BF16)

    def swiglu(x):
        gate = jnp.dot(x, wg, preferred_element_type=F32)
        up = jnp.dot(x, wu, preferred_element_type=F32)
        return (gate * jax.nn.sigmoid(gate) * up).astype(BF16)

    o_ref[...] = swiglu(x_ref[...])

    @pl.when(pl.program_id(0) == 0)
    def _():
        os_ref[...] = swiglu(xs_ref[...])
        wdb_ref[...] = wd_ref[...].astype(BF16)


def _ffn_up(x, xs, wg, wu, wd, layer, *, tm, tn):
    M, K = x.shape
    N = wg.shape[2]
    nj = N // tn
    assert wd.shape[1] == N
    wblk = pl.BlockSpec((None, K, tn), lambda i, j: (layer, 0, j))
    rider = lambda i, j: _rider_block(i, j, nj)
    return pl.pallas_call(
        _ffn_up_kernel, grid=(M // tm, nj),
        in_specs=[pl.BlockSpec((tm, K), lambda i, j: (i, 0)), pl.BlockSpec(xs.shape, lambda i, j: (0, 0)),
                  wblk, wblk, pl.BlockSpec((None, tn, wd.shape[2]), lambda i, j: (layer, rider(i, j), 0))],
        out_specs=[pl.BlockSpec((tm, tn), lambda i, j: (i, j)),
                   pl.BlockSpec((xs.shape[0], tn), lambda i, j: (0, rider(i, j))),
                   pl.BlockSpec((tn, wd.shape[2]), lambda i, j: (rider(i, j), 0))],
        out_shape=[jax.ShapeDtypeStruct((M, N), BF16), jax.ShapeDtypeStruct((xs.shape[0], N), BF16),
                   jax.ShapeDtypeStruct(wd.shape[1:], BF16)],
        compiler_params=_cparams("arbitrary", "arbitrary"), name="ffn_up")(x, xs, wg, wu, wd)


def _rope_kernel(x_ref, cos_ref, sin_ref, o_ref):
    x = x_ref[...]
    lane = lax.broadcasted_iota(jnp.int32, x.shape, 1)
    first_half = (lane % SWA_HEAD) < (SWA_HEAD // 2)
    partner = jnp.where(first_half, pltpu.roll(x, LANES - SWA_HEAD // 2, 1),
                        pltpu.roll(x, SWA_HEAD // 2, 1))
    o_ref[...] = x * cos_ref[...] + partner * sin_ref[...]


def _rope(proj, cos, sin_signed, *, tm):
    M = proj.shape[0]
    width = TOK_WIDTH + SWA_KV_WIDTH
    blk = pl.BlockSpec((tm, LANES), lambda i, s: (i, s))
    tab = pl.BlockSpec((tm, LANES), lambda i, s: (i, 0))
    return pl.pallas_call(
        _rope_kernel, grid=(M // tm, width // LANES),
        in_specs=[blk, tab, tab], out_specs=blk,
        out_shape=jax.ShapeDtypeStruct((M, width), F32),
        compiler_params=_cparams("parallel", "arbitrary"), name="rope")(proj, cos, sin_signed)


def _sink_column(sink_ref, base, rows_per_head, nheads):
    rows = rows_per_head * nheads
    hid = lax.broadcasted_iota(jnp.int32, (rows, 1), 0) // rows_per_head
    col = jnp.zeros((rows, 1), F32)
    for j in range(nheads):
        col = jnp.where(hid == j, sink_ref[base + j], col)
    return col


def _sink_softmax(s, sink):
    m = jnp.maximum(jnp.max(s, axis=-1, keepdims=True), sink)
    p = jnp.exp(s - m)
    return p * (1.0 / (jnp.sum(p, axis=-1, keepdims=True) + jnp.exp(sink - m)))


def _swa_kernel(sink_ref, q_ref, kc_ref, kp_ref, vc_ref, vp_ref, cosc_ref, sinc_ref, cosp_ref, sinp_ref,
                o_ref, krot_ref):
    n = pl.program_id(0)
    nslab_q = TOK_WIDTH // LANES
    slab = lambda ref, s: ref[:, s * LANES:(s + 1) * LANES]
    cos_c, sin_c, cos_p, sin_p = cosc_ref[...], sinc_ref[...], cosp_ref[...], sinp_ref[...]
    lane = lax.broadcasted_iota(jnp.int32, (BLOCK, LANES), 1)
    first_half = (lane % SWA_HEAD) < (SWA_HEAD // 2)
    lo = lane < SWA_HEAD

    def rope(x, cos, sin):
        partner = jnp.where(first_half, pltpu.roll(x, LANES - SWA_HEAD // 2, 1),
                            pltpu.roll(x, SWA_HEAD // 2, 1))
        return x * cos + partner * sin

    kv_slabs = SWA_KV_WIDTH // LANES
    k_cur = [rope(slab(kc_ref, j), cos_c, sin_c) for j in range(kv_slabs)]
    k_prev = [rope(slab(kp_ref, j), cos_p, sin_p) for j in range(kv_slabs)]
    for j in range(kv_slabs):
        krot_ref[:, j * LANES:(j + 1) * LANES] = k_cur[j]
    lane2 = lax.broadcasted_iota(jnp.int32, (2 * BLOCK, LANES), 1)
    kd, vd = [], []
    for g in range(SWA_KV_HEADS):
        j, half = divmod(g, 2)
        keep = (lane2 < SWA_HEAD) if half == 0 else (lane2 >= SWA_HEAD)
        dup = lambda t: jnp.where(keep, t, pltpu.roll(t, SWA_HEAD, 1)).astype(BF16)
        kd.append(dup(jnp.concatenate([k_prev[j], k_cur[j]], axis=0)))
        vd.append(dup(jnp.concatenate([slab(vp_ref, j), slab(vc_ref, j)], axis=0)))
    qi = lax.broadcasted_iota(jnp.int32, (2 * BLOCK, 2 * BLOCK), 0) % BLOCK
    si = lax.broadcasted_iota(jnp.int32, (2 * BLOCK, 2 * BLOCK), 1)
    valid = (si > qi) & (si <= qi + WINDOW) & ((n > 0) | (si >= BLOCK))
    row_lo = lax.broadcasted_iota(jnp.int32, (2 * BLOCK, 1), 0) < BLOCK
    slabs = list(range(nslab_q))
    kv_of = [(2 * s) // SWA_GROUP for s in slabs]
    assert math.frexp(SWA_SCALE)[0] == 0.5
    q = [rope(slab(q_ref, s), cos_c, sin_c) * SWA_SCALE for s in slabs]
    qs = [jnp.concatenate([jnp.where(lo, t, 0.0), jnp.where(lo, 0.0, t)], axis=0).astype(BF16) for t in q]
    sc = [lax.dot_general(t, kd[g], NT_DIMS, preferred_element_type=F32) for t, g in zip(qs, kv_of)]
    sc = [jnp.where(valid, t, -jnp.inf) for t in sc]
    p = [_sink_softmax(t, jnp.where(row_lo, sink_ref[2 * s], sink_ref[2 * s + 1])) for t, s in zip(sc, slabs)]
    o = [jnp.dot(t.astype(BF16), vd[g], preferred_element_type=F32) for t, g in zip(p, kv_of)]
    for s, t in zip(slabs, o):
        o_ref[:, s * LANES:(s + 1) * LANES] = jnp.where(lo, t[:BLOCK], t[BLOCK:]).astype(o_ref.dtype)


def _swa_prompt(proj, cos, sin_signed, sinks):
    T = proj.shape[0]
    kblk, vblk = TOK_WIDTH // SWA_KV_WIDTH, TOK_WIDTH // SWA_KV_WIDTH + 1
    prev = lambda n: jnp.maximum(n - 1, 0)
    kv_spec = lambda blk, row: pl.BlockSpec((BLOCK, SWA_KV_WIDTH), lambda n: (row(n), blk))
    tab = lambda row: pl.BlockSpec((BLOCK, LANES), lambda n: (row(n), 0))
    cur = lambda n: n
    return pl.pallas_call(
        _swa_kernel, grid=(T // BLOCK,),
        in_specs=[pl.BlockSpec(memory_space=pltpu.SMEM),
                  pl.BlockSpec((BLOCK, TOK_WIDTH), lambda n: (n, 0)),
                  kv_spec(kblk, cur), kv_spec(kblk, prev), kv_spec(vblk, cur), kv_spec(vblk, prev),
                  tab(cur), tab(cur), tab(prev), tab(prev)],
        out_specs=[pl.BlockSpec((BLOCK, TOK_WIDTH), lambda n: (n, 0)),
                   pl.BlockSpec((BLOCK, SWA_KV_WIDTH), lambda n: (n, 0))],
        out_shape=[jax.ShapeDtypeStruct((T, TOK_WIDTH), BF16),
                   jax.ShapeDtypeStruct((T, SWA_KV_WIDTH), F32)],
        compiler_params=_cparams("arbitrary"), name="swa_banded")(
            sinks, proj, proj, proj, proj, proj, cos, sin_signed, cos, sin_signed)


def _swa_step_kernel(sink_ref, q_ref, kn_ref, vn_ref, kt_ref, vt_ref, o_ref, kto_ref, vto_ref, *, bs):
    sink = _sink_column(sink_ref, 0, 1, SWA_Q_HEADS)
    newest = lax.broadcasted_iota(jnp.int32, (SWA_HEAD, WINDOW), 1) == WINDOW - 1
    kv_of_head = lax.broadcasted_iota(jnp.int32, (SWA_Q_HEADS, 1), 0) // SWA_GROUP
    kv_heads = list(range(SWA_KV_HEADS))

    def pick(per_kv):
        out = per_kv[0]
        for g in kv_heads[1:]:
            out = jnp.where(kv_of_head == g, per_kv[g], out)
        return out

    def slid(new_ref, cache_ref, b):
        new = new_ref[b]
        return [jnp.where(newest, new[:, g:g + 1], pltpu.roll(cache_ref[b, g], WINDOW - 1, 1)) for g in kv_heads]

    def group(t, carry):
        bs_ = [t * STEP_INTERLEAVE + u for u in range(STEP_INTERLEAVE)]
        kt = [slid(kn_ref, kt_ref, b) for b in bs_]
        vt = [slid(vn_ref, vt_ref, b) for b in bs_]
        for b, kt_, vt_ in zip(bs_, kt, vt):
            for g in kv_heads:
                kto_ref[b, g] = kt_[g]
                vto_ref[b, g] = vt_[g]
        s = [pick([_dot(q_ref[b], t_) for t_ in kt_]) * SWA_SCALE for b, kt_ in zip(bs_, kt)]
        p = [_sink_softmax(s_, sink) for s_ in s]
        for b, p_, vt_ in zip(bs_, p, vt):
            o_ref[b] = pick([_dot_nt(p_, t_) for t_ in vt_])
        return carry

    lax.fori_loop(0, bs // STEP_INTERLEAVE, group, 0)


def _swa_step(q, k_new, v_new, kt, vt, layer, sinks, *, bs):
    B = q.shape[0]
    qblk = pl.BlockSpec((bs, SWA_Q_HEADS, SWA_HEAD), lambda i: (i, 0, 0))
    nblk = pl.BlockSpec((bs, SWA_HEAD, SWA_KV_HEADS), lambda i: (i, 0, 0))
    cshape = (bs, SWA_KV_HEADS, SWA_HEAD, WINDOW)
    cin = pl.BlockSpec((None,) + cshape, lambda i: (layer, i, 0, 0, 0))
    cout = pl.BlockSpec(cshape, lambda i: (i, 0, 0, 0))
    cache = jax.ShapeDtypeStruct(kt.shape[1:], F32)
    return pl.pallas_call(
        functools.partial(_swa_step_kernel, bs=bs), grid=(B // bs,),
        in_specs=[pl.BlockSpec(memory_space=pltpu.SMEM), qblk, nblk, nblk, cin, cin],
        out_specs=[qblk, cout, cout],
        out_shape=[jax.ShapeDtypeStruct(q.shape, F32), cache, cache],
        compiler_params=_cparams("parallel"), name="swa_step")(sinks, q, k_new, v_new, kt, vt)


ROW_TILE = 512
FFN_ROW_TILE = 1024
COL_TILE = 512
STEP_BATCH = 16
FFN_DOWN_K_TILE = FFN_HIDDEN // 2


def _row_tile(m):
    return ROW_TILE if m % ROW_TILE == 0 else m


def _ffn_row_tile(m):
    return FFN_ROW_TILE if m % FFN_ROW_TILE == 0 else _row_tile(m)


def _pad_rows(w, rows):
    return jnp.pad(w, ((0, rows - w.shape[0]), (0, 0)))


def _pad_cols(w, cols):
    return jnp.pad(w, ((0, 0), (0, cols - w.shape[1])))


def _rwkv_weights(w_in, mu, w1, w2, a1, a2, g1, g2):
    w_lora = jnp.concatenate([_pad_cols(w1[0], LORA_PAD), _pad_cols(a1[0], LORA_PAD), g1[0]], axis=1)
    return dict(w_in=w_in.astype(BF16), w_lora=w_lora.astype(BF16), mu=mu[0][:, None, :],
                w2=_pad_rows(w2[0], LORA_PAD).astype(BF16), a2=_pad_rows(a2[0], LORA_PAD).astype(BF16),
                g2=g2[0].astype(BF16))


def _unblock_state(s_bd):
    n = RWKV_HEAD
    return jnp.stack([s_bd[:, :n, :n], s_bd[:, n:, n:]], axis=1).reshape(RWKV_HEADS, n, n)


def _rope_tables(pos):
    half = SWA_HEAD // 2
    inv = ROPE_THETA ** (-jnp.arange(half, dtype=F32) / half)
    ang = pos.astype(F32)[:, None] * inv[None, :]
    cos, sin = jnp.cos(ang), jnp.sin(ang)
    reps = LANES // SWA_HEAD
    return jnp.tile(cos, (1, 2 * reps)), jnp.tile(jnp.concatenate([-sin, sin], axis=1), (1, reps))


def _post_mixer(prompt, sample, sw, layer):
    row = lambda t: t[layer][None, :]
    tok_p, proj_p, qblk, kv, x_p = prompt
    tok_s, mo_s, x_s = sample
    (x1f_p, x1b_p), (x1f_s, x1b_s) = _out_ln(tok_p, proj_p, qblk, kv, x_p, tok_s, mo_s, x_s, sw["w_out"], layer,
                                             row(sw["ln1_g"]), row(sw["ln1_b"]), tm=_row_tile(x_p.shape[0]))
    hff_p, hff_s, wd = _ffn_up(x1b_p, x1b_s, sw["w_gate"], sw["w_up"], sw["w_down"], layer,
                               tm=_ffn_row_tile(x1b_p.shape[0]), tn=COL_TILE)
    return [_ffn_down_ln(h, wd[None], 0, xf, row(sw["ln2_g"]), row(sw["ln2_b"]),
                         tm=_row_tile(xf.shape[0]), tk=FFN_DOWN_K_TILE)
            for h, xf in ((hff_p, x1f_p), (hff_s, x1f_s))]


def kernel(x_prompt, x_sample, mem_prompt, cache_mem_k, cache_mem_v, state_rwkv_shift, state_rwkv_wkv, cache_swa_k, cache_swa_v, w_in_rwkv, rwkv_mu, rwkv_w0, rwkv_w1, rwkv_w2, rwkv_a0, rwkv_a1, rwkv_a2, rwkv_g1, rwkv_g2, rwkv_k_k, rwkv_k_a, rwkv_r_k, rwkv_gn_g, rwkv_gn_b, w_in_swa, swa_sinks, w_mem_kv, w_out, ln1_g, ln1_b, w_gate, w_up, w_down, ln2_g, ln2_b):
    assert DEPTH == 2 and x_prompt.shape[0] == 1 and x_sample.shape[1] == 1
    T = x_prompt.shape[1]
    B = x_sample.shape[0]
    row = lambda t: t[None, :]
    shared = dict(w_out=w_out.astype(BF16), ln1_g=ln1_g, ln1_b=ln1_b, w_gate=w_gate, w_up=w_up,
                  w_down=w_down, ln2_g=ln2_g, ln2_b=ln2_b)
    RW = _rwkv_weights(w_in_rwkv, rwkv_mu, rwkv_w1, rwkv_w2, rwkv_a1, rwkv_a2, rwkv_g1, rwkv_g2)
    rk, gn_g, gn_b = row(rwkv_r_k[0].reshape(-1)), row(rwkv_gn_g[0]), row(rwkv_gn_b[0])
    prep_consts = (row(rwkv_w0[0]), row(rwkv_a0[0]), row(rwkv_k_k[0]), row(rwkv_k_a[0]))
    w_swa = w_in_swa.astype(BF16)
    sinks = swa_sinks[0]
    q_blk_rwkv = 3 * TOK_WIDTH // MEM_WIDTH
    q_blk_swa = (TOK_WIDTH + 2 * SWA_KV_WIDTH) // MEM_WIDTH

    xp, xs = x_prompt[0], x_sample[:, 0]
    tms = _row_tile(B)
    kv = [_proj(mem_prompt[0], w_mem_kv, i, tm=MEM_TOKENS, tn=COL_TILE, name="mem_kv") for i in range(DEPTH)]
    prompt_mem_k = jnp.stack([t[:, :MEM_WIDTH] for t in kv]).reshape(DEPTH, 1, MEM_TOKENS, MEM_HEADS, MEM_HEAD)
    prompt_mem_v = jnp.stack([t[:, MEM_WIDTH:] for t in kv]).reshape(DEPTH, 1, MEM_TOKENS, MEM_HEADS, MEM_HEAD)
    mem_k = cache_mem_k.reshape(DEPTH, B, MEM_TOKENS * MEM_HEADS, MEM_HEAD)
    mem_v = cache_mem_v.reshape(DEPTH, B, MEM_TOKENS * MEM_HEADS, MEM_HEAD)

    def mem_step(q, layer):
        q_rows = jnp.pad(q.reshape(B, MEM_HEADS, MEM_HEAD), ((0, 0), (0, SUBLANES - MEM_HEADS), (0, 0)))
        out = _mem_attn_step(q_rows, mem_k, mem_v, layer, bs=STEP_BATCH)
        return out[:, :MEM_HEADS].reshape(B, MEM_WIDTH).astype(BF16)

    proj_p = _rwkv_in(xp, None, RW["mu"], RW["w_in"], RW["w_lora"], 0, tm=_ffn_row_tile(T))
    tok_p, s_bd = _wkv_prompt(proj_p, RW["w2"], RW["a2"], RW["g2"], prep_consts + (rk, gn_g, gn_b))
    prompt_shift = xp[-1][None, None, :]
    prompt_wkv = _unblock_state(s_bd)[None, None]
    proj_s = _rwkv_in(xs, state_rwkv_shift[0], RW["mu"], RW["w_in"], RW["w_lora"], 0, tm=tms)
    vecs = _rwkv_prep_t(proj_s, RW["w2"], RW["a2"], RW["g2"], *prep_consts)
    lanes_b = lambda t: jnp.broadcast_to(t.reshape(TOK_WIDTH, 1), (TOK_WIDTH, B))
    s_new, tok_t = _wkv_step(state_rwkv_wkv.transpose(0, 2, 3, 4, 1), 0, vecs,
                             (lanes_b(rk), lanes_b(gn_g), lanes_b(gn_b)))
    mo_s = mem_step(proj_s[:, 3 * TOK_WIDTH:3 * TOK_WIDTH + MEM_WIDTH], 0)
    sample_shift = xs[None]
    sample_wkv = s_new.transpose(3, 0, 1, 2)[None]
    xf_p, xf_s = _post_mixer((tok_p, proj_p, q_blk_rwkv, kv[0], xp), (tok_t.T.astype(BF16), mo_s, xs), shared, 0)

    proj_p, proj_s = _proj(xf_p, w_swa, 0, tm=_ffn_row_tile(T), tn=COL_TILE, name="swa_proj", xs=xf_s)
    cos, sin = _rope_tables(jnp.arange(T))
    tok_p, k_rot = _swa_prompt(proj_p, cos, sin, sinks)
    v_last = proj_p[T - WINDOW:, TOK_WIDTH + SWA_KV_WIDTH:TOK_WIDTH + 2 * SWA_KV_WIDTH]
    prompt_swa_k = k_rot[T - WINDOW:].reshape(1, 1, WINDOW, SWA_KV_HEADS, SWA_HEAD)
    prompt_swa_v = v_last.reshape(1, 1, WINDOW, SWA_KV_HEADS, SWA_HEAD)
    cos, sin = _rope_tables(jnp.full((B,), PAST_LEN))
    qk = _rope(proj_s, cos, sin, tm=tms)
    chan_major = lambda t: t.reshape(B, SWA_KV_HEADS, SWA_HEAD).transpose(0, 2, 1)
    k_new = chan_major(qk[:, TOK_WIDTH:])
    v_new = chan_major(proj_s[:, TOK_WIDTH + SWA_KV_WIDTH:TOK_WIDTH + 2 * SWA_KV_WIDTH])
    o, kc, vc = _swa_step(qk[:, :TOK_WIDTH].reshape(B, SWA_Q_HEADS, SWA_HEAD), k_new, v_new,
                          cache_swa_k.transpose(0, 1, 3, 4, 2), cache_swa_v.transpose(0, 1, 3, 4, 2),
                          0, sinks, bs=STEP_BATCH)
    mo_s = mem_step(proj_s[:, TOK_WIDTH + 2 * SWA_KV_WIDTH:], 1)
    sample_swa_k, sample_swa_v = kc.transpose(0, 3, 1, 2)[None], vc.transpose(0, 3, 1, 2)[None]
    y_prompt, y_sample = _post_mixer((tok_p, proj_p, q_blk_swa, kv[1], xf_p),
                                     (o.reshape(B, TOK_WIDTH).astype(BF16), mo_s, xf_s), shared, 1)

    return (y_prompt[None], y_sample[:, None, :], prompt_mem_k, prompt_mem_v, prompt_shift, prompt_wkv,
            prompt_swa_k, prompt_swa_v, sample_shift, sample_wkv, sample_swa_k, sample_swa_v)
```

```python
import functools
import math

import jax
import jax.numpy as jnp
from jax import lax
from jax.experimental import pallas as pl
from jax.experimental.pallas import tpu as pltpu

D_MODEL = 2048
DEPTH = 2
MEM_WIDTH = D_MODEL // 4
TOK_WIDTH = D_MODEL - MEM_WIDTH
RWKV_HEAD = 64
RWKV_HEADS = TOK_WIDTH // RWKV_HEAD
GN_EPS = RWKV_HEAD * 1e-5
SWA_HEAD = 64
SWA_Q_HEADS = TOK_WIDTH // SWA_HEAD
SWA_KV_HEADS = 4
SWA_GROUP = SWA_Q_HEADS // SWA_KV_HEADS
SWA_KV_WIDTH = SWA_KV_HEADS * SWA_HEAD
WINDOW = 128
BLOCK = 128
PAST_LEN = 8192
SWA_SCALE = SWA_HEAD ** -0.5
ROPE_THETA = 10000.0
MEM_TOKENS = 256
MEM_HEADS = 4
MEM_HEAD = MEM_WIDTH // MEM_HEADS
MEM_SCALE = MEM_HEAD ** -0.5
FFN_HIDDEN = int(math.ceil(8 * D_MODEL / 3 / 256)) * 256
ALPHA = (2 * DEPTH) ** 0.25
LN_EPS = 1e-5
LORA_PAD = 128
LORA_IN_WIDTH = 512

LANES = 128
SUBLANES = 8
VMEM_LIMIT_BYTES = 56 * 1024 * 1024

BF16 = jnp.bfloat16
F32 = jnp.float32
NT_DIMS = (((1,), (1,)), ((), ()))
TN_DIMS = (((0,), (0,)), ((), ()))


def _dot(a, b):
    return jnp.dot(a.astype(BF16), b.astype(BF16), preferred_element_type=F32)


def _dot_nt(a, b):
    return lax.dot_general(a.astype(BF16), b.astype(BF16), NT_DIMS, preferred_element_type=F32)


def _dot_tn(a, b):
    return lax.dot_general(a.astype(BF16), b.astype(BF16), TN_DIMS, preferred_element_type=F32)


def _split_dot(x, m):
    hi = x.astype(BF16)
    lo = (x - hi.astype(F32)).astype(BF16)
    return (jnp.dot(hi, m, preferred_element_type=F32)
            + jnp.dot(lo, m, preferred_element_type=F32))


def _head_ones():
    p = lax.broadcasted_iota(jnp.int32, (LANES, LANES), 0)
    q = lax.broadcasted_iota(jnp.int32, (LANES, LANES), 1)
    return ((p // RWKV_HEAD) == (q // RWKV_HEAD)).astype(BF16)


def _cparams(*sem):
    return pltpu.CompilerParams(dimension_semantics=sem, vmem_limit_bytes=VMEM_LIMIT_BYTES)


def _rider_block(i, j, nj):
    return jnp.where(i == 0, j, nj - 1)


def _proj_kernel(*refs, rider):
    if rider:
        x_ref, xs_ref, w_ref, o_ref, os_ref, xb_ref = refs
    else:
        x_ref, w_ref, o_ref, xb_ref = refs

    @pl.when(pl.program_id(1) == 0)
    def _():
        xb_ref[...] = x_ref[...].astype(BF16)

    w = w_ref[...].astype(BF16)
    o_ref[...] = jnp.dot(xb_ref[...], w, preferred_element_type=F32)
    if rider:
        @pl.when(pl.program_id(0) == 0)
        def _():
            os_ref[...] = jnp.dot(xs_ref[...].astype(BF16), w, preferred_element_type=F32)


def _proj(x, w, layer, *, tm, tn, name, xs=None):
    M, K = x.shape
    N = w.shape[2]
    assert M % tm == 0 and N % tn == 0
    nj = N // tn
    rider = xs is not None
    in_specs = [pl.BlockSpec((tm, K), lambda i, j: (i, 0))]
    out_specs = [pl.BlockSpec((tm, tn), lambda i, j: (i, j))]
    out_shape = [jax.ShapeDtypeStruct((M, N), F32)]
    if rider:
        in_specs.append(pl.BlockSpec(xs.shape, lambda i, j: (0, 0)))
        out_specs.append(pl.BlockSpec((xs.shape[0], tn), lambda i, j: (0, _rider_block(i, j, nj))))
        out_shape.append(jax.ShapeDtypeStruct((xs.shape[0], N), F32))
    in_specs.append(pl.BlockSpec((None, K, tn), lambda i, j: (layer, 0, j)))
    out = pl.pallas_call(
        functools.partial(_proj_kernel, rider=rider), grid=(M // tm, nj),
        in_specs=in_specs, out_specs=out_specs, out_shape=out_shape,
        scratch_shapes=[pltpu.VMEM((tm, K), BF16)],
        compiler_params=_cparams("arbitrary", "arbitrary"), name=name)(*([x, xs, w] if rider else [x, w]))
    return out if rider else out[0]


MIX_R, MIX_W, MIX_K, MIX_V, MIX_A, MIX_G = range(6)
RWKV_IN_WIDTH = 3 * TOK_WIDTH + MEM_WIDTH + LORA_IN_WIDTH
RWKV_IN_TILE = LORA_IN_WIDTH
RWKV_IN_KEPT = (MIX_R, MIX_K, MIX_V)
RWKV_IN_SLOT = ([0] * (TOK_WIDTH // RWKV_IN_TILE) + [1] * (TOK_WIDTH // RWKV_IN_TILE)
                + [2] * (TOK_WIDTH // RWKV_IN_TILE) + [3] * (MEM_WIDTH // RWKV_IN_TILE))
LORA_PARTS = ((MIX_W, 0, LORA_PAD), (MIX_A, LORA_PAD, 2 * LORA_PAD), (MIX_G, 2 * LORA_PAD, LORA_IN_WIDTH))
RWKV_IN_SUB = 256


def _rwkv_in_kernel(slot_ref, x_ref, xp_ref, xs_ref, xps_ref, mu_ref, w_ref, wl_ref, o_ref, os_ref,
                    lhs_ref, lhss_ref):
    i, j = pl.program_id(0), pl.program_id(1)
    tm = x_ref.shape[0]
    sub = min(RWKV_IN_SUB, tm)
    blocks = [slice(s0, s0 + sub) for s0 in range(0, tm, sub)]
    plain = len(RWKV_IN_KEPT)
    nproj = pl.num_programs(1) - 1

    def x_and_delta(rows):
        x = x_ref[rows, :]
        if rows.start == 0:
            first = jnp.where(i > 0, xp_ref[SUBLANES - 1:SUBLANES, :], 0.0)
        else:
            first = x_ref[rows.start - 1:rows.start, :]
        rowid = lax.broadcasted_iota(jnp.int32, (sub, 1), 0)
        return x, jnp.where(rowid == 0, first, pltpu.roll(x, 1, 0)) - x

    def rider_x_and_delta(rows):
        x = xs_ref[...]
        return x, xps_ref[...] - x

    def group(delta_fn, row_blocks, kept_ref, out_ref):
        @pl.when(j == 0)
        def _():
            for rows in row_blocks:
                x, d = delta_fn(rows)
                for slot, m in enumerate(RWKV_IN_KEPT):
                    kept_ref[slot, rows, :] = (x + d * mu_ref[m]).astype(BF16)
                kept_ref[plain, rows, :] = x.astype(BF16)

        @pl.when(j < nproj)
        def _():
            out_ref[...] = jnp.dot(kept_ref[slot_ref[j]], w_ref[...], preferred_element_type=F32)

        @pl.when(j == nproj)
        def _():
            for rows in row_blocks:
                x, d = delta_fn(rows)
                for m, lo, hi in LORA_PARTS:
                    out_ref[rows, lo:hi] = jnp.dot((x + d * mu_ref[m]).astype(BF16), wl_ref[:, lo:hi],
                                                   preferred_element_type=F32)

    group(x_and_delta, blocks, lhs_ref, o_ref)

    @pl.when(i == 0)
    def _():
        group(rider_x_and_delta, [slice(None)], lhss_ref, os_ref)


def _rwkv_in(x, xs, xprev_s, mu, w, w_lora, layer, *, tm):
    M, K = x.shape
    Ms = xs.shape[0]
    tn = RWKV_IN_TILE
    nproj = w.shape[2] // tn
    assert M % tm == 0 and w.shape[2] + w_lora.shape[1] == RWKV_IN_WIDTH and nproj == len(RWKV_IN_SLOT)
    assert tm % min(RWKV_IN_SUB, tm) == 0 and tm % SUBLANES == 0
    rows8 = tm // SUBLANES
    slot = jnp.asarray(RWKV_IN_SLOT + [0], jnp.int32)
    cst = lambda i, j, m: (0, 0)
    grid_spec = pltpu.PrefetchScalarGridSpec(
        num_scalar_prefetch=1, grid=(M // tm, nproj + 1),
        in_specs=[pl.BlockSpec((tm, K), lambda i, j, m: (i, 0)),
                  pl.BlockSpec((SUBLANES, K), lambda i, j, m: (jnp.maximum(i * rows8 - 1, 0), 0)),
                  pl.BlockSpec((Ms, K), cst), pl.BlockSpec((Ms, K), cst),
                  pl.BlockSpec(mu.shape, lambda i, j, m: (0, 0, 0)),
                  pl.BlockSpec((None, K, tn), lambda i, j, m: (layer, 0, jnp.minimum(j, nproj - 1))),
                  pl.BlockSpec(w_lora.shape, cst)],
        out_specs=[pl.BlockSpec((tm, tn), lambda i, j, m: (i, j)),
                   pl.BlockSpec((Ms, tn), lambda i, j, m: (0, _rider_block(i, j, nproj + 1)))],
        scratch_shapes=[pltpu.VMEM((len(RWKV_IN_KEPT) + 1, tm, K), BF16),
                        pltpu.VMEM((len(RWKV_IN_KEPT) + 1, Ms, K), BF16)])
    return pl.pallas_call(
        _rwkv_in_kernel, grid_spec=grid_spec,
        out_shape=[jax.ShapeDtypeStruct((M, RWKV_IN_WIDTH), F32), jax.ShapeDtypeStruct((Ms, RWKV_IN_WIDTH), F32)],
        compiler_params=_cparams("arbitrary", "arbitrary"), name="rwkv_in")(slot, x, x, xs, xprev_s, mu, w, w_lora)


def _softplus(z):
    return jnp.maximum(z, 0.0) + jnp.log1p(jnp.exp(-jnp.abs(z)))


def _each(f, *lists):
    return [f(*a) for a in zip(*lists)]


def _lora_hidden(hl):
    return jnp.tanh(hl[:, 0:LORA_PAD]), hl[:, LORA_PAD:2 * LORA_PAD], jax.nn.sigmoid(hl[:, 2 * LORA_PAD:])


def _prep_slabs(k, wl, al, w0, a0, k_k, k_a, ones):
    lw = _each(lambda wl_, w0_: -jnp.exp(-_softplus(-(w0_ + wl_)) - 0.5), wl, w0)
    agate = _each(lambda al_, a0_: jax.nn.sigmoid(a0_ + al_), al, a0)
    kkr = _each(lambda k_, c_: k_ * c_, k, k_k)
    ss = _each(lambda t: _split_dot(t * t, ones), kkr)
    kn = _each(lambda t, s_: t / jnp.maximum(jnp.sqrt(s_), 1e-12), kkr, ss)
    bv = _each(lambda n_, a_: n_ * a_, kn, agate)
    kp = _each(lambda k_, a_, c_: k_ * (1.0 + (a_ - 1.0) * c_), k, agate, k_a)
    return lw, kp, kn, bv


def _rwkv_prep_t_kernel(hl_ref, r_ref, k_ref, v_ref, w2_ref, a2_ref, g2_ref, w0_ref, a0_ref, kk_ref, ka_ref,
                        rt_ref, vt_ref, lw_ref, kp_ref, kn_ref, bv_ref, g_ref):
    hw, ha, hg = _lora_hidden(hl_ref[...])
    outs = _prep_slabs([k_ref[...]], [_dot(hw, w2_ref[...])], [_dot(ha, a2_ref[...])], [w0_ref[...]],
                       [a0_ref[...]], [kk_ref[...]], [ka_ref[...]], _head_ones())
    vals = [r_ref[...], v_ref[...]] + [t[0] for t in outs] + [_dot(hg, g2_ref[...])]
    for ref, val in zip((rt_ref, vt_ref, lw_ref, kp_ref, kn_ref, bv_ref, g_ref), vals):
        ref[...] = val.T


def _rwkv_prep_t(proj, w2p, a2p, g2, w0, a0, k_k, k_a):
    B = proj.shape[0]
    assert B == LANES
    nslab = TOK_WIDTH // LANES
    lora_blk = (RWKV_IN_WIDTH - LORA_IN_WIDTH) // LORA_IN_WIDTH
    col = lambda s: (0, s)
    slab = lambda base: pl.BlockSpec((B, LANES), lambda s, base=base: (0, base + s))
    outs = [jax.ShapeDtypeStruct((TOK_WIDTH, B), F32)] * 7
    return pl.pallas_call(
        _rwkv_prep_t_kernel, grid=(nslab,),
        in_specs=[pl.BlockSpec((B, LORA_IN_WIDTH), lambda s: (0, lora_blk)),
                  slab(0), slab(nslab), slab(2 * nslab),
                  pl.BlockSpec((LORA_PAD, LANES), col),
                  pl.BlockSpec((LORA_PAD, LANES), col),
                  pl.BlockSpec((2 * LORA_PAD, LANES), col),
                  pl.BlockSpec((1, LANES), col), pl.BlockSpec((1, LANES), col),
                  pl.BlockSpec((1, LANES), col), pl.BlockSpec((1, LANES), col)],
        out_specs=[pl.BlockSpec((LANES, B), lambda s: (s, 0))] * 7,
        out_shape=outs, compiler_params=_cparams("arbitrary"),
        name="rwkv_prep_t")(proj, proj, proj, proj, w2p, a2p, g2, w0, a0, k_k, k_a)


def _gn_gate(y, r, kp, v, g, rk, gg, gb, ones):
    inv_n = 1.0 / RWKV_HEAD
    rows = y[0].shape[0]
    sums = _each(lambda y_, r_, k_, rk_: _split_dot(jnp.concatenate([y_, r_ * k_ * rk_], axis=0), ones),
                 y, r, kp, rk)
    d = _each(lambda y_, s_: y_ - s_[:rows] * inv_n, y, sums)
    var = _each(lambda d_: _split_dot(d_ * d_, ones) * inv_n, d)
    return _each(lambda d_, var_, gg_, gb_, s_, v_, g_:
                 (d_ * lax.rsqrt(var_ + GN_EPS) * gg_ + gb_ + s_[rows:] * v_) * g_,
                 d, var, gg, gb, sums, v, g)


WKV_CHUNK = 64


def _wkv_masks():
    n = 2 * WKV_CHUNK
    p = lax.broadcasted_iota(jnp.int32, (n, n), 0)
    q = lax.broadcasted_iota(jnp.int32, (n, n), 1)
    same = lambda b: (p // b) == (q // b)
    pt, qt = p % WKV_CHUNK, q % WKV_CHUNK
    s8, s16, s32, s64 = same(8), same(16), same(32), same(WKV_CHUNK)
    return dict(strict=s64 & (pt > qt), incl=s64 & (pt >= qt), s8=s8,
                e16=s16 & ~s8, e32=s32 & ~s16, e64=s64 & ~s32,
                eye=(p == q).astype(F32))


WKV_PAIRS = TOK_WIDTH // LANES
WKV_STEP_CHUNKS = 2


def _wkv_lanes(r, lw, k, v, kn, bv, mk, tri, lane_lo):
    stack = lambda x: jnp.concatenate([jnp.where(lane_lo, x, 0.0), jnp.where(lane_lo, 0.0, x)], axis=0)
    n = 2 * WKV_CHUNK
    c = _each(lambda t: _split_dot_left(tri, t), lw)
    c_last = _each(lambda t: t[WKV_CHUNK - 1:WKV_CHUNK, :], c)
    e_out = _each(lambda t: jnp.exp(-t), c)
    e_end = _each(lambda t, tl: jnp.exp(tl - t), c, c_last)
    ah = _each(lambda kn_, c_, lw_: stack(-kn_ * jnp.exp(c_ - lw_)), kn, c, lw)
    rh = _each(lambda r_, c_: stack(r_ * jnp.exp(c_)), r, c)
    bh = _each(lambda b_, e_: stack(b_ * e_), bv, e_out)
    kh = _each(lambda k_, e_: stack(k_ * e_), k, e_out)
    bbar = _each(lambda b_, e_: stack(b_ * e_), bv, e_end)
    kbar = _each(lambda k_, e_: stack(k_ * e_), k, e_end)
    vs = _each(stack, v)
    gm = _each(lambda a_, r_, b_, k_: _dot_nt(jnp.concatenate([a_, r_], axis=0),
                                              jnp.concatenate([b_, k_], axis=0)), ah, rh, bh, kh)
    a_ab = _each(lambda g_: jnp.where(mk["strict"], g_[:n, :n], 0.0), gm)
    a_ak = _each(lambda g_: jnp.where(mk["strict"], g_[:n, n:], 0.0), gm)
    l_rb = _each(lambda g_: jnp.where(mk["incl"], g_[n:, :n], 0.0), gm)
    l_rk = _each(lambda g_: jnp.where(mk["incl"], g_[n:, n:], 0.0), gm)
    d1 = _each(lambda a_: jnp.where(mk["s8"], a_, 0.0), a_ab)
    x = _each(lambda d_: mk["eye"] + d_, d1)
    d2 = _each(lambda d_: _dot(d_, d_), d1)
    x = _each(lambda x_, d_: x_ + _dot(x_, d_), x, d2)
    d4 = _each(lambda d_: _dot(d_, d_), d2)
    x = _each(lambda x_, d_: x_ + _dot(x_, d_), x, d4)
    for lvl in ("e16", "e32", "e64"):
        ex = _each(lambda a_, x_: _dot(jnp.where(mk[lvl], a_, 0.0), x_), a_ab, x)
        x = _each(lambda x_, e_: x_ + _dot(x_, e_), x, ex)
    av = _each(_dot, a_ak, vs)
    tw = _each(lambda x_, a_, v_: _dot(x_, jnp.concatenate([a_, v_], axis=1)), x, ah, av)
    lwm = _each(_dot, l_rb, tw)
    lv = _each(_dot, l_rk, vs)
    qm = _each(lambda r_, l_: r_ + l_[:, :n], rh, lwm)
    y0 = _each(lambda l_, v_: l_[:, n:] + v_, lwm, lv)
    mt = _each(lambda t_, b_: _dot_tn(t_[:, :n], b_), tw, bbar)
    nt = _each(lambda t_, b_, v_, k_: _dot_tn(t_[:, n:], b_) + _dot_tn(v_, k_), tw, bbar, vs, kbar)
    return qm, y0, mt, nt, _each(jnp.exp, c_last)


def _split_dot_left(m, x):
    hi = x.astype(BF16)
    lo = (x - hi.astype(F32)).astype(BF16)
    return (jnp.dot(m, hi, preferred_element_type=F32)
            + jnp.dot(m, lo, preferred_element_type=F32))


def _wkv_kernel(r_ref, k_ref, v_ref, hl_ref, w2_ref, a2_ref, g2_ref, w0_ref, a0_ref, kk_ref, ka_ref,
                rk_ref, gg_ref, gb_ref, o_ref, sout_ref, s_scr):
    c_idx = pl.program_id(0)

    @pl.when(c_idx == 0)
    def _():
        s_scr[...] = jnp.zeros_like(s_scr)

    mk = _wkv_masks()
    ti = lax.broadcasted_iota(jnp.int32, (WKV_CHUNK, WKV_CHUNK), 0)
    tj = lax.broadcasted_iota(jnp.int32, (WKV_CHUNK, WKV_CHUNK), 1)
    tri = (ti >= tj).astype(BF16)
    lane_lo = lax.broadcasted_iota(jnp.int32, (WKV_CHUNK, LANES), 1) < RWKV_HEAD
    ones = _head_ones()
    pairs = range(WKV_PAIRS)
    lanes = [(slice(ch * WKV_CHUNK, (ch + 1) * WKV_CHUNK), slice(p * LANES, (p + 1) * LANES))
             for ch in range(WKV_STEP_CHUNKS) for p in pairs]
    cut = lambda t: [t[rows, sl] for rows, sl in lanes]
    rep = lambda t: [t[:, sl] for _, sl in lanes]
    hw, ha, hg = _lora_hidden(hl_ref[...])
    wl, al, g = _dot(hw, w2_ref[...]), _dot(ha, a2_ref[...]), _dot(hg, g2_ref[...])
    r, v = cut(r_ref), cut(v_ref)
    lw, kp, kn, bv = _prep_slabs(cut(k_ref), cut(wl), cut(al), rep(w0_ref), rep(a0_ref),
                                 rep(kk_ref), rep(ka_ref), ones)
    qm, y0, mt, nt, dec = _wkv_lanes(r, lw, kp, v, kn, bv, mk, tri, lane_lo)
    S = [s_scr[p] for p in pairs]
    y = []
    for ch in range(WKV_STEP_CHUNKS):
        part = slice(ch * WKV_PAIRS, (ch + 1) * WKV_PAIRS)
        ys = _each(lambda q_, s_, y_: _dot_nt(q_, s_) + y_, qm[part], S, y0[part])
        y += _each(lambda t: t[:WKV_CHUNK, :] + t[WKV_CHUNK:, :], ys)
        S = _each(lambda s_, d_, m_, n_: s_ * d_ + _dot(s_, m_) + n_, S, dec[part], mt[part], nt[part])
    for p in pairs:
        s_scr[p] = S[p]
    tok = _gn_gate(y, r, kp, v, cut(g), rep(rk_ref), rep(gg_ref), rep(gb_ref), ones)
    for (rows, sl), t in zip(lanes, tok):
        o_ref[rows, sl] = t.astype(o_ref.dtype)

    @pl.when(c_idx == pl.num_programs(0) - 1)
    def _():
        sout_ref[...] = s_scr[...]


def _wkv_prompt(proj, w2p, a2p, g2, consts):
    T = proj.shape[0]
    rows = WKV_STEP_CHUNKS * WKV_CHUNK
    assert T % rows == 0
    lora_blk = (RWKV_IN_WIDTH - LORA_IN_WIDTH) // LORA_IN_WIDTH
    tok = lambda blk: pl.BlockSpec((rows, TOK_WIDTH), lambda c, blk=blk: (c, blk))
    full = lambda a: pl.BlockSpec(a.shape, lambda c: (0,) * a.ndim)
    weights = (w2p, a2p, g2) + tuple(consts)
    return pl.pallas_call(
        _wkv_kernel, grid=(T // rows,),
        in_specs=[tok(0), tok(1), tok(2), pl.BlockSpec((rows, LORA_IN_WIDTH), lambda c: (c, lora_blk))]
        + [full(a) for a in weights],
        out_specs=[tok(0), pl.BlockSpec((WKV_PAIRS, LANES, LANES), lambda c: (0, 0, 0))],
        out_shape=[jax.ShapeDtypeStruct((T, TOK_WIDTH), BF16),
                   jax.ShapeDtypeStruct((WKV_PAIRS, LANES, LANES), F32)],
        scratch_shapes=[pltpu.VMEM((WKV_PAIRS, LANES, LANES), F32)],
        compiler_params=_cparams("arbitrary"), name="wkv_chunked")(proj, proj, proj, proj, *weights)


WKV_STEP_UNROLL = 4


def _wkv_step_kernel(s_ref, r_ref, v_ref, lw_ref, kp_ref, kn_ref, bv_ref, g_ref, rk_ref, gg_ref, gb_ref,
                     so_ref, tok_ref, y_scr):
    n = RWKV_HEAD
    inv_n = 1.0 / n
    for hh in range(2):
        rows = slice(hh * n, (hh + 1) * n)
        a, w = -kn_ref[rows, :], jnp.exp(lw_ref[rows, :])
        b, k, r = bv_ref[rows, :], kp_ref[rows, :], r_ref[rows, :]

        def value_row(i, carry, hh=hh, a=a, w=w, b=b, k=k, r=r):
            s = s_ref[hh, i]
            sa = jnp.sum(s * a, axis=0, keepdims=True)
            s_new = s * w + sa * b + v_ref[pl.ds(hh * n + i, 1), :] * k
            so_ref[hh, i] = s_new
            y_scr[pl.ds(hh * n + i, 1), :] = jnp.sum(s_new * r, axis=0, keepdims=True)
            return carry

        lax.fori_loop(0, n, value_row, 0, unroll=WKV_STEP_UNROLL)
    for hh in range(2):
        rows = slice(hh * n, (hh + 1) * n)
        y = y_scr[rows, :]
        d = y - jnp.sum(y, axis=0, keepdims=True) * inv_n
        var = jnp.sum(d * d, axis=0, keepdims=True) * inv_n
        bonus = jnp.sum(r_ref[rows, :] * kp_ref[rows, :] * rk_ref[rows, :], axis=0, keepdims=True)
        yn = d * lax.rsqrt(var + GN_EPS) * gg_ref[rows, :] + gb_ref[rows, :]
        tok_ref[rows, :] = (yn + bonus * v_ref[rows, :]) * g_ref[rows, :]


def _wkv_step(state, layer, vecs, consts):
    B = state.shape[-1]
    sshape = (2, RWKV_HEAD, RWKV_HEAD, B)
    vblk = pl.BlockSpec((LANES, B), lambda p: (p, 0))
    return pl.pallas_call(
        _wkv_step_kernel, grid=(WKV_PAIRS,),
        in_specs=[pl.BlockSpec((None,) + sshape, lambda p: (layer, p, 0, 0, 0))] + [vblk] * 10,
        out_specs=[pl.BlockSpec(sshape, lambda p: (p, 0, 0, 0)), vblk],
        out_shape=[jax.ShapeDtypeStruct(state.shape[1:], F32), jax.ShapeDtypeStruct((TOK_WIDTH, B), F32)],
        scratch_shapes=[pltpu.VMEM((LANES, B), F32)],
        compiler_params=_cparams("parallel"), name="wkv_step")(state, *vecs, *consts)


STEP_INTERLEAVE = 4


def _softmax_rows(s):
    m = jnp.max(s, axis=-1, keepdims=True)
    e = jnp.exp(s - m)
    return e * (1.0 / jnp.sum(e, axis=-1, keepdims=True))


def _mem_attn_rows(q, k, v):
    heads = [slice(h * MEM_HEAD, (h + 1) * MEM_HEAD) for h in range(MEM_HEADS)]
    s = [_dot_nt(q[:, sl], k[:, sl]) * MEM_SCALE for sl in heads]
    p = [_softmax_rows(t) for t in s]
    return jnp.concatenate([_dot(t, v[:, sl]) for sl, t in zip(heads, p)], axis=1)


def _mem_attn_step_kernel(q_ref, k_ref, v_ref, o_ref, *, bs):
    rows = MEM_TOKENS * MEM_HEADS
    col_head = lax.broadcasted_iota(jnp.int32, (SUBLANES, rows), 1) % MEM_HEADS
    row_head = lax.broadcasted_iota(jnp.int32, (SUBLANES, rows), 0) % MEM_HEADS
    own = col_head == row_head

    def group(t, carry):
        bs_ = [t * STEP_INTERLEAVE + u for u in range(STEP_INTERLEAVE)]
        s = [_dot_nt(q_ref[b], k_ref[b]) * MEM_SCALE for b in bs_]
        p = [_softmax_rows(jnp.where(own, t_, -jnp.inf)) for t_ in s]
        for b, p_ in zip(bs_, p):
            o_ref[b] = _dot(p_, v_ref[b])
        return carry

    lax.fori_loop(0, bs // STEP_INTERLEAVE, group, 0)


def _mem_attn_step(q, mk, mv, layer, *, bs):
    B = q.shape[0]
    qblk = pl.BlockSpec((bs, SUBLANES, MEM_HEAD), lambda i: (i, 0, 0))
    cblk = pl.BlockSpec((None, bs, MEM_TOKENS * MEM_HEADS, MEM_HEAD), lambda i: (layer, i, 0, 0))
    return pl.pallas_call(
        functools.partial(_mem_attn_step_kernel, bs=bs), grid=(B // bs,),
        in_specs=[qblk, cblk, cblk], out_specs=qblk,
        out_shape=jax.ShapeDtypeStruct(q.shape, F32),
        compiler_params=_cparams("parallel"), name="mem_attn_step")(q, mk, mv)


def _deepnorm_ln(res, h, g, beta):
    z = ALPHA * res + h
    mu = jnp.mean(z, axis=-1, keepdims=True)
    d = z - mu
    var = jnp.mean(d * d, axis=-1, keepdims=True)
    return d * lax.rsqrt(var + LN_EPS) * g + beta


def _out_ln_kernel(tok_ref, q_ref, k_ref, v_ref, res_ref, toks_ref, mos_ref, ress_ref, w_ref, g_ref, beta_ref,
                   of_ref, ob_ref, ofs_ref, obs_ref):
    kt = tok_ref.shape[1]

    def project_ln(tok, mo, res):
        h = (jnp.dot(tok, w_ref[0:kt, :], preferred_element_type=F32)
             + jnp.dot(mo.astype(BF16), w_ref[kt:, :], preferred_element_type=F32))
        return _deepnorm_ln(res, h, g_ref[...], beta_ref[...])

    tm = res_ref.shape[0]
    half = tm // 2 if tm % (2 * SUBLANES) == 0 else tm
    k, v = k_ref[...], v_ref[...]
    for rows in (slice(r0, r0 + half) for r0 in range(0, tm, half)):
        out = project_ln(tok_ref[rows, :], _mem_attn_rows(q_ref[rows, :], k, v), res_ref[rows, :])
        of_ref[rows, :] = out
        ob_ref[rows, :] = out.astype(BF16)

    @pl.when(pl.program_id(0) == 0)
    def _():
        out = project_ln(toks_ref[...], mos_ref[...], ress_ref[...])
        ofs_ref[...] = out
        obs_ref[...] = out.astype(BF16)


def _out_ln(tok, proj, qblk, kv, res, tok_s, mo_s, res_s, w, layer, g, beta, *, tm):
    M, Kt = tok.shape
    Ms = res_s.shape[0]
    K, N = w.shape[1], w.shape[2]
    assert Kt + MEM_WIDTH == K and M % tm == 0 and tok_s.shape[1] == Kt and mo_s.shape[1] == MEM_WIDTH
    row = lambda i: (i, 0)
    cst = lambda i: (0, 0)
    in_specs = [pl.BlockSpec((tm, Kt), row), pl.BlockSpec((tm, MEM_WIDTH), lambda i: (i, qblk)),
                pl.BlockSpec((MEM_TOKENS, MEM_WIDTH), lambda i: (0, 0)),
                pl.BlockSpec((MEM_TOKENS, MEM_WIDTH), lambda i: (0, 1)),
                pl.BlockSpec((tm, N), row),
                pl.BlockSpec((Ms, Kt), cst), pl.BlockSpec((Ms, MEM_WIDTH), cst), pl.BlockSpec((Ms, N), cst),
                pl.BlockSpec((None, K, N), lambda i: (layer, 0, 0)),
                pl.BlockSpec((1, N), cst), pl.BlockSpec((1, N), cst)]
    oblk, sblk = pl.BlockSpec((tm, N), row), pl.BlockSpec((Ms, N), cst)
    of, ob, ofs, obs = pl.pallas_call(
        _out_ln_kernel, grid=(M // tm,),
        in_specs=in_specs, out_specs=[oblk, oblk, sblk, sblk],
        out_shape=[jax.ShapeDtypeStruct((M, N), F32), jax.ShapeDtypeStruct((M, N), BF16),
                   jax.ShapeDtypeStruct((Ms, N), F32), jax.ShapeDtypeStruct((Ms, N), BF16)],
        compiler_params=_cparams("arbitrary"), name="out_ln")(
            tok, proj, kv, kv, res, tok_s, mo_s, res_s, w, g, beta)
    return (of, ob), (ofs, obs)


def _ffn_down_ln_kernel(a_ref, as_ref, w_ref, res_ref, ress_ref, g_ref, beta_ref, o_ref, os_ref):
    i, k = pl.program_id(0), pl.program_id(1)
    last = pl.num_programs(1) - 1

    def step(lhs_ref, resid_ref, out_ref):
        part = jnp.dot(lhs_ref[...], w_ref[...], preferred_element_type=F32)

        @pl.when(k == 0)
        def _():
            out_ref[...] = part

        @pl.when((k > 0) & (k < last))
        def _():
            out_ref[...] += part

        @pl.when(k == last)
        def _():
            out_ref[...] = _deepnorm_ln(resid_ref[...], out_ref[...] + part, g_ref[...], beta_ref[...])

    step(a_ref, res_ref, o_ref)

    @pl.when(i == 0)
    def _():
        step(as_ref, ress_ref, os_ref)


def _ffn_down_ln(a, res, a_s, res_s, w, layer, g, beta, *, tm, tk):
    M, K = a.shape
    Ms = a_s.shape[0]
    N = w.shape[2]
    nk = K // tk
    assert w.shape[1] == K and K % tk == 0 and nk >= 2 and M % tm == 0 and a_s.shape[1] == K
    row = lambda i, k: (i, 0)
    cst = lambda i, k: (0, 0)
    return pl.pallas_call(
        _ffn_down_ln_kernel, grid=(M // tm, nk),
        in_specs=[pl.BlockSpec((tm, tk), lambda i, k: (i, k)),
                  pl.BlockSpec((Ms, tk), lambda i, k: (0, _rider_block(i, k, nk))),
                  pl.BlockSpec((None, tk, N), lambda i, k: (layer, k, 0)),
                  pl.BlockSpec((tm, N), row), pl.BlockSpec((Ms, N), cst),
                  pl.BlockSpec((1, N), cst), pl.BlockSpec((1, N), cst)],
        out_specs=[pl.BlockSpec((tm, N), row), pl.BlockSpec((Ms, N), cst)],
        out_shape=[jax.ShapeDtypeStruct((M, N), F32), jax.ShapeDtypeStruct((Ms, N), F32)],
        compiler_params=_cparams("arbitrary", "arbitrary"), name="ffn_down_ln")(
            a, a_s, w, res, res_s, g, beta)


def _ffn_up_kernel(x_ref, xs_ref, wg_ref, wu_ref, wd_ref, o_ref, os_ref, wdb_ref):
    wg, wu = wg_ref[...].astype(BF16), wu_ref[...].astype(BF16)

    def swiglu(x):
        gate = jnp.dot(x, wg, preferred_element_type=F32)
        up = jnp.dot(x, wu, preferred_element_type=F32)
        return (gate * jax.nn.sigmoid(gate) * up).astype(BF16)

    o_ref[...] = swiglu(x_ref[...])

    @pl.when(pl.program_id(0) == 0)
    def _():
        os_ref[...] = swiglu(xs_ref[...])
        wdb_ref[...] = wd_ref[...].astype(BF16)


def _ffn_up(x, xs, wg, wu, wd, layer, *, tm, tn):
    M, K = x.shape
    N = wg.shape[2]
    nj = N // tn
    assert wd.shape[1] == N
    wblk = pl.BlockSpec((None, K, tn), lambda i, j: (layer, 0, j))
    rider = lambda i, j: _rider_block(i, j, nj)
    return pl.pallas_call(
        _ffn_up_kernel, grid=(M // tm, nj),
        in_specs=[pl.BlockSpec((tm, K), lambda i, j: (i, 0)), pl.BlockSpec(xs.shape, lambda i, j: (0, 0)),
                  wblk, wblk, pl.BlockSpec((None, tn, wd.shape[2]), lambda i, j: (layer, rider(i, j), 0))],
        out_specs=[pl.BlockSpec((tm, tn), lambda i, j: (i, j)),
                   pl.BlockSpec((xs.shape[0], tn), lambda i, j: (0, rider(i, j))),
                   pl.BlockSpec((tn, wd.shape[2]), lambda i, j: (rider(i, j), 0))],
        out_shape=[jax.ShapeDtypeStruct((M, N), BF16), jax.ShapeDtypeStruct((xs.shape[0], N), BF16),
                   jax.ShapeDtypeStruct(wd.shape[1:], BF16)],
        compiler_params=_cparams("arbitrary", "arbitrary"), name="ffn_up")(x, xs, wg, wu, wd)


def _rope_kernel(x_ref, cos_ref, sin_ref, o_ref):
    x = x_ref[...]
    lane = lax.broadcasted_iota(jnp.int32, x.shape, 1)
    first_half = (lane % SWA_HEAD) < (SWA_HEAD // 2)
    partner = jnp.where(first_half, pltpu.roll(x, LANES - SWA_HEAD // 2, 1),
                        pltpu.roll(x, SWA_HEAD // 2, 1))
    o_ref[...] = x * cos_ref[...] + partner * sin_ref[...]


def _rope(proj, cos, sin_signed, *, tm):
    M = proj.shape[0]
    width = TOK_WIDTH + SWA_KV_WIDTH
    blk = pl.BlockSpec((tm, LANES), lambda i, s: (i, s))
    tab = pl.BlockSpec((tm, LANES), lambda i, s: (i, 0))
    return pl.pallas_call(
        _rope_kernel, grid=(M // tm, width // LANES),
        in_specs=[blk, tab, tab], out_specs=blk,
        out_shape=jax.ShapeDtypeStruct((M, width), F32),
        compiler_params=_cparams("parallel", "arbitrary"), name="rope")(proj, cos, sin_signed)


def _sink_column(sink_ref, base, rows_per_head, nheads):
    rows = rows_per_head * nheads
    hid = lax.broadcasted_iota(jnp.int32, (rows, 1), 0) // rows_per_head
    col = jnp.zeros((rows, 1), F32)
    for j in range(nheads):
        col = jnp.where(hid == j, sink_ref[base + j], col)
    return col


def _sink_softmax(s, sink):
    m = jnp.maximum(jnp.max(s, axis=-1, keepdims=True), sink)
    p = jnp.exp(s - m)
    return p * (1.0 / (jnp.sum(p, axis=-1, keepdims=True) + jnp.exp(sink - m)))


def _swa_kernel(sink_ref, q_ref, kc_ref, kp_ref, vc_ref, vp_ref, cosc_ref, sinc_ref, cosp_ref, sinp_ref,
                o_ref, krot_ref):
    n = pl.program_id(0)
    nslab_q = TOK_WIDTH // LANES
    slab = lambda ref, s: ref[:, s * LANES:(s + 1) * LANES]
    cos_c, sin_c, cos_p, sin_p = cosc_ref[...], sinc_ref[...], cosp_ref[...], sinp_ref[...]
    lane = lax.broadcasted_iota(jnp.int32, (BLOCK, LANES), 1)
    first_half = (lane % SWA_HEAD) < (SWA_HEAD // 2)
    lo = lane < SWA_HEAD

    def rope(x, cos, sin):
        partner = jnp.where(first_half, pltpu.roll(x, LANES - SWA_HEAD // 2, 1),
                            pltpu.roll(x, SWA_HEAD // 2, 1))
        return x * cos + partner * sin

    kv_slabs = SWA_KV_WIDTH // LANES
    k_cur = [rope(slab(kc_ref, j), cos_c, sin_c) for j in range(kv_slabs)]
    k_prev = [rope(slab(kp_ref, j), cos_p, sin_p) for j in range(kv_slabs)]
    for j in range(kv_slabs):
        krot_ref[:, j * LANES:(j + 1) * LANES] = k_cur[j]
    lane2 = lax.broadcasted_iota(jnp.int32, (2 * BLOCK, LANES), 1)
    kd, vd = [], []
    for g in range(SWA_KV_HEADS):
        j, half = divmod(g, 2)
        keep = (lane2 < SWA_HEAD) if half == 0 else (lane2 >= SWA_HEAD)
        dup = lambda t: jnp.where(keep, t, pltpu.roll(t, SWA_HEAD, 1)).astype(BF16)
        kd.append(dup(jnp.concatenate([k_prev[j], k_cur[j]], axis=0)))
        vd.append(dup(jnp.concatenate([slab(vp_ref, j), slab(vc_ref, j)], axis=0)))
    qi = lax.broadcasted_iota(jnp.int32, (2 * BLOCK, 2 * BLOCK), 0) % BLOCK
    si = lax.broadcasted_iota(jnp.int32, (2 * BLOCK, 2 * BLOCK), 1)
    valid = (si > qi) & (si <= qi + WINDOW) & ((n > 0) | (si >= BLOCK))
    row_lo = lax.broadcasted_iota(jnp.int32, (2 * BLOCK, 1), 0) < BLOCK
    slabs = list(range(nslab_q))
    kv_of = [(2 * s) // SWA_GROUP for s in slabs]
    assert math.frexp(SWA_SCALE)[0] == 0.5
    q = [rope(slab(q_ref, s), cos_c, sin_c) * SWA_SCALE for s in slabs]
    qs = [jnp.concatenate([jnp.where(lo, t, 0.0), jnp.where(lo, 0.0, t)], axis=0).astype(BF16) for t in q]
    sc = [lax.dot_general(t, kd[g], NT_DIMS, preferred_element_type=F32) for t, g in zip(qs, kv_of)]
    sc = [jnp.where(valid, t, -jnp.inf) for t in sc]
    p = [_sink_softmax(t, jnp.where(row_lo, sink_ref[2 * s], sink_ref[2 * s + 1])) for t, s in zip(sc, slabs)]
    o = [jnp.dot(t.astype(BF16), vd[g], preferred_element_type=F32) for t, g in zip(p, kv_of)]
    for s, t in zip(slabs, o):
        o_ref[:, s * LANES:(s + 1) * LANES] = jnp.where(lo, t[:BLOCK], t[BLOCK:]).astype(o_ref.dtype)


def _swa_prompt(proj, cos, sin_signed, sinks):
    T = proj.shape[0]
    kblk, vblk = TOK_WIDTH // SWA_KV_WIDTH, TOK_WIDTH // SWA_KV_WIDTH + 1
    prev = lambda n: jnp.maximum(n - 1, 0)
    kv_spec = lambda blk, row: pl.BlockSpec((BLOCK, SWA_KV_WIDTH), lambda n: (row(n), blk))
    tab = lambda row: pl.BlockSpec((BLOCK, LANES), lambda n: (row(n), 0))
    cur = lambda n: n
    return pl.pallas_call(
        _swa_kernel, grid=(T // BLOCK,),
        in_specs=[pl.BlockSpec(memory_space=pltpu.SMEM),
                  pl.BlockSpec((BLOCK, TOK_WIDTH), lambda n: (n, 0)),
                  kv_spec(kblk, cur), kv_spec(kblk, prev), kv_spec(vblk, cur), kv_spec(vblk, prev),
                  tab(cur), tab(cur), tab(prev), tab(prev)],
        out_specs=[pl.BlockSpec((BLOCK, TOK_WIDTH), lambda n: (n, 0)),
                   pl.BlockSpec((BLOCK, SWA_KV_WIDTH), lambda n: (n, 0))],
        out_shape=[jax.ShapeDtypeStruct((T, TOK_WIDTH), BF16),
                   jax.ShapeDtypeStruct((T, SWA_KV_WIDTH), F32)],
        compiler_params=_cparams("arbitrary"), name="swa_banded")(
            sinks, proj, proj, proj, proj, proj, cos, sin_signed, cos, sin_signed)


def _swa_step_kernel(sink_ref, q_ref, kn_ref, vn_ref, kt_ref, vt_ref, o_ref, kto_ref, vto_ref, *, bs):
    sink = _sink_column(sink_ref, 0, 1, SWA_Q_HEADS)
    newest = lax.broadcasted_iota(jnp.int32, (SWA_HEAD, WINDOW), 1) == WINDOW - 1
    kv_of_head = lax.broadcasted_iota(jnp.int32, (SWA_Q_HEADS, 1), 0) // SWA_GROUP
    kv_heads = list(range(SWA_KV_HEADS))

    def pick(per_kv):
        out = per_kv[0]
        for g in kv_heads[1:]:
            out = jnp.where(kv_of_head == g, per_kv[g], out)
        return out

    def slid(new_ref, cache_ref, b):
        new = new_ref[b]
        return [jnp.where(newest, new[:, g:g + 1], pltpu.roll(cache_ref[b, g], WINDOW - 1, 1)) for g in kv_heads]

    def group(t, carry):
        bs_ = [t * STEP_INTERLEAVE + u for u in range(STEP_INTERLEAVE)]
        kt = [slid(kn_ref, kt_ref, b) for b in bs_]
        vt = [slid(vn_ref, vt_ref, b) for b in bs_]
        for b, kt_, vt_ in zip(bs_, kt, vt):
            for g in kv_heads:
                kto_ref[b, g] = kt_[g]
                vto_ref[b, g] = vt_[g]
        s = [pick([_dot(q_ref[b], t_) for t_ in kt_]) * SWA_SCALE for b, kt_ in zip(bs_, kt)]
        p = [_sink_softmax(s_, sink) for s_ in s]
        for b, p_, vt_ in zip(bs_, p, vt):
            o_ref[b] = pick([_dot_nt(p_, t_) for t_ in vt_])
        return carry

    lax.fori_loop(0, bs // STEP_INTERLEAVE, group, 0)


def _swa_step(q, k_new, v_new, kt, vt, layer, sinks, *, bs):
    B = q.shape[0]
    qblk = pl.BlockSpec((bs, SWA_Q_HEADS, SWA_HEAD), lambda i: (i, 0, 0))
    nblk = pl.BlockSpec((bs, SWA_HEAD, SWA_KV_HEADS), lambda i: (i, 0, 0))
    cshape = (bs, SWA_KV_HEADS, SWA_HEAD, WINDOW)
    cin = pl.BlockSpec((None,) + cshape, lambda i: (layer, i, 0, 0, 0))
    cout = pl.BlockSpec(cshape, lambda i: (i, 0, 0, 0))
    cache = jax.ShapeDtypeStruct(kt.shape[1:], F32)
    return pl.pallas_call(
        functools.partial(_swa_step_kernel, bs=bs), grid=(B // bs,),
        in_specs=[pl.BlockSpec(memory_space=pltpu.SMEM), qblk, nblk, nblk, cin, cin],
        out_specs=[qblk, cout, cout],
        out_shape=[jax.ShapeDtypeStruct(q.shape, F32), cache, cache],
        compiler_params=_cparams("parallel"), name="swa_step")(sinks, q, k_new, v_new, kt, vt)


ROW_TILE = 512
FFN_ROW_TILE = 1024
COL_TILE = 512
STEP_BATCH = 16
FFN_DOWN_K_TILE = FFN_HIDDEN // 2


def _row_tile(m):
    return ROW_TILE if m % ROW_TILE == 0 else m


def _ffn_row_tile(m):
    return FFN_ROW_TILE if m % FFN_ROW_TILE == 0 else _row_tile(m)


def _pad_rows(w, rows):
    return jnp.pad(w, ((0, rows - w.shape[0]), (0, 0)))


def _pad_cols(w, cols):
    return jnp.pad(w, ((0, 0), (0, cols - w.shape[1])))


def _rwkv_weights(w_in, mu, w1, w2, a1, a2, g1, g2):
    w_lora = jnp.concatenate([_pad_cols(w1[0], LORA_PAD), _pad_cols(a1[0], LORA_PAD), g1[0]], axis=1)
    return dict(w_in=w_in.astype(BF16), w_lora=w_lora.astype(BF16), mu=mu[0][:, None, :],
                w2=_pad_rows(w2[0], LORA_PAD).astype(BF16), a2=_pad_rows(a2[0], LORA_PAD).astype(BF16),
                g2=g2[0].astype(BF16))


def _unblock_state(s_bd):
    n = RWKV_HEAD
    return jnp.stack([s_bd[:, :n, :n], s_bd[:, n:, n:]], axis=1).reshape(RWKV_HEADS, n, n)


def _rope_tables(pos):
    half = SWA_HEAD // 2
    inv = ROPE_THETA ** (-jnp.arange(half, dtype=F32) / half)
    ang = pos.astype(F32)[:, None] * inv[None, :]
    cos, sin = jnp.cos(ang), jnp.sin(ang)
    reps = LANES // SWA_HEAD
    return jnp.tile(cos, (1, 2 * reps)), jnp.tile(jnp.concatenate([-sin, sin], axis=1), (1, reps))


def _post_mixer(prompt, sample, sw, layer):
    row = lambda t: t[layer][None, :]
    tok_p, proj_p, qblk, kv, x_p = prompt
    tok_s, mo_s, x_s = sample
    (x1f_p, x1b_p), (x1f_s, x1b_s) = _out_ln(tok_p, proj_p, qblk, kv, x_p, tok_s, mo_s, x_s, sw["w_out"], layer,
                                             row(sw["ln1_g"]), row(sw["ln1_b"]), tm=_row_tile(x_p.shape[0]))
    hff_p, hff_s, wd = _ffn_up(x1b_p, x1b_s, sw["w_gate"], sw["w_up"], sw["w_down"], layer,
                               tm=_ffn_row_tile(x1b_p.shape[0]), tn=COL_TILE)
    return _ffn_down_ln(hff_p, x1f_p, hff_s, x1f_s, wd[None], 0, row(sw["ln2_g"]), row(sw["ln2_b"]),
                        tm=_row_tile(x1f_p.shape[0]), tk=FFN_DOWN_K_TILE)


def kernel(x_prompt, x_sample, mem_prompt, cache_mem_k, cache_mem_v, state_rwkv_shift, state_rwkv_wkv, cache_swa_k, cache_swa_v, w_in_rwkv, rwkv_mu, rwkv_w0, rwkv_w1, rwkv_w2, rwkv_a0, rwkv_a1, rwkv_a2, rwkv_g1, rwkv_g2, rwkv_k_k, rwkv_k_a, rwkv_r_k, rwkv_gn_g, rwkv_gn_b, w_in_swa, swa_sinks, w_mem_kv, w_out, ln1_g, ln1_b, w_gate, w_up, w_down, ln2_g, ln2_b):
    assert DEPTH == 2 and x_prompt.shape[0] == 1 and x_sample.shape[1] == 1
    T = x_prompt.shape[1]
    B = x_sample.shape[0]
    row = lambda t: t[None, :]
    shared = dict(w_out=w_out.astype(BF16), ln1_g=ln1_g, ln1_b=ln1_b, w_gate=w_gate, w_up=w_up,
                  w_down=w_down, ln2_g=ln2_g, ln2_b=ln2_b)
    RW = _rwkv_weights(w_in_rwkv, rwkv_mu, rwkv_w1, rwkv_w2, rwkv_a1, rwkv_a2, rwkv_g1, rwkv_g2)
    rk, gn_g, gn_b = row(rwkv_r_k[0].reshape(-1)), row(rwkv_gn_g[0]), row(rwkv_gn_b[0])
    prep_consts = (row(rwkv_w0[0]), row(rwkv_a0[0]), row(rwkv_k_k[0]), row(rwkv_k_a[0]))
    w_swa = w_in_swa.astype(BF16)
    sinks = swa_sinks[0]
    q_blk_rwkv = 3 * TOK_WIDTH // MEM_WIDTH
    q_blk_swa = (TOK_WIDTH + 2 * SWA_KV_WIDTH) // MEM_WIDTH

    xp, xs = x_prompt[0], x_sample[:, 0]
    tms = _row_tile(B)
    kv = [_proj(mem_prompt[0], w_mem_kv, i, tm=MEM_TOKENS, tn=COL_TILE, name="mem_kv") for i in range(DEPTH)]
    prompt_mem_k = jnp.stack([t[:, :MEM_WIDTH] for t in kv]).reshape(DEPTH, 1, MEM_TOKENS, MEM_HEADS, MEM_HEAD)
    prompt_mem_v = jnp.stack([t[:, MEM_WIDTH:] for t in kv]).reshape(DEPTH, 1, MEM_TOKENS, MEM_HEADS, MEM_HEAD)
    mem_k = cache_mem_k.reshape(DEPTH, B, MEM_TOKENS * MEM_HEADS, MEM_HEAD)
    mem_v = cache_mem_v.reshape(DEPTH, B, MEM_TOKENS * MEM_HEADS, MEM_HEAD)

    def mem_step(q, layer):
        q_rows = jnp.pad(q.reshape(B, MEM_HEADS, MEM_HEAD), ((0, 0), (0, SUBLANES - MEM_HEADS), (0, 0)))
        out = _mem_attn_step(q_rows, mem_k, mem_v, layer, bs=STEP_BATCH)
        return out[:, :MEM_HEADS].reshape(B, MEM_WIDTH).astype(BF16)

    proj_p, proj_s = _rwkv_in(xp, xs, state_rwkv_shift[0], RW["mu"], RW["w_in"], RW["w_lora"], 0,
                              tm=_ffn_row_tile(T))
    tok_p, s_bd = _wkv_prompt(proj_p, RW["w2"], RW["a2"], RW["g2"], prep_consts + (rk, gn_g, gn_b))
    prompt_shift = xp[-1][None, None, :]
    prompt_wkv = _unblock_state(s_bd)[None, None]
    vecs = _rwkv_prep_t(proj_s, RW["w2"], RW["a2"], RW["g2"], *prep_consts)
    lanes_b = lambda t: jnp.broadcast_to(t.reshape(TOK_WIDTH, 1), (TOK_WIDTH, B))
    s_new, tok_t = _wkv_step(state_rwkv_wkv.transpose(0, 2, 3, 4, 1), 0, vecs,
                             (lanes_b(rk), lanes_b(gn_g), lanes_b(gn_b)))
    mo_s = mem_step(proj_s[:, 3 * TOK_WIDTH:3 * TOK_WIDTH + MEM_WIDTH], 0)
    sample_shift = xs[None]
    sample_wkv = s_new.transpose(3, 0, 1, 2)[None]
    xf_p, xf_s = _post_mixer((tok_p, proj_p, q_blk_rwkv, kv[0], xp), (tok_t.T.astype(BF16), mo_s, xs), shared, 0)

    proj_p, proj_s = _proj(xf_p, w_swa, 0, tm=_ffn_row_tile(T), tn=COL_TILE, name="swa_proj", xs=xf_s)
    cos, sin = _rope_tables(jnp.arange(T))
    tok_p, k_rot = _swa_prompt(proj_p, cos, sin, sinks)
    v_last = proj_p[T - WINDOW:, TOK_WIDTH + SWA_KV_WIDTH:TOK_WIDTH + 2 * SWA_KV_WIDTH]
    prompt_swa_k = k_rot[T - WINDOW:].reshape(1, 1, WINDOW, SWA_KV_HEADS, SWA_HEAD)
    prompt_swa_v = v_last.reshape(1, 1, WINDOW, SWA_KV_HEADS, SWA_HEAD)
    cos, sin = _rope_tables(jnp.full((B,), PAST_LEN))
    qk = _rope(proj_s, cos, sin, tm=tms)
    chan_major = lambda t: t.reshape(B, SWA_KV_HEADS, SWA_HEAD).transpose(0, 2, 1)
    k_new = chan_major(qk[:, TOK_WIDTH:])
    v_new = chan_major(proj_s[:, TOK_WIDTH + SWA_KV_WIDTH:TOK_WIDTH + 2 * SWA_KV_WIDTH])
    o, kc, vc = _swa_step(qk[:, :TOK_WIDTH].reshape(B, SWA_Q_HEADS, SWA_HEAD), k_new, v_new,
                          cache_swa_k.transpose(0, 1, 3, 4, 2), cache_swa_v.transpose(0, 1, 3, 4, 2),
                          0, sinks, bs=STEP_BATCH)
    mo_s = mem_step(proj_s[:, TOK_WIDTH + 2 * SWA_KV_WIDTH:], 1)
    sample_swa_k, sample_swa_v = kc.transpose(0, 3, 1, 2)[None], vc.transpose(0, 3, 1, 2)[None]
    y_prompt, y_sample = _post_mixer((tok_p, proj_p, q_blk_swa, kv[1], xf_p),
                                     (o.reshape(B, TOK_WIDTH).astype(BF16), mo_s, xf_s), shared, 1)

    return (y_prompt[None], y_sample[:, None, :], prompt_mem_k, prompt_mem_v, prompt_shift, prompt_wkv,
            prompt_swa_k, prompt_swa_v, sample_shift, sample_wkv, sample_swa_k, sample_swa_v)
```

```python
import functools
import math

import jax
import jax.numpy as jnp
from jax import lax
from jax.experimental import pallas as pl
from jax.experimental.pallas import tpu as pltpu

D_MODEL = 2048
DEPTH = 2
MEM_WIDTH = D_MODEL // 4
TOK_WIDTH = D_MODEL - MEM_WIDTH
RWKV_HEAD = 64
RWKV_HEADS = TOK_WIDTH // RWKV_HEAD
GN_EPS = RWKV_HEAD * 1e-5
SWA_HEAD = 64
SWA_Q_HEADS = TOK_WIDTH // SWA_HEAD
SWA_KV_HEADS = 4
SWA_GROUP = SWA_Q_HEADS // SWA_KV_HEADS
SWA_KV_WIDTH = SWA_KV_HEADS * SWA_HEAD
WINDOW = 128
BLOCK = 128
PAST_LEN = 8192
SWA_SCALE = SWA_HEAD ** -0.5
ROPE_THETA = 10000.0
MEM_TOKENS = 256
MEM_HEADS = 4
MEM_HEAD = MEM_WIDTH // MEM_HEADS
MEM_SCALE = MEM_HEAD ** -0.5
FFN_HIDDEN = int(math.ceil(8 * D_MODEL / 3 / 256)) * 256
ALPHA = (2 * DEPTH) ** 0.25
LN_EPS = 1e-5
LORA_PAD = 128
LORA_IN_WIDTH = 512

LANES = 128
SUBLANES = 8
VMEM_LIMIT_BYTES = 56 * 1024 * 1024

BF16 = jnp.bfloat16
F32 = jnp.float32
NT_DIMS = (((1,), (1,)), ((), ()))
TN_DIMS = (((0,), (0,)), ((), ()))


def _dot(a, b):
    return jnp.dot(a.astype(BF16), b.astype(BF16), preferred_element_type=F32)


def _dot_nt(a, b):
    return lax.dot_general(a.astype(BF16), b.astype(BF16), NT_DIMS, preferred_element_type=F32)


def _dot_tn(a, b):
    return lax.dot_general(a.astype(BF16), b.astype(BF16), TN_DIMS, preferred_element_type=F32)


def _split_dot(x, m):
    hi = x.astype(BF16)
    lo = (x - hi.astype(F32)).astype(BF16)
    return (jnp.dot(hi, m, preferred_element_type=F32)
            + jnp.dot(lo, m, preferred_element_type=F32))


def _head_ones():
    p = lax.broadcasted_iota(jnp.int32, (LANES, LANES), 0)
    q = lax.broadcasted_iota(jnp.int32, (LANES, LANES), 1)
    return ((p // RWKV_HEAD) == (q // RWKV_HEAD)).astype(BF16)


def _cparams(*sem):
    return pltpu.CompilerParams(dimension_semantics=sem, vmem_limit_bytes=VMEM_LIMIT_BYTES)


def _rider_block(i, j, nj):
    return jnp.where(i == 0, j, nj - 1)


def _proj_kernel(*refs, rider):
    if rider:
        x_ref, xs_ref, w_ref, o_ref, os_ref, xb_ref = refs
    else:
        x_ref, w_ref, o_ref, xb_ref = refs

    @pl.when(pl.program_id(1) == 0)
    def _():
        xb_ref[...] = x_ref[...].astype(BF16)

    w = w_ref[...].astype(BF16)
    o_ref[...] = jnp.dot(xb_ref[...], w, preferred_element_type=F32)
    if rider:
        @pl.when(pl.program_id(0) == 0)
        def _():
            os_ref[...] = jnp.dot(xs_ref[...].astype(BF16), w, preferred_element_type=F32)


def _proj(x, w, layer, *, tm, tn, name, xs=None):
    M, K = x.shape
    N = w.shape[2]
    assert M % tm == 0 and N % tn == 0
    nj = N // tn
    rider = xs is not None
    in_specs = [pl.BlockSpec((tm, K), lambda i, j: (i, 0))]
    out_specs = [pl.BlockSpec((tm, tn), lambda i, j: (i, j))]
    out_shape = [jax.ShapeDtypeStruct((M, N), F32)]
    if rider:
        in_specs.append(pl.BlockSpec(xs.shape, lambda i, j: (0, 0)))
        out_specs.append(pl.BlockSpec((xs.shape[0], tn), lambda i, j: (0, _rider_block(i, j, nj))))
        out_shape.append(jax.ShapeDtypeStruct((xs.shape[0], N), F32))
    in_specs.append(pl.BlockSpec((None, K, tn), lambda i, j: (layer, 0, j)))
    out = pl.pallas_call(
        functools.partial(_proj_kernel, rider=rider), grid=(M // tm, nj),
        in_specs=in_specs, out_specs=out_specs, out_shape=out_shape,
        scratch_shapes=[pltpu.VMEM((tm, K), BF16)],
        compiler_params=_cparams("arbitrary", "arbitrary"), name=name)(*([x, xs, w] if rider else [x, w]))
    return out if rider else out[0]


MIX_R, MIX_W, MIX_K, MIX_V, MIX_A, MIX_G = range(6)
RWKV_IN_WIDTH = 3 * TOK_WIDTH + MEM_WIDTH + LORA_IN_WIDTH
RWKV_IN_TILE = LORA_IN_WIDTH
RWKV_IN_KEPT = (MIX_R, MIX_K, MIX_V)
RWKV_IN_SLOT = ([0] * (TOK_WIDTH // RWKV_IN_TILE) + [1] * (TOK_WIDTH // RWKV_IN_TILE)
                + [2] * (TOK_WIDTH // RWKV_IN_TILE) + [3] * (MEM_WIDTH // RWKV_IN_TILE))
LORA_PARTS = ((MIX_W, 0, LORA_PAD), (MIX_A, LORA_PAD, 2 * LORA_PAD), (MIX_G, 2 * LORA_PAD, LORA_IN_WIDTH))
RWKV_IN_SUB = 256


def _rwkv_in_kernel(slot_ref, x_ref, xp_ref, xs_ref, xps_ref, mu_ref, w_ref, wl_ref, o_ref, os_ref,
                    lhs_ref, lhss_ref):
    i, j = pl.program_id(0), pl.program_id(1)
    tm = x_ref.shape[0]
    sub = min(RWKV_IN_SUB, tm)
    blocks = [slice(s0, s0 + sub) for s0 in range(0, tm, sub)]
    plain = len(RWKV_IN_KEPT)
    nproj = pl.num_programs(1) - 1

    def x_and_delta(rows):
        x = x_ref[rows, :]
        if rows.start == 0:
            first = jnp.where(i > 0, xp_ref[SUBLANES - 1:SUBLANES, :], 0.0)
        else:
            first = x_ref[rows.start - 1:rows.start, :]
        rowid = lax.broadcasted_iota(jnp.int32, (sub, 1), 0)
        return x, jnp.where(rowid == 0, first, pltpu.roll(x, 1, 0)) - x

    def rider_x_and_delta(rows):
        x = xs_ref[...]
        return x, xps_ref[...] - x

    def group(delta_fn, row_blocks, kept_ref, out_ref):
        @pl.when(j == 0)
        def _():
            for rows in row_blocks:
                x, d = delta_fn(rows)
                for slot, m in enumerate(RWKV_IN_KEPT):
                    kept_ref[slot, rows, :] = (x + d * mu_ref[m]).astype(BF16)
                kept_ref[plain, rows, :] = x.astype(BF16)

        @pl.when(j < nproj)
        def _():
            out_ref[...] = jnp.dot(kept_ref[slot_ref[j]], w_ref[...], preferred_element_type=F32)

        @pl.when(j == nproj)
        def _():
            for rows in row_blocks:
                x, d = delta_fn(rows)
                for m, lo, hi in LORA_PARTS:
                    out_ref[rows, lo:hi] = jnp.dot((x + d * mu_ref[m]).astype(BF16), wl_ref[:, lo:hi],
                                                   preferred_element_type=F32)

    group(x_and_delta, blocks, lhs_ref, o_ref)

    @pl.when(i == 0)
    def _():
        group(rider_x_and_delta, [slice(None)], lhss_ref, os_ref)


def _rwkv_in(x, xs, xprev_s, mu, w, w_lora, layer, *, tm):
    M, K = x.shape
    Ms = xs.shape[0]
    tn = RWKV_IN_TILE
    nproj = w.shape[2] // tn
    assert M % tm == 0 and w.shape[2] + w_lora.shape[1] == RWKV_IN_WIDTH and nproj == len(RWKV_IN_SLOT)
    assert tm % min(RWKV_IN_SUB, tm) == 0 and tm % SUBLANES == 0
    rows8 = tm // SUBLANES
    slot = jnp.asarray(RWKV_IN_SLOT + [0], jnp.int32)
    cst = lambda i, j, m: (0, 0)
    grid_spec = pltpu.PrefetchScalarGridSpec(
        num_scalar_prefetch=1, grid=(M // tm, nproj + 1),
        in_specs=[pl.BlockSpec((tm, K), lambda i, j, m: (i, 0)),
                  pl.BlockSpec((SUBLANES, K), lambda i, j, m: (jnp.maximum(i * rows8 - 1, 0), 0)),
                  pl.BlockSpec((Ms, K), cst), pl.BlockSpec((Ms, K), cst),
                  pl.BlockSpec(mu.shape, lambda i, j, m: (0, 0, 0)),
                  pl.BlockSpec((None, K, tn), lambda i, j, m: (layer, 0, jnp.minimum(j, nproj - 1))),
                  pl.BlockSpec(w_lora.shape, cst)],
        out_specs=[pl.BlockSpec((tm, tn), lambda i, j, m: (i, j)),
                   pl.BlockSpec((Ms, tn), lambda i, j, m: (0, _rider_block(i, j, nproj + 1)))],
        scratch_shapes=[pltpu.VMEM((len(RWKV_IN_KEPT) + 1, tm, K), BF16),
                        pltpu.VMEM((len(RWKV_IN_KEPT) + 1, Ms, K), BF16)])
    return pl.pallas_call(
        _rwkv_in_kernel, grid_spec=grid_spec,
        out_shape=[jax.ShapeDtypeStruct((M, RWKV_IN_WIDTH), F32), jax.ShapeDtypeStruct((Ms, RWKV_IN_WIDTH), F32)],
        compiler_params=_cparams("arbitrary", "arbitrary"), name="rwkv_in")(slot, x, x, xs, xprev_s, mu, w, w_lora)


def _softplus(z):
    return jnp.maximum(z, 0.0) + jnp.log1p(jnp.exp(-jnp.abs(z)))


def _each(f, *lists):
    return [f(*a) for a in zip(*lists)]


def _lora_hidden(hl):
    return jnp.tanh(hl[:, 0:LORA_PAD]), hl[:, LORA_PAD:2 * LORA_PAD], jax.nn.sigmoid(hl[:, 2 * LORA_PAD:])


def _prep_slabs(k, wl, al, w0, a0, k_k, k_a, ones):
    lw = _each(lambda wl_, w0_: -jnp.exp(-_softplus(-(w0_ + wl_)) - 0.5), wl, w0)
    agate = _each(lambda al_, a0_: jax.nn.sigmoid(a0_ + al_), al, a0)
    kkr = _each(lambda k_, c_: k_ * c_, k, k_k)
    ss = _each(lambda t: _split_dot(t * t, ones), kkr)
    kn = _each(lambda t, s_: t / jnp.maximum(jnp.sqrt(s_), 1e-12), kkr, ss)
    bv = _each(lambda n_, a_: n_ * a_, kn, agate)
    kp = _each(lambda k_, a_, c_: k_ * (1.0 + (a_ - 1.0) * c_), k, agate, k_a)
    return lw, kp, kn, bv


def _rwkv_prep_t_kernel(hl_ref, r_ref, k_ref, v_ref, w2_ref, a2_ref, g2_ref, w0_ref, a0_ref, kk_ref, ka_ref,
                        rt_ref, vt_ref, lw_ref, kp_ref, kn_ref, bv_ref, g_ref):
    hw, ha, hg = _lora_hidden(hl_ref[...])
    outs = _prep_slabs([k_ref[...]], [_dot(hw, w2_ref[...])], [_dot(ha, a2_ref[...])], [w0_ref[...]],
                       [a0_ref[...]], [kk_ref[...]], [ka_ref[...]], _head_ones())
    vals = [r_ref[...], v_ref[...]] + [t[0] for t in outs] + [_dot(hg, g2_ref[...])]
    for ref, val in zip((rt_ref, vt_ref, lw_ref, kp_ref, kn_ref, bv_ref, g_ref), vals):
        ref[...] = val.T


def _rwkv_prep_t(proj, w2p, a2p, g2, w0, a0, k_k, k_a):
    B = proj.shape[0]
    assert B == LANES
    nslab = TOK_WIDTH // LANES
    lora_blk = (RWKV_IN_WIDTH - LORA_IN_WIDTH) // LORA_IN_WIDTH
    col = lambda s: (0, s)
    slab = lambda base: pl.BlockSpec((B, LANES), lambda s, base=base: (0, base + s))
    outs = [jax.ShapeDtypeStruct((TOK_WIDTH, B), F32)] * 7
    return pl.pallas_call(
        _rwkv_prep_t_kernel, grid=(nslab,),
        in_specs=[pl.BlockSpec((B, LORA_IN_WIDTH), lambda s: (0, lora_blk)),
                  slab(0), slab(nslab), slab(2 * nslab),
                  pl.BlockSpec((LORA_PAD, LANES), col),
                  pl.BlockSpec((LORA_PAD, LANES), col),
                  pl.BlockSpec((2 * LORA_PAD, LANES), col),
                  pl.BlockSpec((1, LANES), col), pl.BlockSpec((1, LANES), col),
                  pl.BlockSpec((1, LANES), col), pl.BlockSpec((1, LANES), col)],
        out_specs=[pl.BlockSpec((LANES, B), lambda s: (s, 0))] * 7,
        out_shape=outs, compiler_params=_cparams("arbitrary"),
        name="rwkv_prep_t")(proj, proj, proj, proj, w2p, a2p, g2, w0, a0, k_k, k_a)


def _gn_gate(y, r, kp, v, g, rk, gg, gb, ones):
    inv_n = 1.0 / RWKV_HEAD
    rows = y[0].shape[0]
    sums = _each(lambda y_, r_, k_, rk_: _split_dot(jnp.concatenate([y_, r_ * k_ * rk_], axis=0), ones),
                 y, r, kp, rk)
    d = _each(lambda y_, s_: y_ - s_[:rows] * inv_n, y, sums)
    var = _each(lambda d_: _split_dot(d_ * d_, ones) * inv_n, d)
    return _each(lambda d_, var_, gg_, gb_, s_, v_, g_:
                 (d_ * lax.rsqrt(var_ + GN_EPS) * gg_ + gb_ + s_[rows:] * v_) * g_,
                 d, var, gg, gb, sums, v, g)


WKV_CHUNK = 64


def _wkv_masks():
    n = 2 * WKV_CHUNK
    p = lax.broadcasted_iota(jnp.int32, (n, n), 0)
    q = lax.broadcasted_iota(jnp.int32, (n, n), 1)
    same = lambda b: (p // b) == (q // b)
    pt, qt = p % WKV_CHUNK, q % WKV_CHUNK
    s8, s16, s32, s64 = same(8), same(16), same(32), same(WKV_CHUNK)
    return dict(strict=s64 & (pt > qt), incl=s64 & (pt >= qt), s8=s8,
                e16=s16 & ~s8, e32=s32 & ~s16, e64=s64 & ~s32,
                eye=(p == q).astype(F32))


WKV_PAIRS = TOK_WIDTH // LANES
WKV_STEP_CHUNKS = 2


def _wkv_lanes(r, lw, k, v, kn, bv, mk, tri, lane_lo):
    stack = lambda x: jnp.concatenate([jnp.where(lane_lo, x, 0.0), jnp.where(lane_lo, 0.0, x)], axis=0)
    n = 2 * WKV_CHUNK
    c = _each(lambda t: _split_dot_left(tri, t), lw)
    c_last = _each(lambda t: t[WKV_CHUNK - 1:WKV_CHUNK, :], c)
    e_out = _each(lambda t: jnp.exp(-t), c)
    e_end = _each(lambda t, tl: jnp.exp(tl - t), c, c_last)
    ah = _each(lambda kn_, c_, lw_: stack(-kn_ * jnp.exp(c_ - lw_)), kn, c, lw)
    rh = _each(lambda r_, c_: stack(r_ * jnp.exp(c_)), r, c)
    bh = _each(lambda b_, e_: stack(b_ * e_), bv, e_out)
    kh = _each(lambda k_, e_: stack(k_ * e_), k, e_out)
    bbar = _each(lambda b_, e_: stack(b_ * e_), bv, e_end)
    kbar = _each(lambda k_, e_: stack(k_ * e_), k, e_end)
    vs = _each(stack, v)
    gm = _each(lambda a_, r_, b_, k_: _dot_nt(jnp.concatenate([a_, r_], axis=0),
                                              jnp.concatenate([b_, k_], axis=0)), ah, rh, bh, kh)
    a_ab = _each(lambda g_: jnp.where(mk["strict"], g_[:n, :n], 0.0), gm)
    a_ak = _each(lambda g_: jnp.where(mk["strict"], g_[:n, n:], 0.0), gm)
    l_rb = _each(lambda g_: jnp.where(mk["incl"], g_[n:, :n], 0.0), gm)
    l_rk = _each(lambda g_: jnp.where(mk["incl"], g_[n:, n:], 0.0), gm)
    d1 = _each(lambda a_: jnp.where(mk["s8"], a_, 0.0), a_ab)
    x = _each(lambda d_: mk["eye"] + d_, d1)
    d2 = _each(lambda d_: _dot(d_, d_), d1)
    x = _each(lambda x_, d_: x_ + _dot(x_, d_), x, d2)
    d4 = _each(lambda d_: _dot(d_, d_), d2)
    x = _each(lambda x_, d_: x_ + _dot(x_, d_), x, d4)
    for lvl in ("e16", "e32", "e64"):
        ex = _each(lambda a_, x_: _dot(jnp.where(mk[lvl], a_, 0.0), x_), a_ab, x)
        x = _each(lambda x_, e_: x_ + _dot(x_, e_), x, ex)
    av = _each(_dot, a_ak, vs)
    tw = _each(lambda x_, a_, v_: _dot(x_, jnp.concatenate([a_, v_], axis=1)), x, ah, av)
    lwm = _each(_dot, l_rb, tw)
    lv = _each(_dot, l_rk, vs)
    qm = _each(lambda r_, l_: r_ + l_[:, :n], rh, lwm)
    y0 = _each(lambda l_, v_: l_[:, n:] + v_, lwm, lv)
    mt = _each(lambda t_, b_: _dot_tn(t_[:, :n], b_), tw, bbar)
    nt = _each(lambda t_, b_, v_, k_: _dot_tn(t_[:, n:], b_) + _dot_tn(v_, k_), tw, bbar, vs, kbar)
    return qm, y0, mt, nt, _each(jnp.exp, c_last)


def _split_dot_left(m, x):
    hi = x.astype(BF16)
    lo = (x - hi.astype(F32)).astype(BF16)
    return (jnp.dot(m, hi, preferred_element_type=F32)
            + jnp.dot(m, lo, preferred_element_type=F32))


def _wkv_kernel(r_ref, k_ref, v_ref, hl_ref, w2_ref, a2_ref, g2_ref, w0_ref, a0_ref, kk_ref, ka_ref,
                rk_ref, gg_ref, gb_ref, o_ref, sout_ref, s_scr):
    c_idx = pl.program_id(0)

    @pl.when(c_idx == 0)
    def _():
        s_scr[...] = jnp.zeros_like(s_scr)

    mk = _wkv_masks()
    ti = lax.broadcasted_iota(jnp.int32, (WKV_CHUNK, WKV_CHUNK), 0)
    tj = lax.broadcasted_iota(jnp.int32, (WKV_CHUNK, WKV_CHUNK), 1)
    tri = (ti >= tj).astype(BF16)
    lane_lo = lax.broadcasted_iota(jnp.int32, (WKV_CHUNK, LANES), 1) < RWKV_HEAD
    ones = _head_ones()
    pairs = range(WKV_PAIRS)
    lanes = [(slice(ch * WKV_CHUNK, (ch + 1) * WKV_CHUNK), slice(p * LANES, (p + 1) * LANES))
             for ch in range(WKV_STEP_CHUNKS) for p in pairs]
    cut = lambda t: [t[rows, sl] for rows, sl in lanes]
    rep = lambda t: [t[:, sl] for _, sl in lanes]
    hw, ha, hg = _lora_hidden(hl_ref[...])
    wl, al, g = _dot(hw, w2_ref[...]), _dot(ha, a2_ref[...]), _dot(hg, g2_ref[...])
    r, v = cut(r_ref), cut(v_ref)
    lw, kp, kn, bv = _prep_slabs(cut(k_ref), cut(wl), cut(al), rep(w0_ref), rep(a0_ref),
                                 rep(kk_ref), rep(ka_ref), ones)
    qm, y0, mt, nt, dec = _wkv_lanes(r, lw, kp, v, kn, bv, mk, tri, lane_lo)
    S = [s_scr[p] for p in pairs]
    y = []
    for ch in range(WKV_STEP_CHUNKS):
        part = slice(ch * WKV_PAIRS, (ch + 1) * WKV_PAIRS)
        ys = _each(lambda q_, s_, y_: _dot_nt(q_, s_) + y_, qm[part], S, y0[part])
        y += _each(lambda t: t[:WKV_CHUNK, :] + t[WKV_CHUNK:, :], ys)
        S = _each(lambda s_, d_, m_, n_: s_ * d_ + _dot(s_, m_) + n_, S, dec[part], mt[part], nt[part])
    for p in pairs:
        s_scr[p] = S[p]
    tok = _gn_gate(y, r, kp, v, cut(g), rep(rk_ref), rep(gg_ref), rep(gb_ref), ones)
    for (rows, sl), t in zip(lanes, tok):
        o_ref[rows, sl] = t.astype(o_ref.dtype)

    @pl.when(c_idx == pl.num_programs(0) - 1)
    def _():
        sout_ref[...] = s_scr[...]


def _wkv_prompt(proj, w2p, a2p, g2, consts):
    T = proj.shape[0]
    rows = WKV_STEP_CHUNKS * WKV_CHUNK
    assert T % rows == 0
    lora_blk = (RWKV_IN_WIDTH - LORA_IN_WIDTH) // LORA_IN_WIDTH
    tok = lambda blk: pl.BlockSpec((rows, TOK_WIDTH), lambda c, blk=blk: (c, blk))
    full = lambda a: pl.BlockSpec(a.shape, lambda c: (0,) * a.ndim)
    weights = (w2p, a2p, g2) + tuple(consts)
    return pl.pallas_call(
        _wkv_kernel, grid=(T // rows,),
        in_specs=[tok(0), tok(1), tok(2), pl.BlockSpec((rows, LORA_IN_WIDTH), lambda c: (c, lora_blk))]
        + [full(a) for a in weights],
        out_specs=[tok(0), pl.BlockSpec((WKV_PAIRS, LANES, LANES), lambda c: (0, 0, 0))],
        out_shape=[jax.ShapeDtypeStruct((T, TOK_WIDTH), BF16),
                   jax.ShapeDtypeStruct((WKV_PAIRS, LANES, LANES), F32)],
        scratch_shapes=[pltpu.VMEM((WKV_PAIRS, LANES, LANES), F32)],
        compiler_params=_cparams("arbitrary"), name="wkv_chunked")(proj, proj, proj, proj, *weights)


WKV_STEP_UNROLL = 4


def _wkv_step_kernel(s_ref, r_ref, v_ref, lw_ref, kp_ref, kn_ref, bv_ref, g_ref, rk_ref, gg_ref, gb_ref,
                     so_ref, tok_ref, y_scr):
    n = RWKV_HEAD
    inv_n = 1.0 / n
    for hh in range(2):
        rows = slice(hh * n, (hh + 1) * n)
        a, w = -kn_ref[rows, :], jnp.exp(lw_ref[rows, :])
        b, k, r = bv_ref[rows, :], kp_ref[rows, :], r_ref[rows, :]

        def value_row(i, carry, hh=hh, a=a, w=w, b=b, k=k, r=r):
            s = s_ref[hh, i]
            sa = jnp.sum(s * a, axis=0, keepdims=True)
            s_new = s * w + sa * b + v_ref[pl.ds(hh * n + i, 1), :] * k
            so_ref[hh, i] = s_new
            y_scr[pl.ds(hh * n + i, 1), :] = jnp.sum(s_new * r, axis=0, keepdims=True)
            return carry

        lax.fori_loop(0, n, value_row, 0, unroll=WKV_STEP_UNROLL)
    for hh in range(2):
        rows = slice(hh * n, (hh + 1) * n)
        y = y_scr[rows, :]
        d = y - jnp.sum(y, axis=0, keepdims=True) * inv_n
        var = jnp.sum(d * d, axis=0, keepdims=True) * inv_n
        bonus = jnp.sum(r_ref[rows, :] * kp_ref[rows, :] * rk_ref[rows, :], axis=0, keepdims=True)
        yn = d * lax.rsqrt(var + GN_EPS) * gg_ref[rows, :] + gb_ref[rows, :]
        tok_ref[rows, :] = (yn + bonus * v_ref[rows, :]) * g_ref[rows, :]


def _wkv_step(state, layer, vecs, consts):
    B = state.shape[-1]
    sshape = (2, RWKV_HEAD, RWKV_HEAD, B)
    vblk = pl.BlockSpec((LANES, B), lambda p: (p, 0))
    return pl.pallas_call(
        _wkv_step_kernel, grid=(WKV_PAIRS,),
        in_specs=[pl.BlockSpec((None,) + sshape, lambda p: (layer, p, 0, 0, 0))] + [vblk] * 10,
        out_specs=[pl.BlockSpec(sshape, lambda p: (p, 0, 0, 0)), vblk],
        out_shape=[jax.ShapeDtypeStruct(state.shape[1:], F32), jax.ShapeDtypeStruct((TOK_WIDTH, B), F32)],
        scratch_shapes=[pltpu.VMEM((LANES, B), F32)],
        compiler_params=_cparams("parallel"), name="wkv_step")(state, *vecs, *consts)


STEP_INTERLEAVE = 4


def _softmax_rows(s):
    m = jnp.max(s, axis=-1, keepdims=True)
    e = jnp.exp(s - m)
    return e * (1.0 / jnp.sum(e, axis=-1, keepdims=True))


def _mem_attn_rows(q, k, v):
    heads = [slice(h * MEM_HEAD, (h + 1) * MEM_HEAD) for h in range(MEM_HEADS)]
    s = [_dot_nt(q[:, sl], k[:, sl]) * MEM_SCALE for sl in heads]
    p = [_softmax_rows(t) for t in s]
    return jnp.concatenate([_dot(t, v[:, sl]) for sl, t in zip(heads, p)], axis=1)


def _mem_attn_step_kernel(q_ref, k_ref, v_ref, o_ref, *, bs):
    rows = MEM_TOKENS * MEM_HEADS
    col_head = lax.broadcasted_iota(jnp.int32, (SUBLANES, rows), 1) % MEM_HEADS
    row_head = lax.broadcasted_iota(jnp.int32, (SUBLANES, rows), 0) % MEM_HEADS
    own = col_head == row_head

    def group(t, carry):
        bs_ = [t * STEP_INTERLEAVE + u for u in range(STEP_INTERLEAVE)]
        s = [_dot_nt(q_ref[b], k_ref[b]) * MEM_SCALE for b in bs_]
        p = [_softmax_rows(jnp.where(own, t_, -jnp.inf)) for t_ in s]
        for b, p_ in zip(bs_, p):
            o_ref[b] = _dot(p_, v_ref[b])
        return carry

    lax.fori_loop(0, bs // STEP_INTERLEAVE, group, 0)


def _mem_attn_step(q, mk, mv, layer, *, bs):
    B = q.shape[0]
    qblk = pl.BlockSpec((bs, SUBLANES, MEM_HEAD), lambda i: (i, 0, 0))
    cblk = pl.BlockSpec((None, bs, MEM_TOKENS * MEM_HEADS, MEM_HEAD), lambda i: (layer, i, 0, 0))
    return pl.pallas_call(
        functools.partial(_mem_attn_step_kernel, bs=bs), grid=(B // bs,),
        in_specs=[qblk, cblk, cblk], out_specs=qblk,
        out_shape=jax.ShapeDtypeStruct(q.shape, F32),
        compiler_params=_cparams("parallel"), name="mem_attn_step")(q, mk, mv)


def _deepnorm_ln(res, h, g, beta):
    z = ALPHA * res + h
    mu = jnp.mean(z, axis=-1, keepdims=True)
    d = z - mu
    var = jnp.mean(d * d, axis=-1, keepdims=True)
    return d * lax.rsqrt(var + LN_EPS) * g + beta


def _out_ln_kernel(tok_ref, q_ref, k_ref, v_ref, res_ref, toks_ref, mos_ref, ress_ref, w_ref, g_ref, beta_ref,
                   of_ref, ob_ref, ofs_ref, obs_ref):
    kt = tok_ref.shape[1]

    def project_ln(tok, mo, res):
        h = (jnp.dot(tok, w_ref[0:kt, :], preferred_element_type=F32)
             + jnp.dot(mo.astype(BF16), w_ref[kt:, :], preferred_element_type=F32))
        return _deepnorm_ln(res, h, g_ref[...], beta_ref[...])

    tm = res_ref.shape[0]
    half = tm // 2 if tm % (2 * SUBLANES) == 0 else tm
    k, v = k_ref[...], v_ref[...]
    for rows in (slice(r0, r0 + half) for r0 in range(0, tm, half)):
        out = project_ln(tok_ref[rows, :], _mem_attn_rows(q_ref[rows, :], k, v), res_ref[rows, :])
        of_ref[rows, :] = out
        ob_ref[rows, :] = out.astype(BF16)

    @pl.when(pl.program_id(0) == 0)
    def _():
        out = project_ln(toks_ref[...], mos_ref[...], ress_ref[...])
        ofs_ref[...] = out
        obs_ref[...] = out.astype(BF16)


def _out_ln(tok, proj, qblk, kv, res, tok_s, mo_s, res_s, w, layer, g, beta, *, tm):
    M, Kt = tok.shape
    Ms = res_s.shape[0]
    K, N = w.shape[1], w.shape[2]
    assert Kt + MEM_WIDTH == K and M % tm == 0 and tok_s.shape[1] == Kt and mo_s.shape[1] == MEM_WIDTH
    row = lambda i: (i, 0)
    cst = lambda i: (0, 0)
    in_specs = [pl.BlockSpec((tm, Kt), row), pl.BlockSpec((tm, MEM_WIDTH), lambda i: (i, qblk)),
                pl.BlockSpec((MEM_TOKENS, MEM_WIDTH), lambda i: (0, 0)),
                pl.BlockSpec((MEM_TOKENS, MEM_WIDTH), lambda i: (0, 1)),
                pl.BlockSpec((tm, N), row),
                pl.BlockSpec((Ms, Kt), cst), pl.BlockSpec((Ms, MEM_WIDTH), cst), pl.BlockSpec((Ms, N), cst),
                pl.BlockSpec((None, K, N), lambda i: (layer, 0, 0)),
                pl.BlockSpec((1, N), cst), pl.BlockSpec((1, N), cst)]
    oblk, sblk = pl.BlockSpec((tm, N), row), pl.BlockSpec((Ms, N), cst)
    of, ob, ofs, obs = pl.pallas_call(
        _out_ln_kernel, grid=(M // tm,),
        in_specs=in_specs, out_specs=[oblk, oblk, sblk, sblk],
        out_shape=[jax.ShapeDtypeStruct((M, N), F32), jax.ShapeDtypeStruct((M, N), BF16),
                   jax.ShapeDtypeStruct((Ms, N), F32), jax.ShapeDtypeStruct((Ms, N), BF16)],
        compiler_params=_cparams("arbitrary"), name="out_ln")(
            tok, proj, kv, kv, res, tok_s, mo_s, res_s, w, g, beta)
    return (of, ob), (ofs, obs)


def _ffn_down_ln_kernel(a_ref, as_ref, w_ref, res_ref, ress_ref, g_ref, beta_ref, o_ref, os_ref):
    i, k = pl.program_id(0), pl.program_id(1)
    last = pl.num_programs(1) - 1

    def step(lhs_ref, resid_ref, out_ref):
        part = lambda rows: jnp.dot(lhs_ref[rows, :], w_ref[...], preferred_element_type=F32)
        rows_all = slice(None)
        tm = out_ref.shape[0]
        half = tm // 2 if tm % (2 * SUBLANES) == 0 else tm

        @pl.when(k == 0)
        def _():
            out_ref[...] = part(rows_all)

        @pl.when((k > 0) & (k < last))
        def _():
            out_ref[...] += part(rows_all)

        @pl.when(k == last)
        def _():
            for rows in (slice(r0, r0 + half) for r0 in range(0, tm, half)):
                out_ref[rows, :] = _deepnorm_ln(resid_ref[rows, :], out_ref[rows, :] + part(rows),
                                                g_ref[...], beta_ref[...])

    step(a_ref, res_ref, o_ref)

    @pl.when(i == 0)
    def _():
        step(as_ref, ress_ref, os_ref)


def _ffn_down_ln(a, res, a_s, res_s, w, layer, g, beta, *, tm, tk):
    M, K = a.shape
    Ms = a_s.shape[0]
    N = w.shape[2]
    nk = K // tk
    assert w.shape[1] == K and K % tk == 0 and nk >= 2 and M % tm == 0 and a_s.shape[1] == K
    row = lambda i, k: (i, 0)
    cst = lambda i, k: (0, 0)
    return pl.pallas_call(
        _ffn_down_ln_kernel, grid=(M // tm, nk),
        in_specs=[pl.BlockSpec((tm, tk), lambda i, k: (i, k)),
                  pl.BlockSpec((Ms, tk), lambda i, k: (0, _rider_block(i, k, nk))),
                  pl.BlockSpec((None, tk, N), lambda i, k: (layer, k, 0)),
                  pl.BlockSpec((tm, N), row), pl.BlockSpec((Ms, N), cst),
                  pl.BlockSpec((1, N), cst), pl.BlockSpec((1, N), cst)],
        out_specs=[pl.BlockSpec((tm, N), row), pl.BlockSpec((Ms, N), cst)],
        out_shape=[jax.ShapeDtypeStruct((M, N), F32), jax.ShapeDtypeStruct((Ms, N), F32)],
        compiler_params=_cparams("arbitrary", "arbitrary"), name="ffn_down_ln")(
            a, a_s, w, res, res_s, g, beta)


def _ffn_up_kernel(x_ref, xs_ref, wg_ref, wu_ref, wd_ref, o_ref, os_ref, wdb_ref):
    wg, wu = wg_ref[...].astype(BF16), wu_ref[...].astype(BF16)

    def swiglu(x):
        gate = jnp.dot(x, wg, preferred_element_type=F32)
        up = jnp.dot(x, wu, preferred_element_type=F32)
        return (gate * jax.nn.sigmoid(gate) * up).astype(BF16)

    o_ref[...] = swiglu(x_ref[...])

    @pl.when(pl.program_id(0) == 0)
    def _():
        os_ref[...] = swiglu(xs_ref[...])
        wdb_ref[...] = wd_ref[...].astype(BF16)


def _ffn_up(x, xs, wg, wu, wd, layer, *, tm, tn):
    M, K = x.shape
    N = wg.shape[2]
    nj = N // tn
    assert wd.shape[1] == N
    wblk = pl.BlockSpec((None, K, tn), lambda i, j: (layer, 0, j))
    rider = lambda i, j: _rider_block(i, j, nj)
    return pl.pallas_call(
        _ffn_up_kernel, grid=(M // tm, nj),
        in_specs=[pl.BlockSpec((tm, K), lambda i, j: (i, 0)), pl.BlockSpec(xs.shape, lambda i, j: (0, 0)),
                  wblk, wblk, pl.BlockSpec((None, tn, wd.shape[2]), lambda i, j: (layer, rider(i, j), 0))],
        out_specs=[pl.BlockSpec((tm, tn), lambda i, j: (i, j)),
                   pl.BlockSpec((xs.shape[0], tn), lambda i, j: (0, rider(i, j))),
                   pl.BlockSpec((tn, wd.shape[2]), lambda i, j: (rider(i, j), 0))],
        out_shape=[jax.ShapeDtypeStruct((M, N), BF16), jax.ShapeDtypeStruct((xs.shape[0], N), BF16),
                   jax.ShapeDtypeStruct(wd.shape[1:], BF16)],
        compiler_params=_cparams("arbitrary", "arbitrary"), name="ffn_up")(x, xs, wg, wu, wd)


def _rope_kernel(x_ref, cos_ref, sin_ref, o_ref):
    x = x_ref[...]
    lane = lax.broadcasted_iota(jnp.int32, x.shape, 1)
    first_half = (lane % SWA_HEAD) < (SWA_HEAD // 2)
    partner = jnp.where(first_half, pltpu.roll(x, LANES - SWA_HEAD // 2, 1),
                        pltpu.roll(x, SWA_HEAD // 2, 1))
    o_ref[...] = x * cos_ref[...] + partner * sin_ref[...]


def _rope(proj, cos, sin_signed, *, tm):
    M = proj.shape[0]
    width = TOK_WIDTH + SWA_KV_WIDTH
    blk = pl.BlockSpec((tm, LANES), lambda i, s: (i, s))
    tab = pl.BlockSpec((tm, LANES), lambda i, s: (i, 0))
    return pl.pallas_call(
        _rope_kernel, grid=(M // tm, width // LANES),
        in_specs=[blk, tab, tab], out_specs=blk,
        out_shape=jax.ShapeDtypeStruct((M, width), F32),
        compiler_params=_cparams("parallel", "arbitrary"), name="rope")(proj, cos, sin_signed)


def _sink_column(sink_ref, base, rows_per_head, nheads):
    rows = rows_per_head * nheads
    hid = lax.broadcasted_iota(jnp.int32, (rows, 1), 0) // rows_per_head
    col = jnp.zeros((rows, 1), F32)
    for j in range(nheads):
        col = jnp.where(hid == j, sink_ref[base + j], col)
    return col


def _sink_softmax(s, sink):
    m = jnp.maximum(jnp.max(s, axis=-1, keepdims=True), sink)
    p = jnp.exp(s - m)
    return p * (1.0 / (jnp.sum(p, axis=-1, keepdims=True) + jnp.exp(sink - m)))


def _swa_kernel(sink_ref, q_ref, kc_ref, kp_ref, vc_ref, vp_ref, cosc_ref, sinc_ref, cosp_ref, sinp_ref,
                o_ref, krot_ref):
    n = pl.program_id(0)
    nslab_q = TOK_WIDTH // LANES
    slab = lambda ref, s: ref[:, s * LANES:(s + 1) * LANES]
    cos_c, sin_c, cos_p, sin_p = cosc_ref[...], sinc_ref[...], cosp_ref[...], sinp_ref[...]
    lane = lax.broadcasted_iota(jnp.int32, (BLOCK, LANES), 1)
    first_half = (lane % SWA_HEAD) < (SWA_HEAD // 2)
    lo = lane < SWA_HEAD

    def rope(x, cos, sin):
        partner = jnp.where(first_half, pltpu.roll(x, LANES - SWA_HEAD // 2, 1),
                            pltpu.roll(x, SWA_HEAD // 2, 1))
        return x * cos + partner * sin

    kv_slabs = SWA_KV_WIDTH // LANES
    k_cur = [rope(slab(kc_ref, j), cos_c, sin_c) for j in range(kv_slabs)]
    k_prev = [rope(slab(kp_ref, j), cos_p, sin_p) for j in range(kv_slabs)]
    for j in range(kv_slabs):
        krot_ref[:, j * LANES:(j + 1) * LANES] = k_cur[j]
    lane2 = lax.broadcasted_iota(jnp.int32, (2 * BLOCK, LANES), 1)
    kd, vd = [], []
    for g in range(SWA_KV_HEADS):
        j, half = divmod(g, 2)
        keep = (lane2 < SWA_HEAD) if half == 0 else (lane2 >= SWA_HEAD)
        dup = lambda t: jnp.where(keep, t, pltpu.roll(t, SWA_HEAD, 1)).astype(BF16)
        kd.append(dup(jnp.concatenate([k_prev[j], k_cur[j]], axis=0)))
        vd.append(dup(jnp.concatenate([slab(vp_ref, j), slab(vc_ref, j)], axis=0)))
    qi = lax.broadcasted_iota(jnp.int32, (2 * BLOCK, 2 * BLOCK), 0) % BLOCK
    si = lax.broadcasted_iota(jnp.int32, (2 * BLOCK, 2 * BLOCK), 1)
    valid = (si > qi) & (si <= qi + WINDOW) & ((n > 0) | (si >= BLOCK))
    row_lo = lax.broadcasted_iota(jnp.int32, (2 * BLOCK, 1), 0) < BLOCK
    slabs = list(range(nslab_q))
    kv_of = [(2 * s) // SWA_GROUP for s in slabs]
    assert math.frexp(SWA_SCALE)[0] == 0.5
    q = [rope(slab(q_ref, s), cos_c, sin_c) * SWA_SCALE for s in slabs]
    qs = [jnp.concatenate([jnp.where(lo, t, 0.0), jnp.where(lo, 0.0, t)], axis=0).astype(BF16) for t in q]
    sc = [lax.dot_general(t, kd[g], NT_DIMS, preferred_element_type=F32) for t, g in zip(qs, kv_of)]
    sc = [jnp.where(valid, t, -jnp.inf) for t in sc]
    p = [_sink_softmax(t, jnp.where(row_lo, sink_ref[2 * s], sink_ref[2 * s + 1])) for t, s in zip(sc, slabs)]
    o = [jnp.dot(t.astype(BF16), vd[g], preferred_element_type=F32) for t, g in zip(p, kv_of)]
    for s, t in zip(slabs, o):
        o_ref[:, s * LANES:(s + 1) * LANES] = jnp.where(lo, t[:BLOCK], t[BLOCK:]).astype(o_ref.dtype)


def _swa_prompt(proj, cos, sin_signed, sinks):
    T = proj.shape[0]
    kblk, vblk = TOK_WIDTH // SWA_KV_WIDTH, TOK_WIDTH // SWA_KV_WIDTH + 1
    prev = lambda n: jnp.maximum(n - 1, 0)
    kv_spec = lambda blk, row: pl.BlockSpec((BLOCK, SWA_KV_WIDTH), lambda n: (row(n), blk))
    tab = lambda row: pl.BlockSpec((BLOCK, LANES), lambda n: (row(n), 0))
    cur = lambda n: n
    return pl.pallas_call(
        _swa_kernel, grid=(T // BLOCK,),
        in_specs=[pl.BlockSpec(memory_space=pltpu.SMEM),
                  pl.BlockSpec((BLOCK, TOK_WIDTH), lambda n: (n, 0)),
                  kv_spec(kblk, cur), kv_spec(kblk, prev), kv_spec(vblk, cur), kv_spec(vblk, prev),
                  tab(cur), tab(cur), tab(prev), tab(prev)],
        out_specs=[pl.BlockSpec((BLOCK, TOK_WIDTH), lambda n: (n, 0)),
                   pl.BlockSpec((BLOCK, SWA_KV_WIDTH), lambda n: (n, 0))],
        out_shape=[jax.ShapeDtypeStruct((T, TOK_WIDTH), BF16),
                   jax.ShapeDtypeStruct((T, SWA_KV_WIDTH), F32)],
        compiler_params=_cparams("arbitrary"), name="swa_banded")(
            sinks, proj, proj, proj, proj, proj, cos, sin_signed, cos, sin_signed)


def _swa_step_kernel(sink_ref, q_ref, kn_ref, vn_ref, kt_ref, vt_ref, o_ref, kto_ref, vto_ref, *, bs):
    sink = _sink_column(sink_ref, 0, 1, SWA_Q_HEADS)
    newest = lax.broadcasted_iota(jnp.int32, (SWA_HEAD, WINDOW), 1) == WINDOW - 1
    kv_of_head = lax.broadcasted_iota(jnp.int32, (SWA_Q_HEADS, 1), 0) // SWA_GROUP
    kv_heads = list(range(SWA_KV_HEADS))

    def pick(per_kv):
        out = per_kv[0]
        for g in kv_heads[1:]:
            out = jnp.where(kv_of_head == g, per_kv[g], out)
        return out

    def slid(new_ref, cache_ref, b):
        new = new_ref[b]
        return [jnp.where(newest, new[:, g:g + 1], pltpu.roll(cache_ref[b, g], WINDOW - 1, 1)) for g in kv_heads]

    def group(t, carry):
        bs_ = [t * STEP_INTERLEAVE + u for u in range(STEP_INTERLEAVE)]
        kt = [slid(kn_ref, kt_ref, b) for b in bs_]
        vt = [slid(vn_ref, vt_ref, b) for b in bs_]
        for b, kt_, vt_ in zip(bs_, kt, vt):
            for g in kv_heads:
                kto_ref[b, g] = kt_[g]
                vto_ref[b, g] = vt_[g]
        s = [pick([_dot(q_ref[b], t_) for t_ in kt_]) * SWA_SCALE for b, kt_ in zip(bs_, kt)]
        p = [_sink_softmax(s_, sink) for s_ in s]
        for b, p_, vt_ in zip(bs_, p, vt):
            o_ref[b] = pick([_dot_nt(p_, t_) for t_ in vt_])
        return carry

    lax.fori_loop(0, bs // STEP_INTERLEAVE, group, 0)


def _swa_step(q, k_new, v_new, kt, vt, layer, sinks, *, bs):
    B = q.shape[0]
    qblk = pl.BlockSpec((bs, SWA_Q_HEADS, SWA_HEAD), lambda i: (i, 0, 0))
    nblk = pl.BlockSpec((bs, SWA_HEAD, SWA_KV_HEADS), lambda i: (i, 0, 0))
    cshape = (bs, SWA_KV_HEADS, SWA_HEAD, WINDOW)
    cin = pl.BlockSpec((None,) + cshape, lambda i: (layer, i, 0, 0, 0))
    cout = pl.BlockSpec(cshape, lambda i: (i, 0, 0, 0))
    cache = jax.ShapeDtypeStruct(kt.shape[1:], F32)
    return pl.pallas_call(
        functools.partial(_swa_step_kernel, bs=bs), grid=(B // bs,),
        in_specs=[pl.BlockSpec(memory_space=pltpu.SMEM), qblk, nblk, nblk, cin, cin],
        out_specs=[qblk, cout, cout],
        out_shape=[jax.ShapeDtypeStruct(q.shape, F32), cache, cache],
        compiler_params=_cparams("parallel"), name="swa_step")(sinks, q, k_new, v_new, kt, vt)


ROW_TILE = 512
FFN_ROW_TILE = 1024
COL_TILE = 512
STEP_BATCH = 16
FFN_DOWN_K_TILE = FFN_HIDDEN // 2


def _row_tile(m):
    return ROW_TILE if m % ROW_TILE == 0 else m


def _ffn_row_tile(m):
    return FFN_ROW_TILE if m % FFN_ROW_TILE == 0 else _row_tile(m)


def _pad_rows(w, rows):
    return jnp.pad(w, ((0, rows - w.shape[0]), (0, 0)))


def _pad_cols(w, cols):
    return jnp.pad(w, ((0, 0), (0, cols - w.shape[1])))


def _rwkv_weights(w_in, mu, w1, w2, a1, a2, g1, g2):
    w_lora = jnp.concatenate([_pad_cols(w1[0], LORA_PAD), _pad_cols(a1[0], LORA_PAD), g1[0]], axis=1)
    return dict(w_in=w_in.astype(BF16), w_lora=w_lora.astype(BF16), mu=mu[0][:, None, :],
                w2=_pad_rows(w2[0], LORA_PAD).astype(BF16), a2=_pad_rows(a2[0], LORA_PAD).astype(BF16),
                g2=g2[0].astype(BF16))


def _unblock_state(s_bd):
    n = RWKV_HEAD
    return jnp.stack([s_bd[:, :n, :n], s_bd[:, n:, n:]], axis=1).reshape(RWKV_HEADS, n, n)


def _rope_tables(pos):
    half = SWA_HEAD // 2
    inv = ROPE_THETA ** (-jnp.arange(half, dtype=F32) / half)
    ang = pos.astype(F32)[:, None] * inv[None, :]
    cos, sin = jnp.cos(ang), jnp.sin(ang)
    reps = LANES // SWA_HEAD
    return jnp.tile(cos, (1, 2 * reps)), jnp.tile(jnp.concatenate([-sin, sin], axis=1), (1, reps))


def _post_mixer(prompt, sample, sw, layer):
    row = lambda t: t[layer][None, :]
    tok_p, proj_p, qblk, kv, x_p = prompt
    tok_s, mo_s, x_s = sample
    (x1f_p, x1b_p), (x1f_s, x1b_s) = _out_ln(tok_p, proj_p, qblk, kv, x_p, tok_s, mo_s, x_s, sw["w_out"], layer,
                                             row(sw["ln1_g"]), row(sw["ln1_b"]), tm=_row_tile(x_p.shape[0]))
    hff_p, hff_s, wd = _ffn_up(x1b_p, x1b_s, sw["w_gate"], sw["w_up"], sw["w_down"], layer,
                               tm=_ffn_row_tile(x1b_p.shape[0]), tn=COL_TILE)
    return _ffn_down_ln(hff_p, x1f_p, hff_s, x1f_s, wd[None], 0, row(sw["ln2_g"]), row(sw["ln2_b"]),
                        tm=_row_tile(x1f_p.shape[0]), tk=FFN_DOWN_K_TILE)


def kernel(x_prompt, x_sample, mem_prompt, cache_mem_k, cache_mem_v, state_rwkv_shift, state_rwkv_wkv, cache_swa_k, cache_swa_v, w_in_rwkv, rwkv_mu, rwkv_w0, rwkv_w1, rwkv_w2, rwkv_a0, rwkv_a1, rwkv_a2, rwkv_g1, rwkv_g2, rwkv_k_k, rwkv_k_a, rwkv_r_k, rwkv_gn_g, rwkv_gn_b, w_in_swa, swa_sinks, w_mem_kv, w_out, ln1_g, ln1_b, w_gate, w_up, w_down, ln2_g, ln2_b):
    assert DEPTH == 2 and x_prompt.shape[0] == 1 and x_sample.shape[1] == 1
    T = x_prompt.shape[1]
    B = x_sample.shape[0]
    row = lambda t: t[None, :]
    shared = dict(w_out=w_out.astype(BF16), ln1_g=ln1_g, ln1_b=ln1_b, w_gate=w_gate, w_up=w_up,
                  w_down=w_down, ln2_g=ln2_g, ln2_b=ln2_b)
    RW = _rwkv_weights(w_in_rwkv, rwkv_mu, rwkv_w1, rwkv_w2, rwkv_a1, rwkv_a2, rwkv_g1, rwkv_g2)
    rk, gn_g, gn_b = row(rwkv_r_k[0].reshape(-1)), row(rwkv_gn_g[0]), row(rwkv_gn_b[0])
    prep_consts = (row(rwkv_w0[0]), row(rwkv_a0[0]), row(rwkv_k_k[0]), row(rwkv_k_a[0]))
    w_swa = w_in_swa.astype(BF16)
    sinks = swa_sinks[0]
    q_blk_rwkv = 3 * TOK_WIDTH // MEM_WIDTH
    q_blk_swa = (TOK_WIDTH + 2 * SWA_KV_WIDTH) // MEM_WIDTH

    xp, xs = x_prompt[0], x_sample[:, 0]
    tms = _row_tile(B)
    kv = [_proj(mem_prompt[0], w_mem_kv, i, tm=MEM_TOKENS, tn=COL_TILE, name="mem_kv") for i in range(DEPTH)]
    prompt_mem_k = jnp.stack([t[:, :MEM_WIDTH] for t in kv]).reshape(DEPTH, 1, MEM_TOKENS, MEM_HEADS, MEM_HEAD)
    prompt_mem_v = jnp.stack([t[:, MEM_WIDTH:] for t in kv]).reshape(DEPTH, 1, MEM_TOKENS, MEM_HEADS, MEM_HEAD)
    mem_k = cache_mem_k.reshape(DEPTH, B, MEM_TOKENS * MEM_HEADS, MEM_HEAD)
    mem_v = cache_mem_v.reshape(DEPTH, B, MEM_TOKENS * MEM_HEADS, MEM_HEAD)

    def mem_step(q, layer):
        q_rows = jnp.pad(q.reshape(B, MEM_HEADS, MEM_HEAD), ((0, 0), (0, SUBLANES - MEM_HEADS), (0, 0)))
        out = _mem_attn_step(q_rows, mem_k, mem_v, layer, bs=STEP_BATCH)
        return out[:, :MEM_HEADS].reshape(B, MEM_WIDTH).astype(BF16)

    proj_p, proj_s = _rwkv_in(xp, xs, state_rwkv_shift[0], RW["mu"], RW["w_in"], RW["w_lora"], 0,
                              tm=_ffn_row_tile(T))
    tok_p, s_bd = _wkv_prompt(proj_p, RW["w2"], RW["a2"], RW["g2"], prep_consts + (rk, gn_g, gn_b))
    prompt_shift = xp[-1][None, None, :]
    prompt_wkv = _unblock_state(s_bd)[None, None]
    vecs = _rwkv_prep_t(proj_s, RW["w2"], RW["a2"], RW["g2"], *prep_consts)
    lanes_b = lambda t: jnp.broadcast_to(t.reshape(TOK_WIDTH, 1), (TOK_WIDTH, B))
    s_new, tok_t = _wkv_step(state_rwkv_wkv.transpose(0, 2, 3, 4, 1), 0, vecs,
                             (lanes_b(rk), lanes_b(gn_g), lanes_b(gn_b)))
    mo_s = mem_step(proj_s[:, 3 * TOK_WIDTH:3 * TOK_WIDTH + MEM_WIDTH], 0)
    sample_shift = xs[None]
    sample_wkv = s_new.transpose(3, 0, 1, 2)[None]
    xf_p, xf_s = _post_mixer((tok_p, proj_p, q_blk_rwkv, kv[0], xp), (tok_t.T.astype(BF16), mo_s, xs), shared, 0)

    proj_p, proj_s = _proj(xf_p, w_swa, 0, tm=_ffn_row_tile(T), tn=COL_TILE, name="swa_proj", xs=xf_s)
    cos, sin = _rope_tables(jnp.arange(T))
    tok_p, k_rot = _swa_prompt(proj_p, cos, sin, sinks)
    v_last = proj_p[T - WINDOW:, TOK_WIDTH + SWA_KV_WIDTH:TOK_WIDTH + 2 * SWA_KV_WIDTH]
    prompt_swa_k = k_rot[T - WINDOW:].reshape(1, 1, WINDOW, SWA_KV_HEADS, SWA_HEAD)
    prompt_swa_v = v_last.reshape(1, 1, WINDOW, SWA_KV_HEADS, SWA_HEAD)
    cos, sin = _rope_tables(jnp.full((B,), PAST_LEN))
    qk = _rope(proj_s, cos, sin, tm=tms)
    chan_major = lambda t: t.reshape(B, SWA_KV_HEADS, SWA_HEAD).transpose(0, 2, 1)
    k_new = chan_major(qk[:, TOK_WIDTH:])
    v_new = chan_major(proj_s[:, TOK_WIDTH + SWA_KV_WIDTH:TOK_WIDTH + 2 * SWA_KV_WIDTH])
    o, kc, vc = _swa_step(qk[:, :TOK_WIDTH].reshape(B, SWA_Q_HEADS, SWA_HEAD), k_new, v_new,
                          cache_swa_k.transpose(0, 1, 3, 4, 2), cache_swa_v.transpose(0, 1, 3, 4, 2),
                          0, sinks, bs=STEP_BATCH)
    mo_s = mem_step(proj_s[:, TOK_WIDTH + 2 * SWA_KV_WIDTH:], 1)
    sample_swa_k, sample_swa_v = kc.transpose(0, 3, 1, 2)[None], vc.transpose(0, 3, 1, 2)[None]
    y_prompt, y_sample = _post_mixer((tok_p, proj_p, q_blk_swa, kv[1], xf_p),
                                     (o.reshape(B, TOK_WIDTH).astype(BF16), mo_s, xf_s), shared, 1)

    return (y_prompt[None], y_sample[:, None, :], prompt_mem_k, prompt_mem_v, prompt_shift, prompt_wkv,
            prompt_swa_k, prompt_swa_v, sample_shift, sample_wkv, sample_swa_k, sample_swa_v)
```

```python
import functools
import math

import jax
import jax.numpy as jnp
from jax import lax
from jax.experimental import pallas as pl
from jax.experimental.pallas import tpu as pltpu

D_MODEL = 2048
DEPTH = 2
MEM_WIDTH = D_MODEL // 4
TOK_WIDTH = D_MODEL - MEM_WIDTH
RWKV_HEAD = 64
RWKV_HEADS = TOK_WIDTH // RWKV_HEAD
GN_EPS = RWKV_HEAD * 1e-5
SWA_HEAD = 64
SWA_Q_HEADS = TOK_WIDTH // SWA_HEAD
SWA_KV_HEADS = 4
SWA_GROUP = SWA_Q_HEADS // SWA_KV_HEADS
SWA_KV_WIDTH = SWA_KV_HEADS * SWA_HEAD
WINDOW = 128
BLOCK = 128
PAST_LEN = 8192
SWA_SCALE = SWA_HEAD ** -0.5
ROPE_THETA = 10000.0
MEM_TOKENS = 256
MEM_HEADS = 4
MEM_HEAD = MEM_WIDTH // MEM_HEADS
MEM_SCALE = MEM_HEAD ** -0.5
FFN_HIDDEN = int(math.ceil(8 * D_MODEL / 3 / 256)) * 256
ALPHA = (2 * DEPTH) ** 0.25
LN_EPS = 1e-5
LORA_PAD = 128
LORA_IN_WIDTH = 512

LANES = 128
SUBLANES = 8
VMEM_LIMIT_BYTES = 56 * 1024 * 1024

BF16 = jnp.bfloat16
F32 = jnp.float32
NT_DIMS = (((1,), (1,)), ((), ()))
TN_DIMS = (((0,), (0,)), ((), ()))


def _dot(a, b):
    return jnp.dot(a.astype(BF16), b.astype(BF16), preferred_element_type=F32)


def _dot_nt(a, b):
    return lax.dot_general(a.astype(BF16), b.astype(BF16), NT_DIMS, preferred_element_type=F32)


def _dot_tn(a, b):
    return lax.dot_general(a.astype(BF16), b.astype(BF16), TN_DIMS, preferred_element_type=F32)


def _split_dot(x, m):
    hi = x.astype(BF16)
    lo = (x - hi.astype(F32)).astype(BF16)
    return (jnp.dot(hi, m, preferred_element_type=F32)
            + jnp.dot(lo, m, preferred_element_type=F32))


def _head_ones():
    p = lax.broadcasted_iota(jnp.int32, (LANES, LANES), 0)
    q = lax.broadcasted_iota(jnp.int32, (LANES, LANES), 1)
    return ((p // RWKV_HEAD) == (q // RWKV_HEAD)).astype(BF16)


def _cparams(*sem):
    return pltpu.CompilerParams(dimension_semantics=sem, vmem_limit_bytes=VMEM_LIMIT_BYTES)


def _rider_block(i, j, nj):
    return jnp.where(i == 0, j, nj - 1)


def _proj_kernel(*refs, rider):
    if rider:
        x_ref, xs_ref, w_ref, o_ref, os_ref, xb_ref = refs
    else:
        x_ref, w_ref, o_ref, xb_ref = refs

    @pl.when(pl.program_id(1) == 0)
    def _():
        xb_ref[...] = x_ref[...].astype(BF16)

    w = w_ref[...].astype(BF16)
    o_ref[...] = jnp.dot(xb_ref[...], w, preferred_element_type=F32)
    if rider:
        @pl.when(pl.program_id(0) == 0)
        def _():
            os_ref[...] = jnp.dot(xs_ref[...].astype(BF16), w, preferred_element_type=F32)


def _proj(x, w, layer, *, tm, tn, name, xs=None):
    M, K = x.shape
    N = w.shape[2]
    assert M % tm == 0 and N % tn == 0
    nj = N // tn
    rider = xs is not None
    in_specs = [pl.BlockSpec((tm, K), lambda i, j: (i, 0))]
    out_specs = [pl.BlockSpec((tm, tn), lambda i, j: (i, j))]
    out_shape = [jax.ShapeDtypeStruct((M, N), F32)]
    if rider:
        in_specs.append(pl.BlockSpec(xs.shape, lambda i, j: (0, 0)))
        out_specs.append(pl.BlockSpec((xs.shape[0], tn), lambda i, j: (0, _rider_block(i, j, nj))))
        out_shape.append(jax.ShapeDtypeStruct((xs.shape[0], N), F32))
    in_specs.append(pl.BlockSpec((None, K, tn), lambda i, j: (layer, 0, j)))
    out = pl.pallas_call(
        functools.partial(_proj_kernel, rider=rider), grid=(M // tm, nj),
        in_specs=in_specs, out_specs=out_specs, out_shape=out_shape,
        scratch_shapes=[pltpu.VMEM((tm, K), BF16)],
        compiler_params=_cparams("arbitrary", "arbitrary"), name=name)(*([x, xs, w] if rider else [x, w]))
    return out if rider else out[0]


MIX_R, MIX_W, MIX_K, MIX_V, MIX_A, MIX_G = range(6)
RWKV_IN_WIDTH = 3 * TOK_WIDTH + MEM_WIDTH + LORA_IN_WIDTH
RWKV_IN_TILE = LORA_IN_WIDTH
RWKV_IN_KEPT = (MIX_R, MIX_K, MIX_V)
RWKV_IN_SLOT = ([0] * (TOK_WIDTH // RWKV_IN_TILE) + [1] * (TOK_WIDTH // RWKV_IN_TILE)
                + [2] * (TOK_WIDTH // RWKV_IN_TILE) + [3] * (MEM_WIDTH // RWKV_IN_TILE))
LORA_PARTS = ((MIX_W, 0, LORA_PAD), (MIX_A, LORA_PAD, 2 * LORA_PAD), (MIX_G, 2 * LORA_PAD, LORA_IN_WIDTH))
RWKV_IN_SUB = 256


def _rwkv_in_kernel(slot_ref, x_ref, xp_ref, xs_ref, xps_ref, mu_ref, w_ref, wl_ref, o_ref, os_ref,
                    lhs_ref, lhss_ref):
    i, j = pl.program_id(0), pl.program_id(1)
    tm = x_ref.shape[0]
    sub = min(RWKV_IN_SUB, tm)
    blocks = [slice(s0, s0 + sub) for s0 in range(0, tm, sub)]
    plain = len(RWKV_IN_KEPT)
    nproj = pl.num_programs(1) - 1

    def x_and_delta(rows):
        x = x_ref[rows, :]
        if rows.start == 0:
            first = jnp.where(i > 0, xp_ref[SUBLANES - 1:SUBLANES, :], 0.0)
        else:
            first = x_ref[rows.start - 1:rows.start, :]
        rowid = lax.broadcasted_iota(jnp.int32, (sub, 1), 0)
        return x, jnp.where(rowid == 0, first, pltpu.roll(x, 1, 0)) - x

    def rider_x_and_delta(rows):
        x = xs_ref[...]
        return x, xps_ref[...] - x

    def group(delta_fn, row_blocks, kept_ref, out_ref):
        @pl.when(j == 0)
        def _():
            for rows in row_blocks:
                x, d = delta_fn(rows)
                for slot, m in enumerate(RWKV_IN_KEPT):
                    kept_ref[slot, rows, :] = (x + d * mu_ref[m]).astype(BF16)
                kept_ref[plain, rows, :] = x.astype(BF16)

        @pl.when(j < nproj)
        def _():
            out_ref[...] = jnp.dot(kept_ref[slot_ref[j]], w_ref[...], preferred_element_type=F32)

        @pl.when(j == nproj)
        def _():
            for rows in row_blocks:
                x, d = delta_fn(rows)
                for m, lo, hi in LORA_PARTS:
                    out_ref[rows, lo:hi] = jnp.dot((x + d * mu_ref[m]).astype(BF16), wl_ref[:, lo:hi],
                                                   preferred_element_type=F32)

    group(x_and_delta, blocks, lhs_ref, o_ref)

    @pl.when(i == 0)
    def _():
        group(rider_x_and_delta, [slice(None)], lhss_ref, os_ref)


def _rwkv_in(x, xs, xprev_s, mu, w, w_lora, layer, *, tm):
    M, K = x.shape
    Ms = xs.shape[0]
    tn = RWKV_IN_TILE
    nproj = w.shape[2] // tn
    assert M % tm == 0 and w.shape[2] + w_lora.shape[1] == RWKV_IN_WIDTH and nproj == len(RWKV_IN_SLOT)
    assert tm % min(RWKV_IN_SUB, tm) == 0 and tm % SUBLANES == 0
    rows8 = tm // SUBLANES
    slot = jnp.asarray(RWKV_IN_SLOT + [0], jnp.int32)
    cst = lambda i, j, m: (0, 0)
    grid_spec = pltpu.PrefetchScalarGridSpec(
        num_scalar_prefetch=1, grid=(M // tm, nproj + 1),
        in_specs=[pl.BlockSpec((tm, K), lambda i, j, m: (i, 0)),
                  pl.BlockSpec((SUBLANES, K), lambda i, j, m: (jnp.maximum(i * rows8 - 1, 0), 0)),
                  pl.BlockSpec((Ms, K), cst), pl.BlockSpec((Ms, K), cst),
                  pl.BlockSpec(mu.shape, lambda i, j, m: (0, 0, 0)),
                  pl.BlockSpec((None, K, tn), lambda i, j, m: (layer, 0, jnp.minimum(j, nproj - 1))),
                  pl.BlockSpec(w_lora.shape, cst)],
        out_specs=[pl.BlockSpec((tm, tn), lambda i, j, m: (i, j)),
                   pl.BlockSpec((Ms, tn), lambda i, j, m: (0, _rider_block(i, j, nproj + 1)))],
        scratch_shapes=[pltpu.VMEM((len(RWKV_IN_KEPT) + 1, tm, K), BF16),
                        pltpu.VMEM((len(RWKV_IN_KEPT) + 1, Ms, K), BF16)])
    return pl.pallas_call(
        _rwkv_in_kernel, grid_spec=grid_spec,
        out_shape=[jax.ShapeDtypeStruct((M, RWKV_IN_WIDTH), F32), jax.ShapeDtypeStruct((Ms, RWKV_IN_WIDTH), F32)],
        compiler_params=_cparams("arbitrary", "arbitrary"), name="rwkv_in")(slot, x, x, xs, xprev_s, mu, w, w_lora)


def _softplus(z):
    return jnp.maximum(z, 0.0) + jnp.log1p(jnp.exp(-jnp.abs(z)))


def _each(f, *lists):
    return [f(*a) for a in zip(*lists)]


def _lora_hidden(hl):
    return jnp.tanh(hl[:, 0:LORA_PAD]), hl[:, LORA_PAD:2 * LORA_PAD], jax.nn.sigmoid(hl[:, 2 * LORA_PAD:])


def _prep_slabs(k, wl, al, w0, a0, k_k, k_a, ones):
    lw = _each(lambda wl_, w0_: -jnp.exp(-_softplus(-(w0_ + wl_)) - 0.5), wl, w0)
    agate = _each(lambda al_, a0_: jax.nn.sigmoid(a0_ + al_), al, a0)
    kkr = _each(lambda k_, c_: k_ * c_, k, k_k)
    ss = _each(lambda t: _split_dot(t * t, ones), kkr)
    kn = _each(lambda t, s_: t / jnp.maximum(jnp.sqrt(s_), 1e-12), kkr, ss)
    bv = _each(lambda n_, a_: n_ * a_, kn, agate)
    kp = _each(lambda k_, a_, c_: k_ * (1.0 + (a_ - 1.0) * c_), k, agate, k_a)
    return lw, kp, kn, bv


def _rwkv_prep_t_kernel(hl_ref, r_ref, k_ref, v_ref, w2_ref, a2_ref, g2_ref, w0_ref, a0_ref, kk_ref, ka_ref,
                        rt_ref, vt_ref, lw_ref, kp_ref, kn_ref, bv_ref, g_ref):
    hw, ha, hg = _lora_hidden(hl_ref[...])
    outs = _prep_slabs([k_ref[...]], [_dot(hw, w2_ref[...])], [_dot(ha, a2_ref[...])], [w0_ref[...]],
                       [a0_ref[...]], [kk_ref[...]], [ka_ref[...]], _head_ones())
    vals = [r_ref[...], v_ref[...]] + [t[0] for t in outs] + [_dot(hg, g2_ref[...])]
    for ref, val in zip((rt_ref, vt_ref, lw_ref, kp_ref, kn_ref, bv_ref, g_ref), vals):
        ref[...] = val.T


def _rwkv_prep_t(proj, w2p, a2p, g2, w0, a0, k_k, k_a):
    B = proj.shape[0]
    assert B == LANES
    nslab = TOK_WIDTH // LANES
    lora_blk = (RWKV_IN_WIDTH - LORA_IN_WIDTH) // LORA_IN_WIDTH
    col = lambda s: (0, s)
    slab = lambda base: pl.BlockSpec((B, LANES), lambda s, base=base: (0, base + s))
    outs = [jax.ShapeDtypeStruct((TOK_WIDTH, B), F32)] * 7
    return pl.pallas_call(
        _rwkv_prep_t_kernel, grid=(nslab,),
        in_specs=[pl.BlockSpec((B, LORA_IN_WIDTH), lambda s: (0, lora_blk)),
                  slab(0), slab(nslab), slab(2 * nslab),
                  pl.BlockSpec((LORA_PAD, LANES), col),
                  pl.BlockSpec((LORA_PAD, LANES), col),
                  pl.BlockSpec((2 * LORA_PAD, LANES), col),
                  pl.BlockSpec((1, LANES), col), pl.BlockSpec((1, LANES), col),
                  pl.BlockSpec((1, LANES), col), pl.BlockSpec((1, LANES), col)],
        out_specs=[pl.BlockSpec((LANES, B), lambda s: (s, 0))] * 7,
        out_shape=outs, compiler_params=_cparams("arbitrary"),
        name="rwkv_prep_t")(proj, proj, proj, proj, w2p, a2p, g2, w0, a0, k_k, k_a)


def _gn_gate(y, r, kp, v, g, rk, gg, gb, ones):
    inv_n = 1.0 / RWKV_HEAD
    rows = y[0].shape[0]
    sums = _each(lambda y_, r_, k_, rk_: _split_dot(jnp.concatenate([y_, r_ * k_ * rk_], axis=0), ones),
                 y, r, kp, rk)
    d = _each(lambda y_, s_: y_ - s_[:rows] * inv_n, y, sums)
    var = _each(lambda d_: _split_dot(d_ * d_, ones) * inv_n, d)
    return _each(lambda d_, var_, gg_, gb_, s_, v_, g_:
                 (d_ * lax.rsqrt(var_ + GN_EPS) * gg_ + gb_ + s_[rows:] * v_) * g_,
                 d, var, gg, gb, sums, v, g)


WKV_CHUNK = 64


def _wkv_masks():
    n = 2 * WKV_CHUNK
    p = lax.broadcasted_iota(jnp.int32, (n, n), 0)
    q = lax.broadcasted_iota(jnp.int32, (n, n), 1)
    same = lambda b: (p // b) == (q // b)
    pt, qt = p % WKV_CHUNK, q % WKV_CHUNK
    s8, s16, s32, s64 = same(8), same(16), same(32), same(WKV_CHUNK)
    return dict(strict=s64 & (pt > qt), incl=s64 & (pt >= qt), s8=s8,
                e16=s16 & ~s8, e32=s32 & ~s16, e64=s64 & ~s32,
                eye=(p == q).astype(F32))


WKV_PAIRS = TOK_WIDTH // LANES
WKV_STEP_CHUNKS = 2


def _wkv_lanes(r, lw, k, v, kn, bv, mk, tri, lane_lo):
    stack = lambda x: jnp.concatenate([jnp.where(lane_lo, x, 0.0), jnp.where(lane_lo, 0.0, x)], axis=0)
    n = 2 * WKV_CHUNK
    c = _each(lambda t: _split_dot_left(tri, t), lw)
    c_last = _each(lambda t: t[WKV_CHUNK - 1:WKV_CHUNK, :], c)
    e_out = _each(lambda t: jnp.exp(-t), c)
    e_end = _each(lambda t, tl: jnp.exp(tl - t), c, c_last)
    ah = _each(lambda kn_, c_, lw_: stack(-kn_ * jnp.exp(c_ - lw_)), kn, c, lw)
    rh = _each(lambda r_, c_: stack(r_ * jnp.exp(c_)), r, c)
    bh = _each(lambda b_, e_: stack(b_ * e_), bv, e_out)
    kh = _each(lambda k_, e_: stack(k_ * e_), k, e_out)
    bbar = _each(lambda b_, e_: stack(b_ * e_), bv, e_end)
    kbar = _each(lambda k_, e_: stack(k_ * e_), k, e_end)
    vs = _each(stack, v)
    gm = _each(lambda a_, r_, b_, k_: _dot_nt(jnp.concatenate([a_, r_], axis=0),
                                              jnp.concatenate([b_, k_], axis=0)), ah, rh, bh, kh)
    a_ab = _each(lambda g_: jnp.where(mk["strict"], g_[:n, :n], 0.0), gm)
    a_ak = _each(lambda g_: jnp.where(mk["strict"], g_[:n, n:], 0.0), gm)
    l_rb = _each(lambda g_: jnp.where(mk["incl"], g_[n:, :n], 0.0), gm)
    l_rk = _each(lambda g_: jnp.where(mk["incl"], g_[n:, n:], 0.0), gm)
    d1 = _each(lambda a_: jnp.where(mk["s8"], a_, 0.0), a_ab)
    x = _each(lambda d_: mk["eye"] + d_, d1)
    d2 = _each(lambda d_: _dot(d_, d_), d1)
    x = _each(lambda x_, d_: x_ + _dot(x_, d_), x, d2)
    d4 = _each(lambda d_: _dot(d_, d_), d2)
    x = _each(lambda x_, d_: x_ + _dot(x_, d_), x, d4)
    for lvl in ("e16", "e32", "e64"):
        ex = _each(lambda a_, x_: _dot(jnp.where(mk[lvl], a_, 0.0), x_), a_ab, x)
        x = _each(lambda x_, e_: x_ + _dot(x_, e_), x, ex)
    av = _each(_dot, a_ak, vs)
    tw = _each(lambda x_, a_, v_: _dot(x_, jnp.concatenate([a_, v_], axis=1)), x, ah, av)
    lwm = _each(_dot, l_rb, tw)
    lv = _each(_dot, l_rk, vs)
    qm = _each(lambda r_, l_: r_ + l_[:, :n], rh, lwm)
    y0 = _each(lambda l_, v_: l_[:, n:] + v_, lwm, lv)
    mt = _each(lambda t_, b_: _dot_tn(t_[:, :n], b_), tw, bbar)
    nt = _each(lambda t_, b_, v_, k_: _dot_tn(t_[:, n:], b_) + _dot_tn(v_, k_), tw, bbar, vs, kbar)
    return qm, y0, mt, nt, _each(jnp.exp, c_last)


def _split_dot_left(m, x):
    hi = x.astype(BF16)
    lo = (x - hi.astype(F32)).astype(BF16)
    return (jnp.dot(m, hi, preferred_element_type=F32)
            + jnp.dot(m, lo, preferred_element_type=F32))


def _wkv_kernel(r_ref, k_ref, v_ref, hl_ref, w2_ref, a2_ref, g2_ref, w0_ref, a0_ref, kk_ref, ka_ref,
                rk_ref, gg_ref, gb_ref, o_ref, sout_ref, s_scr):
    c_idx = pl.program_id(0)

    @pl.when(c_idx == 0)
    def _():
        s_scr[...] = jnp.zeros_like(s_scr)

    mk = _wkv_masks()
    ti = lax.broadcasted_iota(jnp.int32, (WKV_CHUNK, WKV_CHUNK), 0)
    tj = lax.broadcasted_iota(jnp.int32, (WKV_CHUNK, WKV_CHUNK), 1)
    tri = (ti >= tj).astype(BF16)
    lane_lo = lax.broadcasted_iota(jnp.int32, (WKV_CHUNK, LANES), 1) < RWKV_HEAD
    ones = _head_ones()
    pairs = range(WKV_PAIRS)
    lanes = [(slice(ch * WKV_CHUNK, (ch + 1) * WKV_CHUNK), slice(p * LANES, (p + 1) * LANES))
             for ch in range(WKV_STEP_CHUNKS) for p in pairs]
    cut = lambda t: [t[rows, sl] for rows, sl in lanes]
    rep = lambda t: [t[:, sl] for _, sl in lanes]
    hw, ha, hg = _lora_hidden(hl_ref[...])
    wl, al, g = _dot(hw, w2_ref[...]), _dot(ha, a2_ref[...]), _dot(hg, g2_ref[...])
    r, v = cut(r_ref), cut(v_ref)
    lw, kp, kn, bv = _prep_slabs(cut(k_ref), cut(wl), cut(al), rep(w0_ref), rep(a0_ref),
                                 rep(kk_ref), rep(ka_ref), ones)
    qm, y0, mt, nt, dec = _wkv_lanes(r, lw, kp, v, kn, bv, mk, tri, lane_lo)
    S = [s_scr[p] for p in pairs]
    y = []
    for ch in range(WKV_STEP_CHUNKS):
        part = slice(ch * WKV_PAIRS, (ch + 1) * WKV_PAIRS)
        ys = _each(lambda q_, s_, y_: _dot_nt(q_, s_) + y_, qm[part], S, y0[part])
        y += _each(lambda t: t[:WKV_CHUNK, :] + t[WKV_CHUNK:, :], ys)
        S = _each(lambda s_, d_, m_, n_: s_ * d_ + _dot(s_, m_) + n_, S, dec[part], mt[part], nt[part])
    for p in pairs:
        s_scr[p] = S[p]
    tok = _gn_gate(y, r, kp, v, cut(g), rep(rk_ref), rep(gg_ref), rep(gb_ref), ones)
    for (rows, sl), t in zip(lanes, tok):
        o_ref[rows, sl] = t.astype(o_ref.dtype)

    @pl.when(c_idx == pl.num_programs(0) - 1)
    def _():
        sout_ref[...] = s_scr[...]


def _wkv_prompt(proj, w2p, a2p, g2, consts):
    T = proj.shape[0]
    rows = WKV_STEP_CHUNKS * WKV_CHUNK
    assert T % rows == 0
    lora_blk = (RWKV_IN_WIDTH - LORA_IN_WIDTH) // LORA_IN_WIDTH
    tok = lambda blk: pl.BlockSpec((rows, TOK_WIDTH), lambda c, blk=blk: (c, blk))
    full = lambda a: pl.BlockSpec(a.shape, lambda c: (0,) * a.ndim)
    weights = (w2p, a2p, g2) + tuple(consts)
    return pl.pallas_call(
        _wkv_kernel, grid=(T // rows,),
        in_specs=[tok(0), tok(1), tok(2), pl.BlockSpec((rows, LORA_IN_WIDTH), lambda c: (c, lora_blk))]
        + [full(a) for a in weights],
        out_specs=[tok(0), pl.BlockSpec((WKV_PAIRS, LANES, LANES), lambda c: (0, 0, 0))],
        out_shape=[jax.ShapeDtypeStruct((T, TOK_WIDTH), BF16),
                   jax.ShapeDtypeStruct((WKV_PAIRS, LANES, LANES), F32)],
        scratch_shapes=[pltpu.VMEM((WKV_PAIRS, LANES, LANES), F32)],
        compiler_params=_cparams("arbitrary"), name="wkv_chunked")(proj, proj, proj, proj, *weights)


WKV_STEP_UNROLL = 4


def _wkv_step_kernel(s_ref, r_ref, v_ref, lw_ref, kp_ref, kn_ref, bv_ref, g_ref, rk_ref, gg_ref, gb_ref,
                     so_ref, tok_ref, y_scr):
    n = RWKV_HEAD
    inv_n = 1.0 / n
    for hh in range(2):
        rows = slice(hh * n, (hh + 1) * n)
        a, w = -kn_ref[rows, :], jnp.exp(lw_ref[rows, :])
        b, k, r = bv_ref[rows, :], kp_ref[rows, :], r_ref[rows, :]

        def value_row(i, carry, hh=hh, a=a, w=w, b=b, k=k, r=r):
            s = s_ref[hh, i]
            sa = jnp.sum(s * a, axis=0, keepdims=True)
            s_new = s * w + sa * b + v_ref[pl.ds(hh * n + i, 1), :] * k
            so_ref[hh, i] = s_new
            y_scr[pl.ds(hh * n + i, 1), :] = jnp.sum(s_new * r, axis=0, keepdims=True)
            return carry

        lax.fori_loop(0, n, value_row, 0, unroll=WKV_STEP_UNROLL)
    for hh in range(2):
        rows = slice(hh * n, (hh + 1) * n)
        y = y_scr[rows, :]
        d = y - jnp.sum(y, axis=0, keepdims=True) * inv_n
        var = jnp.sum(d * d, axis=0, keepdims=True) * inv_n
        bonus = jnp.sum(r_ref[rows, :] * kp_ref[rows, :] * rk_ref[rows, :], axis=0, keepdims=True)
        yn = d * lax.rsqrt(var + GN_EPS) * gg_ref[rows, :] + gb_ref[rows, :]
        tok_ref[rows, :] = (yn + bonus * v_ref[rows, :]) * g_ref[rows, :]


def _wkv_step(state, layer, vecs, consts):
    B = state.shape[-1]
    sshape = (2, RWKV_HEAD, RWKV_HEAD, B)
    vblk = pl.BlockSpec((LANES, B), lambda p: (p, 0))
    return pl.pallas_call(
        _wkv_step_kernel, grid=(WKV_PAIRS,),
        in_specs=[pl.BlockSpec((None,) + sshape, lambda p: (layer, p, 0, 0, 0))] + [vblk] * 10,
        out_specs=[pl.BlockSpec(sshape, lambda p: (p, 0, 0, 0)), vblk],
        out_shape=[jax.ShapeDtypeStruct(state.shape[1:], F32), jax.ShapeDtypeStruct((TOK_WIDTH, B), F32)],
        scratch_shapes=[pltpu.VMEM((LANES, B), F32)],
        compiler_params=_cparams("parallel"), name="wkv_step")(state, *vecs, *consts)


STEP_INTERLEAVE = 4


def _softmax_rows(s):
    m = jnp.max(s, axis=-1, keepdims=True)
    e = jnp.exp(s - m)
    return e * (1.0 / jnp.sum(e, axis=-1, keepdims=True))


def _mem_attn_rows(q, k, v):
    heads = [slice(h * MEM_HEAD, (h + 1) * MEM_HEAD) for h in range(MEM_HEADS)]
    s = [_dot_nt(q[:, sl], k[:, sl]) * MEM_SCALE for sl in heads]
    p = [_softmax_rows(t) for t in s]
    return jnp.concatenate([_dot(t, v[:, sl]) for sl, t in zip(heads, p)], axis=1)


def _mem_attn_step_kernel(q_ref, k_ref, v_ref, o_ref, *, bs):
    rows = MEM_TOKENS * MEM_HEADS
    col_head = lax.broadcasted_iota(jnp.int32, (SUBLANES, rows), 1) % MEM_HEADS
    row_head = lax.broadcasted_iota(jnp.int32, (SUBLANES, rows), 0) % MEM_HEADS
    own = col_head == row_head

    def group(t, carry):
        bs_ = [t * STEP_INTERLEAVE + u for u in range(STEP_INTERLEAVE)]
        s = [_dot_nt(q_ref[b], k_ref[b]) * MEM_SCALE for b in bs_]
        p = [_softmax_rows(jnp.where(own, t_, -jnp.inf)) for t_ in s]
        for b, p_ in zip(bs_, p):
            o_ref[b] = _dot(p_, v_ref[b])
        return carry

    lax.fori_loop(0, bs // STEP_INTERLEAVE, group, 0)


def _mem_attn_step(q, mk, mv, layer, *, bs):
    B = q.shape[0]
    qblk = pl.BlockSpec((bs, SUBLANES, MEM_HEAD), lambda i: (i, 0, 0))
    cblk = pl.BlockSpec((None, bs, MEM_TOKENS * MEM_HEADS, MEM_HEAD), lambda i: (layer, i, 0, 0))
    return pl.pallas_call(
        functools.partial(_mem_attn_step_kernel, bs=bs), grid=(B // bs,),
        in_specs=[qblk, cblk, cblk], out_specs=qblk,
        out_shape=jax.ShapeDtypeStruct(q.shape, F32),
        compiler_params=_cparams("parallel"), name="mem_attn_step")(q, mk, mv)


def _deepnorm_ln(res, h, g, beta):
    z = ALPHA * res + h
    mu = jnp.mean(z, axis=-1, keepdims=True)
    d = z - mu
    var = jnp.mean(d * d, axis=-1, keepdims=True)
    return d * lax.rsqrt(var + LN_EPS) * g + beta


def _out_ln_kernel(tok_ref, q_ref, k_ref, v_ref, res_ref, toks_ref, mos_ref, ress_ref, w_ref, g_ref, beta_ref,
                   of_ref, ob_ref, ofs_ref, obs_ref):
    kt = tok_ref.shape[1]

    def project_ln(tok, mo, res):
        h = (jnp.dot(tok, w_ref[0:kt, :], preferred_element_type=F32)
             + jnp.dot(mo.astype(BF16), w_ref[kt:, :], preferred_element_type=F32))
        return _deepnorm_ln(res, h, g_ref[...], beta_ref[...])

    tm = res_ref.shape[0]
    half = tm // 2 if tm % (2 * SUBLANES) == 0 else tm
    k, v = k_ref[...], v_ref[...]
    for rows in (slice(r0, r0 + half) for r0 in range(0, tm, half)):
        out = project_ln(tok_ref[rows, :], _mem_attn_rows(q_ref[rows, :], k, v), res_ref[rows, :])
        of_ref[rows, :] = out
        ob_ref[rows, :] = out.astype(BF16)

    @pl.when(pl.program_id(0) == 0)
    def _():
        out = project_ln(toks_ref[...], mos_ref[...], ress_ref[...])
        ofs_ref[...] = out
        obs_ref[...] = out.astype(BF16)


def _out_ln(tok, proj, qblk, kv, res, tok_s, mo_s, res_s, w, layer, g, beta, *, tm):
    M, Kt = tok.shape
    Ms = res_s.shape[0]
    K, N = w.shape[1], w.shape[2]
    assert Kt + MEM_WIDTH == K and M % tm == 0 and tok_s.shape[1] == Kt and mo_s.shape[1] == MEM_WIDTH
    row = lambda i: (i, 0)
    cst = lambda i: (0, 0)
    in_specs = [pl.BlockSpec((tm, Kt), row), pl.BlockSpec((tm, MEM_WIDTH), lambda i: (i, qblk)),
                pl.BlockSpec((MEM_TOKENS, MEM_WIDTH), lambda i: (0, 0)),
                pl.BlockSpec((MEM_TOKENS, MEM_WIDTH), lambda i: (0, 1)),
                pl.BlockSpec((tm, N), row),
                pl.BlockSpec((Ms, Kt), cst), pl.BlockSpec((Ms, MEM_WIDTH), cst), pl.BlockSpec((Ms, N), cst),
                pl.BlockSpec((None, K, N), lambda i: (layer, 0, 0)),
                pl.BlockSpec((1, N), cst), pl.BlockSpec((1, N), cst)]
    oblk, sblk = pl.BlockSpec((tm, N), row), pl.BlockSpec((Ms, N), cst)
    of, ob, ofs, obs = pl.pallas_call(
        _out_ln_kernel, grid=(M // tm,),
        in_specs=in_specs, out_specs=[oblk, oblk, sblk, sblk],
        out_shape=[jax.ShapeDtypeStruct((M, N), F32), jax.ShapeDtypeStruct((M, N), BF16),
                   jax.ShapeDtypeStruct((Ms, N), F32), jax.ShapeDtypeStruct((Ms, N), BF16)],
        compiler_params=_cparams("arbitrary"), name="out_ln")(
            tok, proj, kv, kv, res, tok_s, mo_s, res_s, w, g, beta)
    return (of, ob), (ofs, obs)


def _ffn_down_ln_kernel(a_ref, as_ref, w_ref, res_ref, ress_ref, g_ref, beta_ref, o_ref, os_ref, *copies):
    i, k = pl.program_id(0), pl.program_id(1)
    last = pl.num_programs(1) - 1
    ob_ref, obs_ref = copies if copies else (None, None)

    def step(lhs_ref, resid_ref, out_ref, copy_ref):
        part = lambda rows: jnp.dot(lhs_ref[rows, :], w_ref[...], preferred_element_type=F32)
        rows_all = slice(None)
        tm = out_ref.shape[0]
        half = tm // 2 if tm % (2 * SUBLANES) == 0 else tm

        @pl.when(k == 0)
        def _():
            out_ref[...] = part(rows_all)

        @pl.when((k > 0) & (k < last))
        def _():
            out_ref[...] += part(rows_all)

        @pl.when(k == last)
        def _():
            for rows in (slice(r0, r0 + half) for r0 in range(0, tm, half)):
                out = _deepnorm_ln(resid_ref[rows, :], out_ref[rows, :] + part(rows), g_ref[...], beta_ref[...])
                out_ref[rows, :] = out
                if copy_ref is not None:
                    copy_ref[rows, :] = out.astype(BF16)

    step(a_ref, res_ref, o_ref, ob_ref)

    @pl.when(i == 0)
    def _():
        step(as_ref, ress_ref, os_ref, obs_ref)


def _ffn_down_ln(a, res, a_s, res_s, w, layer, g, beta, *, tm, tk, bf16_copies):
    M, K = a.shape
    Ms = a_s.shape[0]
    N = w.shape[2]
    nk = K // tk
    assert w.shape[1] == K and K % tk == 0 and nk >= 2 and M % tm == 0 and a_s.shape[1] == K
    row = lambda i, k: (i, 0)
    cst = lambda i, k: (0, 0)
    dtypes = (F32, BF16) if bf16_copies else (F32,)
    return pl.pallas_call(
        _ffn_down_ln_kernel, grid=(M // tm, nk),
        in_specs=[pl.BlockSpec((tm, tk), lambda i, k: (i, k)),
                  pl.BlockSpec((Ms, tk), lambda i, k: (0, _rider_block(i, k, nk))),
                  pl.BlockSpec((None, tk, N), lambda i, k: (layer, k, 0)),
                  pl.BlockSpec((tm, N), row), pl.BlockSpec((Ms, N), cst),
                  pl.BlockSpec((1, N), cst), pl.BlockSpec((1, N), cst)],
        out_specs=[pl.BlockSpec((tm, N), row), pl.BlockSpec((Ms, N), cst)] * len(dtypes),
        out_shape=[jax.ShapeDtypeStruct(shape, dt) for dt in dtypes for shape in ((M, N), (Ms, N))],
        compiler_params=_cparams("arbitrary", "arbitrary"), name="ffn_down_ln")(
            a, a_s, w, res, res_s, g, beta)


def _ffn_up_kernel(x_ref, xs_ref, wg_ref, wu_ref, wd_ref, o_ref, os_ref, wdb_ref):
    wg, wu = wg_ref[...].astype(BF16), wu_ref[...].astype(BF16)

    def swiglu(x):
        gate = jnp.dot(x, wg, preferred_element_type=F32)
        up = jnp.dot(x, wu, preferred_element_type=F32)
        return (gate * jax.nn.sigmoid(gate) * up).astype(BF16)

    o_ref[...] = swiglu(x_ref[...])

    @pl.when(pl.program_id(0) == 0)
    def _():
        os_ref[...] = swiglu(xs_ref[...])
        wdb_ref[...] = wd_ref[...].astype(BF16)


def _ffn_up(x, xs, wg, wu, wd, layer, *, tm, tn):
    M, K = x.shape
    N = wg.shape[2]
    nj = N // tn
    assert wd.shape[1] == N
    wblk = pl.BlockSpec((None, K, tn), lambda i, j: (layer, 0, j))
    rider = lambda i, j: _rider_block(i, j, nj)
    return pl.pallas_call(
        _ffn_up_kernel, grid=(M // tm, nj),
        in_specs=[pl.BlockSpec((tm, K), lambda i, j: (i, 0)), pl.BlockSpec(xs.shape, lambda i, j: (0, 0)),
                  wblk, wblk, pl.BlockSpec((None, tn, wd.shape[2]), lambda i, j: (layer, rider(i, j), 0))],
        out_specs=[pl.BlockSpec((tm, tn), lambda i, j: (i, j)),
                   pl.BlockSpec((xs.shape[0], tn), lambda i, j: (0, rider(i, j))),
                   pl.BlockSpec((tn, wd.shape[2]), lambda i, j: (rider(i, j), 0))],
        out_shape=[jax.ShapeDtypeStruct((M, N), BF16), jax.ShapeDtypeStruct((xs.shape[0], N), BF16),
                   jax.ShapeDtypeStruct(wd.shape[1:], BF16)],
        compiler_params=_cparams("arbitrary", "arbitrary"), name="ffn_up")(x, xs, wg, wu, wd)


def _rope_kernel(x_ref, cos_ref, sin_ref, o_ref):
    x = x_ref[...]
    lane = lax.broadcasted_iota(jnp.int32, x.shape, 1)
    first_half = (lane % SWA_HEAD) < (SWA_HEAD // 2)
    partner = jnp.where(first_half, pltpu.roll(x, LANES - SWA_HEAD // 2, 1),
                        pltpu.roll(x, SWA_HEAD // 2, 1))
    o_ref[...] = x * cos_ref[...] + partner * sin_ref[...]


def _rope(proj, cos, sin_signed, *, tm):
    M = proj.shape[0]
    width = TOK_WIDTH + SWA_KV_WIDTH
    blk = pl.BlockSpec((tm, LANES), lambda i, s: (i, s))
    tab = pl.BlockSpec((tm, LANES), lambda i, s: (i, 0))
    return pl.pallas_call(
        _rope_kernel, grid=(M // tm, width // LANES),
        in_specs=[blk, tab, tab], out_specs=blk,
        out_shape=jax.ShapeDtypeStruct((M, width), F32),
        compiler_params=_cparams("parallel", "arbitrary"), name="rope")(proj, cos, sin_signed)


def _sink_column(sink_ref, base, rows_per_head, nheads):
    rows = rows_per_head * nheads
    hid = lax.broadcasted_iota(jnp.int32, (rows, 1), 0) // rows_per_head
    col = jnp.zeros((rows, 1), F32)
    for j in range(nheads):
        col = jnp.where(hid == j, sink_ref[base + j], col)
    return col


def _sink_softmax(s, sink):
    m = jnp.maximum(jnp.max(s, axis=-1, keepdims=True), sink)
    p = jnp.exp(s - m)
    return p * (1.0 / (jnp.sum(p, axis=-1, keepdims=True) + jnp.exp(sink - m)))


def _swa_kernel(sink_ref, q_ref, kc_ref, kp_ref, vc_ref, vp_ref, cosc_ref, sinc_ref, cosp_ref, sinp_ref,
                o_ref, krot_ref):
    n = pl.program_id(0)
    nslab_q = TOK_WIDTH // LANES
    slab = lambda ref, s: ref[:, s * LANES:(s + 1) * LANES]
    cos_c, sin_c, cos_p, sin_p = cosc_ref[...], sinc_ref[...], cosp_ref[...], sinp_ref[...]
    lane = lax.broadcasted_iota(jnp.int32, (BLOCK, LANES), 1)
    first_half = (lane % SWA_HEAD) < (SWA_HEAD // 2)
    lo = lane < SWA_HEAD

    def rope(x, cos, sin):
        partner = jnp.where(first_half, pltpu.roll(x, LANES - SWA_HEAD // 2, 1),
                            pltpu.roll(x, SWA_HEAD // 2, 1))
        return x * cos + partner * sin

    kv_slabs = SWA_KV_WIDTH // LANES
    k_cur = [rope(slab(kc_ref, j), cos_c, sin_c) for j in range(kv_slabs)]
    k_prev = [rope(slab(kp_ref, j), cos_p, sin_p) for j in range(kv_slabs)]
    for j in range(kv_slabs):
        krot_ref[:, j * LANES:(j + 1) * LANES] = k_cur[j]
    lane2 = lax.broadcasted_iota(jnp.int32, (2 * BLOCK, LANES), 1)
    kd, vd = [], []
    for g in range(SWA_KV_HEADS):
        j, half = divmod(g, 2)
        keep = (lane2 < SWA_HEAD) if half == 0 else (lane2 >= SWA_HEAD)
        dup = lambda t: jnp.where(keep, t, pltpu.roll(t, SWA_HEAD, 1)).astype(BF16)
        kd.append(dup(jnp.concatenate([k_prev[j], k_cur[j]], axis=0)))
        vd.append(dup(jnp.concatenate([slab(vp_ref, j), slab(vc_ref, j)], axis=0)))
    qi = lax.broadcasted_iota(jnp.int32, (2 * BLOCK, 2 * BLOCK), 0) % BLOCK
    si = lax.broadcasted_iota(jnp.int32, (2 * BLOCK, 2 * BLOCK), 1)
    valid = (si > qi) & (si <= qi + WINDOW) & ((n > 0) | (si >= BLOCK))
    row_lo = lax.broadcasted_iota(jnp.int32, (2 * BLOCK, 1), 0) < BLOCK
    slabs = list(range(nslab_q))
    kv_of = [(2 * s) // SWA_GROUP for s in slabs]
    assert math.frexp(SWA_SCALE)[0] == 0.5
    q = [rope(slab(q_ref, s), cos_c, sin_c) * SWA_SCALE for s in slabs]
    qs = [jnp.concatenate([jnp.where(lo, t, 0.0), jnp.where(lo, 0.0, t)], axis=0).astype(BF16) for t in q]
    sc = [lax.dot_general(t, kd[g], NT_DIMS, preferred_element_type=F32) for t, g in zip(qs, kv_of)]
    sc = [jnp.where(valid, t, -jnp.inf) for t in sc]
    p = [_sink_softmax(t, jnp.where(row_lo, sink_ref[2 * s], sink_ref[2 * s + 1])) for t, s in zip(sc, slabs)]
    o = [jnp.dot(t.astype(BF16), vd[g], preferred_element_type=F32) for t, g in zip(p, kv_of)]
    for s, t in zip(slabs, o):
        o_ref[:, s * LANES:(s + 1) * LANES] = jnp.where(lo, t[:BLOCK], t[BLOCK:]).astype(o_ref.dtype)


def _swa_prompt(proj, cos, sin_signed, sinks):
    T = proj.shape[0]
    kblk, vblk = TOK_WIDTH // SWA_KV_WIDTH, TOK_WIDTH // SWA_KV_WIDTH + 1
    prev = lambda n: jnp.maximum(n - 1, 0)
    kv_spec = lambda blk, row: pl.BlockSpec((BLOCK, SWA_KV_WIDTH), lambda n: (row(n), blk))
    tab = lambda row: pl.BlockSpec((BLOCK, LANES), lambda n: (row(n), 0))
    cur = lambda n: n
    return pl.pallas_call(
        _swa_kernel, grid=(T // BLOCK,),
        in_specs=[pl.BlockSpec(memory_space=pltpu.SMEM),
                  pl.BlockSpec((BLOCK, TOK_WIDTH), lambda n: (n, 0)),
                  kv_spec(kblk, cur), kv_spec(kblk, prev), kv_spec(vblk, cur), kv_spec(vblk, prev),
                  tab(cur), tab(cur), tab(prev), tab(prev)],
        out_specs=[pl.BlockSpec((BLOCK, TOK_WIDTH), lambda n: (n, 0)),
                   pl.BlockSpec((BLOCK, SWA_KV_WIDTH), lambda n: (n, 0))],
        out_shape=[jax.ShapeDtypeStruct((T, TOK_WIDTH), BF16),
                   jax.ShapeDtypeStruct((T, SWA_KV_WIDTH), F32)],
        compiler_params=_cparams("arbitrary"), name="swa_banded")(
            sinks, proj, proj, proj, proj, proj, cos, sin_signed, cos, sin_signed)


def _swa_step_kernel(sink_ref, q_ref, kn_ref, vn_ref, kt_ref, vt_ref, o_ref, kto_ref, vto_ref, *, bs):
    sink = _sink_column(sink_ref, 0, 1, SWA_Q_HEADS)
    newest = lax.broadcasted_iota(jnp.int32, (SWA_HEAD, WINDOW), 1) == WINDOW - 1
    kv_of_head = lax.broadcasted_iota(jnp.int32, (SWA_Q_HEADS, 1), 0) // SWA_GROUP
    kv_heads = list(range(SWA_KV_HEADS))

    def pick(per_kv):
        out = per_kv[0]
        for g in kv_heads[1:]:
            out = jnp.where(kv_of_head == g, per_kv[g], out)
        return out

    def slid(new_ref, cache_ref, b):
        new = new_ref[b]
        return [jnp.where(newest, new[:, g:g + 1], pltpu.roll(cache_ref[b, g], WINDOW - 1, 1)) for g in kv_heads]

    def group(t, carry):
        bs_ = [t * STEP_INTERLEAVE + u for u in range(STEP_INTERLEAVE)]
        kt = [slid(kn_ref, kt_ref, b) for b in bs_]
        vt = [slid(vn_ref, vt_ref, b) for b in bs_]
        for b, kt_, vt_ in zip(bs_, kt, vt):
            for g in kv_heads:
                kto_ref[b, g] = kt_[g]
                vto_ref[b, g] = vt_[g]
        s = [pick([_dot(q_ref[b], t_) for t_ in kt_]) * SWA_SCALE for b, kt_ in zip(bs_, kt)]
        p = [_sink_softmax(s_, sink) for s_ in s]
        for b, p_, vt_ in zip(bs_, p, vt):
            o_ref[b] = pick([_dot_nt(p_, t_) for t_ in vt_])
        return carry

    lax.fori_loop(0, bs // STEP_INTERLEAVE, group, 0)


def _swa_step(q, k_new, v_new, kt, vt, layer, sinks, *, bs):
    B = q.shape[0]
    qblk = pl.BlockSpec((bs, SWA_Q_HEADS, SWA_HEAD), lambda i: (i, 0, 0))
    nblk = pl.BlockSpec((bs, SWA_HEAD, SWA_KV_HEADS), lambda i: (i, 0, 0))
    cshape = (bs, SWA_KV_HEADS, SWA_HEAD, WINDOW)
    cin = pl.BlockSpec((None,) + cshape, lambda i: (layer, i, 0, 0, 0))
    cout = pl.BlockSpec(cshape, lambda i: (i, 0, 0, 0))
    cache = jax.ShapeDtypeStruct(kt.shape[1:], F32)
    return pl.pallas_call(
        functools.partial(_swa_step_kernel, bs=bs), grid=(B // bs,),
        in_specs=[pl.BlockSpec(memory_space=pltpu.SMEM), qblk, nblk, nblk, cin, cin],
        out_specs=[qblk, cout, cout],
        out_shape=[jax.ShapeDtypeStruct(q.shape, F32), cache, cache],
        compiler_params=_cparams("parallel"), name="swa_step")(sinks, q, k_new, v_new, kt, vt)


ROW_TILE = 512
FFN_ROW_TILE = 1024
COL_TILE = 512
STEP_BATCH = 16
FFN_DOWN_K_TILE = FFN_HIDDEN // 2


def _row_tile(m):
    return ROW_TILE if m % ROW_TILE == 0 else m


def _ffn_row_tile(m):
    return FFN_ROW_TILE if m % FFN_ROW_TILE == 0 else _row_tile(m)


def _pad_rows(w, rows):
    return jnp.pad(w, ((0, rows - w.shape[0]), (0, 0)))


def _pad_cols(w, cols):
    return jnp.pad(w, ((0, 0), (0, cols - w.shape[1])))


def _rwkv_weights(w_in, mu, w1, w2, a1, a2, g1, g2):
    w_lora = jnp.concatenate([_pad_cols(w1[0], LORA_PAD), _pad_cols(a1[0], LORA_PAD), g1[0]], axis=1)
    return dict(w_in=w_in.astype(BF16), w_lora=w_lora.astype(BF16), mu=mu[0][:, None, :],
                w2=_pad_rows(w2[0], LORA_PAD).astype(BF16), a2=_pad_rows(a2[0], LORA_PAD).astype(BF16),
                g2=g2[0].astype(BF16))


def _unblock_state(s_bd):
    n = RWKV_HEAD
    return jnp.stack([s_bd[:, :n, :n], s_bd[:, n:, n:]], axis=1).reshape(RWKV_HEADS, n, n)


def _rope_tables(pos):
    half = SWA_HEAD // 2
    inv = ROPE_THETA ** (-jnp.arange(half, dtype=F32) / half)
    ang = pos.astype(F32)[:, None] * inv[None, :]
    cos, sin = jnp.cos(ang), jnp.sin(ang)
    reps = LANES // SWA_HEAD
    return jnp.tile(cos, (1, 2 * reps)), jnp.tile(jnp.concatenate([-sin, sin], axis=1), (1, reps))


def _post_mixer(prompt, sample, sw, layer, bf16_copies):
    row = lambda t: t[layer][None, :]
    tok_p, proj_p, qblk, kv, x_p = prompt
    tok_s, mo_s, x_s = sample
    (x1f_p, x1b_p), (x1f_s, x1b_s) = _out_ln(tok_p, proj_p, qblk, kv, x_p, tok_s, mo_s, x_s, sw["w_out"], layer,
                                             row(sw["ln1_g"]), row(sw["ln1_b"]), tm=_row_tile(x_p.shape[0]))
    hff_p, hff_s, wd = _ffn_up(x1b_p, x1b_s, sw["w_gate"], sw["w_up"], sw["w_down"], layer,
                               tm=_ffn_row_tile(x1b_p.shape[0]), tn=COL_TILE)
    return _ffn_down_ln(hff_p, x1f_p, hff_s, x1f_s, wd[None], 0, row(sw["ln2_g"]), row(sw["ln2_b"]),
                        tm=_row_tile(x1f_p.shape[0]), tk=FFN_DOWN_K_TILE, bf16_copies=bf16_copies)


def kernel(x_prompt, x_sample, mem_prompt, cache_mem_k, cache_mem_v, state_rwkv_shift, state_rwkv_wkv, cache_swa_k, cache_swa_v, w_in_rwkv, rwkv_mu, rwkv_w0, rwkv_w1, rwkv_w2, rwkv_a0, rwkv_a1, rwkv_a2, rwkv_g1, rwkv_g2, rwkv_k_k, rwkv_k_a, rwkv_r_k, rwkv_gn_g, rwkv_gn_b, w_in_swa, swa_sinks, w_mem_kv, w_out, ln1_g, ln1_b, w_gate, w_up, w_down, ln2_g, ln2_b):
    assert DEPTH == 2 and x_prompt.shape[0] == 1 and x_sample.shape[1] == 1
    T = x_prompt.shape[1]
    B = x_sample.shape[0]
    row = lambda t: t[None, :]
    shared = dict(w_out=w_out.astype(BF16), ln1_g=ln1_g, ln1_b=ln1_b, w_gate=w_gate, w_up=w_up,
                  w_down=w_down, ln2_g=ln2_g, ln2_b=ln2_b)
    RW = _rwkv_weights(w_in_rwkv, rwkv_mu, rwkv_w1, rwkv_w2, rwkv_a1, rwkv_a2, rwkv_g1, rwkv_g2)
    rk, gn_g, gn_b = row(rwkv_r_k[0].reshape(-1)), row(rwkv_gn_g[0]), row(rwkv_gn_b[0])
    prep_consts = (row(rwkv_w0[0]), row(rwkv_a0[0]), row(rwkv_k_k[0]), row(rwkv_k_a[0]))
    w_swa = w_in_swa.astype(BF16)
    sinks = swa_sinks[0]
    q_blk_rwkv = 3 * TOK_WIDTH // MEM_WIDTH
    q_blk_swa = (TOK_WIDTH + 2 * SWA_KV_WIDTH) // MEM_WIDTH

    xp, xs = x_prompt[0], x_sample[:, 0]
    tms = _row_tile(B)
    kv = [_proj(mem_prompt[0], w_mem_kv, i, tm=MEM_TOKENS, tn=COL_TILE, name="mem_kv") for i in range(DEPTH)]
    prompt_mem_k = jnp.stack([t[:, :MEM_WIDTH] for t in kv]).reshape(DEPTH, 1, MEM_TOKENS, MEM_HEADS, MEM_HEAD)
    prompt_mem_v = jnp.stack([t[:, MEM_WIDTH:] for t in kv]).reshape(DEPTH, 1, MEM_TOKENS, MEM_HEADS, MEM_HEAD)
    mem_k = cache_mem_k.reshape(DEPTH, B, MEM_TOKENS * MEM_HEADS, MEM_HEAD)
    mem_v = cache_mem_v.reshape(DEPTH, B, MEM_TOKENS * MEM_HEADS, MEM_HEAD)

    def mem_step(q, layer):
        q_rows = jnp.pad(q.reshape(B, MEM_HEADS, MEM_HEAD), ((0, 0), (0, SUBLANES - MEM_HEADS), (0, 0)))
        out = _mem_attn_step(q_rows, mem_k, mem_v, layer, bs=STEP_BATCH)
        return out[:, :MEM_HEADS].reshape(B, MEM_WIDTH).astype(BF16)

    proj_p, proj_s = _rwkv_in(xp, xs, state_rwkv_shift[0], RW["mu"], RW["w_in"], RW["w_lora"], 0,
                              tm=_ffn_row_tile(T))
    tok_p, s_bd = _wkv_prompt(proj_p, RW["w2"], RW["a2"], RW["g2"], prep_consts + (rk, gn_g, gn_b))
    prompt_shift = xp[-1][None, None, :]
    prompt_wkv = _unblock_state(s_bd)[None, None]
    vecs = _rwkv_prep_t(proj_s, RW["w2"], RW["a2"], RW["g2"], *prep_consts)
    lanes_b = lambda t: jnp.broadcast_to(t.reshape(TOK_WIDTH, 1), (TOK_WIDTH, B))
    s_new, tok_t = _wkv_step(state_rwkv_wkv.transpose(0, 2, 3, 4, 1), 0, vecs,
                             (lanes_b(rk), lanes_b(gn_g), lanes_b(gn_b)))
    mo_s = mem_step(proj_s[:, 3 * TOK_WIDTH:3 * TOK_WIDTH + MEM_WIDTH], 0)
    sample_shift = xs[None]
    sample_wkv = s_new.transpose(3, 0, 1, 2)[None]
    xf_p, xf_s, xb_p, xb_s = _post_mixer((tok_p, proj_p, q_blk_rwkv, kv[0], xp),
                                         (tok_t.T.astype(BF16), mo_s, xs), shared, 0, bf16_copies=True)

    proj_p, proj_s = _proj(xb_p, w_swa, 0, tm=_ffn_row_tile(T), tn=COL_TILE, name="swa_proj", xs=xb_s)
    cos, sin = _rope_tables(jnp.arange(T))
    tok_p, k_rot = _swa_prompt(proj_p, cos, sin, sinks)
    v_last = proj_p[T - WINDOW:, TOK_WIDTH + SWA_KV_WIDTH:TOK_WIDTH + 2 * SWA_KV_WIDTH]
    prompt_swa_k = k_rot[T - WINDOW:].reshape(1, 1, WINDOW, SWA_KV_HEADS, SWA_HEAD)
    prompt_swa_v = v_last.reshape(1, 1, WINDOW, SWA_KV_HEADS, SWA_HEAD)
    cos, sin = _rope_tables(jnp.full((B,), PAST_LEN))
    qk = _rope(proj_s, cos, sin, tm=tms)
    chan_major = lambda t: t.reshape(B, SWA_KV_HEADS, SWA_HEAD).transpose(0, 2, 1)
    k_new = chan_major(qk[:, TOK_WIDTH:])
    v_new = chan_major(proj_s[:, TOK_WIDTH + SWA_KV_WIDTH:TOK_WIDTH + 2 * SWA_KV_WIDTH])
    o, kc, vc = _swa_step(qk[:, :TOK_WIDTH].reshape(B, SWA_Q_HEADS, SWA_HEAD), k_new, v_new,
                          cache_swa_k.transpose(0, 1, 3, 4, 2), cache_swa_v.transpose(0, 1, 3, 4, 2),
                          0, sinks, bs=STEP_BATCH)
    mo_s = mem_step(proj_s[:, TOK_WIDTH + 2 * SWA_KV_WIDTH:], 1)
    sample_swa_k, sample_swa_v = kc.transpose(0, 3, 1, 2)[None], vc.transpose(0, 3, 1, 2)[None]
    y_prompt, y_sample = _post_mixer((tok_p, proj_p, q_blk_swa, kv[1], xf_p),
                                     (o.reshape(B, TOK_WIDTH).astype(BF16), mo_s, xf_s), shared, 1,
                                     bf16_copies=False)

    return (y_prompt[None], y_sample[:, None, :], prompt_mem_k, prompt_mem_v, prompt_shift, prompt_wkv,
            prompt_swa_k, prompt_swa_v, sample_shift, sample_wkv, sample_swa_k, sample_swa_v)
```

```python
import functools
import math

import jax
import jax.numpy as jnp
from jax import lax
from jax.experimental import pallas as pl
from jax.experimental.pallas import tpu as pltpu

D_MODEL = 2048
DEPTH = 2
MEM_WIDTH = D_MODEL // 4
TOK_WIDTH = D_MODEL - MEM_WIDTH
RWKV_HEAD = 64
RWKV_HEADS = TOK_WIDTH // RWKV_HEAD
GN_EPS = RWKV_HEAD * 1e-5
SWA_HEAD = 64
SWA_Q_HEADS = TOK_WIDTH // SWA_HEAD
SWA_KV_HEADS = 4
SWA_GROUP = SWA_Q_HEADS // SWA_KV_HEADS
SWA_KV_WIDTH = SWA_KV_HEADS * SWA_HEAD
WINDOW = 128
BLOCK = 128
PAST_LEN = 8192
SWA_SCALE = SWA_HEAD ** -0.5
ROPE_THETA = 10000.0
MEM_TOKENS = 256
MEM_HEADS = 4
MEM_HEAD = MEM_WIDTH // MEM_HEADS
MEM_SCALE = MEM_HEAD ** -0.5
FFN_HIDDEN = int(math.ceil(8 * D_MODEL / 3 / 256)) * 256
ALPHA = (2 * DEPTH) ** 0.25
LN_EPS = 1e-5
LORA_PAD = 128
LORA_IN_WIDTH = 512

LANES = 128
SUBLANES = 8
VMEM_LIMIT_BYTES = 56 * 1024 * 1024

BF16 = jnp.bfloat16
F32 = jnp.float32
NT_DIMS = (((1,), (1,)), ((), ()))
TN_DIMS = (((0,), (0,)), ((), ()))


def _dot(a, b):
    return jnp.dot(a.astype(BF16), b.astype(BF16), preferred_element_type=F32)


def _dot_nt(a, b):
    return lax.dot_general(a.astype(BF16), b.astype(BF16), NT_DIMS, preferred_element_type=F32)


def _dot_tn(a, b):
    return lax.dot_general(a.astype(BF16), b.astype(BF16), TN_DIMS, preferred_element_type=F32)


def _split_dot(x, m):
    hi = x.astype(BF16)
    lo = (x - hi.astype(F32)).astype(BF16)
    return (jnp.dot(hi, m, preferred_element_type=F32)
            + jnp.dot(lo, m, preferred_element_type=F32))


def _head_ones():
    p = lax.broadcasted_iota(jnp.int32, (LANES, LANES), 0)
    q = lax.broadcasted_iota(jnp.int32, (LANES, LANES), 1)
    return ((p // RWKV_HEAD) == (q // RWKV_HEAD)).astype(BF16)


def _cparams(*sem):
    return pltpu.CompilerParams(dimension_semantics=sem, vmem_limit_bytes=VMEM_LIMIT_BYTES)


def _rider_block(i, j, nj):
    return jnp.where(i == 0, j, nj - 1)


def _proj_kernel(*refs, rider):
    if rider:
        x_ref, xs_ref, w_ref, o_ref, os_ref, xb_ref = refs
    else:
        x_ref, w_ref, o_ref, xb_ref = refs

    @pl.when(pl.program_id(1) == 0)
    def _():
        xb_ref[...] = x_ref[...].astype(BF16)

    w = w_ref[...].astype(BF16)
    o_ref[...] = jnp.dot(xb_ref[...], w, preferred_element_type=F32)
    if rider:
        @pl.when(pl.program_id(0) == 0)
        def _():
            os_ref[...] = jnp.dot(xs_ref[...].astype(BF16), w, preferred_element_type=F32)


def _proj(x, w, layer, *, tm, tn, name, xs=None):
    M, K = x.shape
    N = w.shape[2]
    assert M % tm == 0 and N % tn == 0
    nj = N // tn
    rider = xs is not None
    in_specs = [pl.BlockSpec((tm, K), lambda i, j: (i, 0))]
    out_specs = [pl.BlockSpec((tm, tn), lambda i, j: (i, j))]
    out_shape = [jax.ShapeDtypeStruct((M, N), F32)]
    if rider:
        in_specs.append(pl.BlockSpec(xs.shape, lambda i, j: (0, 0)))
        out_specs.append(pl.BlockSpec((xs.shape[0], tn), lambda i, j: (0, _rider_block(i, j, nj))))
        out_shape.append(jax.ShapeDtypeStruct((xs.shape[0], N), F32))
    in_specs.append(pl.BlockSpec((None, K, tn), lambda i, j: (layer, 0, j)))
    out = pl.pallas_call(
        functools.partial(_proj_kernel, rider=rider), grid=(M // tm, nj),
        in_specs=in_specs, out_specs=out_specs, out_shape=out_shape,
        scratch_shapes=[pltpu.VMEM((tm, K), BF16)],
        compiler_params=_cparams("arbitrary", "arbitrary"), name=name)(*([x, xs, w] if rider else [x, w]))
    return out if rider else out[0]


MIX_R, MIX_W, MIX_K, MIX_V, MIX_A, MIX_G = range(6)
RWKV_IN_WIDTH = 3 * TOK_WIDTH + MEM_WIDTH + LORA_IN_WIDTH
RWKV_IN_TILE = LORA_IN_WIDTH
RWKV_IN_KEPT = (MIX_R, MIX_K, MIX_V)
RWKV_IN_SLOT = ([0] * (TOK_WIDTH // RWKV_IN_TILE) + [1] * (TOK_WIDTH // RWKV_IN_TILE)
                + [2] * (TOK_WIDTH // RWKV_IN_TILE) + [3] * (MEM_WIDTH // RWKV_IN_TILE))
LORA_PARTS = ((MIX_W, 0, LORA_PAD), (MIX_A, LORA_PAD, 2 * LORA_PAD), (MIX_G, 2 * LORA_PAD, LORA_IN_WIDTH))
RWKV_IN_SUB = 256


def _rwkv_in_kernel(slot_ref, x_ref, xp_ref, xs_ref, xps_ref, mu_ref, w_ref, wl_ref, o_ref, os_ref,
                    lhs_ref, lhss_ref):
    i, j = pl.program_id(0), pl.program_id(1)
    tm = x_ref.shape[0]
    sub = min(RWKV_IN_SUB, tm)
    blocks = [slice(s0, s0 + sub) for s0 in range(0, tm, sub)]
    plain = len(RWKV_IN_KEPT)
    nproj = pl.num_programs(1) - 1

    def x_and_delta(rows):
        x = x_ref[rows, :]
        if rows.start == 0:
            first = jnp.where(i > 0, xp_ref[SUBLANES - 1:SUBLANES, :], 0.0)
        else:
            first = x_ref[rows.start - 1:rows.start, :]
        rowid = lax.broadcasted_iota(jnp.int32, (sub, 1), 0)
        return x, jnp.where(rowid == 0, first, pltpu.roll(x, 1, 0)) - x

    def rider_x_and_delta(rows):
        x = xs_ref[...]
        return x, xps_ref[...] - x

    def group(delta_fn, row_blocks, kept_ref, out_ref):
        @pl.when(j == 0)
        def _():
            for rows in row_blocks:
                x, d = delta_fn(rows)
                for slot, m in enumerate(RWKV_IN_KEPT):
                    kept_ref[slot, rows, :] = (x + d * mu_ref[m]).astype(BF16)
                kept_ref[plain, rows, :] = x.astype(BF16)

        @pl.when(j < nproj)
        def _():
            out_ref[...] = jnp.dot(kept_ref[slot_ref[j]], w_ref[...], preferred_element_type=F32)

        @pl.when(j == nproj)
        def _():
            for rows in row_blocks:
                x, d = delta_fn(rows)
                for m, lo, hi in LORA_PARTS:
                    out_ref[rows, lo:hi] = jnp.dot((x + d * mu_ref[m]).astype(BF16), wl_ref[:, lo:hi],
                                                   preferred_element_type=F32)

    group(x_and_delta, blocks, lhs_ref, o_ref)

    @pl.when(i == 0)
    def _():
        group(rider_x_and_delta, [slice(None)], lhss_ref, os_ref)


def _rwkv_in(x, xs, xprev_s, mu, w, w_lora, layer, *, tm):
    M, K = x.shape
    Ms = xs.shape[0]
    tn = RWKV_IN_TILE
    nproj = w.shape[2] // tn
    assert M % tm == 0 and w.shape[2] + w_lora.shape[1] == RWKV_IN_WIDTH and nproj == len(RWKV_IN_SLOT)
    assert tm % min(RWKV_IN_SUB, tm) == 0 and tm % SUBLANES == 0
    rows8 = tm // SUBLANES
    slot = jnp.asarray(RWKV_IN_SLOT + [0], jnp.int32)
    cst = lambda i, j, m: (0, 0)
    grid_spec = pltpu.PrefetchScalarGridSpec(
        num_scalar_prefetch=1, grid=(M // tm, nproj + 1),
        in_specs=[pl.BlockSpec((tm, K), lambda i, j, m: (i, 0)),
                  pl.BlockSpec((SUBLANES, K), lambda i, j, m: (jnp.maximum(i * rows8 - 1, 0), 0)),
                  pl.BlockSpec((Ms, K), cst), pl.BlockSpec((Ms, K), cst),
                  pl.BlockSpec(mu.shape, lambda i, j, m: (0, 0, 0)),
                  pl.BlockSpec((None, K, tn), lambda i, j, m: (layer, 0, jnp.minimum(j, nproj - 1))),
                  pl.BlockSpec(w_lora.shape, cst)],
        out_specs=[pl.BlockSpec((tm, tn), lambda i, j, m: (i, j)),
                   pl.BlockSpec((Ms, tn), lambda i, j, m: (0, _rider_block(i, j, nproj + 1)))],
        scratch_shapes=[pltpu.VMEM((len(RWKV_IN_KEPT) + 1, tm, K), BF16),
                        pltpu.VMEM((len(RWKV_IN_KEPT) + 1, Ms, K), BF16)])
    return pl.pallas_call(
        _rwkv_in_kernel, grid_spec=grid_spec,
        out_shape=[jax.ShapeDtypeStruct((M, RWKV_IN_WIDTH), F32), jax.ShapeDtypeStruct((Ms, RWKV_IN_WIDTH), F32)],
        compiler_params=_cparams("arbitrary", "arbitrary"), name="rwkv_in")(slot, x, x, xs, xprev_s, mu, w, w_lora)


def _softplus(z):
    return jnp.maximum(z, 0.0) + jnp.log1p(jnp.exp(-jnp.abs(z)))


def _each(f, *lists):
    return [f(*a) for a in zip(*lists)]


def _lora_hidden(hl):
    return jnp.tanh(hl[:, 0:LORA_PAD]), hl[:, LORA_PAD:2 * LORA_PAD], jax.nn.sigmoid(hl[:, 2 * LORA_PAD:])


def _prep_slabs(k, wl, al, w0, a0, k_k, k_a, ones):
    lw = _each(lambda wl_, w0_: -jnp.exp(-_softplus(-(w0_ + wl_)) - 0.5), wl, w0)
    agate = _each(lambda al_, a0_: jax.nn.sigmoid(a0_ + al_), al, a0)
    kkr = _each(lambda k_, c_: k_ * c_, k, k_k)
    ss = _each(lambda t: _split_dot(t * t, ones), kkr)
    kn = _each(lambda t, s_: t / jnp.maximum(jnp.sqrt(s_), 1e-12), kkr, ss)
    bv = _each(lambda n_, a_: n_ * a_, kn, agate)
    kp = _each(lambda k_, a_, c_: k_ * (1.0 + (a_ - 1.0) * c_), k, agate, k_a)
    return lw, kp, kn, bv


def _rwkv_prep_t_kernel(hl_ref, r_ref, k_ref, v_ref, w2_ref, a2_ref, g2_ref, w0_ref, a0_ref, kk_ref, ka_ref,
                        rt_ref, vt_ref, lw_ref, kp_ref, kn_ref, bv_ref, g_ref):
    hw, ha, hg = _lora_hidden(hl_ref[...])
    outs = _prep_slabs([k_ref[...]], [_dot(hw, w2_ref[...])], [_dot(ha, a2_ref[...])], [w0_ref[...]],
                       [a0_ref[...]], [kk_ref[...]], [ka_ref[...]], _head_ones())
    vals = [r_ref[...], v_ref[...]] + [t[0] for t in outs] + [_dot(hg, g2_ref[...])]
    for ref, val in zip((rt_ref, vt_ref, lw_ref, kp_ref, kn_ref, bv_ref, g_ref), vals):
        ref[...] = val.T


def _rwkv_prep_t(proj, w2p, a2p, g2, w0, a0, k_k, k_a):
    B = proj.shape[0]
    assert B == LANES
    nslab = TOK_WIDTH // LANES
    lora_blk = (RWKV_IN_WIDTH - LORA_IN_WIDTH) // LORA_IN_WIDTH
    col = lambda s: (0, s)
    slab = lambda base: pl.BlockSpec((B, LANES), lambda s, base=base: (0, base + s))
    outs = [jax.ShapeDtypeStruct((TOK_WIDTH, B), F32)] * 7
    return pl.pallas_call(
        _rwkv_prep_t_kernel, grid=(nslab,),
        in_specs=[pl.BlockSpec((B, LORA_IN_WIDTH), lambda s: (0, lora_blk)),
                  slab(0), slab(nslab), slab(2 * nslab),
                  pl.BlockSpec((LORA_PAD, LANES), col),
                  pl.BlockSpec((LORA_PAD, LANES), col),
                  pl.BlockSpec((2 * LORA_PAD, LANES), col),
                  pl.BlockSpec((1, LANES), col), pl.BlockSpec((1, LANES), col),
                  pl.BlockSpec((1, LANES), col), pl.BlockSpec((1, LANES), col)],
        out_specs=[pl.BlockSpec((LANES, B), lambda s: (s, 0))] * 7,
        out_shape=outs, compiler_params=_cparams("arbitrary"),
        name="rwkv_prep_t")(proj, proj, proj, proj, w2p, a2p, g2, w0, a0, k_k, k_a)


def _gn_gate(y, r, kp, v, g, rk, gg, gb, ones):
    inv_n = 1.0 / RWKV_HEAD
    rows = y[0].shape[0]
    sums = _each(lambda y_, r_, k_, rk_: _split_dot(jnp.concatenate([y_, r_ * k_ * rk_], axis=0), ones),
                 y, r, kp, rk)
    d = _each(lambda y_, s_: y_ - s_[:rows] * inv_n, y, sums)
    var = _each(lambda d_: _split_dot(d_ * d_, ones) * inv_n, d)
    return _each(lambda d_, var_, gg_, gb_, s_, v_, g_:
                 (d_ * lax.rsqrt(var_ + GN_EPS) * gg_ + gb_ + s_[rows:] * v_) * g_,
                 d, var, gg, gb, sums, v, g)


WKV_CHUNK = 64


def _wkv_masks():
    n = 2 * WKV_CHUNK
    p = lax.broadcasted_iota(jnp.int32, (n, n), 0)
    q = lax.broadcasted_iota(jnp.int32, (n, n), 1)
    same = lambda b: (p // b) == (q // b)
    pt, qt = p % WKV_CHUNK, q % WKV_CHUNK
    s8, s16, s32, s64 = same(8), same(16), same(32), same(WKV_CHUNK)
    return dict(strict=s64 & (pt > qt), incl=s64 & (pt >= qt), s8=s8,
                e16=s16 & ~s8, e32=s32 & ~s16, e64=s64 & ~s32,
                eye=(p == q).astype(F32))


WKV_PAIRS = TOK_WIDTH // LANES
WKV_STEP_CHUNKS = 2


def _wkv_lanes(r, lw, k, v, kn, bv, mk, tri, lane_lo):
    stack = lambda x: jnp.concatenate([jnp.where(lane_lo, x, 0.0), jnp.where(lane_lo, 0.0, x)], axis=0)
    n = 2 * WKV_CHUNK
    c = _each(lambda t: _split_dot_left(tri, t), lw)
    c_last = _each(lambda t: t[WKV_CHUNK - 1:WKV_CHUNK, :], c)
    e_out = _each(lambda t: jnp.exp(-t), c)
    e_end = _each(lambda t, tl: jnp.exp(tl - t), c, c_last)
    ah = _each(lambda kn_, c_, lw_: stack(-kn_ * jnp.exp(c_ - lw_)), kn, c, lw)
    rh = _each(lambda r_, c_: stack(r_ * jnp.exp(c_)), r, c)
    bh = _each(lambda b_, e_: stack(b_ * e_), bv, e_out)
    kh = _each(lambda k_, e_: stack(k_ * e_), k, e_out)
    bbar = _each(lambda b_, e_: stack(b_ * e_), bv, e_end)
    kbar = _each(lambda k_, e_: stack(k_ * e_), k, e_end)
    vs = _each(stack, v)
    gm = _each(lambda a_, r_, b_, k_: _dot_nt(jnp.concatenate([a_, r_], axis=0),
                                              jnp.concatenate([b_, k_], axis=0)), ah, rh, bh, kh)
    a_ab = _each(lambda g_: jnp.where(mk["strict"], g_[:n, :n], 0.0), gm)
    a_ak = _each(lambda g_: jnp.where(mk["strict"], g_[:n, n:], 0.0), gm)
    l_rb = _each(lambda g_: jnp.where(mk["incl"], g_[n:, :n], 0.0), gm)
    l_rk = _each(lambda g_: jnp.where(mk["incl"], g_[n:, n:], 0.0), gm)
    d1 = _each(lambda a_: jnp.where(mk["s8"], a_, 0.0), a_ab)
    x = _each(lambda d_: mk["eye"] + d_, d1)
    d2 = _each(lambda d_: _dot(d_, d_), d1)
    x = _each(lambda x_, d_: x_ + _dot(x_, d_), x, d2)
    d4 = _each(lambda d_: _dot(d_, d_), d2)
    x = _each(lambda x_, d_: x_ + _dot(x_, d_), x, d4)
    for lvl in ("e16", "e32", "e64"):
        ex = _each(lambda a_, x_: _dot(jnp.where(mk[lvl], a_, 0.0), x_), a_ab, x)
        x = _each(lambda x_, e_: x_ + _dot(x_, e_), x, ex)
    av = _each(_dot, a_ak, vs)
    tw = _each(lambda x_, a_, v_: _dot(x_, jnp.concatenate([a_, v_], axis=1)), x, ah, av)
    lwm = _each(_dot, l_rb, tw)
    lv = _each(_dot, l_rk, vs)
    qm = _each(lambda r_, l_: r_ + l_[:, :n], rh, lwm)
    y0 = _each(lambda l_, v_: l_[:, n:] + v_, lwm, lv)
    mt = _each(lambda t_, b_: _dot_tn(t_[:, :n], b_), tw, bbar)
    nt = _each(lambda t_, b_, v_, k_: _dot_tn(t_[:, n:], b_) + _dot_tn(v_, k_), tw, bbar, vs, kbar)
    return qm, y0, mt, nt, _each(jnp.exp, c_last)


def _split_dot_left(m, x):
    hi = x.astype(BF16)
    lo = (x - hi.astype(F32)).astype(BF16)
    return (jnp.dot(m, hi, preferred_element_type=F32)
            + jnp.dot(m, lo, preferred_element_type=F32))


def _wkv_kernel(r_ref, k_ref, v_ref, hl_ref, w2_ref, a2_ref, g2_ref, w0_ref, a0_ref, kk_ref, ka_ref,
                rk_ref, gg_ref, gb_ref, o_ref, sout_ref, s_scr):
    c_idx = pl.program_id(0)

    @pl.when(c_idx == 0)
    def _():
        s_scr[...] = jnp.zeros_like(s_scr)

    mk = _wkv_masks()
    ti = lax.broadcasted_iota(jnp.int32, (WKV_CHUNK, WKV_CHUNK), 0)
    tj = lax.broadcasted_iota(jnp.int32, (WKV_CHUNK, WKV_CHUNK), 1)
    tri = (ti >= tj).astype(BF16)
    lane_lo = lax.broadcasted_iota(jnp.int32, (WKV_CHUNK, LANES), 1) < RWKV_HEAD
    ones = _head_ones()
    pairs = range(WKV_PAIRS)
    lanes = [(slice(ch * WKV_CHUNK, (ch + 1) * WKV_CHUNK), slice(p * LANES, (p + 1) * LANES))
             for ch in range(WKV_STEP_CHUNKS) for p in pairs]
    cut = lambda t: [t[rows, sl] for rows, sl in lanes]
    rep = lambda t: [t[:, sl] for _, sl in lanes]
    hw, ha, hg = _lora_hidden(hl_ref[...])
    wl, al, g = _dot(hw, w2_ref[...]), _dot(ha, a2_ref[...]), _dot(hg, g2_ref[...])
    r, v = cut(r_ref), cut(v_ref)
    lw, kp, kn, bv = _prep_slabs(cut(k_ref), cut(wl), cut(al), rep(w0_ref), rep(a0_ref),
                                 rep(kk_ref), rep(ka_ref), ones)
    qm, y0, mt, nt, dec = _wkv_lanes(r, lw, kp, v, kn, bv, mk, tri, lane_lo)
    S = [s_scr[p] for p in pairs]
    y = []
    for ch in range(WKV_STEP_CHUNKS):
        part = slice(ch * WKV_PAIRS, (ch + 1) * WKV_PAIRS)
        ys = _each(lambda q_, s_, y_: _dot_nt(q_, s_) + y_, qm[part], S, y0[part])
        y += _each(lambda t: t[:WKV_CHUNK, :] + t[WKV_CHUNK:, :], ys)
        S = _each(lambda s_, d_, m_, n_: s_ * d_ + _dot(s_, m_) + n_, S, dec[part], mt[part], nt[part])
    for p in pairs:
        s_scr[p] = S[p]
    tok = _gn_gate(y, r, kp, v, cut(g), rep(rk_ref), rep(gg_ref), rep(gb_ref), ones)
    for (rows, sl), t in zip(lanes, tok):
        o_ref[rows, sl] = t.astype(o_ref.dtype)

    @pl.when(c_idx == pl.num_programs(0) - 1)
    def _():
        sout_ref[...] = s_scr[...]


def _wkv_prompt(proj, w2p, a2p, g2, consts):
    T = proj.shape[0]
    rows = WKV_STEP_CHUNKS * WKV_CHUNK
    assert T % rows == 0
    lora_blk = (RWKV_IN_WIDTH - LORA_IN_WIDTH) // LORA_IN_WIDTH
    tok = lambda blk: pl.BlockSpec((rows, TOK_WIDTH), lambda c, blk=blk: (c, blk))
    full = lambda a: pl.BlockSpec(a.shape, lambda c: (0,) * a.ndim)
    weights = (w2p, a2p, g2) + tuple(consts)
    return pl.pallas_call(
        _wkv_kernel, grid=(T // rows,),
        in_specs=[tok(0), tok(1), tok(2), pl.BlockSpec((rows, LORA_IN_WIDTH), lambda c: (c, lora_blk))]
        + [full(a) for a in weights],
        out_specs=[tok(0), pl.BlockSpec((WKV_PAIRS, LANES, LANES), lambda c: (0, 0, 0))],
        out_shape=[jax.ShapeDtypeStruct((T, TOK_WIDTH), BF16),
                   jax.ShapeDtypeStruct((WKV_PAIRS, LANES, LANES), F32)],
        scratch_shapes=[pltpu.VMEM((WKV_PAIRS, LANES, LANES), F32)],
        compiler_params=_cparams("arbitrary"), name="wkv_chunked")(proj, proj, proj, proj, *weights)


WKV_STEP_UNROLL = 4


def _wkv_step_kernel(s_ref, r_ref, v_ref, lw_ref, kp_ref, kn_ref, bv_ref, g_ref, rk_ref, gg_ref, gb_ref,
                     so_ref, tok_ref, y_scr):
    n = RWKV_HEAD
    inv_n = 1.0 / n
    for hh in range(2):
        rows = slice(hh * n, (hh + 1) * n)
        a, w = -kn_ref[rows, :], jnp.exp(lw_ref[rows, :])
        b, k, r = bv_ref[rows, :], kp_ref[rows, :], r_ref[rows, :]

        def value_row(i, carry, hh=hh, a=a, w=w, b=b, k=k, r=r):
            s = s_ref[hh, i]
            sa = jnp.sum(s * a, axis=0, keepdims=True)
            s_new = s * w + sa * b + v_ref[pl.ds(hh * n + i, 1), :] * k
            so_ref[hh, i] = s_new
            y_scr[pl.ds(hh * n + i, 1), :] = jnp.sum(s_new * r, axis=0, keepdims=True)
            return carry

        lax.fori_loop(0, n, value_row, 0, unroll=WKV_STEP_UNROLL)
    for hh in range(2):
        rows = slice(hh * n, (hh + 1) * n)
        y = y_scr[rows, :]
        d = y - jnp.sum(y, axis=0, keepdims=True) * inv_n
        var = jnp.sum(d * d, axis=0, keepdims=True) * inv_n
        bonus = jnp.sum(r_ref[rows, :] * kp_ref[rows, :] * rk_ref[rows, :], axis=0, keepdims=True)
        yn = d * lax.rsqrt(var + GN_EPS) * gg_ref[rows, :] + gb_ref[rows, :]
        tok_ref[rows, :] = (yn + bonus * v_ref[rows, :]) * g_ref[rows, :]


def _wkv_step(state, layer, vecs, consts):
    B = state.shape[-1]
    sshape = (2, RWKV_HEAD, RWKV_HEAD, B)
    vblk = pl.BlockSpec((LANES, B), lambda p: (p, 0))
    return pl.pallas_call(
        _wkv_step_kernel, grid=(WKV_PAIRS,),
        in_specs=[pl.BlockSpec((None,) + sshape, lambda p: (layer, p, 0, 0, 0))] + [vblk] * 10,
        out_specs=[pl.BlockSpec(sshape, lambda p: (p, 0, 0, 0)), vblk],
        out_shape=[jax.ShapeDtypeStruct(state.shape[1:], F32), jax.ShapeDtypeStruct((TOK_WIDTH, B), F32)],
        scratch_shapes=[pltpu.VMEM((LANES, B), F32)],
        compiler_params=_cparams("parallel"), name="wkv_step")(state, *vecs, *consts)


STEP_INTERLEAVE = 4


def _softmax_rows(s):
    m = jnp.max(s, axis=-1, keepdims=True)
    e = jnp.exp(s - m)
    return e * (1.0 / jnp.sum(e, axis=-1, keepdims=True))


def _mem_attn_rows(q, k, v):
    heads = [slice(h * MEM_HEAD, (h + 1) * MEM_HEAD) for h in range(MEM_HEADS)]
    s = [_dot_nt(q[:, sl], k[:, sl]) * MEM_SCALE for sl in heads]
    p = [_softmax_rows(t) for t in s]
    return jnp.concatenate([_dot(t, v[:, sl]) for sl, t in zip(heads, p)], axis=1)


def _mem_attn_step_kernel(q_ref, k_ref, v_ref, o_ref, *, bs):
    rows = MEM_TOKENS * MEM_HEADS
    col_head = lax.broadcasted_iota(jnp.int32, (SUBLANES, rows), 1) % MEM_HEADS
    row_head = lax.broadcasted_iota(jnp.int32, (SUBLANES, rows), 0) % MEM_HEADS
    own = col_head == row_head

    def group(t, carry):
        bs_ = [t * STEP_INTERLEAVE + u for u in range(STEP_INTERLEAVE)]
        s = [_dot_nt(q_ref[b], k_ref[b]) * MEM_SCALE for b in bs_]
        p = [_softmax_rows(jnp.where(own, t_, -jnp.inf)) for t_ in s]
        for b, p_ in zip(bs_, p):
            o_ref[b] = _dot(p_, v_ref[b])
        return carry

    lax.fori_loop(0, bs // STEP_INTERLEAVE, group, 0)


def _mem_attn_step(q, mk, mv, layer, *, bs):
    B = q.shape[0]
    qblk = pl.BlockSpec((bs, SUBLANES, MEM_HEAD), lambda i: (i, 0, 0))
    cblk = pl.BlockSpec((None, bs, MEM_TOKENS * MEM_HEADS, MEM_HEAD), lambda i: (layer, i, 0, 0))
    return pl.pallas_call(
        functools.partial(_mem_attn_step_kernel, bs=bs), grid=(B // bs,),
        in_specs=[qblk, cblk, cblk], out_specs=qblk,
        out_shape=jax.ShapeDtypeStruct(q.shape, F32),
        compiler_params=_cparams("parallel"), name="mem_attn_step")(q, mk, mv)


def _deepnorm_ln(res, h, g, beta):
    z = ALPHA * res + h
    mu = jnp.mean(z, axis=-1, keepdims=True)
    d = z - mu
    var = jnp.mean(d * d, axis=-1, keepdims=True)
    return d * lax.rsqrt(var + LN_EPS) * g + beta


def _out_ln_kernel(tok_ref, q_ref, k_ref, v_ref, res_ref, toks_ref, mos_ref, ress_ref, w_ref, g_ref, beta_ref,
                   of_ref, ob_ref, ofs_ref, obs_ref):
    kt = tok_ref.shape[1]

    def project_ln(tok, mo, res):
        h = (jnp.dot(tok, w_ref[0:kt, :], preferred_element_type=F32)
             + jnp.dot(mo.astype(BF16), w_ref[kt:, :], preferred_element_type=F32))
        return _deepnorm_ln(res, h, g_ref[...], beta_ref[...])

    tm = res_ref.shape[0]
    half = tm // 2 if tm % (2 * SUBLANES) == 0 else tm
    k, v = k_ref[...], v_ref[...]
    for rows in (slice(r0, r0 + half) for r0 in range(0, tm, half)):
        out = project_ln(tok_ref[rows, :], _mem_attn_rows(q_ref[rows, :], k, v), res_ref[rows, :])
        of_ref[rows, :] = out
        ob_ref[rows, :] = out.astype(BF16)

    @pl.when(pl.program_id(0) == 0)
    def _():
        out = project_ln(toks_ref[...], mos_ref[...], ress_ref[...])
        ofs_ref[...] = out
        obs_ref[...] = out.astype(BF16)


def _out_ln(tok, proj, qblk, kv, res, tok_s, mo_s, res_s, w, layer, g, beta, *, tm):
    M, Kt = tok.shape
    Ms = res_s.shape[0]
    K, N = w.shape[1], w.shape[2]
    assert Kt + MEM_WIDTH == K and M % tm == 0 and tok_s.shape[1] == Kt and mo_s.shape[1] == MEM_WIDTH
    row = lambda i: (i, 0)
    cst = lambda i: (0, 0)
    in_specs = [pl.BlockSpec((tm, Kt), row), pl.BlockSpec((tm, MEM_WIDTH), lambda i: (i, qblk)),
                pl.BlockSpec((MEM_TOKENS, MEM_WIDTH), lambda i: (0, 0)),
                pl.BlockSpec((MEM_TOKENS, MEM_WIDTH), lambda i: (0, 1)),
                pl.BlockSpec((tm, N), row),
                pl.BlockSpec((Ms, Kt), cst), pl.BlockSpec((Ms, MEM_WIDTH), cst), pl.BlockSpec((Ms, N), cst),
                pl.BlockSpec((None, K, N), lambda i: (layer, 0, 0)),
                pl.BlockSpec((1, N), cst), pl.BlockSpec((1, N), cst)]
    oblk, sblk = pl.BlockSpec((tm, N), row), pl.BlockSpec((Ms, N), cst)
    of, ob, ofs, obs = pl.pallas_call(
        _out_ln_kernel, grid=(M // tm,),
        in_specs=in_specs, out_specs=[oblk, oblk, sblk, sblk],
        out_shape=[jax.ShapeDtypeStruct((M, N), F32), jax.ShapeDtypeStruct((M, N), BF16),
                   jax.ShapeDtypeStruct((Ms, N), F32), jax.ShapeDtypeStruct((Ms, N), BF16)],
        compiler_params=_cparams("arbitrary"), name="out_ln")(
            tok, proj, kv, kv, res, tok_s, mo_s, res_s, w, g, beta)
    return (of, ob), (ofs, obs)


def _ffn_down_ln_kernel(a_ref, as_ref, w_ref, res_ref, ress_ref, g_ref, beta_ref, o_ref, os_ref):
    i, k = pl.program_id(0), pl.program_id(1)
    last = pl.num_programs(1) - 1

    def step(lhs_ref, resid_ref, out_ref):
        part = lambda rows: jnp.dot(lhs_ref[rows, :], w_ref[...], preferred_element_type=F32)
        rows_all = slice(None)
        tm = out_ref.shape[0]
        half = tm // 2 if tm % (2 * SUBLANES) == 0 else tm

        @pl.when(k == 0)
        def _():
            out_ref[...] = part(rows_all)

        @pl.when((k > 0) & (k < last))
        def _():
            out_ref[...] += part(rows_all)

        @pl.when(k == last)
        def _():
            for rows in (slice(r0, r0 + half) for r0 in range(0, tm, half)):
                out_ref[rows, :] = _deepnorm_ln(resid_ref[rows, :], out_ref[rows, :] + part(rows),
                                                g_ref[...], beta_ref[...])

    step(a_ref, res_ref, o_ref)

    @pl.when(i == 0)
    def _():
        step(as_ref, ress_ref, os_ref)


def _ffn_down_ln(a, res, a_s, res_s, w, layer, g, beta, *, tm, tk):
    M, K = a.shape
    Ms = a_s.shape[0]
    N = w.shape[2]
    nk = K // tk
    assert w.shape[1] == K and K % tk == 0 and nk >= 2 and M % tm == 0 and a_s.shape[1] == K
    row = lambda i, k: (i, 0)
    cst = lambda i, k: (0, 0)
    return pl.pallas_call(
        _ffn_down_ln_kernel, grid=(M // tm, nk),
        in_specs=[pl.BlockSpec((tm, tk), lambda i, k: (i, k)),
                  pl.BlockSpec((Ms, tk), lambda i, k: (0, _rider_block(i, k, nk))),
                  pl.BlockSpec((None, tk, N), lambda i, k: (layer, k, 0)),
                  pl.BlockSpec((tm, N), row), pl.BlockSpec((Ms, N), cst),
                  pl.BlockSpec((1, N), cst), pl.BlockSpec((1, N), cst)],
        out_specs=[pl.BlockSpec((tm, N), row), pl.BlockSpec((Ms, N), cst)],
        out_shape=[jax.ShapeDtypeStruct((M, N), F32), jax.ShapeDtypeStruct((Ms, N), F32)],
        compiler_params=_cparams("arbitrary", "arbitrary"), name="ffn_down_ln")(
            a, a_s, w, res, res_s, g, beta)


def _ffn_up_kernel(x_ref, xs_ref, wg_ref, wu_ref, wd_ref, o_ref, os_ref, wdb_ref):
    wg, wu = wg_ref[...].astype(BF16), wu_ref[...].astype(BF16)

    def swiglu(x):
        gate = jnp.dot(x, wg, preferred_element_type=F32)
        up = jnp.dot(x, wu, preferred_element_type=F32)
        return (gate * jax.nn.sigmoid(gate) * up).astype(BF16)

    o_ref[...] = swiglu(x_ref[...])

    @pl.when(pl.program_id(0) == 0)
    def _():
        os_ref[...] = swiglu(xs_ref[...])
        wdb_ref[...] = wd_ref[...].astype(BF16)


def _ffn_up(x, xs, wg, wu, wd, layer, *, tm, tn):
    M, K = x.shape
    N = wg.shape[2]
    nj = N // tn
    assert wd.shape[1] == N
    wblk = pl.BlockSpec((None, K, tn), lambda i, j: (layer, 0, j))
    rider = lambda i, j: _rider_block(i, j, nj)
    return pl.pallas_call(
        _ffn_up_kernel, grid=(M // tm, nj),
        in_specs=[pl.BlockSpec((tm, K), lambda i, j: (i, 0)), pl.BlockSpec(xs.shape, lambda i, j: (0, 0)),
                  wblk, wblk, pl.BlockSpec((None, tn, wd.shape[2]), lambda i, j: (layer, rider(i, j), 0))],
        out_specs=[pl.BlockSpec((tm, tn), lambda i, j: (i, j)),
                   pl.BlockSpec((xs.shape[0], tn), lambda i, j: (0, rider(i, j))),
                   pl.BlockSpec((tn, wd.shape[2]), lambda i, j: (rider(i, j), 0))],
        out_shape=[jax.ShapeDtypeStruct((M, N), BF16), jax.ShapeDtypeStruct((xs.shape[0], N), BF16),
                   jax.ShapeDtypeStruct(wd.shape[1:], BF16)],
        compiler_params=_cparams("arbitrary", "arbitrary"), name="ffn_up")(x, xs, wg, wu, wd)


def _rope_kernel(x_ref, cos_ref, sin_ref, o_ref):
    x = x_ref[...]
    lane = lax.broadcasted_iota(jnp.int32, x.shape, 1)
    first_half = (lane % SWA_HEAD) < (SWA_HEAD // 2)
    partner = jnp.where(first_half, pltpu.roll(x, LANES - SWA_HEAD // 2, 1),
                        pltpu.roll(x, SWA_HEAD // 2, 1))
    o_ref[...] = x * cos_ref[...] + partner * sin_ref[...]


def _rope(proj, cos, sin_signed, *, tm):
    M = proj.shape[0]
    width = TOK_WIDTH + SWA_KV_WIDTH
    blk = pl.BlockSpec((tm, LANES), lambda i, s: (i, s))
    tab = pl.BlockSpec((tm, LANES), lambda i, s: (i, 0))
    return pl.pallas_call(
        _rope_kernel, grid=(M // tm, width // LANES),
        in_specs=[blk, tab, tab], out_specs=blk,
        out_shape=jax.ShapeDtypeStruct((M, width), F32),
        compiler_params=_cparams("parallel", "arbitrary"), name="rope")(proj, cos, sin_signed)


def _sink_column(sink_ref, base, rows_per_head, nheads):
    rows = rows_per_head * nheads
    hid = lax.broadcasted_iota(jnp.int32, (rows, 1), 0) // rows_per_head
    col = jnp.zeros((rows, 1), F32)
    for j in range(nheads):
        col = jnp.where(hid == j, sink_ref[base + j], col)
    return col


def _sink_softmax(s, sink):
    m = jnp.maximum(jnp.max(s, axis=-1, keepdims=True), sink)
    p = jnp.exp(s - m)
    return p * (1.0 / (jnp.sum(p, axis=-1, keepdims=True) + jnp.exp(sink - m)))


def _swa_kernel(sink_ref, q_ref, kc_ref, kp_ref, vc_ref, vp_ref, cosc_ref, sinc_ref, cosp_ref, sinp_ref,
                o_ref, krot_ref):
    n = pl.program_id(0)
    nslab_q = TOK_WIDTH // LANES
    slab = lambda ref, s: ref[:, s * LANES:(s + 1) * LANES]
    cos_c, sin_c, cos_p, sin_p = cosc_ref[...], sinc_ref[...], cosp_ref[...], sinp_ref[...]
    lane = lax.broadcasted_iota(jnp.int32, (BLOCK, LANES), 1)
    first_half = (lane % SWA_HEAD) < (SWA_HEAD // 2)
    lo = lane < SWA_HEAD

    def rope(x, cos, sin):
        partner = jnp.where(first_half, pltpu.roll(x, LANES - SWA_HEAD // 2, 1),
                            pltpu.roll(x, SWA_HEAD // 2, 1))
        return x * cos + partner * sin

    kv_slabs = SWA_KV_WIDTH // LANES
    k_cur = [rope(slab(kc_ref, j), cos_c, sin_c) for j in range(kv_slabs)]
    k_prev = [rope(slab(kp_ref, j), cos_p, sin_p) for j in range(kv_slabs)]
    for j in range(kv_slabs):
        krot_ref[:, j * LANES:(j + 1) * LANES] = k_cur[j]
    lane2 = lax.broadcasted_iota(jnp.int32, (2 * BLOCK, LANES), 1)
    kd, vd = [], []
    for g in range(SWA_KV_HEADS):
        j, half = divmod(g, 2)
        keep = (lane2 < SWA_HEAD) if half == 0 else (lane2 >= SWA_HEAD)
        dup = lambda t: jnp.where(keep, t, pltpu.roll(t, SWA_HEAD, 1)).astype(BF16)
        kd.append(dup(jnp.concatenate([k_prev[j], k_cur[j]], axis=0)))
        vd.append(dup(jnp.concatenate([slab(vp_ref, j), slab(vc_ref, j)], axis=0)))
    qi = lax.broadcasted_iota(jnp.int32, (2 * BLOCK, 2 * BLOCK), 0) % BLOCK
    si = lax.broadcasted_iota(jnp.int32, (2 * BLOCK, 2 * BLOCK), 1)
    valid = (si > qi) & (si <= qi + WINDOW) & ((n > 0) | (si >= BLOCK))
    row_lo = lax.broadcasted_iota(jnp.int32, (2 * BLOCK, 1), 0) < BLOCK
    slabs = list(range(nslab_q))
    kv_of = [(2 * s) // SWA_GROUP for s in slabs]
    assert math.frexp(SWA_SCALE)[0] == 0.5
    q = [rope(slab(q_ref, s), cos_c, sin_c) * SWA_SCALE for s in slabs]
    qs = [jnp.concatenate([jnp.where(lo, t, 0.0), jnp.where(lo, 0.0, t)], axis=0).astype(BF16) for t in q]
    sc = [lax.dot_general(t, kd[g], NT_DIMS, preferred_element_type=F32) for t, g in zip(qs, kv_of)]
    sc = [jnp.where(valid, t, -jnp.inf) for t in sc]
    p = [_sink_softmax(t, jnp.where(row_lo, sink_ref[2 * s], sink_ref[2 * s + 1])) for t, s in zip(sc, slabs)]
    o = [jnp.dot(t.astype(BF16), vd[g], preferred_element_type=F32) for t, g in zip(p, kv_of)]
    for s, t in zip(slabs, o):
        o_ref[:, s * LANES:(s + 1) * LANES] = jnp.where(lo, t[:BLOCK], t[BLOCK:]).astype(o_ref.dtype)


def _swa_prompt(proj, cos, sin_signed, sinks):
    T = proj.shape[0]
    kblk, vblk = TOK_WIDTH // SWA_KV_WIDTH, TOK_WIDTH // SWA_KV_WIDTH + 1
    prev = lambda n: jnp.maximum(n - 1, 0)
    kv_spec = lambda blk, row: pl.BlockSpec((BLOCK, SWA_KV_WIDTH), lambda n: (row(n), blk))
    tab = lambda row: pl.BlockSpec((BLOCK, LANES), lambda n: (row(n), 0))
    cur = lambda n: n
    return pl.pallas_call(
        _swa_kernel, grid=(T // BLOCK,),
        in_specs=[pl.BlockSpec(memory_space=pltpu.SMEM),
                  pl.BlockSpec((BLOCK, TOK_WIDTH), lambda n: (n, 0)),
                  kv_spec(kblk, cur), kv_spec(kblk, prev), kv_spec(vblk, cur), kv_spec(vblk, prev),
                  tab(cur), tab(cur), tab(prev), tab(prev)],
        out_specs=[pl.BlockSpec((BLOCK, TOK_WIDTH), lambda n: (n, 0)),
                   pl.BlockSpec((BLOCK, SWA_KV_WIDTH), lambda n: (n, 0))],
        out_shape=[jax.ShapeDtypeStruct((T, TOK_WIDTH), BF16),
                   jax.ShapeDtypeStruct((T, SWA_KV_WIDTH), F32)],
        compiler_params=_cparams("arbitrary"), name="swa_banded")(
            sinks, proj, proj, proj, proj, proj, cos, sin_signed, cos, sin_signed)


def _swa_step_kernel(sink_ref, q_ref, kn_ref, vn_ref, kt_ref, vt_ref, o_ref, kto_ref, vto_ref, *, bs):
    sink = _sink_column(sink_ref, 0, 1, SWA_Q_HEADS)
    newest = lax.broadcasted_iota(jnp.int32, (SWA_HEAD, WINDOW), 1) == WINDOW - 1
    kv_of_head = lax.broadcasted_iota(jnp.int32, (SWA_Q_HEADS, 1), 0) // SWA_GROUP
    kv_heads = list(range(SWA_KV_HEADS))

    def pick(per_kv):
        out = per_kv[0]
        for g in kv_heads[1:]:
            out = jnp.where(kv_of_head == g, per_kv[g], out)
        return out

    def slid(new_ref, cache_ref, b):
        new = new_ref[b]
        return [jnp.where(newest, new[:, g:g + 1], pltpu.roll(cache_ref[b, g], WINDOW - 1, 1)) for g in kv_heads]

    def group(t, carry):
        bs_ = [t * STEP_INTERLEAVE + u for u in range(STEP_INTERLEAVE)]
        kt = [slid(kn_ref, kt_ref, b) for b in bs_]
        vt = [slid(vn_ref, vt_ref, b) for b in bs_]
        for b, kt_, vt_ in zip(bs_, kt, vt):
            for g in kv_heads:
                kto_ref[b, g] = kt_[g]
                vto_ref[b, g] = vt_[g]
        s = [pick([_dot(q_ref[b], t_) for t_ in kt_]) * SWA_SCALE for b, kt_ in zip(bs_, kt)]
        p = [_sink_softmax(s_, sink) for s_ in s]
        for b, p_, vt_ in zip(bs_, p, vt):
            o_ref[b] = pick([_dot_nt(p_, t_) for t_ in vt_])
        return carry

    lax.fori_loop(0, bs // STEP_INTERLEAVE, group, 0)


def _swa_step(q, k_new, v_new, kt, vt, layer, sinks, *, bs):
    B = q.shape[0]
    qblk = pl.BlockSpec((bs, SWA_Q_HEADS, SWA_HEAD), lambda i: (i, 0, 0))
    nblk = pl.BlockSpec((bs, SWA_HEAD, SWA_KV_HEADS), lambda i: (i, 0, 0))
    cshape = (bs, SWA_KV_HEADS, SWA_HEAD, WINDOW)
    cin = pl.BlockSpec((None,) + cshape, lambda i: (layer, i, 0, 0, 0))
    cout = pl.BlockSpec(cshape, lambda i: (i, 0, 0, 0))
    cache = jax.ShapeDtypeStruct(kt.shape[1:], F32)
    return pl.pallas_call(
        functools.partial(_swa_step_kernel, bs=bs), grid=(B // bs,),
        in_specs=[pl.BlockSpec(memory_space=pltpu.SMEM), qblk, nblk, nblk, cin, cin],
        out_specs=[qblk, cout, cout],
        out_shape=[jax.ShapeDtypeStruct(q.shape, F32), cache, cache],
        compiler_params=_cparams("parallel"), name="swa_step")(sinks, q, k_new, v_new, kt, vt)


ROW_TILE = 512
FFN_ROW_TILE = 1024
FFN_UP_ROW_TILE = 2048
FFN_UP_COL_TILE = 256
COL_TILE = 512
STEP_BATCH = 16
FFN_DOWN_K_TILE = FFN_HIDDEN // 2


def _row_tile(m):
    return ROW_TILE if m % ROW_TILE == 0 else m


def _ffn_row_tile(m):
    return FFN_ROW_TILE if m % FFN_ROW_TILE == 0 else _row_tile(m)


def _ffn_up_row_tile(m):
    return FFN_UP_ROW_TILE if m % FFN_UP_ROW_TILE == 0 else _ffn_row_tile(m)


def _pad_rows(w, rows):
    return jnp.pad(w, ((0, rows - w.shape[0]), (0, 0)))


def _pad_cols(w, cols):
    return jnp.pad(w, ((0, 0), (0, cols - w.shape[1])))


def _rwkv_weights(w_in, mu, w1, w2, a1, a2, g1, g2):
    w_lora = jnp.concatenate([_pad_cols(w1[0], LORA_PAD), _pad_cols(a1[0], LORA_PAD), g1[0]], axis=1)
    return dict(w_in=w_in.astype(BF16), w_lora=w_lora.astype(BF16), mu=mu[0][:, None, :],
                w2=_pad_rows(w2[0], LORA_PAD).astype(BF16), a2=_pad_rows(a2[0], LORA_PAD).astype(BF16),
                g2=g2[0].astype(BF16))


def _unblock_state(s_bd):
    n = RWKV_HEAD
    return jnp.stack([s_bd[:, :n, :n], s_bd[:, n:, n:]], axis=1).reshape(RWKV_HEADS, n, n)


def _rope_tables(pos):
    half = SWA_HEAD // 2
    inv = ROPE_THETA ** (-jnp.arange(half, dtype=F32) / half)
    ang = pos.astype(F32)[:, None] * inv[None, :]
    cos, sin = jnp.cos(ang), jnp.sin(ang)
    reps = LANES // SWA_HEAD
    return jnp.tile(cos, (1, 2 * reps)), jnp.tile(jnp.concatenate([-sin, sin], axis=1), (1, reps))


def _post_mixer(prompt, sample, sw, layer):
    row = lambda t: t[layer][None, :]
    tok_p, proj_p, qblk, kv, x_p = prompt
    tok_s, mo_s, x_s = sample
    (x1f_p, x1b_p), (x1f_s, x1b_s) = _out_ln(tok_p, proj_p, qblk, kv, x_p, tok_s, mo_s, x_s, sw["w_out"], layer,
                                             row(sw["ln1_g"]), row(sw["ln1_b"]), tm=_row_tile(x_p.shape[0]))
    hff_p, hff_s, wd = _ffn_up(x1b_p, x1b_s, sw["w_gate"], sw["w_up"], sw["w_down"], layer,
                               tm=_ffn_up_row_tile(x1b_p.shape[0]), tn=FFN_UP_COL_TILE)
    return _ffn_down_ln(hff_p, x1f_p, hff_s, x1f_s, wd[None], 0, row(sw["ln2_g"]), row(sw["ln2_b"]),
                        tm=_row_tile(x1f_p.shape[0]), tk=FFN_DOWN_K_TILE)


def kernel(x_prompt, x_sample, mem_prompt, cache_mem_k, cache_mem_v, state_rwkv_shift, state_rwkv_wkv, cache_swa_k, cache_swa_v, w_in_rwkv, rwkv_mu, rwkv_w0, rwkv_w1, rwkv_w2, rwkv_a0, rwkv_a1, rwkv_a2, rwkv_g1, rwkv_g2, rwkv_k_k, rwkv_k_a, rwkv_r_k, rwkv_gn_g, rwkv_gn_b, w_in_swa, swa_sinks, w_mem_kv, w_out, ln1_g, ln1_b, w_gate, w_up, w_down, ln2_g, ln2_b):
    assert DEPTH == 2 and x_prompt.shape[0] == 1 and x_sample.shape[1] == 1
    T = x_prompt.shape[1]
    B = x_sample.shape[0]
    row = lambda t: t[None, :]
    shared = dict(w_out=w_out.astype(BF16), ln1_g=ln1_g, ln1_b=ln1_b, w_gate=w_gate, w_up=w_up,
                  w_down=w_down, ln2_g=ln2_g, ln2_b=ln2_b)
    RW = _rwkv_weights(w_in_rwkv, rwkv_mu, rwkv_w1, rwkv_w2, rwkv_a1, rwkv_a2, rwkv_g1, rwkv_g2)
    rk, gn_g, gn_b = row(rwkv_r_k[0].reshape(-1)), row(rwkv_gn_g[0]), row(rwkv_gn_b[0])
    prep_consts = (row(rwkv_w0[0]), row(rwkv_a0[0]), row(rwkv_k_k[0]), row(rwkv_k_a[0]))
    w_swa = w_in_swa.astype(BF16)
    sinks = swa_sinks[0]
    q_blk_rwkv = 3 * TOK_WIDTH // MEM_WIDTH
    q_blk_swa = (TOK_WIDTH + 2 * SWA_KV_WIDTH) // MEM_WIDTH

    xp, xs = x_prompt[0], x_sample[:, 0]
    tms = _row_tile(B)
    kv = [_proj(mem_prompt[0], w_mem_kv, i, tm=MEM_TOKENS, tn=COL_TILE, name="mem_kv") for i in range(DEPTH)]
    prompt_mem_k = jnp.stack([t[:, :MEM_WIDTH] for t in kv]).reshape(DEPTH, 1, MEM_TOKENS, MEM_HEADS, MEM_HEAD)
    prompt_mem_v = jnp.stack([t[:, MEM_WIDTH:] for t in kv]).reshape(DEPTH, 1, MEM_TOKENS, MEM_HEADS, MEM_HEAD)
    mem_k = cache_mem_k.reshape(DEPTH, B, MEM_TOKENS * MEM_HEADS, MEM_HEAD)
    mem_v = cache_mem_v.reshape(DEPTH, B, MEM_TOKENS * MEM_HEADS, MEM_HEAD)

    def mem_step(q, layer):
        q_rows = jnp.pad(q.reshape(B, MEM_HEADS, MEM_HEAD), ((0, 0), (0, SUBLANES - MEM_HEADS), (0, 0)))
        out = _mem_attn_step(q_rows, mem_k, mem_v, layer, bs=STEP_BATCH)
        return out[:, :MEM_HEADS].reshape(B, MEM_WIDTH).astype(BF16)

    proj_p, proj_s = _rwkv_in(xp, xs, state_rwkv_shift[0], RW["mu"], RW["w_in"], RW["w_lora"], 0,
                              tm=_ffn_row_tile(T))
    tok_p, s_bd = _wkv_prompt(proj_p, RW["w2"], RW["a2"], RW["g2"], prep_consts + (rk, gn_g, gn_b))
    prompt_shift = xp[-1][None, None, :]
    prompt_wkv = _unblock_state(s_bd)[None, None]
    vecs = _rwkv_prep_t(proj_s, RW["w2"], RW["a2"], RW["g2"], *prep_consts)
    lanes_b = lambda t: jnp.broadcast_to(t.reshape(TOK_WIDTH, 1), (TOK_WIDTH, B))
    s_new, tok_t = _wkv_step(state_rwkv_wkv.transpose(0, 2, 3, 4, 1), 0, vecs,
                             (lanes_b(rk), lanes_b(gn_g), lanes_b(gn_b)))
    mo_s = mem_step(proj_s[:, 3 * TOK_WIDTH:3 * TOK_WIDTH + MEM_WIDTH], 0)
    sample_shift = xs[None]
    sample_wkv = s_new.transpose(3, 0, 1, 2)[None]
    xf_p, xf_s = _post_mixer((tok_p, proj_p, q_blk_rwkv, kv[0], xp), (tok_t.T.astype(BF16), mo_s, xs), shared, 0)

    proj_p, proj_s = _proj(xf_p, w_swa, 0, tm=_ffn_row_tile(T), tn=COL_TILE, name="swa_proj", xs=xf_s)
    cos, sin = _rope_tables(jnp.arange(T))
    tok_p, k_rot = _swa_prompt(proj_p, cos, sin, sinks)
    v_last = proj_p[T - WINDOW:, TOK_WIDTH + SWA_KV_WIDTH:TOK_WIDTH + 2 * SWA_KV_WIDTH]
    prompt_swa_k = k_rot[T - WINDOW:].reshape(1, 1, WINDOW, SWA_KV_HEADS, SWA_HEAD)
    prompt_swa_v = v_last.reshape(1, 1, WINDOW, SWA_KV_HEADS, SWA_HEAD)
    cos, sin = _rope_tables(jnp.full((B,), PAST_LEN))
    qk = _rope(proj_s, cos, sin, tm=tms)
    chan_major = lambda t: t.reshape(B, SWA_KV_HEADS, SWA_HEAD).transpose(0, 2, 1)
    k_new = chan_major(qk[:, TOK_WIDTH:])
    v_new = chan_major(proj_s[:, TOK_WIDTH + SWA_KV_WIDTH:TOK_WIDTH + 2 * SWA_KV_WIDTH])
    o, kc, vc = _swa_step(qk[:, :TOK_WIDTH].reshape(B, SWA_Q_HEADS, SWA_HEAD), k_new, v_new,
                          cache_swa_k.transpose(0, 1, 3, 4, 2), cache_swa_v.transpose(0, 1, 3, 4, 2),
                          0, sinks, bs=STEP_BATCH)
    mo_s = mem_step(proj_s[:, TOK_WIDTH + 2 * SWA_KV_WIDTH:], 1)
    sample_swa_k, sample_swa_v = kc.transpose(0, 3, 1, 2)[None], vc.transpose(0, 3, 1, 2)[None]
    y_prompt, y_sample = _post_mixer((tok_p, proj_p, q_blk_swa, kv[1], xf_p),
                                     (o.reshape(B, TOK_WIDTH).astype(BF16), mo_s, xf_s), shared, 1)

    return (y_prompt[None], y_sample[:, None, :], prompt_mem_k, prompt_mem_v, prompt_shift, prompt_wkv,
            prompt_swa_k, prompt_swa_v, sample_shift, sample_wkv, sample_swa_k, sample_swa_v)
```

```python
import functools
import math

import jax
import jax.numpy as jnp
from jax import lax
from jax.experimental import pallas as pl
from jax.experimental.pallas import tpu as pltpu

D_MODEL = 2048
DEPTH = 2
MEM_WIDTH = D_MODEL // 4
TOK_WIDTH = D_MODEL - MEM_WIDTH
RWKV_HEAD = 64
RWKV_HEADS = TOK_WIDTH // RWKV_HEAD
GN_EPS = RWKV_HEAD * 1e-5
SWA_HEAD = 64
SWA_Q_HEADS = TOK_WIDTH // SWA_HEAD
SWA_KV_HEADS = 4
SWA_GROUP = SWA_Q_HEADS // SWA_KV_HEADS
SWA_KV_WIDTH = SWA_KV_HEADS * SWA_HEAD
WINDOW = 128
BLOCK = 128
PAST_LEN = 8192
SWA_SCALE = SWA_HEAD ** -0.5
ROPE_THETA = 10000.0
MEM_TOKENS = 256
MEM_HEADS = 4
MEM_HEAD = MEM_WIDTH // MEM_HEADS
MEM_SCALE = MEM_HEAD ** -0.5
FFN_HIDDEN = int(math.ceil(8 * D_MODEL / 3 / 256)) * 256
ALPHA = (2 * DEPTH) ** 0.25
LN_EPS = 1e-5
LORA_PAD = 128
LORA_IN_WIDTH = 512

LANES = 128
SUBLANES = 8
VMEM_LIMIT_BYTES = 56 * 1024 * 1024

BF16 = jnp.bfloat16
F32 = jnp.float32
NT_DIMS = (((1,), (1,)), ((), ()))
TN_DIMS = (((0,), (0,)), ((), ()))


def _dot(a, b):
    return jnp.dot(a.astype(BF16), b.astype(BF16), preferred_element_type=F32)


def _dot_nt(a, b):
    return lax.dot_general(a.astype(BF16), b.astype(BF16), NT_DIMS, preferred_element_type=F32)


def _dot_tn(a, b):
    return lax.dot_general(a.astype(BF16), b.astype(BF16), TN_DIMS, preferred_element_type=F32)


def _split_dot(x, m):
    hi = x.astype(BF16)
    lo = (x - hi.astype(F32)).astype(BF16)
    return (jnp.dot(hi, m, preferred_element_type=F32)
            + jnp.dot(lo, m, preferred_element_type=F32))


def _head_ones():
    p = lax.broadcasted_iota(jnp.int32, (LANES, LANES), 0)
    q = lax.broadcasted_iota(jnp.int32, (LANES, LANES), 1)
    return ((p // RWKV_HEAD) == (q // RWKV_HEAD)).astype(BF16)


def _cparams(*sem):
    return pltpu.CompilerParams(dimension_semantics=sem, vmem_limit_bytes=VMEM_LIMIT_BYTES)


def _rider_block(i, j, nj):
    return jnp.where(i == 0, j, nj - 1)


def _proj_kernel(*refs, rider):
    if rider:
        x_ref, xs_ref, w_ref, o_ref, os_ref, xb_ref = refs
    else:
        x_ref, w_ref, o_ref, xb_ref = refs

    @pl.when(pl.program_id(1) == 0)
    def _():
        xb_ref[...] = x_ref[...].astype(BF16)

    w = w_ref[...].astype(BF16)
    o_ref[...] = jnp.dot(xb_ref[...], w, preferred_element_type=F32)
    if rider:
        @pl.when(pl.program_id(0) == 0)
        def _():
            os_ref[...] = jnp.dot(xs_ref[...].astype(BF16), w, preferred_element_type=F32)


def _proj(x, w, layer, *, tm, tn, name, xs=None):
    M, K = x.shape
    N = w.shape[2]
    assert M % tm == 0 and N % tn == 0
    nj = N // tn
    rider = xs is not None
    in_specs = [pl.BlockSpec((tm, K), lambda i, j: (i, 0))]
    out_specs = [pl.BlockSpec((tm, tn), lambda i, j: (i, j))]
    out_shape = [jax.ShapeDtypeStruct((M, N), F32)]
    if rider:
        in_specs.append(pl.BlockSpec(xs.shape, lambda i, j: (0, 0)))
        out_specs.append(pl.BlockSpec((xs.shape[0], tn), lambda i, j: (0, _rider_block(i, j, nj))))
        out_shape.append(jax.ShapeDtypeStruct((xs.shape[0], N), F32))
    in_specs.append(pl.BlockSpec((None, K, tn), lambda i, j: (layer, 0, j)))
    out = pl.pallas_call(
        functools.partial(_proj_kernel, rider=rider), grid=(M // tm, nj),
        in_specs=in_specs, out_specs=out_specs, out_shape=out_shape,
        scratch_shapes=[pltpu.VMEM((tm, K), BF16)],
        compiler_params=_cparams("arbitrary", "arbitrary"), name=name)(*([x, xs, w] if rider else [x, w]))
    return out if rider else out[0]


MIX_R, MIX_W, MIX_K, MIX_V, MIX_A, MIX_G = range(6)
RWKV_IN_WIDTH = 3 * TOK_WIDTH + MEM_WIDTH + LORA_IN_WIDTH
RWKV_IN_TILE = LORA_IN_WIDTH
RWKV_IN_KEPT = (MIX_R, MIX_K, MIX_V)
RWKV_IN_SLOT = ([0] * (TOK_WIDTH // RWKV_IN_TILE) + [1] * (TOK_WIDTH // RWKV_IN_TILE)
                + [2] * (TOK_WIDTH // RWKV_IN_TILE) + [3] * (MEM_WIDTH // RWKV_IN_TILE))
LORA_PARTS = ((MIX_W, 0, LORA_PAD), (MIX_A, LORA_PAD, 2 * LORA_PAD), (MIX_G, 2 * LORA_PAD, LORA_IN_WIDTH))
RWKV_IN_SUB = 256


def _rwkv_in_kernel(slot_ref, x_ref, xp_ref, xs_ref, xps_ref, mu_ref, w_ref, wl_ref, o_ref, os_ref,
                    lhs_ref, lhss_ref):
    i, j = pl.program_id(0), pl.program_id(1)
    tm = x_ref.shape[0]
    sub = min(RWKV_IN_SUB, tm)
    blocks = [slice(s0, s0 + sub) for s0 in range(0, tm, sub)]
    plain = len(RWKV_IN_KEPT)
    nproj = pl.num_programs(1) - 1

    def x_and_delta(rows):
        x = x_ref[rows, :]
        if rows.start == 0:
            first = jnp.where(i > 0, xp_ref[SUBLANES - 1:SUBLANES, :], 0.0)
        else:
            first = x_ref[rows.start - 1:rows.start, :]
        rowid = lax.broadcasted_iota(jnp.int32, (sub, 1), 0)
        return x, jnp.where(rowid == 0, first, pltpu.roll(x, 1, 0)) - x

    def rider_x_and_delta(rows):
        x = xs_ref[...]
        return x, xps_ref[...] - x

    def group(delta_fn, row_blocks, kept_ref, out_ref):
        @pl.when(j == 0)
        def _():
            for rows in row_blocks:
                x, d = delta_fn(rows)
                for slot, m in enumerate(RWKV_IN_KEPT):
                    kept_ref[slot, rows, :] = (x + d * mu_ref[m]).astype(BF16)
                kept_ref[plain, rows, :] = x.astype(BF16)

        @pl.when(j < nproj)
        def _():
            out_ref[...] = jnp.dot(kept_ref[slot_ref[j]], w_ref[...], preferred_element_type=F32)

        @pl.when(j == nproj)
        def _():
            for rows in row_blocks:
                x, d = delta_fn(rows)
                for m, lo, hi in LORA_PARTS:
                    out_ref[rows, lo:hi] = jnp.dot((x + d * mu_ref[m]).astype(BF16), wl_ref[:, lo:hi],
                                                   preferred_element_type=F32)

    group(x_and_delta, blocks, lhs_ref, o_ref)

    @pl.when(i == 0)
    def _():
        group(rider_x_and_delta, [slice(None)], lhss_ref, os_ref)


def _rwkv_in(x, xs, xprev_s, mu, w, w_lora, layer, *, tm):
    M, K = x.shape
    Ms = xs.shape[0]
    tn = RWKV_IN_TILE
    nproj = w.shape[2] // tn
    assert M % tm == 0 and w.shape[2] + w_lora.shape[1] == RWKV_IN_WIDTH and nproj == len(RWKV_IN_SLOT)
    assert tm % min(RWKV_IN_SUB, tm) == 0 and tm % SUBLANES == 0
    rows8 = tm // SUBLANES
    slot = jnp.asarray(RWKV_IN_SLOT + [0], jnp.int32)
    cst = lambda i, j, m: (0, 0)
    grid_spec = pltpu.PrefetchScalarGridSpec(
        num_scalar_prefetch=1, grid=(M // tm, nproj + 1),
        in_specs=[pl.BlockSpec((tm, K), lambda i, j, m: (i, 0)),
                  pl.BlockSpec((SUBLANES, K), lambda i, j, m: (jnp.maximum(i * rows8 - 1, 0), 0)),
                  pl.BlockSpec((Ms, K), cst), pl.BlockSpec((Ms, K), cst),
                  pl.BlockSpec(mu.shape, lambda i, j, m: (0, 0, 0)),
                  pl.BlockSpec((None, K, tn), lambda i, j, m: (layer, 0, jnp.minimum(j, nproj - 1))),
                  pl.BlockSpec(w_lora.shape, cst)],
        out_specs=[pl.BlockSpec((tm, tn), lambda i, j, m: (i, j)),
                   pl.BlockSpec((Ms, tn), lambda i, j, m: (0, _rider_block(i, j, nproj + 1)))],
        scratch_shapes=[pltpu.VMEM((len(RWKV_IN_KEPT) + 1, tm, K), BF16),
                        pltpu.VMEM((len(RWKV_IN_KEPT) + 1, Ms, K), BF16)])
    return pl.pallas_call(
        _rwkv_in_kernel, grid_spec=grid_spec,
        out_shape=[jax.ShapeDtypeStruct((M, RWKV_IN_WIDTH), F32), jax.ShapeDtypeStruct((Ms, RWKV_IN_WIDTH), F32)],
        compiler_params=_cparams("arbitrary", "arbitrary"), name="rwkv_in")(slot, x, x, xs, xprev_s, mu, w, w_lora)


def _softplus(z):
    return jnp.maximum(z, 0.0) + jnp.log1p(jnp.exp(-jnp.abs(z)))


def _each(f, *lists):
    return [f(*a) for a in zip(*lists)]


def _lora_hidden(hl):
    return jnp.tanh(hl[:, 0:LORA_PAD]), hl[:, LORA_PAD:2 * LORA_PAD], jax.nn.sigmoid(hl[:, 2 * LORA_PAD:])


def _prep_slabs(k, wl, al, w0, a0, k_k, k_a, ones):
    lw = _each(lambda wl_, w0_: -jnp.exp(-_softplus(-(w0_ + wl_)) - 0.5), wl, w0)
    agate = _each(lambda al_, a0_: jax.nn.sigmoid(a0_ + al_), al, a0)
    kkr = _each(lambda k_, c_: k_ * c_, k, k_k)
    ss = _each(lambda t: _split_dot(t * t, ones), kkr)
    kn = _each(lambda t, s_: t / jnp.maximum(jnp.sqrt(s_), 1e-12), kkr, ss)
    bv = _each(lambda n_, a_: n_ * a_, kn, agate)
    kp = _each(lambda k_, a_, c_: k_ * (1.0 + (a_ - 1.0) * c_), k, agate, k_a)
    return lw, kp, kn, bv


def _rwkv_prep_t_kernel(hl_ref, r_ref, k_ref, v_ref, w2_ref, a2_ref, g2_ref, w0_ref, a0_ref, kk_ref, ka_ref,
                        rt_ref, vt_ref, lw_ref, kp_ref, kn_ref, bv_ref, g_ref):
    hw, ha, hg = _lora_hidden(hl_ref[...])
    outs = _prep_slabs([k_ref[...]], [_dot(hw, w2_ref[...])], [_dot(ha, a2_ref[...])], [w0_ref[...]],
                       [a0_ref[...]], [kk_ref[...]], [ka_ref[...]], _head_ones())
    vals = [r_ref[...], v_ref[...]] + [t[0] for t in outs] + [_dot(hg, g2_ref[...])]
    for ref, val in zip((rt_ref, vt_ref, lw_ref, kp_ref, kn_ref, bv_ref, g_ref), vals):
        ref[...] = val.T


def _rwkv_prep_t(proj, w2p, a2p, g2, w0, a0, k_k, k_a):
    B = proj.shape[0]
    assert B == LANES
    nslab = TOK_WIDTH // LANES
    lora_blk = (RWKV_IN_WIDTH - LORA_IN_WIDTH) // LORA_IN_WIDTH
    col = lambda s: (0, s)
    slab = lambda base: pl.BlockSpec((B, LANES), lambda s, base=base: (0, base + s))
    outs = [jax.ShapeDtypeStruct((TOK_WIDTH, B), F32)] * 7
    return pl.pallas_call(
        _rwkv_prep_t_kernel, grid=(nslab,),
        in_specs=[pl.BlockSpec((B, LORA_IN_WIDTH), lambda s: (0, lora_blk)),
                  slab(0), slab(nslab), slab(2 * nslab),
                  pl.BlockSpec((LORA_PAD, LANES), col),
                  pl.BlockSpec((LORA_PAD, LANES), col),
                  pl.BlockSpec((2 * LORA_PAD, LANES), col),
                  pl.BlockSpec((1, LANES), col), pl.BlockSpec((1, LANES), col),
                  pl.BlockSpec((1, LANES), col), pl.BlockSpec((1, LANES), col)],
        out_specs=[pl.BlockSpec((LANES, B), lambda s: (s, 0))] * 7,
        out_shape=outs, compiler_params=_cparams("arbitrary"),
        name="rwkv_prep_t")(proj, proj, proj, proj, w2p, a2p, g2, w0, a0, k_k, k_a)


def _gn_gate(y, r, kp, v, g, rk, gg, gb, ones):
    inv_n = 1.0 / RWKV_HEAD
    rows = y[0].shape[0]
    sums = _each(lambda y_, r_, k_, rk_: _split_dot(jnp.concatenate([y_, r_ * k_ * rk_], axis=0), ones),
                 y, r, kp, rk)
    d = _each(lambda y_, s_: y_ - s_[:rows] * inv_n, y, sums)
    var = _each(lambda d_: _split_dot(d_ * d_, ones) * inv_n, d)
    return _each(lambda d_, var_, gg_, gb_, s_, v_, g_:
                 (d_ * lax.rsqrt(var_ + GN_EPS) * gg_ + gb_ + s_[rows:] * v_) * g_,
                 d, var, gg, gb, sums, v, g)


WKV_CHUNK = 64


def _wkv_masks():
    n = 2 * WKV_CHUNK
    p = lax.broadcasted_iota(jnp.int32, (n, n), 0)
    q = lax.broadcasted_iota(jnp.int32, (n, n), 1)
    same = lambda b: (p // b) == (q // b)
    pt, qt = p % WKV_CHUNK, q % WKV_CHUNK
    s8, s16, s32, s64 = same(8), same(16), same(32), same(WKV_CHUNK)
    return dict(strict=s64 & (pt > qt), incl=s64 & (pt >= qt), s8=s8,
                e16=s16 & ~s8, e32=s32 & ~s16, e64=s64 & ~s32,
                eye=(p == q).astype(F32))


WKV_PAIRS = TOK_WIDTH // LANES
WKV_STEP_CHUNKS = 4


def _wkv_lanes(r, lw, k, v, kn, bv, mk, tri, lane_lo):
    stack = lambda x: jnp.concatenate([jnp.where(lane_lo, x, 0.0), jnp.where(lane_lo, 0.0, x)], axis=0)
    n = 2 * WKV_CHUNK
    c = _each(lambda t: _split_dot_left(tri, t), lw)
    c_last = _each(lambda t: t[WKV_CHUNK - 1:WKV_CHUNK, :], c)
    e_out = _each(lambda t: jnp.exp(-t), c)
    e_end = _each(lambda t, tl: jnp.exp(tl - t), c, c_last)
    ah = _each(lambda kn_, c_, lw_: stack(-kn_ * jnp.exp(c_ - lw_)), kn, c, lw)
    rh = _each(lambda r_, c_: stack(r_ * jnp.exp(c_)), r, c)
    bh = _each(lambda b_, e_: stack(b_ * e_), bv, e_out)
    kh = _each(lambda k_, e_: stack(k_ * e_), k, e_out)
    bbar = _each(lambda b_, e_: stack(b_ * e_), bv, e_end)
    kbar = _each(lambda k_, e_: stack(k_ * e_), k, e_end)
    vs = _each(stack, v)
    gm = _each(lambda a_, r_, b_, k_: _dot_nt(jnp.concatenate([a_, r_], axis=0),
                                              jnp.concatenate([b_, k_], axis=0)), ah, rh, bh, kh)
    a_ab = _each(lambda g_: jnp.where(mk["strict"], g_[:n, :n], 0.0), gm)
    a_ak = _each(lambda g_: jnp.where(mk["strict"], g_[:n, n:], 0.0), gm)
    l_rb = _each(lambda g_: jnp.where(mk["incl"], g_[n:, :n], 0.0), gm)
    l_rk = _each(lambda g_: jnp.where(mk["incl"], g_[n:, n:], 0.0), gm)
    d1 = _each(lambda a_: jnp.where(mk["s8"], a_, 0.0), a_ab)
    x = _each(lambda d_: mk["eye"] + d_, d1)
    d2 = _each(lambda d_: _dot(d_, d_), d1)
    x = _each(lambda x_, d_: x_ + _dot(x_, d_), x, d2)
    d4 = _each(lambda d_: _dot(d_, d_), d2)
    x = _each(lambda x_, d_: x_ + _dot(x_, d_), x, d4)
    for lvl in ("e16", "e32", "e64"):
        ex = _each(lambda a_, x_: _dot(jnp.where(mk[lvl], a_, 0.0), x_), a_ab, x)
        x = _each(lambda x_, e_: x_ + _dot(x_, e_), x, ex)
    av = _each(_dot, a_ak, vs)
    tw = _each(lambda x_, a_, v_: _dot(x_, jnp.concatenate([a_, v_], axis=1)), x, ah, av)
    lwm = _each(_dot, l_rb, tw)
    lv = _each(_dot, l_rk, vs)
    qm = _each(lambda r_, l_: r_ + l_[:, :n], rh, lwm)
    y0 = _each(lambda l_, v_: l_[:, n:] + v_, lwm, lv)
    mt = _each(lambda t_, b_: _dot_tn(t_[:, :n], b_), tw, bbar)
    nt = _each(lambda t_, b_, v_, k_: _dot_tn(t_[:, n:], b_) + _dot_tn(v_, k_), tw, bbar, vs, kbar)
    return qm, y0, mt, nt, _each(jnp.exp, c_last)


def _split_dot_left(m, x):
    hi = x.astype(BF16)
    lo = (x - hi.astype(F32)).astype(BF16)
    return (jnp.dot(m, hi, preferred_element_type=F32)
            + jnp.dot(m, lo, preferred_element_type=F32))


def _wkv_kernel(r_ref, k_ref, v_ref, hl_ref, w2_ref, a2_ref, g2_ref, w0_ref, a0_ref, kk_ref, ka_ref,
                rk_ref, gg_ref, gb_ref, o_ref, sout_ref, s_scr):
    c_idx = pl.program_id(0)

    @pl.when(c_idx == 0)
    def _():
        s_scr[...] = jnp.zeros_like(s_scr)

    mk = _wkv_masks()
    ti = lax.broadcasted_iota(jnp.int32, (WKV_CHUNK, WKV_CHUNK), 0)
    tj = lax.broadcasted_iota(jnp.int32, (WKV_CHUNK, WKV_CHUNK), 1)
    tri = (ti >= tj).astype(BF16)
    lane_lo = lax.broadcasted_iota(jnp.int32, (WKV_CHUNK, LANES), 1) < RWKV_HEAD
    ones = _head_ones()
    pairs = range(WKV_PAIRS)
    lanes = [(slice(ch * WKV_CHUNK, (ch + 1) * WKV_CHUNK), slice(p * LANES, (p + 1) * LANES))
             for ch in range(WKV_STEP_CHUNKS) for p in pairs]
    cut = lambda t: [t[rows, sl] for rows, sl in lanes]
    rep = lambda t: [t[:, sl] for _, sl in lanes]
    hw, ha, hg = _lora_hidden(hl_ref[...])
    wl, al, g = _dot(hw, w2_ref[...]), _dot(ha, a2_ref[...]), _dot(hg, g2_ref[...])
    r, v = cut(r_ref), cut(v_ref)
    lw, kp, kn, bv = _prep_slabs(cut(k_ref), cut(wl), cut(al), rep(w0_ref), rep(a0_ref),
                                 rep(kk_ref), rep(ka_ref), ones)
    qm, y0, mt, nt, dec = _wkv_lanes(r, lw, kp, v, kn, bv, mk, tri, lane_lo)
    S = [s_scr[p] for p in pairs]
    y = []
    for ch in range(WKV_STEP_CHUNKS):
        part = slice(ch * WKV_PAIRS, (ch + 1) * WKV_PAIRS)
        ys = _each(lambda q_, s_, y_: _dot_nt(q_, s_) + y_, qm[part], S, y0[part])
        y += _each(lambda t: t[:WKV_CHUNK, :] + t[WKV_CHUNK:, :], ys)
        S = _each(lambda s_, d_, m_, n_: s_ * d_ + _dot(s_, m_) + n_, S, dec[part], mt[part], nt[part])
    for p in pairs:
        s_scr[p] = S[p]
    tok = _gn_gate(y, r, kp, v, cut(g), rep(rk_ref), rep(gg_ref), rep(gb_ref), ones)
    for (rows, sl), t in zip(lanes, tok):
        o_ref[rows, sl] = t.astype(o_ref.dtype)

    @pl.when(c_idx == pl.num_programs(0) - 1)
    def _():
        sout_ref[...] = s_scr[...]


def _wkv_prompt(proj, w2p, a2p, g2, consts):
    T = proj.shape[0]
    rows = WKV_STEP_CHUNKS * WKV_CHUNK
    assert T % rows == 0
    lora_blk = (RWKV_IN_WIDTH - LORA_IN_WIDTH) // LORA_IN_WIDTH
    tok = lambda blk: pl.BlockSpec((rows, TOK_WIDTH), lambda c, blk=blk: (c, blk))
    full = lambda a: pl.BlockSpec(a.shape, lambda c: (0,) * a.ndim)
    weights = (w2p, a2p, g2) + tuple(consts)
    return pl.pallas_call(
        _wkv_kernel, grid=(T // rows,),
        in_specs=[tok(0), tok(1), tok(2), pl.BlockSpec((rows, LORA_IN_WIDTH), lambda c: (c, lora_blk))]
        + [full(a) for a in weights],
        out_specs=[tok(0), pl.BlockSpec((WKV_PAIRS, LANES, LANES), lambda c: (0, 0, 0))],
        out_shape=[jax.ShapeDtypeStruct((T, TOK_WIDTH), BF16),
                   jax.ShapeDtypeStruct((WKV_PAIRS, LANES, LANES), F32)],
        scratch_shapes=[pltpu.VMEM((WKV_PAIRS, LANES, LANES), F32)],
        compiler_params=_cparams("arbitrary"), name="wkv_chunked")(proj, proj, proj, proj, *weights)


WKV_STEP_UNROLL = 4


def _wkv_step_kernel(s_ref, r_ref, v_ref, lw_ref, kp_ref, kn_ref, bv_ref, g_ref, rk_ref, gg_ref, gb_ref,
                     so_ref, tok_ref, y_scr):
    n = RWKV_HEAD
    inv_n = 1.0 / n
    for hh in range(2):
        rows = slice(hh * n, (hh + 1) * n)
        a, w = -kn_ref[rows, :], jnp.exp(lw_ref[rows, :])
        b, k, r = bv_ref[rows, :], kp_ref[rows, :], r_ref[rows, :]

        def value_row(i, carry, hh=hh, a=a, w=w, b=b, k=k, r=r):
            s = s_ref[hh, i]
            sa = jnp.sum(s * a, axis=0, keepdims=True)
            s_new = s * w + sa * b + v_ref[pl.ds(hh * n + i, 1), :] * k
            so_ref[hh, i] = s_new
            y_scr[pl.ds(hh * n + i, 1), :] = jnp.sum(s_new * r, axis=0, keepdims=True)
            return carry

        lax.fori_loop(0, n, value_row, 0, unroll=WKV_STEP_UNROLL)
    for hh in range(2):
        rows = slice(hh * n, (hh + 1) * n)
        y = y_scr[rows, :]
        d = y - jnp.sum(y, axis=0, keepdims=True) * inv_n
        var = jnp.sum(d * d, axis=0, keepdims=True) * inv_n
        bonus = jnp.sum(r_ref[rows, :] * kp_ref[rows, :] * rk_ref[rows, :], axis=0, keepdims=True)
        yn = d * lax.rsqrt(var + GN_EPS) * gg_ref[rows, :] + gb_ref[rows, :]
        tok_ref[rows, :] = (yn + bonus * v_ref[rows, :]) * g_ref[rows, :]


def _wkv_step(state, layer, vecs, consts):
    B = state.shape[-1]
    sshape = (2, RWKV_HEAD, RWKV_HEAD, B)
    vblk = pl.BlockSpec((LANES, B), lambda p: (p, 0))
    return pl.pallas_call(
        _wkv_step_kernel, grid=(WKV_PAIRS,),
        in_specs=[pl.BlockSpec((None,) + sshape, lambda p: (layer, p, 0, 0, 0))] + [vblk] * 10,
        out_specs=[pl.BlockSpec(sshape, lambda p: (p, 0, 0, 0)), vblk],
        out_shape=[jax.ShapeDtypeStruct(state.shape[1:], F32), jax.ShapeDtypeStruct((TOK_WIDTH, B), F32)],
        scratch_shapes=[pltpu.VMEM((LANES, B), F32)],
        compiler_params=_cparams("parallel"), name="wkv_step")(state, *vecs, *consts)


STEP_INTERLEAVE = 4


def _softmax_rows(s):
    m = jnp.max(s, axis=-1, keepdims=True)
    e = jnp.exp(s - m)
    return e * (1.0 / jnp.sum(e, axis=-1, keepdims=True))


def _mem_attn_rows(q, k, v):
    heads = [slice(h * MEM_HEAD, (h + 1) * MEM_HEAD) for h in range(MEM_HEADS)]
    s = [_dot_nt(q[:, sl], k[:, sl]) * MEM_SCALE for sl in heads]
    p = [_softmax_rows(t) for t in s]
    return jnp.concatenate([_dot(t, v[:, sl]) for sl, t in zip(heads, p)], axis=1)


def _mem_attn_step_kernel(q_ref, k_ref, v_ref, o_ref, *, bs):
    rows = MEM_TOKENS * MEM_HEADS
    col_head = lax.broadcasted_iota(jnp.int32, (SUBLANES, rows), 1) % MEM_HEADS
    row_head = lax.broadcasted_iota(jnp.int32, (SUBLANES, rows), 0) % MEM_HEADS
    own = col_head == row_head

    def group(t, carry):
        bs_ = [t * STEP_INTERLEAVE + u for u in range(STEP_INTERLEAVE)]
        s = [_dot_nt(q_ref[b], k_ref[b]) * MEM_SCALE for b in bs_]
        p = [_softmax_rows(jnp.where(own, t_, -jnp.inf)) for t_ in s]
        for b, p_ in zip(bs_, p):
            o_ref[b] = _dot(p_, v_ref[b])
        return carry

    lax.fori_loop(0, bs // STEP_INTERLEAVE, group, 0)


def _mem_attn_step(q, mk, mv, layer, *, bs):
    B = q.shape[0]
    qblk = pl.BlockSpec((bs, SUBLANES, MEM_HEAD), lambda i: (i, 0, 0))
    cblk = pl.BlockSpec((None, bs, MEM_TOKENS * MEM_HEADS, MEM_HEAD), lambda i: (layer, i, 0, 0))
    return pl.pallas_call(
        functools.partial(_mem_attn_step_kernel, bs=bs), grid=(B // bs,),
        in_specs=[qblk, cblk, cblk], out_specs=qblk,
        out_shape=jax.ShapeDtypeStruct(q.shape, F32),
        compiler_params=_cparams("parallel"), name="mem_attn_step")(q, mk, mv)


def _deepnorm_ln(res, h, g, beta):
    z = ALPHA * res + h
    mu = jnp.mean(z, axis=-1, keepdims=True)
    d = z - mu
    var = jnp.mean(d * d, axis=-1, keepdims=True)
    return d * lax.rsqrt(var + LN_EPS) * g + beta


def _out_ln_kernel(tok_ref, q_ref, k_ref, v_ref, res_ref, toks_ref, mos_ref, ress_ref, w_ref, g_ref, beta_ref,
                   of_ref, ob_ref, ofs_ref, obs_ref):
    kt = tok_ref.shape[1]

    def project_ln(tok, mo, res):
        h = (jnp.dot(tok, w_ref[0:kt, :], preferred_element_type=F32)
             + jnp.dot(mo.astype(BF16), w_ref[kt:, :], preferred_element_type=F32))
        return _deepnorm_ln(res, h, g_ref[...], beta_ref[...])

    tm = res_ref.shape[0]
    half = tm // 2 if tm % (2 * SUBLANES) == 0 else tm
    k, v = k_ref[...], v_ref[...]
    for rows in (slice(r0, r0 + half) for r0 in range(0, tm, half)):
        out = project_ln(tok_ref[rows, :], _mem_attn_rows(q_ref[rows, :], k, v), res_ref[rows, :])
        of_ref[rows, :] = out
        ob_ref[rows, :] = out.astype(BF16)

    @pl.when(pl.program_id(0) == 0)
    def _():
        out = project_ln(toks_ref[...], mos_ref[...], ress_ref[...])
        ofs_ref[...] = out
        obs_ref[...] = out.astype(BF16)


def _out_ln(tok, proj, qblk, kv, res, tok_s, mo_s, res_s, w, layer, g, beta, *, tm):
    M, Kt = tok.shape
    Ms = res_s.shape[0]
    K, N = w.shape[1], w.shape[2]
    assert Kt + MEM_WIDTH == K and M % tm == 0 and tok_s.shape[1] == Kt and mo_s.shape[1] == MEM_WIDTH
    row = lambda i: (i, 0)
    cst = lambda i: (0, 0)
    in_specs = [pl.BlockSpec((tm, Kt), row), pl.BlockSpec((tm, MEM_WIDTH), lambda i: (i, qblk)),
                pl.BlockSpec((MEM_TOKENS, MEM_WIDTH), lambda i: (0, 0)),
                pl.BlockSpec((MEM_TOKENS, MEM_WIDTH), lambda i: (0, 1)),
                pl.BlockSpec((tm, N), row),
                pl.BlockSpec((Ms, Kt), cst), pl.BlockSpec((Ms, MEM_WIDTH), cst), pl.BlockSpec((Ms, N), cst),
                pl.BlockSpec((None, K, N), lambda i: (layer, 0, 0)),
                pl.BlockSpec((1, N), cst), pl.BlockSpec((1, N), cst)]
    oblk, sblk = pl.BlockSpec((tm, N), row), pl.BlockSpec((Ms, N), cst)
    of, ob, ofs, obs = pl.pallas_call(
        _out_ln_kernel, grid=(M // tm,),
        in_specs=in_specs, out_specs=[oblk, oblk, sblk, sblk],
        out_shape=[jax.ShapeDtypeStruct((M, N), F32), jax.ShapeDtypeStruct((M, N), BF16),
                   jax.ShapeDtypeStruct((Ms, N), F32), jax.ShapeDtypeStruct((Ms, N), BF16)],
        compiler_params=_cparams("arbitrary"), name="out_ln")(
            tok, proj, kv, kv, res, tok_s, mo_s, res_s, w, g, beta)
    return (of, ob), (ofs, obs)


def _ffn_down_ln_kernel(a_ref, as_ref, w_ref, res_ref, ress_ref, g_ref, beta_ref, o_ref, os_ref):
    i, k = pl.program_id(0), pl.program_id(1)
    last = pl.num_programs(1) - 1

    def step(lhs_ref, resid_ref, out_ref):
        part = lambda rows: jnp.dot(lhs_ref[rows, :], w_ref[...], preferred_element_type=F32)
        rows_all = slice(None)
        tm = out_ref.shape[0]
        half = tm // 2 if tm % (2 * SUBLANES) == 0 else tm

        @pl.when(k == 0)
        def _():
            out_ref[...] = part(rows_all)

        @pl.when((k > 0) & (k < last))
        def _():
            out_ref[...] += part(rows_all)

        @pl.when(k == last)
        def _():
            for rows in (slice(r0, r0 + half) for r0 in range(0, tm, half)):
                out_ref[rows, :] = _deepnorm_ln(resid_ref[rows, :], out_ref[rows, :] + part(rows),
                                                g_ref[...], beta_ref[...])

    step(a_ref, res_ref, o_ref)

    @pl.when(i == 0)
    def _():
        step(as_ref, ress_ref, os_ref)


def _ffn_down_ln(a, res, a_s, res_s, w, layer, g, beta, *, tm, tk):
    M, K = a.shape
    Ms = a_s.shape[0]
    N = w.shape[2]
    nk = K // tk
    assert w.shape[1] == K and K % tk == 0 and nk >= 2 and M % tm == 0 and a_s.shape[1] == K
    row = lambda i, k: (i, 0)
    cst = lambda i, k: (0, 0)
    return pl.pallas_call(
        _ffn_down_ln_kernel, grid=(M // tm, nk),
        in_specs=[pl.BlockSpec((tm, tk), lambda i, k: (i, k)),
                  pl.BlockSpec((Ms, tk), lambda i, k: (0, _rider_block(i, k, nk))),
                  pl.BlockSpec((None, tk, N), lambda i, k: (layer, k, 0)),
                  pl.BlockSpec((tm, N), row), pl.BlockSpec((Ms, N), cst),
                  pl.BlockSpec((1, N), cst), pl.BlockSpec((1, N), cst)],
        out_specs=[pl.BlockSpec((tm, N), row), pl.BlockSpec((Ms, N), cst)],
        out_shape=[jax.ShapeDtypeStruct((M, N), F32), jax.ShapeDtypeStruct((Ms, N), F32)],
        compiler_params=_cparams("arbitrary", "arbitrary"), name="ffn_down_ln")(
            a, a_s, w, res, res_s, g, beta)


def _ffn_up_kernel(x_ref, xs_ref, wg_ref, wu_ref, wd_ref, o_ref, os_ref, wdb_ref):
    wg, wu = wg_ref[...].astype(BF16), wu_ref[...].astype(BF16)

    def swiglu(x):
        gate = jnp.dot(x, wg, preferred_element_type=F32)
        up = jnp.dot(x, wu, preferred_element_type=F32)
        return (gate * jax.nn.sigmoid(gate) * up).astype(BF16)

    o_ref[...] = swiglu(x_ref[...])

    @pl.when(pl.program_id(0) == 0)
    def _():
        os_ref[...] = swiglu(xs_ref[...])
        wdb_ref[...] = wd_ref[...].astype(BF16)


def _ffn_up(x, xs, wg, wu, wd, layer, *, tm, tn):
    M, K = x.shape
    N = wg.shape[2]
    nj = N // tn
    assert wd.shape[1] == N
    wblk = pl.BlockSpec((None, K, tn), lambda i, j: (layer, 0, j))
    rider = lambda i, j: _rider_block(i, j, nj)
    return pl.pallas_call(
        _ffn_up_kernel, grid=(M // tm, nj),
        in_specs=[pl.BlockSpec((tm, K), lambda i, j: (i, 0)), pl.BlockSpec(xs.shape, lambda i, j: (0, 0)),
                  wblk, wblk, pl.BlockSpec((None, tn, wd.shape[2]), lambda i, j: (layer, rider(i, j), 0))],
        out_specs=[pl.BlockSpec((tm, tn), lambda i, j: (i, j)),
                   pl.BlockSpec((xs.shape[0], tn), lambda i, j: (0, rider(i, j))),
                   pl.BlockSpec((tn, wd.shape[2]), lambda i, j: (rider(i, j), 0))],
        out_shape=[jax.ShapeDtypeStruct((M, N), BF16), jax.ShapeDtypeStruct((xs.shape[0], N), BF16),
                   jax.ShapeDtypeStruct(wd.shape[1:], BF16)],
        compiler_params=_cparams("arbitrary", "arbitrary"), name="ffn_up")(x, xs, wg, wu, wd)


def _rope_kernel(x_ref, cos_ref, sin_ref, o_ref):
    x = x_ref[...]
    lane = lax.broadcasted_iota(jnp.int32, x.shape, 1)
    first_half = (lane % SWA_HEAD) < (SWA_HEAD // 2)
    partner = jnp.where(first_half, pltpu.roll(x, LANES - SWA_HEAD // 2, 1),
                        pltpu.roll(x, SWA_HEAD // 2, 1))
    o_ref[...] = x * cos_ref[...] + partner * sin_ref[...]


def _rope(proj, cos, sin_signed, *, tm):
    M = proj.shape[0]
    width = TOK_WIDTH + SWA_KV_WIDTH
    blk = pl.BlockSpec((tm, LANES), lambda i, s: (i, s))
    tab = pl.BlockSpec((tm, LANES), lambda i, s: (i, 0))
    return pl.pallas_call(
        _rope_kernel, grid=(M // tm, width // LANES),
        in_specs=[blk, tab, tab], out_specs=blk,
        out_shape=jax.ShapeDtypeStruct((M, width), F32),
        compiler_params=_cparams("parallel", "arbitrary"), name="rope")(proj, cos, sin_signed)


def _sink_column(sink_ref, base, rows_per_head, nheads):
    rows = rows_per_head * nheads
    hid = lax.broadcasted_iota(jnp.int32, (rows, 1), 0) // rows_per_head
    col = jnp.zeros((rows, 1), F32)
    for j in range(nheads):
        col = jnp.where(hid == j, sink_ref[base + j], col)
    return col


def _sink_softmax(s, sink):
    m = jnp.maximum(jnp.max(s, axis=-1, keepdims=True), sink)
    p = jnp.exp(s - m)
    return p * (1.0 / (jnp.sum(p, axis=-1, keepdims=True) + jnp.exp(sink - m)))


def _swa_kernel(sink_ref, q_ref, kc_ref, kp_ref, vc_ref, vp_ref, cosc_ref, sinc_ref, cosp_ref, sinp_ref,
                o_ref, krot_ref):
    n = pl.program_id(0)
    nslab_q = TOK_WIDTH // LANES
    slab = lambda ref, s: ref[:, s * LANES:(s + 1) * LANES]
    cos_c, sin_c, cos_p, sin_p = cosc_ref[...], sinc_ref[...], cosp_ref[...], sinp_ref[...]
    lane = lax.broadcasted_iota(jnp.int32, (BLOCK, LANES), 1)
    first_half = (lane % SWA_HEAD) < (SWA_HEAD // 2)
    lo = lane < SWA_HEAD

    def rope(x, cos, sin):
        partner = jnp.where(first_half, pltpu.roll(x, LANES - SWA_HEAD // 2, 1),
                            pltpu.roll(x, SWA_HEAD // 2, 1))
        return x * cos + partner * sin

    kv_slabs = SWA_KV_WIDTH // LANES
    k_cur = [rope(slab(kc_ref, j), cos_c, sin_c) for j in range(kv_slabs)]
    k_prev = [rope(slab(kp_ref, j), cos_p, sin_p) for j in range(kv_slabs)]
    for j in range(kv_slabs):
        krot_ref[:, j * LANES:(j + 1) * LANES] = k_cur[j]
    lane2 = lax.broadcasted_iota(jnp.int32, (2 * BLOCK, LANES), 1)
    kd, vd = [], []
    for g in range(SWA_KV_HEADS):
        j, half = divmod(g, 2)
        keep = (lane2 < SWA_HEAD) if half == 0 else (lane2 >= SWA_HEAD)
        dup = lambda t: jnp.where(keep, t, pltpu.roll(t, SWA_HEAD, 1)).astype(BF16)
        kd.append(dup(jnp.concatenate([k_prev[j], k_cur[j]], axis=0)))
        vd.append(dup(jnp.concatenate([slab(vp_ref, j), slab(vc_ref, j)], axis=0)))
    qi = lax.broadcasted_iota(jnp.int32, (2 * BLOCK, 2 * BLOCK), 0) % BLOCK
    si = lax.broadcasted_iota(jnp.int32, (2 * BLOCK, 2 * BLOCK), 1)
    valid = (si > qi) & (si <= qi + WINDOW) & ((n > 0) | (si >= BLOCK))
    row_lo = lax.broadcasted_iota(jnp.int32, (2 * BLOCK, 1), 0) < BLOCK
    slabs = list(range(nslab_q))
    kv_of = [(2 * s) // SWA_GROUP for s in slabs]
    assert math.frexp(SWA_SCALE)[0] == 0.5
    q = [rope(slab(q_ref, s), cos_c, sin_c) * SWA_SCALE for s in slabs]
    qs = [jnp.concatenate([jnp.where(lo, t, 0.0), jnp.where(lo, 0.0, t)], axis=0).astype(BF16) for t in q]
    sc = [lax.dot_general(t, kd[g], NT_DIMS, preferred_element_type=F32) for t, g in zip(qs, kv_of)]
    sc = [jnp.where(valid, t, -jnp.inf) for t in sc]
    p = [_sink_softmax(t, jnp.where(row_lo, sink_ref[2 * s], sink_ref[2 * s + 1])) for t, s in zip(sc, slabs)]
    o = [jnp.dot(t.astype(BF16), vd[g], preferred_element_type=F32) for t, g in zip(p, kv_of)]
    for s, t in zip(slabs, o):
        o_ref[:, s * LANES:(s + 1) * LANES] = jnp.where(lo, t[:BLOCK], t[BLOCK:]).astype(o_ref.dtype)


def _swa_prompt(proj, cos, sin_signed, sinks):
    T = proj.shape[0]
    kblk, vblk = TOK_WIDTH // SWA_KV_WIDTH, TOK_WIDTH // SWA_KV_WIDTH + 1
    prev = lambda n: jnp.maximum(n - 1, 0)
    kv_spec = lambda blk, row: pl.BlockSpec((BLOCK, SWA_KV_WIDTH), lambda n: (row(n), blk))
    tab = lambda row: pl.BlockSpec((BLOCK, LANES), lambda n: (row(n), 0))
    cur = lambda n: n
    return pl.pallas_call(
        _swa_kernel, grid=(T // BLOCK,),
        in_specs=[pl.BlockSpec(memory_space=pltpu.SMEM),
                  pl.BlockSpec((BLOCK, TOK_WIDTH), lambda n: (n, 0)),
                  kv_spec(kblk, cur), kv_spec(kblk, prev), kv_spec(vblk, cur), kv_spec(vblk, prev),
                  tab(cur), tab(cur), tab(prev), tab(prev)],
        out_specs=[pl.BlockSpec((BLOCK, TOK_WIDTH), lambda n: (n, 0)),
                   pl.BlockSpec((BLOCK, SWA_KV_WIDTH), lambda n: (n, 0))],
        out_shape=[jax.ShapeDtypeStruct((T, TOK_WIDTH), BF16),
                   jax.ShapeDtypeStruct((T, SWA_KV_WIDTH), F32)],
        compiler_params=_cparams("arbitrary"), name="swa_banded")(
            sinks, proj, proj, proj, proj, proj, cos, sin_signed, cos, sin_signed)


def _swa_step_kernel(sink_ref, q_ref, kn_ref, vn_ref, kt_ref, vt_ref, o_ref, kto_ref, vto_ref, *, bs):
    sink = _sink_column(sink_ref, 0, 1, SWA_Q_HEADS)
    newest = lax.broadcasted_iota(jnp.int32, (SWA_HEAD, WINDOW), 1) == WINDOW - 1
    kv_of_head = lax.broadcasted_iota(jnp.int32, (SWA_Q_HEADS, 1), 0) // SWA_GROUP
    kv_heads = list(range(SWA_KV_HEADS))

    def pick(per_kv):
        out = per_kv[0]
        for g in kv_heads[1:]:
            out = jnp.where(kv_of_head == g, per_kv[g], out)
        return out

    def slid(new_ref, cache_ref, b):
        new = new_ref[b]
        return [jnp.where(newest, new[:, g:g + 1], pltpu.roll(cache_ref[b, g], WINDOW - 1, 1)) for g in kv_heads]

    def group(t, carry):
        bs_ = [t * STEP_INTERLEAVE + u for u in range(STEP_INTERLEAVE)]
        kt = [slid(kn_ref, kt_ref, b) for b in bs_]
        vt = [slid(vn_ref, vt_ref, b) for b in bs_]
        for b, kt_, vt_ in zip(bs_, kt, vt):
            for g in kv_heads:
                kto_ref[b, g] = kt_[g]
                vto_ref[b, g] = vt_[g]
        s = [pick([_dot(q_ref[b], t_) for t_ in kt_]) * SWA_SCALE for b, kt_ in zip(bs_, kt)]
        p = [_sink_softmax(s_, sink) for s_ in s]
        for b, p_, vt_ in zip(bs_, p, vt):
            o_ref[b] = pick([_dot_nt(p_, t_) for t_ in vt_])
        return carry

    lax.fori_loop(0, bs // STEP_INTERLEAVE, group, 0)


def _swa_step(q, k_new, v_new, kt, vt, layer, sinks, *, bs):
    B = q.shape[0]
    qblk = pl.BlockSpec((bs, SWA_Q_HEADS, SWA_HEAD), lambda i: (i, 0, 0))
    nblk = pl.BlockSpec((bs, SWA_HEAD, SWA_KV_HEADS), lambda i: (i, 0, 0))
    cshape = (bs, SWA_KV_HEADS, SWA_HEAD, WINDOW)
    cin = pl.BlockSpec((None,) + cshape, lambda i: (layer, i, 0, 0, 0))
    cout = pl.BlockSpec(cshape, lambda i: (i, 0, 0, 0))
    cache = jax.ShapeDtypeStruct(kt.shape[1:], F32)
    return pl.pallas_call(
        functools.partial(_swa_step_kernel, bs=bs), grid=(B // bs,),
        in_specs=[pl.BlockSpec(memory_space=pltpu.SMEM), qblk, nblk, nblk, cin, cin],
        out_specs=[qblk, cout, cout],
        out_shape=[jax.ShapeDtypeStruct(q.shape, F32), cache, cache],
        compiler_params=_cparams("parallel"), name="swa_step")(sinks, q, k_new, v_new, kt, vt)


ROW_TILE = 512
FFN_ROW_TILE = 1024
COL_TILE = 512
STEP_BATCH = 16
FFN_DOWN_K_TILE = FFN_HIDDEN // 2


def _row_tile(m):
    return ROW_TILE if m % ROW_TILE == 0 else m


def _ffn_row_tile(m):
    return FFN_ROW_TILE if m % FFN_ROW_TILE == 0 else _row_tile(m)


def _pad_rows(w, rows):
    return jnp.pad(w, ((0, rows - w.shape[0]), (0, 0)))


def _pad_cols(w, cols):
    return jnp.pad(w, ((0, 0), (0, cols - w.shape[1])))


def _rwkv_weights(w_in, mu, w1, w2, a1, a2, g1, g2):
    w_lora = jnp.concatenate([_pad_cols(w1[0], LORA_PAD), _pad_cols(a1[0], LORA_PAD), g1[0]], axis=1)
    return dict(w_in=w_in.astype(BF16), w_lora=w_lora.astype(BF16), mu=mu[0][:, None, :],
                w2=_pad_rows(w2[0], LORA_PAD).astype(BF16), a2=_pad_rows(a2[0], LORA_PAD).astype(BF16),
                g2=g2[0].astype(BF16))


def _unblock_state(s_bd):
    n = RWKV_HEAD
    return jnp.stack([s_bd[:, :n, :n], s_bd[:, n:, n:]], axis=1).reshape(RWKV_HEADS, n, n)


def _rope_tables(pos):
    half = SWA_HEAD // 2
    inv = ROPE_THETA ** (-jnp.arange(half, dtype=F32) / half)
    ang = pos.astype(F32)[:, None] * inv[None, :]
    cos, sin = jnp.cos(ang), jnp.sin(ang)
    reps = LANES // SWA_HEAD
    return jnp.tile(cos, (1, 2 * reps)), jnp.tile(jnp.concatenate([-sin, sin], axis=1), (1, reps))


def _post_mixer(prompt, sample, sw, layer):
    row = lambda t: t[layer][None, :]
    tok_p, proj_p, qblk, kv, x_p = prompt
    tok_s, mo_s, x_s = sample
    (x1f_p, x1b_p), (x1f_s, x1b_s) = _out_ln(tok_p, proj_p, qblk, kv, x_p, tok_s, mo_s, x_s, sw["w_out"], layer,
                                             row(sw["ln1_g"]), row(sw["ln1_b"]), tm=_row_tile(x_p.shape[0]))
    hff_p, hff_s, wd = _ffn_up(x1b_p, x1b_s, sw["w_gate"], sw["w_up"], sw["w_down"], layer,
                               tm=_ffn_row_tile(x1b_p.shape[0]), tn=COL_TILE)
    return _ffn_down_ln(hff_p, x1f_p, hff_s, x1f_s, wd[None], 0, row(sw["ln2_g"]), row(sw["ln2_b"]),
                        tm=_row_tile(x1f_p.shape[0]), tk=FFN_DOWN_K_TILE)


def kernel(x_prompt, x_sample, mem_prompt, cache_mem_k, cache_mem_v, state_rwkv_shift, state_rwkv_wkv, cache_swa_k, cache_swa_v, w_in_rwkv, rwkv_mu, rwkv_w0, rwkv_w1, rwkv_w2, rwkv_a0, rwkv_a1, rwkv_a2, rwkv_g1, rwkv_g2, rwkv_k_k, rwkv_k_a, rwkv_r_k, rwkv_gn_g, rwkv_gn_b, w_in_swa, swa_sinks, w_mem_kv, w_out, ln1_g, ln1_b, w_gate, w_up, w_down, ln2_g, ln2_b):
    assert DEPTH == 2 and x_prompt.shape[0] == 1 and x_sample.shape[1] == 1
    T = x_prompt.shape[1]
    B = x_sample.shape[0]
    row = lambda t: t[None, :]
    shared = dict(w_out=w_out.astype(BF16), ln1_g=ln1_g, ln1_b=ln1_b, w_gate=w_gate, w_up=w_up,
                  w_down=w_down, ln2_g=ln2_g, ln2_b=ln2_b)
    RW = _rwkv_weights(w_in_rwkv, rwkv_mu, rwkv_w1, rwkv_w2, rwkv_a1, rwkv_a2, rwkv_g1, rwkv_g2)
    rk, gn_g, gn_b = row(rwkv_r_k[0].reshape(-1)), row(rwkv_gn_g[0]), row(rwkv_gn_b[0])
    prep_consts = (row(rwkv_w0[0]), row(rwkv_a0[0]), row(rwkv_k_k[0]), row(rwkv_k_a[0]))
    w_swa = w_in_swa.astype(BF16)
    sinks = swa_sinks[0]
    q_blk_rwkv = 3 * TOK_WIDTH // MEM_WIDTH
    q_blk_swa = (TOK_WIDTH + 2 * SWA_KV_WIDTH) // MEM_WIDTH

    xp, xs = x_prompt[0], x_sample[:, 0]
    tms = _row_tile(B)
    kv = [_proj(mem_prompt[0], w_mem_kv, i, tm=MEM_TOKENS, tn=COL_TILE, name="mem_kv") for i in range(DEPTH)]
    prompt_mem_k = jnp.stack([t[:, :MEM_WIDTH] for t in kv]).reshape(DEPTH, 1, MEM_TOKENS, MEM_HEADS, MEM_HEAD)
    prompt_mem_v = jnp.stack([t[:, MEM_WIDTH:] for t in kv]).reshape(DEPTH, 1, MEM_TOKENS, MEM_HEADS, MEM_HEAD)
    mem_k = cache_mem_k.reshape(DEPTH, B, MEM_TOKENS * MEM_HEADS, MEM_HEAD)
    mem_v = cache_mem_v.reshape(DEPTH, B, MEM_TOKENS * MEM_HEADS, MEM_HEAD)

    def mem_step(q, layer):
        q_rows = jnp.pad(q.reshape(B, MEM_HEADS, MEM_HEAD), ((0, 0), (0, SUBLANES - MEM_HEADS), (0, 0)))
        out = _mem_attn_step(q_rows, mem_k, mem_v, layer, bs=STEP_BATCH)
        return out[:, :MEM_HEADS].reshape(B, MEM_WIDTH).astype(BF16)

    proj_p, proj_s = _rwkv_in(xp, xs, state_rwkv_shift[0], RW["mu"], RW["w_in"], RW["w_lora"], 0,
                              tm=_ffn_row_tile(T))
    tok_p, s_bd = _wkv_prompt(proj_p, RW["w2"], RW["a2"], RW["g2"], prep_consts + (rk, gn_g, gn_b))
    prompt_shift = xp[-1][None, None, :]
    prompt_wkv = _unblock_state(s_bd)[None, None]
    vecs = _rwkv_prep_t(proj_s, RW["w2"], RW["a2"], RW["g2"], *prep_consts)
    lanes_b = lambda t: jnp.broadcast_to(t.reshape(TOK_WIDTH, 1), (TOK_WIDTH, B))
    s_new, tok_t = _wkv_step(state_rwkv_wkv.transpose(0, 2, 3, 4, 1), 0, vecs,
                             (lanes_b(rk), lanes_b(gn_g), lanes_b(gn_b)))
    mo_s = mem_step(proj_s[:, 3 * TOK_WIDTH:3 * TOK_WIDTH + MEM_WIDTH], 0)
    sample_shift = xs[None]
    sample_wkv = s_new.transpose(3, 0, 1, 2)[None]
    xf_p, xf_s = _post_mixer((tok_p, proj_p, q_blk_rwkv, kv[0], xp), (tok_t.T.astype(BF16), mo_s, xs), shared, 0)

    proj_p, proj_s = _proj(xf_p, w_swa, 0, tm=_ffn_row_tile(T), tn=COL_TILE, name="swa_proj", xs=xf_s)
    cos, sin = _rope_tables(jnp.arange(T))
    tok_p, k_rot = _swa_prompt(proj_p, cos, sin, sinks)
    v_last = proj_p[T - WINDOW:, TOK_WIDTH + SWA_KV_WIDTH:TOK_WIDTH + 2 * SWA_KV_WIDTH]
    prompt_swa_k = k_rot[T - WINDOW:].reshape(1, 1, WINDOW, SWA_KV_HEADS, SWA_HEAD)
    prompt_swa_v = v_last.reshape(1, 1, WINDOW, SWA_KV_HEADS, SWA_HEAD)
    cos, sin = _rope_tables(jnp.full((B,), PAST_LEN))
    qk = _rope(proj_s, cos, sin, tm=tms)
    chan_major = lambda t: t.reshape(B, SWA_KV_HEADS, SWA_HEAD).transpose(0, 2, 1)
    k_new = chan_major(qk[:, TOK_WIDTH:])
    v_new = chan_major(proj_s[:, TOK_WIDTH + SWA_KV_WIDTH:TOK_WIDTH + 2 * SWA_KV_WIDTH])
    o, kc, vc = _swa_step(qk[:, :TOK_WIDTH].reshape(B, SWA_Q_HEADS, SWA_HEAD), k_new, v_new,
                          cache_swa_k.transpose(0, 1, 3, 4, 2), cache_swa_v.transpose(0, 1, 3, 4, 2),
                          0, sinks, bs=STEP_BATCH)
    mo_s = mem_step(proj_s[:, TOK_WIDTH + 2 * SWA_KV_WIDTH:], 1)
    sample_swa_k, sample_swa_v = kc.transpose(0, 3, 1, 2)[None], vc.transpose(0, 3, 1, 2)[None]
    y_prompt, y_sample = _post_mixer((tok_p, proj_p, q_blk_swa, kv[1], xf_p),
                                     (o.reshape(B, TOK_WIDTH).astype(BF16), mo_s, xf_s), shared, 1)

    return (y_prompt[None], y_sample[:, None, :], prompt_mem_k, prompt_mem_v, prompt_shift, prompt_wkv,
            prompt_swa_k, prompt_swa_v, sample_shift, sample_wkv, sample_swa_k, sample_swa_v)
```
